```python
import math
import jax, jax.numpy as jnp
from jax import lax
import numpy as np

D_MODEL = 1024
BATCH = 2
SEQ = 8192
DEPTH = 1

N_HEADS = 8
HEAD_DIM = 64
ATTN_WIDTH = N_HEADS * HEAD_DIM
IDX_HEADS = 8
IDX_DIM = 64
TOPK_MAX = 256
Q_BLOCK = 128
N_BUCKETS = 32
MAX_DISTANCE = 128
CONV_WIDTH = 512
CONV_K = 3
N_BRANCH = 2
N_GROUPS = 4
EXPERTS_PER_GROUP = 8
N_EXPERTS = N_GROUPS * EXPERTS_PER_GROUP
TOP_K_EXPERTS = 2
D_EXPERT = 512
EPS = 1e-6

SPLITS = (ATTN_WIDTH, ATTN_WIDTH, ATTN_WIDTH,
          IDX_HEADS * IDX_DIM, IDX_DIM, IDX_HEADS,
          CONV_WIDTH, CONV_WIDTH, CONV_WIDTH,
          N_BRANCH * D_MODEL)
IN_COLS = sum(SPLITS)

kernel_name = "hybrid_dsa_shortconv_hmoe_block"


def rms_norm(x, g):
    xf = x.astype(jnp.float32)
    y = xf * lax.rsqrt(jnp.mean(xf * xf, axis=-1, keepdims=True) + EPS)
    return y * g.astype(jnp.float32)


def t5_bucket(rel):
    n = jnp.maximum(rel, 0)
    max_exact = N_BUCKETS // 2
    nf = jnp.maximum(n, 1).astype(jnp.float32)
    large = max_exact + (jnp.log(nf / max_exact) / math.log(MAX_DISTANCE / max_exact)
                         * (N_BUCKETS - max_exact)).astype(jnp.int32)
    large = jnp.minimum(large, N_BUCKETS - 1)
    return jnp.where(n < max_exact, n, large)


def dsa_attention(q, k, v, q_idx, k_idx, w_idx, rel_bias):
    B, S, H, Dh = q.shape
    K = min(TOPK_MAX, S // 4)
    nb = S // Q_BLOCK
    key_pos = jnp.arange(S)
    k_idx_f = k_idx.astype(jnp.float32)
    neg = jnp.finfo(jnp.float32).min

    def to_blocks(a):
        return a.reshape(B, nb, Q_BLOCK, *a.shape[2:]).swapaxes(0, 1)

    def block(args):
        qb, qib, wb, start = args
        qpos = start + jnp.arange(Q_BLOCK)
        dots = jnp.einsum('bqhd,bsd->bqhs', qib.astype(jnp.float32), k_idx_f) * (IDX_DIM ** -0.5)
        score = jnp.einsum('bqh,bqhs->bqs', wb.astype(jnp.float32), jax.nn.relu(dots)) * (IDX_HEADS ** -0.5)
        causal = key_pos[None, :] <= qpos[:, None]
        score = jnp.where(causal[None], score, neg)
        _, idx = lax.top_k(score, K)
        gather = jax.vmap(lambda kb, ib: kb[ib])
        kg = gather(k, idx).astype(jnp.float32)
        vg = gather(v, idx).astype(jnp.float32)
        logits = jnp.einsum('bqhd,bqkhd->bqhk', qb.astype(jnp.float32), kg) * (Dh ** -0.5)
        rel = qpos[None, :, None] - idx
        bias = rel_bias.astype(jnp.float32)[t5_bucket(rel)]
        logits = logits + bias.transpose(0, 1, 3, 2)
        logits = jnp.where((rel >= 0)[:, :, None, :], logits, -jnp.inf)
        p = jax.nn.softmax(logits, axis=-1)
        return jnp.einsum('bqhk,bqkhd->bqhd', p, vg)

    outs = lax.map(block, (to_blocks(q), to_blocks(q_idx), to_blocks(w_idx),
                           jnp.arange(nb, dtype=jnp.int32) * Q_BLOCK))
    return outs.swapaxes(0, 1).reshape(B, S, H * Dh)


def causal_short_conv(u, w):
    S = u.shape[1]
    up = jnp.pad(u, ((0, 0), (CONV_K - 1, 0), (0, 0)))
    return sum(w[j] * up[:, j:j + S] for j in range(CONV_K))


def hier_moe(h, wg, bg, we, be, w_gate, w_up, w_down):
    B, S, D = h.shape
    t = h.reshape(-1, D)
    T = t.shape[0]
    g_logits = t @ wg.astype(jnp.float32) + bg.astype(jnp.float32)
    g_prob = jax.nn.softmax(g_logits, axis=-1)
    g_sel = jnp.argmax(g_logits, axis=-1)
    g_w = jnp.take_along_axis(g_prob, g_sel[:, None], axis=1)[:, 0]
    e_logits = (t @ we.astype(jnp.float32) + be.astype(jnp.float32)).reshape(T, N_GROUPS, EXPERTS_PER_GROUP)
    e_sel_logits = e_logits[jnp.arange(T), g_sel]
    e_prob = jax.nn.softmax(e_sel_logits, axis=-1)
    top_p, top_i = lax.top_k(e_prob, TOP_K_EXPERTS)
    top_p = top_p / jnp.sum(top_p, axis=-1, keepdims=True)
    expert_id = g_sel[:, None] * EXPERTS_PER_GROUP + top_i
    weights = g_w[:, None] * top_p
    gate_dense = jnp.einsum('tk,tke->te', weights, jax.nn.one_hot(expert_id, N_EXPERTS, dtype=jnp.float32))

    def expert_step(acc, params):
        wg_e, wu_e, wd_e, gcol = params
        hid = jax.nn.silu(t @ wg_e.astype(jnp.float32)) * (t @ wu_e.astype(jnp.float32))
        return acc + gcol[:, None] * (hid @ wd_e.astype(jnp.float32)), None

    acc, _ = lax.scan(expert_step, jnp.zeros((T, D), jnp.float32),
                      (w_gate, w_up, w_down, gate_dense.T))
    return acc.reshape(B, S, D)


def setup_inputs(seed: int = 0) -> dict:
    key = jax.random.key(seed)
    ks = jax.random.split(key, 24)
    f32 = jnp.float32
    D = D_MODEL
    nrm = lambda k, shape, s: jax.random.normal(k, shape, f32) * s
    return {
        "x": nrm(ks[0], (BATCH, SEQ, D), 1.0),
        "c": nrm(ks[1], (BATCH, D), 1.0),
        "w_ada": nrm(ks[2], (DEPTH, D, 6 * D), 0.5 * D ** -0.5),
        "b_ada": nrm(ks[3], (DEPTH, 6 * D), 0.01),
        "norm1_g": 1.0 + nrm(ks[4], (DEPTH, D), 0.01),
        "w_in": nrm(ks[5], (DEPTH, D, IN_COLS), D ** -0.5),
        "rel_bias": nrm(ks[6], (N_BUCKETS, N_HEADS), 0.5),
        "conv_w": nrm(ks[7], (DEPTH, CONV_K, CONV_WIDTH), CONV_K ** -0.5),
        "w_attn_branch": nrm(ks[8], (DEPTH, ATTN_WIDTH, D), ATTN_WIDTH ** -0.5),
        "w_conv_branch": nrm(ks[9], (DEPTH, CONV_WIDTH, D), CONV_WIDTH ** -0.5),
        "w_out": nrm(ks[10], (DEPTH, D, D), D ** -0.5),
        "norm2_g": 1.0 + nrm(ks[11], (DEPTH, D), 0.01),
        "w_router_group": nrm(ks[12], (DEPTH, D, N_GROUPS), D ** -0.5),
        "b_router_group": nrm(ks[13], (DEPTH, N_GROUPS), 0.01),
        "w_router_expert": nrm(ks[14], (DEPTH, D, N_EXPERTS), D ** -0.5),
        "b_router_expert": nrm(ks[15], (DEPTH, N_EXPERTS), 0.01),
        "w_gate_e": nrm(ks[16], (DEPTH, N_EXPERTS, D, D_EXPERT), D ** -0.5),
        "w_up_e": nrm(ks[17], (DEPTH, N_EXPERTS, D, D_EXPERT), D ** -0.5),
        "w_down_e": nrm(ks[18], (DEPTH, N_EXPERTS, D_EXPERT, D), D_EXPERT ** -0.5),
        "norm_f_g": 1.0 + nrm(ks[19], (D,), 0.01),
    }


def reference(x, c, w_ada, b_ada, norm1_g, w_in, rel_bias, conv_w, w_attn_branch, w_conv_branch,
              w_out, norm2_g, w_router_group, b_router_group, w_router_expert, b_router_expert,
              w_gate_e, w_up_e, w_down_e, norm_f_g):
    B, S, D = x.shape
    offsets = np.cumsum(SPLITS)[:-1].tolist()
    h_res = x.astype(jnp.float32)
    c_act = jax.nn.silu(c.astype(jnp.float32))
    for l in range(DEPTH):
        mod = c_act @ w_ada[l].astype(jnp.float32) + b_ada[l].astype(jnp.float32)
        shift1, scale1, gate1, shift2, scale2, gate2 = [m[:, None, :] for m in jnp.split(mod, 6, axis=-1)]

        h = rms_norm(h_res, norm1_g[l]) * (1.0 + scale1) + shift1
        proj = h @ w_in[l].astype(jnp.float32)
        q, k, v, qi, ki, wi, cb, cc, cx, gl = jnp.split(proj, offsets, axis=-1)
        attn = dsa_attention(q.reshape(B, S, N_HEADS, HEAD_DIM), k.reshape(B, S, N_HEADS, HEAD_DIM),
                             v.reshape(B, S, N_HEADS, HEAD_DIM), qi.reshape(B, S, IDX_HEADS, IDX_DIM),
                             ki, wi, rel_bias)
        y_attn = attn @ w_attn_branch[l].astype(jnp.float32)
        conv = causal_short_conv(cc * cx, conv_w[l].astype(jnp.float32))
        y_conv = (cb * conv) @ w_conv_branch[l].astype(jnp.float32)
        gates = jax.nn.sigmoid(gl).reshape(B, S, N_BRANCH, D)
        merged = gates[:, :, 0] * y_attn + gates[:, :, 1] * y_conv
        h_res = h_res + gate1 * (merged @ w_out[l].astype(jnp.float32))

        h2 = rms_norm(h_res, norm2_g[l]) * (1.0 + scale2) + shift2
        h_res = h_res + gate2 * hier_moe(h2, w_router_group[l], b_router_group[l], w_router_expert[l],
                                         b_router_expert[l], w_gate_e[l], w_up_e[l], w_down_e[l])
    return rms_norm(h_res, norm_f_g).astype(x.dtype)
```

```python
import functools
import math

import numpy as np
import jax
import jax.numpy as jnp
from jax import lax
from jax.experimental import pallas as pl
from jax.experimental.pallas import tpu as pltpu

F32 = jnp.float32
BF16 = jnp.bfloat16
I32 = jnp.int32

D_MODEL = 1024
N_HEADS = 8
HEAD_DIM = 64
ATTN_WIDTH = N_HEADS * HEAD_DIM
IDX_HEADS = 8
IDX_DIM = 64
TOPK_MAX = 256
N_BUCKETS = 32
MAX_DISTANCE = 128
CONV_WIDTH = 512
CONV_K = 3
N_GROUPS = 4
EXPERTS_PER_GROUP = 8
N_EXPERTS = N_GROUPS * EXPERTS_PER_GROUP
D_EXPERT = 512
EPS = 1e-6

LANES = 128
VMEM_LIMIT = 56 * 1024 * 1024

MASK_NEG = -1e30
SCORE_NEG = float(np.finfo(np.float32).min)


def _t5_bucket_starts():
    max_exact = N_BUCKETS // 2
    n = np.arange(0, MAX_DISTANCE + 1)
    nf = np.maximum(n, 1).astype(np.float32)
    val = (np.log(nf / np.float32(max_exact)) / np.float32(math.log(MAX_DISTANCE / max_exact))
           * np.float32(N_BUCKETS - max_exact)).astype(np.float32)
    inner = val[max_exact + 1:MAX_DISTANCE]
    assert np.min(np.abs(inner - np.round(inner))) > 1e-3
    large = np.minimum(max_exact + val.astype(np.int32), N_BUCKETS - 1)
    bucket = np.where(n < max_exact, n, large)
    assert bucket[MAX_DISTANCE] == N_BUCKETS - 1 and np.all(np.diff(bucket) >= 0)
    return [int(np.argmax(bucket >= b)) for b in range(N_BUCKETS)]


_BUCKET_START = _t5_bucket_starts()


def _ada_kernel(c_ref, w_ref, b_ref, o_ref):
    c = c_ref[...]
    ca = c * jax.nn.sigmoid(c)
    o_ref[...] = jnp.dot(ca, w_ref[...], preferred_element_type=F32,
                         precision=lax.Precision.HIGHEST) + b_ref[...]


def _ada_call(c_pad, w, b):
    rows, d = c_pad.shape
    n = w.shape[1]
    tn = 1536
    return pl.pallas_call(
        _ada_kernel,
        grid=(n // tn,),
        in_specs=[pl.BlockSpec((rows, d), lambda j: (0, 0)),
                  pl.BlockSpec((d, tn), lambda j: (0, j)),
                  pl.BlockSpec((1, tn), lambda j: (0, j))],
        out_specs=pl.BlockSpec((rows, tn), lambda j: (0, j)),
        out_shape=jax.ShapeDtypeStruct((rows, n), F32),
        compiler_params=pltpu.CompilerParams(dimension_semantics=("arbitrary",),
                                             vmem_limit_bytes=VMEM_LIMIT),
        name="ada",
    )(c_pad, w, b)


def _rms_mod(x, g, scale, shift):
    ms = jnp.mean(x * x, axis=-1, keepdims=True)
    y = x * lax.rsqrt(ms + EPS) * g
    return y * (1.0 + scale) + shift


def _proj_kernel(x_ref, mod_ref, g_ref, wa_ref, wb_ref, wc_ref, wd_ref, cw_ref,
                 q_ref, k_ref, v_ref, qi_ref, kid_ref, wi_ref, z_ref, gate_ref, ubuf, *, tm):
    h = _rms_mod(x_ref[...], g_ref[...], mod_ref[1:2, :], mod_ref[0:1, :])
    hb = h.astype(BF16)

    a = jnp.dot(hb, wa_ref[...], preferred_element_type=F32)
    q_ref[...] = (a[:, 0:512] * (HEAD_DIM ** -0.5)).astype(BF16)
    k_ref[...] = a[:, 512:1024].astype(BF16)
    v_ref[...] = a[:, 1024:1536].astype(BF16)
    qi_ref[...] = a[:, 1536:2048].astype(BF16)

    b = jnp.dot(hb, wb_ref[...], preferred_element_type=F32)
    kid_ref[...] = b[:, 0:LANES].astype(BF16)
    wi_ref[...] = b[:, LANES:2 * LANES] * ((IDX_DIM ** -0.5) * (IDX_HEADS ** -0.5))

    cpart = jnp.dot(hb, wc_ref[...], preferred_element_type=F32)
    cb = cpart[:, 0:512]
    u = cpart[:, 512:1024] * cpart[:, 1024:1536]

    @pl.when(pl.program_id(1) == 0)
    def _():
        ubuf[0:8, :] = jnp.zeros((8, CONV_WIDTH), F32)

    ubuf[8:tm + 8, :] = u
    conv = (cw_ref[0:1, :] * ubuf[6:tm + 6, :] + cw_ref[1:2, :] * ubuf[7:tm + 7, :]
            + cw_ref[2:3, :] * u)
    z_ref[...] = (cb * conv).astype(BF16)
    ubuf[0:8, :] = ubuf[tm:tm + 8, :]

    d = jnp.dot(hb, wd_ref[...], preferred_element_type=F32)
    gate_ref[...] = jax.nn.sigmoid(d).astype(BF16)


def _proj_call(x, mod, g1, wa, wb, wc, wd, cw, tm):
    bsz, s, d = x.shape
    nt = s // tm
    tok = lambda w: pl.BlockSpec((None, tm, w), lambda b, t: (b, t, 0))
    full = lambda arr: pl.BlockSpec(arr.shape, lambda b, t: (0,) * arr.ndim)
    out_widths = [(512, BF16), (512, BF16), (512, BF16), (512, BF16), (LANES, BF16), (LANES, F32),
                  (512, BF16), (2 * D_MODEL, BF16)]
    return pl.pallas_call(
        functools.partial(_proj_kernel, tm=tm),
        grid=(bsz, nt),
        in_specs=[tok(d), pl.BlockSpec((None, 6, d), lambda b, t: (b, 0, 0)), full(g1),
                  full(wa), full(wb), full(wc), full(wd), full(cw)],
        out_specs=[tok(w) for w, _ in out_widths],
        out_shape=[jax.ShapeDtypeStruct((bsz, s, w), dt) for w, dt in out_widths],
        scratch_shapes=[pltpu.VMEM((tm + 8, CONV_WIDTH), F32)],
        compiler_params=pltpu.CompilerParams(dimension_semantics=("arbitrary", "arbitrary"),
                                             vmem_limit_bytes=VMEM_LIMIT),
        name="proj",
    )(x, mod, g1, wa, wb, wc, wd, cw)


def _to_key(f):
    b = pltpu.bitcast(f, I32)
    return jnp.where(b < 0, b ^ 0x7FFFFFFF, b)


def _from_key(k):
    return pltpu.bitcast(jnp.where(k < 0, k ^ 0x7FFFFFFF, k), F32)


def _attn_kernel(rb_ref, q_ref, qi_ref, wi_ref, k_ref, v_ref, kid_ref, o_ref,
                 sc_ref, tb_ref, qm_ref, qim_ref, wb_ref, m_ref, l_ref, acc_ref,
                 lok_ref, hik_ref, clo_ref, chi_ref, done_ref, *, tq, topk):
    c = tq
    i = pl.program_id(1)
    nt_dims = (((1,), (1,)), ((), ()))
    lane = lax.broadcasted_iota(I32, (tq, LANES), 1)
    low_half = lane < HEAD_DIM
    row = lax.broadcasted_iota(I32, (tq, c), 0)
    col = lax.broadcasted_iota(I32, (tq, c), 1)

    @pl.when((pl.program_id(0) == 0) & (i == 0))
    def _():
        for kind in range(2):
            rel = row - col + kind * c
            n = jnp.maximum(rel, 0)
            for h in range(N_HEADS):
                t = jnp.full((tq, c), rb_ref[0, h], F32)
                for b in range(1, N_BUCKETS):
                    t = jnp.where(n >= _BUCKET_START[b], rb_ref[b, h], t)
                t = t - rb_ref[N_BUCKETS - 1, h]
                tb_ref[h, kind] = jnp.where(rel >= 0, t, MASK_NEG)

    for h in range(N_HEADS):
        hp, half = divmod(h, 2)
        keep = low_half if half == 0 else jnp.logical_not(low_half)
        qp = q_ref[:, hp * LANES:(hp + 1) * LANES]
        qm_ref[h] = jnp.where(keep, qp, jnp.zeros_like(qp))
        qip = qi_ref[:, hp * LANES:(hp + 1) * LANES]
        qim_ref[h] = jnp.where(keep, qip, jnp.zeros_like(qip))
        wb_ref[h] = jnp.broadcast_to(wi_ref[:, h:h + 1], (tq, LANES))

    def score_chunk(j, diag):
        start = pl.multiple_of(j * c, c)
        kc = kid_ref[pl.ds(start, c), :]
        acc = jnp.zeros((tq, c), F32)
        for h in range(N_HEADS):
            d = lax.dot_general(qim_ref[h], kc, nt_dims, preferred_element_type=F32)
            w = wb_ref[h]
            wfull = jnp.concatenate([w] * (c // LANES), axis=1)
            acc = acc + jnp.maximum(d, 0.0) * wfull
        if diag:
            causal = col <= row
            lo_src = jnp.where(causal, acc, -SCORE_NEG)
            acc = jnp.where(causal, acc, SCORE_NEG)
        else:
            lo_src = acc
        sc_ref[:, pl.ds(start, c)] = acc
        return (jnp.max(acc, axis=1, keepdims=True), jnp.min(lo_src, axis=1, keepdims=True))

    def score_body(j, carry):
        mx, mn = score_chunk(j, False)
        return jnp.maximum(carry[0], mx), jnp.minimum(carry[1], mn)

    rmax, rmin = lax.fori_loop(0, i, score_body,
                               (jnp.full((tq, 1), SCORE_NEG, F32), jnp.full((tq, 1), -SCORE_NEG, F32)))
    mx, mn = score_chunk(i, True)
    rmax = jnp.maximum(rmax, mx)
    rmin = jnp.minimum(rmin, mn)

    nchunks = i + 1
    rowv = lax.broadcasted_iota(I32, (tq, 1), 0)
    n_causal = i * tq + rowv + 1

    def count_ge(thr):
        def body(j, cnt):
            start = pl.multiple_of(j * c, c)
            tile = sc_ref[:, pl.ds(start, c)]
            return cnt + jnp.sum(jnp.where(tile >= thr, 1.0, 0.0), axis=1, keepdims=True)
        return lax.fori_loop(0, nchunks, body, jnp.zeros((tq, 1), F32))

    kf = float(topk)
    lo_k0 = _to_key(rmin)
    hi_k0 = _to_key(rmax) + 1
    lok_ref[...] = lo_k0
    hik_ref[...] = hi_k0
    clo_ref[...] = n_causal.astype(F32)
    chi_ref[...] = jnp.zeros((tq, 1), F32)
    done0 = jnp.where((n_causal <= topk) | (hi_k0 <= lo_k0 + 1), 1, 0)
    done_ref[...] = done0

    def cond(st):
        return st[0] > 0

    def step(st):
        _, it = st
        lo_k, hi_k, c_lo, c_hi = lok_ref[...], hik_ref[...], clo_ref[...], chi_ref[...]
        active = done_ref[...] == 0
        mid_f = 0.5 * _from_key(lo_k) + 0.5 * _from_key(hi_k)
        mid_a = _to_key(mid_f)
        mid_b = (lo_k >> 1) + (hi_k >> 1) + (lo_k & hi_k & 1)
        mid_k = jnp.where(it < 20, mid_a, mid_b)
        mid_k = jnp.minimum(jnp.maximum(mid_k, lo_k + 1), hi_k - 1)
        cnt = count_ge(_from_key(mid_k))
        ge = cnt >= kf
        up = ge & active
        dn = jnp.logical_not(ge) & active
        lo_k = jnp.where(up, mid_k, lo_k)
        c_lo = jnp.where(up, cnt, c_lo)
        hi_k = jnp.where(dn, mid_k, hi_k)
        c_hi = jnp.where(dn, cnt, c_hi)
        done = jnp.where(active & (c_lo != kf) & (hi_k > lo_k + 1), 0, 1)
        lok_ref[...] = lo_k
        hik_ref[...] = hi_k
        clo_ref[...] = c_lo
        chi_ref[...] = c_hi
        done_ref[...] = done
        return jnp.sum(1 - done), it + 1

    lax.while_loop(cond, step, (jnp.sum(1 - done0), jnp.int32(0)))
    lo_k, c_lo, c_hi = lok_ref[...], clo_ref[...], chi_ref[...]
    thr = _from_key(lo_k)

    tied = (n_causal > topk) & (c_lo > kf)
    n_take = kf - c_hi
    big_idx = jnp.int32(2 ** 30)

    def tie_cut():
        def count_tied_le(x):
            def body(j, cnt):
                start = pl.multiple_of(j * c, c)
                tile = sc_ref[:, pl.ds(start, c)]
                hit = (tile == thr) & ((col + start) <= x)
                return cnt + jnp.sum(jnp.where(hit, 1.0, 0.0), axis=1, keepdims=True)
            return lax.fori_loop(0, nchunks, body, jnp.zeros((tq, 1), F32))

        def bis(_, st):
            lo_x, hi_x = st
            mid = (lo_x + hi_x) >> 1
            ok = count_tied_le(mid) >= n_take
            return jnp.where(ok, lo_x, mid), jnp.where(ok, mid, hi_x)

        nbits = int(math.ceil(math.log2(sc_ref.shape[1]))) + 1
        _, hi_x = lax.fori_loop(0, nbits, bis,
                                (jnp.full((tq, 1), -1, I32), jnp.full((tq, 1), sc_ref.shape[1], I32)))
        return jnp.where(tied, hi_x, big_idx)

    cut = lax.cond(jnp.sum(jnp.where(tied, 1, 0)) > 0, tie_cut,
                   lambda: jnp.full((tq, 1), big_idx, I32))

    def mask_body(j, _):
        start = pl.multiple_of(j * c, c)
        tile = sc_ref[:, pl.ds(start, c)]
        sel = (tile > thr) | ((tile == thr) & ((col + start) <= cut))
        sc_ref[:, pl.ds(start, c)] = jnp.where(sel, 0.0, MASK_NEG)
        return 0

    lax.fori_loop(0, nchunks, mask_body, 0)

    m_ref[...] = jnp.full(m_ref.shape, MASK_NEG, F32)
    l_ref[...] = jnp.zeros(l_ref.shape, F32)
    acc_ref[...] = jnp.zeros(acc_ref.shape, F32)

    def attend(j, kind):
        start = pl.multiple_of(j * c, c)
        mb = sc_ref[:, pl.ds(start, c)]
        for hp in range(N_HEADS // 2):
            kp = k_ref[pl.ds(start, c), hp * LANES:(hp + 1) * LANES]
            vp = v_ref[pl.ds(start, c), hp * LANES:(hp + 1) * LANES]
            acc = acc_ref[hp]
            for half in range(2):
                h = 2 * hp + half
                s = lax.dot_general(qm_ref[h], kp, nt_dims, preferred_element_type=F32) + mb
                if kind is not None:
                    s = s + tb_ref[h, kind]
                m_old = m_ref[h]
                m_new = jnp.maximum(m_old, jnp.max(s, axis=1, keepdims=True))
                alpha = jnp.exp(m_old - m_new)
                p = jnp.exp(s - m_new)
                l_ref[h] = alpha * l_ref[h] + jnp.sum(p, axis=1, keepdims=True)
                m_ref[h] = m_new
                pv = jnp.dot(p.astype(BF16), vp, preferred_element_type=F32)
                keep = low_half if half == 0 else jnp.logical_not(low_half)
                acc = jnp.where(keep, alpha * acc + pv, acc)
            acc_ref[hp] = acc

    def far_body(j, _):
        attend(j, None)
        return 0

    lax.fori_loop(0, jnp.maximum(i - 1, 0), far_body, 0)

    @pl.when(i >= 1)
    def _():
        attend(i - 1, 1)

    attend(i, 0)

    for hp in range(N_HEADS // 2):
        inv = jnp.where(low_half, 1.0 / l_ref[2 * hp], 1.0 / l_ref[2 * hp + 1])
        o_ref[:, hp * LANES:(hp + 1) * LANES] = (acc_ref[hp] * inv).astype(BF16)


def _attn_call(rel_bias, q, qi, wi, k, v, kid, tq):
    bsz, s, _ = q.shape
    nq = s // tq
    topk = min(TOPK_MAX, s // 4)
    assert tq == topk and tq % LANES == 0 and tq >= MAX_DISTANCE
    blk = lambda w: pl.BlockSpec((None, tq, w), lambda b, i: (b, i, 0))
    whole = lambda w: pl.BlockSpec((None, s, w), lambda b, i: (b, 0, 0), pipeline_mode=pl.Buffered(1))
    return pl.pallas_call(
        functools.partial(_attn_kernel, tq=tq, topk=topk),
        grid=(bsz, nq),
        in_specs=[pl.BlockSpec(memory_space=pltpu.SMEM),
                  blk(ATTN_WIDTH), blk(ATTN_WIDTH), blk(LANES),
                  whole(ATTN_WIDTH), whole(ATTN_WIDTH), whole(LANES)],
        out_specs=blk(ATTN_WIDTH),
        out_shape=jax.ShapeDtypeStruct((bsz, s, ATTN_WIDTH), BF16),
        scratch_shapes=[
            pltpu.VMEM((tq, s), F32),
            pltpu.VMEM((N_HEADS, 2, tq, tq), F32),
            pltpu.VMEM((N_HEADS, tq, LANES), BF16),
            pltpu.VMEM((N_HEADS, tq, LANES), BF16),
            pltpu.VMEM((N_HEADS, tq, LANES), F32),
            pltpu.VMEM((N_HEADS, tq, 1), F32),
            pltpu.VMEM((N_HEADS, tq, 1), F32),
            pltpu.VMEM((N_HEADS // 2, tq, LANES), F32),
            pltpu.VMEM((tq, 1), I32), pltpu.VMEM((tq, 1), I32),
            pltpu.VMEM((tq, 1), F32), pltpu.VMEM((tq, 1), F32),
            pltpu.VMEM((tq, 1), I32),
        ],
        compiler_params=pltpu.CompilerParams(dimension_semantics=("arbitrary", "arbitrary"),
                                             vmem_limit_bytes=VMEM_LIMIT),
        name="attn",
    )(rel_bias, q, qi, wi, k, v, kid)


def _post_kernel(attn_ref, z_ref, gate_ref, x_ref, mod_ref, wa_ref, wc_ref, wo_ref, g2_ref,
                 wr_ref, br_ref, hres_ref, h2_ref, gd_ref):
    ya = jnp.dot(attn_ref[...], wa_ref[...], preferred_element_type=F32)
    yc = jnp.dot(z_ref[...], wc_ref[...], preferred_element_type=F32)
    merged = gate_ref[:, 0:D_MODEL].astype(F32) * ya + gate_ref[:, D_MODEL:].astype(F32) * yc
    o = jnp.dot(merged.astype(BF16), wo_ref[...], preferred_element_type=F32)
    hres = x_ref[...] + mod_ref[2:3, :] * o
    hres_ref[...] = hres
    h2 = _rms_mod(hres, g2_ref[...], mod_ref[4:5, :], mod_ref[3:4, :])
    h2_ref[...] = h2.astype(BF16)

    lg = jnp.dot(h2, wr_ref[...], preferred_element_type=F32,
                 precision=lax.Precision.HIGHEST) + br_ref[...]
    tm = lg.shape[0]
    lane = lax.broadcasted_iota(I32, (tm, LANES), 1)
    ninf = -jnp.inf
    gmask = (lane >= N_EXPERTS) & (lane < N_EXPERTS + N_GROUPS)
    gl = jnp.where(gmask, lg, ninf)
    gmax = jnp.max(gl, axis=1, keepdims=True)
    g_sel = jnp.min(jnp.where(gmask & (gl == gmax), lane - N_EXPERTS, N_GROUPS), axis=1, keepdims=True)
    g_w = 1.0 / jnp.sum(jnp.where(gmask, jnp.exp(gl - gmax), 0.0), axis=1, keepdims=True)

    emask = (lane < N_EXPERTS) & ((lane // EXPERTS_PER_GROUP) == g_sel)
    el = jnp.where(emask, lg, ninf)
    emax = jnp.max(el, axis=1, keepdims=True)
    ee = jnp.where(emask, jnp.exp(el - emax), 0.0)
    e_prob = ee / jnp.sum(ee, axis=1, keepdims=True)
    p1 = jnp.max(jnp.where(emask, e_prob, -1.0), axis=1, keepdims=True)
    i1 = jnp.min(jnp.where(emask & (e_prob == p1), lane, LANES), axis=1, keepdims=True)
    rest = emask & (lane != i1)
    p2 = jnp.max(jnp.where(rest, e_prob, -1.0), axis=1, keepdims=True)
    i2 = jnp.min(jnp.where(rest & (e_prob == p2), lane, LANES), axis=1, keepdims=True)
    psum = p1 + p2
    w1 = g_w * (p1 / psum)
    w2 = g_w * (p2 / psum)
    gd_ref[...] = jnp.where(lane == i1, w1, 0.0) + jnp.where(lane == i2, w2, 0.0)


def _post_call(attn, z, gates, x, mod, wa, wc, wo, g2, wr, br, tm):
    bsz, s, d = x.shape
    nt = s // tm
    tok = lambda w: pl.BlockSpec((None, tm, w), lambda b, t: (b, t, 0))
    full = lambda arr: pl.BlockSpec(arr.shape, lambda b, t: (0,) * arr.ndim)
    return pl.pallas_call(
        _post_kernel,
        grid=(bsz, nt),
        in_specs=[tok(ATTN_WIDTH), tok(CONV_WIDTH), tok(2 * d), tok(d),
                  pl.BlockSpec((None, 6, d), lambda b, t: (b, 0, 0)),
                  full(wa), full(wc), full(wo), full(g2), full(wr), full(br)],
        out_specs=[tok(d), tok(d), tok(LANES)],
        out_shape=[jax.ShapeDtypeStruct((bsz, s, d), F32), jax.ShapeDtypeStruct((bsz, s, d), BF16),
                   jax.ShapeDtypeStruct((bsz, s, LANES), F32)],
        compiler_params=pltpu.CompilerParams(dimension_semantics=("arbitrary", "arbitrary"),
                                             vmem_limit_bytes=VMEM_LIMIT),
        name="post",
    )(attn, z, gates, x, mod, wa, wc, wo, g2, wr, br)


def _moe_kernel(h2_ref, gd_ref, wg_ref, wu_ref, wd_ref, hres_ref, mod_ref, gf_ref, o_ref, acc_ref):
    e = pl.program_id(2)

    @pl.when(e == 0)
    def _():
        acc_ref[...] = jnp.zeros(acc_ref.shape, F32)

    gd = gd_ref[...]
    lane = lax.broadcasted_iota(I32, gd.shape, 1)
    gcol = jnp.sum(jnp.where(lane == e, gd, 0.0), axis=1, keepdims=True)
    h2 = h2_ref[...]
    a = jnp.dot(h2, wg_ref[...], preferred_element_type=F32)
    b = jnp.dot(h2, wu_ref[...], preferred_element_type=F32)
    hid = (a * jax.nn.sigmoid(a)) * b
    y = jnp.dot(hid.astype(BF16), wd_ref[...], preferred_element_type=F32)
    acc_ref[...] += gcol * y

    @pl.when(e == pl.num_programs(2) - 1)
    def _():
        hfin = hres_ref[...] + mod_ref[5:6, :] * acc_ref[...]
        ms = jnp.mean(hfin * hfin, axis=-1, keepdims=True)
        o_ref[...] = hfin * lax.rsqrt(ms + EPS) * gf_ref[...]


def _moe_call(h2, gd, wg, wu, wd, hres, mod, gf, tm):
    bsz, s, d = hres.shape
    nt = s // tm
    ne = wg.shape[0]
    tok = lambda w: pl.BlockSpec((None, tm, w), lambda b, t, e: (b, t, 0))
    return pl.pallas_call(
        _moe_kernel,
        grid=(bsz, nt, ne),
        in_specs=[tok(d), tok(LANES),
                  pl.BlockSpec((None, d, D_EXPERT), lambda b, t, e: (e, 0, 0)),
                  pl.BlockSpec((None, d, D_EXPERT), lambda b, t, e: (e, 0, 0)),
                  pl.BlockSpec((None, D_EXPERT, d), lambda b, t, e: (e, 0, 0)),
                  tok(d), pl.BlockSpec((None, 6, d), lambda b, t, e: (b, 0, 0)),
                  pl.BlockSpec((1, d), lambda b, t, e: (0, 0))],
        out_specs=tok(d),
        out_shape=jax.ShapeDtypeStruct((bsz, s, d), F32),
        scratch_shapes=[pltpu.VMEM((tm, d), F32)],
        compiler_params=pltpu.CompilerParams(
            dimension_semantics=("arbitrary", "arbitrary", "arbitrary"), vmem_limit_bytes=VMEM_LIMIT),
        name="moe",
    )(h2, gd, wg, wu, wd, hres, mod, gf)


def kernel(x, c, w_ada, b_ada, norm1_g, w_in, rel_bias, conv_w, w_attn_branch, w_conv_branch, w_out,
           norm2_g, w_router_group, b_router_group, w_router_expert, b_router_expert,
           w_gate_e, w_up_e, w_down_e, norm_f_g):
    bsz, s, d = x.shape
    depth = w_ada.shape[0]
    h = x.astype(F32)
    c_pad = jnp.zeros((8, d), F32).at[:bsz].set(c.astype(F32))
    out = h
    for l in range(depth):
        mod = _ada_call(c_pad, w_ada[l].astype(F32), b_ada[l].astype(F32)[None, :])
        mod = mod[:bsz].reshape(bsz, 6, d)

        w = w_in[l]
        o_q, o_k, o_v, o_qi = 0, 512, 1024, 1536
        o_ki, o_wi, o_cb, o_gl = 2048, 2112, 2120, 3656
        wa = w[:, o_q:o_ki].astype(BF16)
        ki_w = w[:, o_ki:o_wi]
        wb = jnp.concatenate([ki_w, ki_w, w[:, o_wi:o_cb],
                              jnp.zeros((d, LANES - IDX_HEADS), w.dtype)], axis=1).astype(BF16)
        wc = w[:, o_cb:o_gl].astype(BF16)
        wd = w[:, o_gl:].astype(BF16)

        q, k, v, qi, kid, wi, z, gates = _proj_call(
            h, mod, norm1_g[l].astype(F32)[None, :], wa, wb, wc, wd, conv_w[l].astype(F32),
            tm=min(512, s))

        attn = _attn_call(rel_bias.astype(F32), q, qi, wi, k, v, kid, tq=min(TOPK_MAX, s // 4))

        wr = jnp.concatenate([w_router_expert[l], w_router_group[l],
                              jnp.zeros((d, LANES - N_EXPERTS - N_GROUPS), F32)], axis=1).astype(F32)
        br = jnp.concatenate([b_router_expert[l], b_router_group[l],
                              jnp.zeros((LANES - N_EXPERTS - N_GROUPS,), F32)])[None, :].astype(F32)
        hres, h2, gd = _post_call(
            attn, z, gates, h, mod, w_attn_branch[l].astype(BF16), w_conv_branch[l].astype(BF16),
            w_out[l].astype(BF16), norm2_g[l].astype(F32)[None, :], wr, br, tm=min(512, s))

        assert depth == 1, "the final RMSNorm is fused into the expert kernel of the only layer"
        gf = norm_f_g.astype(F32)[None, :]
        out = _moe_call(h2, gd, w_gate_e[l].astype(BF16), w_up_e[l].astype(BF16),
                        w_down_e[l].astype(BF16), hres, mod, gf, tm=min(1024, s))
        h = out
    return out.astype(x.dtype)
```

```python
import functools
import math

import numpy as np
import jax
import jax.numpy as jnp
from jax import lax
from jax.experimental import pallas as pl
from jax.experimental.pallas import tpu as pltpu

F32 = jnp.float32
BF16 = jnp.bfloat16
I32 = jnp.int32

D_MODEL = 1024
N_HEADS = 8
HEAD_DIM = 64
ATTN_WIDTH = N_HEADS * HEAD_DIM
IDX_HEADS = 8
IDX_DIM = 64
TOPK_MAX = 256
N_BUCKETS = 32
MAX_DISTANCE = 128
CONV_WIDTH = 512
CONV_K = 3
N_GROUPS = 4
EXPERTS_PER_GROUP = 8
N_EXPERTS = N_GROUPS * EXPERTS_PER_GROUP
D_EXPERT = 512
EPS = 1e-6

LANES = 128
VMEM_LIMIT = 56 * 1024 * 1024

LOG2E = math.log2(math.e)
MASK_NEG = -1e30
SCORE_NEG = float(np.finfo(np.float32).min)


def _t5_bucket_starts():
    max_exact = N_BUCKETS // 2
    n = np.arange(0, MAX_DISTANCE + 1)
    nf = np.maximum(n, 1).astype(np.float32)
    val = (np.log(nf / np.float32(max_exact)) / np.float32(math.log(MAX_DISTANCE / max_exact))
           * np.float32(N_BUCKETS - max_exact)).astype(np.float32)
    inner = val[max_exact + 1:MAX_DISTANCE]
    assert np.min(np.abs(inner - np.round(inner))) > 1e-3
    large = np.minimum(max_exact + val.astype(np.int32), N_BUCKETS - 1)
    bucket = np.where(n < max_exact, n, large)
    assert bucket[MAX_DISTANCE] == N_BUCKETS - 1 and np.all(np.diff(bucket) >= 0)
    return [int(np.argmax(bucket >= b)) for b in range(N_BUCKETS)]


_BUCKET_START = _t5_bucket_starts()


def _ada_kernel(c_ref, w_ref, b_ref, o_ref):
    c = c_ref[...]
    ca = c * jax.nn.sigmoid(c)
    o_ref[...] = jnp.dot(ca, w_ref[...], preferred_element_type=F32,
                         precision=lax.Precision.HIGHEST) + b_ref[...]


def _ada_call(c_pad, w, b):
    rows, d = c_pad.shape
    n = w.shape[1]
    tn = 1536
    return pl.pallas_call(
        _ada_kernel,
        grid=(n // tn,),
        in_specs=[pl.BlockSpec((rows, d), lambda j: (0, 0)),
                  pl.BlockSpec((d, tn), lambda j: (0, j)),
                  pl.BlockSpec((1, tn), lambda j: (0, j))],
        out_specs=pl.BlockSpec((rows, tn), lambda j: (0, j)),
        out_shape=jax.ShapeDtypeStruct((rows, n), F32),
        compiler_params=pltpu.CompilerParams(dimension_semantics=("arbitrary",),
                                             vmem_limit_bytes=VMEM_LIMIT),
        name="ada",
    )(c_pad, w, b)


def _rms_mod(x, g, scale, shift):
    ms = jnp.mean(x * x, axis=-1, keepdims=True)
    y = x * lax.rsqrt(ms + EPS) * g
    return y * (1.0 + scale) + shift


def _proj_kernel(x_ref, mod_ref, g_ref, wa_ref, wb_ref, wc_ref, wd_ref, cw_ref,
                 q_ref, k_ref, v_ref, qi_ref, kid_ref, wi_ref, z_ref, gate_ref, ubuf, *, tm):
    h = _rms_mod(x_ref[...], g_ref[...], mod_ref[1:2, :], mod_ref[0:1, :])
    hb = h.astype(BF16)

    a = jnp.dot(hb, wa_ref[...], preferred_element_type=F32)
    q_ref[...] = (a[:, 0:512] * (HEAD_DIM ** -0.5 * LOG2E)).astype(BF16)
    k_ref[...] = a[:, 512:1024].astype(BF16)
    v_ref[...] = a[:, 1024:1536].astype(BF16)
    qi_ref[...] = a[:, 1536:2048].astype(BF16)

    b = jnp.dot(hb, wb_ref[...], preferred_element_type=F32)
    kid_ref[...] = b[:, 0:LANES].astype(BF16)
    wi_ref[...] = b[:, LANES:2 * LANES] * ((IDX_DIM ** -0.5) * (IDX_HEADS ** -0.5))

    cpart = jnp.dot(hb, wc_ref[...], preferred_element_type=F32)
    cb = cpart[:, 0:512]
    u = cpart[:, 512:1024] * cpart[:, 1024:1536]

    @pl.when(pl.program_id(1) == 0)
    def _():
        ubuf[0:8, :] = jnp.zeros((8, CONV_WIDTH), F32)

    ubuf[8:tm + 8, :] = u
    conv = (cw_ref[0:1, :] * ubuf[6:tm + 6, :] + cw_ref[1:2, :] * ubuf[7:tm + 7, :]
            + cw_ref[2:3, :] * u)
    z_ref[...] = (cb * conv).astype(BF16)
    ubuf[0:8, :] = ubuf[tm:tm + 8, :]

    d = jnp.dot(hb, wd_ref[...], preferred_element_type=F32)
    gate_ref[...] = jax.nn.sigmoid(d).astype(BF16)


def _proj_call(x, mod, g1, wa, wb, wc, wd, cw, tm):
    bsz, s, d = x.shape
    nt = s // tm
    tok = lambda w: pl.BlockSpec((None, tm, w), lambda b, t: (b, t, 0))
    full = lambda arr: pl.BlockSpec(arr.shape, lambda b, t: (0,) * arr.ndim)
    out_widths = [(512, BF16), (512, BF16), (512, BF16), (512, BF16), (LANES, BF16), (LANES, F32),
                  (512, BF16), (2 * D_MODEL, BF16)]
    return pl.pallas_call(
        functools.partial(_proj_kernel, tm=tm),
        grid=(bsz, nt),
        in_specs=[tok(d), pl.BlockSpec((None, 6, d), lambda b, t: (b, 0, 0)), full(g1),
                  full(wa), full(wb), full(wc), full(wd), full(cw)],
        out_specs=[tok(w) for w, _ in out_widths],
        out_shape=[jax.ShapeDtypeStruct((bsz, s, w), dt) for w, dt in out_widths],
        scratch_shapes=[pltpu.VMEM((tm + 8, CONV_WIDTH), F32)],
        compiler_params=pltpu.CompilerParams(dimension_semantics=("arbitrary", "arbitrary"),
                                             vmem_limit_bytes=VMEM_LIMIT),
        name="proj",
    )(x, mod, g1, wa, wb, wc, wd, cw)


def _to_key(f):
    b = pltpu.bitcast(f, I32)
    return jnp.where(b < 0, b ^ 0x7FFFFFFF, b)


def _from_key(k):
    return pltpu.bitcast(jnp.where(k < 0, k ^ 0x7FFFFFFF, k), F32)


def _fold_lanes(x, op):
    parts = [x[:, t * LANES:(t + 1) * LANES] for t in range(x.shape[1] // LANES)]
    while len(parts) > 1:
        nxt = [op(parts[a], parts[a + 1]) for a in range(0, len(parts) - 1, 2)]
        parts = nxt + ([parts[-1]] if len(parts) % 2 else [])
    return parts[0]


def _tile_lanes(x, n):
    return x if n == 1 else jnp.concatenate([x] * n, axis=1)


def _attn_kernel(rb_ref, q_ref, qi_ref, wi_ref, k_ref, v_ref, kid_ref, o_ref,
                 sct_ref, mb_ref, tb_ref, qm_ref, qim_ref, m_ref, l_ref, acc_ref, *, tq, topk):
    c = tq
    i = pl.program_id(1)
    nt_dims = (((1,), (1,)), ((), ()))
    lane = lax.broadcasted_iota(I32, (tq, LANES), 1)
    low_half = lane < HEAD_DIM
    row = lax.broadcasted_iota(I32, (tq, c), 0)
    col = lax.broadcasted_iota(I32, (tq, c), 1)

    @pl.when((pl.program_id(0) == 0) & (i == 0))
    def _():
        for kind in range(2):
            rel = row - col + kind * c
            n = jnp.maximum(rel, 0)
            for h in range(N_HEADS):
                t = jnp.full((tq, c), rb_ref[0, h], F32)
                for b in range(1, N_BUCKETS):
                    t = jnp.where(n >= _BUCKET_START[b], rb_ref[b, h], t)
                t = (t - rb_ref[N_BUCKETS - 1, h]) * LOG2E
                tb_ref[h, kind] = jnp.where(rel >= 0, t, MASK_NEG)

    for h in range(N_HEADS):
        hp, half = divmod(h, 2)
        keep = low_half if half == 0 else jnp.logical_not(low_half)
        qp = q_ref[:, hp * LANES:(hp + 1) * LANES]
        qm_ref[h] = jnp.where(keep, qp, jnp.zeros_like(qp))
        qip = qi_ref[:, hp * LANES:(hp + 1) * LANES]
        qim_ref[h] = jnp.where(keep, qip, jnp.zeros_like(qip))

    w_t = jnp.transpose(wi_ref[...])
    krow = lax.broadcasted_iota(I32, (c, tq), 0)
    qcol = lax.broadcasted_iota(I32, (c, tq), 1)

    def fold_rows(x, op):
        parts = [x[r * 8:(r + 1) * 8, :] for r in range(x.shape[0] // 8)]
        while len(parts) > 1:
            nxt = [op(parts[a], parts[a + 1]) for a in range(0, len(parts) - 1, 2)]
            parts = nxt + ([parts[-1]] if len(parts) % 2 else [])
        return parts[0]

    def score_chunk(j, diag):
        start = pl.multiple_of(j * c, c)
        kc = kid_ref[pl.ds(start, c), :]
        acc = jnp.zeros((c, tq), F32)
        for h in range(N_HEADS):
            d = lax.dot_general(kc, qim_ref[h], nt_dims, preferred_element_type=F32)
            acc = acc + jnp.maximum(d, 0.0) * w_t[h:h + 1, :]
        if diag:
            causal = krow <= qcol
            lo_src = jnp.where(causal, acc, -SCORE_NEG)
            acc = jnp.where(causal, acc, SCORE_NEG)
        else:
            lo_src = acc
        sct_ref[pl.ds(start, c), :] = acc
        return fold_rows(acc, jnp.maximum), fold_rows(lo_src, jnp.minimum)

    def score_body(j, carry):
        mx, mn = score_chunk(j, False)
        return jnp.maximum(carry[0], mx), jnp.minimum(carry[1], mn)

    pmax, pmin = lax.fori_loop(0, i, score_body, (jnp.full((8, tq), SCORE_NEG, F32),
                                                  jnp.full((8, tq), -SCORE_NEG, F32)))
    mx, mn = score_chunk(i, True)
    rmax = jnp.max(jnp.maximum(pmax, mx), axis=0, keepdims=True)
    rmin = jnp.min(jnp.minimum(pmin, mn), axis=0, keepdims=True)

    nchunks = i + 1
    n_causal = i * tq + lax.broadcasted_iota(I32, (1, tq), 1) + 1

    def count_keys(pred):
        def body(j, part):
            start = pl.multiple_of(j * c, c)
            hit = pred(sct_ref[pl.ds(start, c), :], start)
            return part + fold_rows(jnp.where(hit, 1.0, 0.0), jnp.add)
        part = lax.fori_loop(0, nchunks, body, jnp.zeros((8, tq), F32))
        return jnp.sum(part, axis=0, keepdims=True)

    kf = float(topk)
    done0 = jnp.where(n_causal <= topk, 1, 0)

    def cond(st):
        return st[0] > 0

    def step(st):
        _, lo, hi, c_lo, c_hi, done, fb = st
        active = done == 0
        span = jnp.maximum(c_lo - c_hi, 1.0)
        t_int = lo + (hi - lo) * ((c_lo - (kf - 0.5)) / span)
        t_bis = 0.5 * lo + 0.5 * hi
        use_int = (fb == 0) & (t_int > lo) & (t_int < hi)
        mid = jnp.where(use_int, t_int, t_bis)
        go = active & (mid > lo) & (mid < hi)
        cnt = count_keys(lambda tile, start: tile >= mid)
        ge = cnt >= kf
        up = ge & go
        dn = jnp.logical_not(ge) & go
        lo = jnp.where(up, mid, lo)
        hi = jnp.where(dn, mid, hi)
        c_lo = jnp.where(up, cnt, c_lo)
        c_hi = jnp.where(dn, cnt, c_hi)
        fb = jnp.where(use_int & ((c_lo - c_hi) > 0.5 * span), 1, 0)
        done = jnp.where(go & (c_lo != kf), 0, 1)
        return jnp.sum(1 - done), lo, hi, c_lo, c_hi, done, fb

    _, thr, _, c_lo, c_hi, _, _ = lax.while_loop(
        cond, step, (jnp.sum(1 - done0), rmin, _from_key(_to_key(rmax) + 1), n_causal.astype(F32),
                     jnp.zeros((1, tq), F32), done0, jnp.zeros((1, tq), I32)))

    tied = (n_causal > topk) & (c_lo > kf)
    any_tied = jnp.sum(jnp.where(tied, 1, 0)) > 0

    @pl.when(any_tied)
    def _():
        n_take = kf - c_hi

        def bis(_, st):
            lo_x, hi_x = st
            mid = (lo_x + hi_x) >> 1
            ok = count_keys(lambda tile, start: (tile == thr) & ((krow + start) <= mid)) >= n_take
            return jnp.where(ok, lo_x, mid), jnp.where(ok, mid, hi_x)

        nbits = int(math.ceil(math.log2(sct_ref.shape[0]))) + 1
        _, hi_x = lax.fori_loop(0, nbits, bis, (jnp.full((1, tq), -1, I32),
                                                jnp.full((1, tq), sct_ref.shape[0], I32)))
        cut = jnp.where(tied, hi_x, jnp.int32(2 ** 30))

        def mask_body(j, _):
            start = pl.multiple_of(j * c, c)
            tile = sct_ref[pl.ds(start, c), :]
            sel = (tile > thr) | ((tile == thr) & ((krow + start) <= cut))
            mb_ref[:, pl.ds(start, c)] = jnp.transpose(jnp.where(sel, 0.0, MASK_NEG))
            return 0

        lax.fori_loop(0, nchunks, mask_body, 0)

    @pl.when(jnp.logical_not(any_tied))
    def _():
        def mask_body(j, _):
            start = pl.multiple_of(j * c, c)
            tile = sct_ref[pl.ds(start, c), :]
            mb_ref[:, pl.ds(start, c)] = jnp.transpose(jnp.where(tile >= thr, 0.0, MASK_NEG))
            return 0

        lax.fori_loop(0, nchunks, mask_body, 0)

    m_ref[...] = jnp.full(m_ref.shape, MASK_NEG, F32)
    l_ref[...] = jnp.zeros(l_ref.shape, F32)
    acc_ref[...] = jnp.zeros(acc_ref.shape, F32)

    def attend(start, width, kind):
        nt = width // LANES
        ones = jnp.ones((width, LANES), BF16)
        for h in range(N_HEADS):
            hp = h // 2
            kp = k_ref[pl.ds(start, width), hp * LANES:(hp + 1) * LANES]
            vp = v_ref[pl.ds(start, width), hp * LANES:(hp + 1) * LANES]
            s = lax.dot_general(qm_ref[h], kp, nt_dims, preferred_element_type=F32)
            s = s + mb_ref[:, pl.ds(start, width)]
            if kind is not None:
                s = s + tb_ref[h, kind]
            m_old = m_ref[h]
            m_new = jnp.maximum(m_old, jnp.max(_fold_lanes(s, jnp.maximum), axis=1, keepdims=True))
            p = jnp.exp2(s - _tile_lanes(m_new, nt)).astype(BF16)
            pv = jnp.dot(p, jnp.concatenate([vp, ones], axis=1), preferred_element_type=F32)
            alpha = jnp.exp2(m_old - m_new)
            m_ref[h] = m_new
            l_ref[h] = alpha * l_ref[h] + pv[:, LANES:]
            acc_ref[h] = alpha * acc_ref[h] + pv[:, :LANES]

    wide = 2 * c
    n_far = jnp.maximum(i - 1, 0)

    def far_body(j, _):
        attend(pl.multiple_of(j * wide, wide), wide, None)
        return 0

    lax.fori_loop(0, n_far // 2, far_body, 0)

    @pl.when(n_far % 2 == 1)
    def _():
        attend(pl.multiple_of((n_far - 1) * c, c), c, None)

    @pl.when(i >= 1)
    def _():
        attend(pl.multiple_of((i - 1) * c, c), c, 1)

    attend(pl.multiple_of(i * c, c), c, 0)

    for hp in range(N_HEADS // 2):
        lo = acc_ref[2 * hp] / l_ref[2 * hp]
        hi = acc_ref[2 * hp + 1] / l_ref[2 * hp + 1]
        o_ref[:, hp * LANES:(hp + 1) * LANES] = jnp.where(low_half, lo, hi).astype(BF16)


def _attn_call(rel_bias, q, qi, wi, k, v, kid, tq):
    bsz, s, _ = q.shape
    nq = s // tq
    topk = min(TOPK_MAX, s // 4)
    assert tq % LANES == 0 and tq >= MAX_DISTANCE
    blk = lambda w: pl.BlockSpec((None, tq, w), lambda b, i: (b, i, 0))
    whole = lambda w: pl.BlockSpec((None, s, w), lambda b, i: (b, 0, 0), pipeline_mode=pl.Buffered(1))
    return pl.pallas_call(
        functools.partial(_attn_kernel, tq=tq, topk=topk),
        grid=(bsz, nq),
        in_specs=[pl.BlockSpec(memory_space=pltpu.SMEM),
                  blk(ATTN_WIDTH), blk(ATTN_WIDTH), blk(LANES),
                  whole(ATTN_WIDTH), whole(ATTN_WIDTH), whole(LANES)],
        out_specs=blk(ATTN_WIDTH),
        out_shape=jax.ShapeDtypeStruct((bsz, s, ATTN_WIDTH), BF16),
        scratch_shapes=[
            pltpu.VMEM((s, tq), F32),
            pltpu.VMEM((tq, s), F32),
            pltpu.VMEM((N_HEADS, 2, tq, tq), F32),
            pltpu.VMEM((N_HEADS, tq, LANES), BF16),
            pltpu.VMEM((N_HEADS, tq, LANES), BF16),
            pltpu.VMEM((N_HEADS, tq, LANES), F32),
            pltpu.VMEM((N_HEADS, tq, LANES), F32),
            pltpu.VMEM((N_HEADS, tq, LANES), F32),
        ],
        compiler_params=pltpu.CompilerParams(dimension_semantics=("arbitrary", "arbitrary"),
                                             vmem_limit_bytes=VMEM_LIMIT),
        name="attn",
    )(rel_bias, q, qi, wi, k, v, kid)


def _post_kernel(attn_ref, z_ref, gate_ref, x_ref, mod_ref, wa_ref, wc_ref, wo_ref, g2_ref,
                 wr_ref, br_ref, hres_ref, h2_ref, gd_ref):
    ya = jnp.dot(attn_ref[...], wa_ref[...], preferred_element_type=F32)
    yc = jnp.dot(z_ref[...], wc_ref[...], preferred_element_type=F32)
    merged = gate_ref[:, 0:D_MODEL].astype(F32) * ya + gate_ref[:, D_MODEL:].astype(F32) * yc
    o = jnp.dot(merged.astype(BF16), wo_ref[...], preferred_element_type=F32)
    hres = x_ref[...] + mod_ref[2:3, :] * o
    hres_ref[...] = hres
    h2 = _rms_mod(hres, g2_ref[...], mod_ref[4:5, :], mod_ref[3:4, :])
    h2_ref[...] = h2.astype(BF16)

    lg = jnp.dot(h2, wr_ref[...], preferred_element_type=F32,
                 precision=lax.Precision.HIGHEST) + br_ref[...]
    tm = lg.shape[0]
    lane = lax.broadcasted_iota(I32, (tm, LANES), 1)
    ninf = -jnp.inf
    gmask = (lane >= N_EXPERTS) & (lane < N_EXPERTS + N_GROUPS)
    gl = jnp.where(gmask, lg, ninf)
    gmax = jnp.max(gl, axis=1, keepdims=True)
    g_sel = jnp.min(jnp.where(gmask & (gl == gmax), lane - N_EXPERTS, N_GROUPS), axis=1, keepdims=True)
    g_w = 1.0 / jnp.sum(jnp.where(gmask, jnp.exp(gl - gmax), 0.0), axis=1, keepdims=True)

    emask = (lane < N_EXPERTS) & ((lane // EXPERTS_PER_GROUP) == g_sel)
    el = jnp.where(emask, lg, ninf)
    emax = jnp.max(el, axis=1, keepdims=True)
    ee = jnp.where(emask, jnp.exp(el - emax), 0.0)
    e_prob = ee / jnp.sum(ee, axis=1, keepdims=True)
    p1 = jnp.max(jnp.where(emask, e_prob, -1.0), axis=1, keepdims=True)
    i1 = jnp.min(jnp.where(emask & (e_prob == p1), lane, LANES), axis=1, keepdims=True)
    rest = emask & (lane != i1)
    p2 = jnp.max(jnp.where(rest, e_prob, -1.0), axis=1, keepdims=True)
    i2 = jnp.min(jnp.where(rest & (e_prob == p2), lane, LANES), axis=1, keepdims=True)
    psum = p1 + p2
    w1 = g_w * (p1 / psum)
    w2 = g_w * (p2 / psum)
    gd_ref[...] = jnp.where(lane == i1, w1, 0.0) + jnp.where(lane == i2, w2, 0.0)


def _post_call(attn, z, gates, x, mod, wa, wc, wo, g2, wr, br, tm):
    bsz, s, d = x.shape
    nt = s // tm
    tok = lambda w: pl.BlockSpec((None, tm, w), lambda b, t: (b, t, 0))
    full = lambda arr: pl.BlockSpec(arr.shape, lambda b, t: (0,) * arr.ndim)
    return pl.pallas_call(
        _post_kernel,
        grid=(bsz, nt),
        in_specs=[tok(ATTN_WIDTH), tok(CONV_WIDTH), tok(2 * d), tok(d),
                  pl.BlockSpec((None, 6, d), lambda b, t: (b, 0, 0)),
                  full(wa), full(wc), full(wo), full(g2), full(wr), full(br)],
        out_specs=[tok(d), tok(d), tok(LANES)],
        out_shape=[jax.ShapeDtypeStruct((bsz, s, d), F32), jax.ShapeDtypeStruct((bsz, s, d), BF16),
                   jax.ShapeDtypeStruct((bsz, s, LANES), F32)],
        compiler_params=pltpu.CompilerParams(dimension_semantics=("arbitrary", "arbitrary"),
                                             vmem_limit_bytes=VMEM_LIMIT),
        name="post",
    )(attn, z, gates, x, mod, wa, wc, wo, g2, wr, br)


def _moe_kernel(h2_ref, gd_ref, wg_ref, wu_ref, wd_ref, hres_ref, mod_ref, gf_ref, o_ref, acc_ref):
    e = pl.program_id(2)

    @pl.when(e == 0)
    def _():
        acc_ref[...] = jnp.zeros(acc_ref.shape, F32)

    gd = gd_ref[...]
    lane = lax.broadcasted_iota(I32, gd.shape, 1)
    gcol = jnp.sum(jnp.where(lane == e, gd, 0.0), axis=1, keepdims=True)
    h2 = h2_ref[...]
    a = jnp.dot(h2, wg_ref[...], preferred_element_type=F32)
    b = jnp.dot(h2, wu_ref[...], preferred_element_type=F32)
    hid = (a * jax.nn.sigmoid(a)) * b
    y = jnp.dot(hid.astype(BF16), wd_ref[...], preferred_element_type=F32)
    acc_ref[...] += gcol * y

    @pl.when(e == pl.num_programs(2) - 1)
    def _():
        hfin = hres_ref[...] + mod_ref[5:6, :] * acc_ref[...]
        ms = jnp.mean(hfin * hfin, axis=-1, keepdims=True)
        o_ref[...] = hfin * lax.rsqrt(ms + EPS) * gf_ref[...]


def _moe_call(h2, gd, wg, wu, wd, hres, mod, gf, tm):
    bsz, s, d = hres.shape
    nt = s // tm
    ne = wg.shape[0]
    tok = lambda w: pl.BlockSpec((None, tm, w), lambda b, t, e: (b, t, 0))
    return pl.pallas_call(
        _moe_kernel,
        grid=(bsz, nt, ne),
        in_specs=[tok(d), tok(LANES),
                  pl.BlockSpec((None, d, D_EXPERT), lambda b, t, e: (e, 0, 0)),
                  pl.BlockSpec((None, d, D_EXPERT), lambda b, t, e: (e, 0, 0)),
                  pl.BlockSpec((None, D_EXPERT, d), lambda b, t, e: (e, 0, 0)),
                  tok(d), pl.BlockSpec((None, 6, d), lambda b, t, e: (b, 0, 0)),
                  pl.BlockSpec((1, d), lambda b, t, e: (0, 0))],
        out_specs=tok(d),
        out_shape=jax.ShapeDtypeStruct((bsz, s, d), F32),
        scratch_shapes=[pltpu.VMEM((tm, d), F32)],
        compiler_params=pltpu.CompilerParams(
            dimension_semantics=("arbitrary", "arbitrary", "arbitrary"), vmem_limit_bytes=VMEM_LIMIT),
        name="moe",
    )(h2, gd, wg, wu, wd, hres, mod, gf)


def kernel(x, c, w_ada, b_ada, norm1_g, w_in, rel_bias, conv_w, w_attn_branch, w_conv_branch, w_out,
           norm2_g, w_router_group, b_router_group, w_router_expert, b_router_expert,
           w_gate_e, w_up_e, w_down_e, norm_f_g):
    bsz, s, d = x.shape
    depth = w_ada.shape[0]
    h = x.astype(F32)
    c_pad = jnp.zeros((8, d), F32).at[:bsz].set(c.astype(F32))
    out = h
    for l in range(depth):
        mod = _ada_call(c_pad, w_ada[l].astype(F32), b_ada[l].astype(F32)[None, :])
        mod = mod[:bsz].reshape(bsz, 6, d)

        w = w_in[l]
        o_q, o_k, o_v, o_qi = 0, 512, 1024, 1536
        o_ki, o_wi, o_cb, o_gl = 2048, 2112, 2120, 3656
        wa = w[:, o_q:o_ki].astype(BF16)
        ki_w = w[:, o_ki:o_wi]
        wb = jnp.concatenate([ki_w, ki_w, w[:, o_wi:o_cb],
                              jnp.zeros((d, LANES - IDX_HEADS), w.dtype)], axis=1).astype(BF16)
        wc = w[:, o_cb:o_gl].astype(BF16)
        wd = w[:, o_gl:].astype(BF16)

        q, k, v, qi, kid, wi, z, gates = _proj_call(
            h, mod, norm1_g[l].astype(F32)[None, :], wa, wb, wc, wd, conv_w[l].astype(F32),
            tm=min(512, s))

        attn = _attn_call(rel_bias.astype(F32), q, qi, wi, k, v, kid, tq=min(TOPK_MAX, s // 4))

        wr = jnp.concatenate([w_router_expert[l], w_router_group[l],
                              jnp.zeros((d, LANES - N_EXPERTS - N_GROUPS), F32)], axis=1).astype(F32)
        br = jnp.concatenate([b_router_expert[l], b_router_group[l],
                              jnp.zeros((LANES - N_EXPERTS - N_GROUPS,), F32)])[None, :].astype(F32)
        hres, h2, gd = _post_call(
            attn, z, gates, h, mod, w_attn_branch[l].astype(BF16), w_conv_branch[l].astype(BF16),
            w_out[l].astype(BF16), norm2_g[l].astype(F32)[None, :], wr, br, tm=min(512, s))

        assert depth == 1, "the final RMSNorm is fused into the expert kernel of the only layer"
        gf = norm_f_g.astype(F32)[None, :]
        out = _moe_call(h2, gd, w_gate_e[l].astype(BF16), w_up_e[l].astype(BF16),
                        w_down_e[l].astype(BF16), hres, mod, gf, tm=min(1024, s))
        h = out
    return out.astype(x.dtype)
```

```python
import functools
import math

import numpy as np
import jax
import jax.numpy as jnp
from jax import lax
from jax.experimental import pallas as pl
from jax.experimental.pallas import tpu as pltpu

F32 = jnp.float32
BF16 = jnp.bfloat16
I32 = jnp.int32

D_MODEL = 1024
N_HEADS = 8
HEAD_DIM = 64
ATTN_WIDTH = N_HEADS * HEAD_DIM
IDX_HEADS = 8
IDX_DIM = 64
TOPK_MAX = 256
N_BUCKETS = 32
MAX_DISTANCE = 128
CONV_WIDTH = 512
CONV_K = 3
N_GROUPS = 4
EXPERTS_PER_GROUP = 8
N_EXPERTS = N_GROUPS * EXPERTS_PER_GROUP
D_EXPERT = 512
EPS = 1e-6

LANES = 128
VMEM_LIMIT = 56 * 1024 * 1024

LOG2E = math.log2(math.e)
MASK_NEG = -1e30
SCORE_NEG = float(np.finfo(np.float32).min)
TINY_F32 = float(np.finfo(np.float32).tiny)
KEY_SPACE_AFTER = 28


def _t5_bucket_starts():
    max_exact = N_BUCKETS // 2
    n = np.arange(0, MAX_DISTANCE + 1)
    nf = np.maximum(n, 1).astype(np.float32)
    val = (np.log(nf / np.float32(max_exact)) / np.float32(math.log(MAX_DISTANCE / max_exact))
           * np.float32(N_BUCKETS - max_exact)).astype(np.float32)
    inner = val[max_exact + 1:MAX_DISTANCE]
    assert np.min(np.abs(inner - np.round(inner))) > 1e-3
    large = np.minimum(max_exact + val.astype(np.int32), N_BUCKETS - 1)
    bucket = np.where(n < max_exact, n, large)
    assert bucket[MAX_DISTANCE] == N_BUCKETS - 1 and np.all(np.diff(bucket) >= 0)
    return [int(np.argmax(bucket >= b)) for b in range(N_BUCKETS)]


_BUCKET_START = _t5_bucket_starts()


def _ada_kernel(c_ref, w_ref, b_ref, o_ref):
    c = c_ref[...]
    ca = c * jax.nn.sigmoid(c)
    o_ref[...] = jnp.dot(ca, w_ref[...], preferred_element_type=F32,
                         precision=lax.Precision.HIGHEST) + b_ref[...]


def _ada_call(c_pad, w, b):
    rows, d = c_pad.shape
    n = w.shape[1]
    tn = 1536
    return pl.pallas_call(
        _ada_kernel,
        grid=(n // tn,),
        in_specs=[pl.BlockSpec((rows, d), lambda j: (0, 0)),
                  pl.BlockSpec((d, tn), lambda j: (0, j)),
                  pl.BlockSpec((1, tn), lambda j: (0, j))],
        out_specs=pl.BlockSpec((rows, tn), lambda j: (0, j)),
        out_shape=jax.ShapeDtypeStruct((rows, n), F32),
        compiler_params=pltpu.CompilerParams(dimension_semantics=("arbitrary",),
                                             vmem_limit_bytes=VMEM_LIMIT),
        name="ada",
    )(c_pad, w, b)


def _rms_mod(x, g, scale, shift):
    ms = jnp.mean(x * x, axis=-1, keepdims=True)
    y = x * lax.rsqrt(ms + EPS) * g
    return y * (1.0 + scale) + shift


def _proj_kernel(x_ref, mod_ref, g_ref, wa_ref, wb_ref, wc_ref, wd_ref, cw_ref,
                 q_ref, k_ref, v_ref, qi_ref, kid_ref, wi_ref, z_ref, gate_ref, ubuf, *, tm):
    h = _rms_mod(x_ref[...], g_ref[...], mod_ref[1:2, :], mod_ref[0:1, :])
    hb = h.astype(BF16)

    a = jnp.dot(hb, wa_ref[...], preferred_element_type=F32)
    q_ref[...] = (a[:, 0:512] * (HEAD_DIM ** -0.5 * LOG2E)).astype(BF16)
    k_ref[...] = a[:, 512:1024].astype(BF16)
    v_ref[...] = a[:, 1024:1536].astype(BF16)
    qi_ref[...] = a[:, 1536:2048].astype(BF16)

    b = jnp.dot(hb, wb_ref[...], preferred_element_type=F32)
    kid_ref[...] = b[:, 0:LANES].astype(BF16)
    wi_ref[...] = b[:, LANES:2 * LANES] * ((IDX_DIM ** -0.5) * (IDX_HEADS ** -0.5))

    cpart = jnp.dot(hb, wc_ref[...], preferred_element_type=F32)
    cb = cpart[:, 0:512]
    u = cpart[:, 512:1024] * cpart[:, 1024:1536]

    @pl.when(pl.program_id(1) == 0)
    def _():
        ubuf[0:8, :] = jnp.zeros((8, CONV_WIDTH), F32)

    ubuf[8:tm + 8, :] = u
    conv = (cw_ref[0:1, :] * ubuf[6:tm + 6, :] + cw_ref[1:2, :] * ubuf[7:tm + 7, :]
            + cw_ref[2:3, :] * u)
    z_ref[...] = (cb * conv).astype(BF16)
    ubuf[0:8, :] = ubuf[tm:tm + 8, :]

    d = jnp.dot(hb, wd_ref[...], preferred_element_type=F32)
    gate_ref[...] = jax.nn.sigmoid(d).astype(BF16)


def _proj_call(x, mod, g1, wa, wb, wc, wd, cw, tm):
    bsz, s, d = x.shape
    nt = s // tm
    tok = lambda w: pl.BlockSpec((None, tm, w), lambda b, t: (b, t, 0))
    full = lambda arr: pl.BlockSpec(arr.shape, lambda b, t: (0,) * arr.ndim)
    out_widths = [(512, BF16), (512, BF16), (512, BF16), (512, BF16), (LANES, BF16), (LANES, F32),
                  (512, BF16), (2 * D_MODEL, BF16)]
    return pl.pallas_call(
        functools.partial(_proj_kernel, tm=tm),
        grid=(bsz, nt),
        in_specs=[tok(d), pl.BlockSpec((None, 6, d), lambda b, t: (b, 0, 0)), full(g1),
                  full(wa), full(wb), full(wc), full(wd), full(cw)],
        out_specs=[tok(w) for w, _ in out_widths],
        out_shape=[jax.ShapeDtypeStruct((bsz, s, w), dt) for w, dt in out_widths],
        scratch_shapes=[pltpu.VMEM((tm + 8, CONV_WIDTH), F32)],
        compiler_params=pltpu.CompilerParams(dimension_semantics=("arbitrary", "arbitrary"),
                                             vmem_limit_bytes=VMEM_LIMIT),
        name="proj",
    )(x, mod, g1, wa, wb, wc, wd, cw)


def _to_key(f):
    b = pltpu.bitcast(f, I32)
    return jnp.where(b < 0, b ^ 0x7FFFFFFF, b)


def _from_key(k):
    return pltpu.bitcast(jnp.where(k < 0, k ^ 0x7FFFFFFF, k), F32)


def _fold_lanes(x, op):
    parts = [x[:, t * LANES:(t + 1) * LANES] for t in range(x.shape[1] // LANES)]
    while len(parts) > 1:
        nxt = [op(parts[a], parts[a + 1]) for a in range(0, len(parts) - 1, 2)]
        parts = nxt + ([parts[-1]] if len(parts) % 2 else [])
    return parts[0]


def _tile_lanes(x, n):
    return x if n == 1 else jnp.concatenate([x] * n, axis=1)


def _attn_kernel(rb_ref, q_ref, qi_ref, wi_ref, k_ref, v_ref, kid_ref, o_ref,
                 sct_ref, mb_ref, tb_ref, qm_ref, qim_ref, m_ref, l_ref, acc_ref, *, tq, topk):
    c = tq
    i = pl.program_id(1)
    nt_dims = (((1,), (1,)), ((), ()))
    lane = lax.broadcasted_iota(I32, (tq, LANES), 1)
    low_half = lane < HEAD_DIM
    row = lax.broadcasted_iota(I32, (tq, c), 0)
    col = lax.broadcasted_iota(I32, (tq, c), 1)

    @pl.when((pl.program_id(0) == 0) & (i == 0))
    def _():
        for kind in range(2):
            rel = row - col + kind * c
            n = jnp.maximum(rel, 0)
            for h in range(N_HEADS):
                t = jnp.full((tq, c), rb_ref[0, h], F32)
                for b in range(1, N_BUCKETS):
                    t = jnp.where(n >= _BUCKET_START[b], rb_ref[b, h], t)
                t = (t - rb_ref[N_BUCKETS - 1, h]) * LOG2E
                tb_ref[h, kind] = jnp.where(rel >= 0, t, MASK_NEG)

    for h in range(N_HEADS):
        hp, half = divmod(h, 2)
        keep = low_half if half == 0 else jnp.logical_not(low_half)
        qp = q_ref[:, hp * LANES:(hp + 1) * LANES]
        qm_ref[h] = jnp.where(keep, qp, jnp.zeros_like(qp))
        qip = qi_ref[:, hp * LANES:(hp + 1) * LANES]
        qim_ref[h] = jnp.where(keep, qip, jnp.zeros_like(qip))

    w_t = jnp.transpose(wi_ref[...])
    krow = lax.broadcasted_iota(I32, (c, tq), 0)
    qcol = lax.broadcasted_iota(I32, (c, tq), 1)

    def fold_rows(x, op):
        parts = [x[r * 8:(r + 1) * 8, :] for r in range(x.shape[0] // 8)]
        while len(parts) > 1:
            nxt = [op(parts[a], parts[a + 1]) for a in range(0, len(parts) - 1, 2)]
            parts = nxt + ([parts[-1]] if len(parts) % 2 else [])
        return parts[0]

    def score_chunk(j, diag):
        start = pl.multiple_of(j * c, c)
        kc = kid_ref[pl.ds(start, c), :]
        acc = jnp.zeros((c, tq), F32)
        for h in range(N_HEADS):
            d = lax.dot_general(kc, qim_ref[h], nt_dims, preferred_element_type=F32)
            acc = acc + jnp.maximum(d, 0.0) * w_t[h:h + 1, :]
        if diag:
            causal = krow <= qcol
            lo_src = jnp.where(causal, acc, -SCORE_NEG)
            acc = jnp.where(causal, acc, SCORE_NEG)
        else:
            lo_src = acc
        sct_ref[pl.ds(start, c), :] = acc
        return fold_rows(acc, jnp.maximum), fold_rows(lo_src, jnp.minimum)

    def score_body(j, carry):
        mx, mn = score_chunk(j, False)
        return jnp.maximum(carry[0], mx), jnp.minimum(carry[1], mn)

    pmax, pmin = lax.fori_loop(0, i, score_body, (jnp.full((8, tq), SCORE_NEG, F32),
                                                  jnp.full((8, tq), -SCORE_NEG, F32)))
    mx, mn = score_chunk(i, True)
    rmax = jnp.max(jnp.maximum(pmax, mx), axis=0, keepdims=True)
    rmin = jnp.min(jnp.minimum(pmin, mn), axis=0, keepdims=True)

    nchunks = i + 1
    n_causal = i * tq + lax.broadcasted_iota(I32, (1, tq), 1) + 1

    def count_keys(pred):
        def body(j, part):
            start = pl.multiple_of(j * c, c)
            hit = pred(sct_ref[pl.ds(start, c), :], start)
            return part + fold_rows(jnp.where(hit, 1.0, 0.0), jnp.add)
        part = lax.fori_loop(0, nchunks, body, jnp.zeros((8, tq), F32))
        return jnp.sum(part, axis=0, keepdims=True)

    kf = float(topk)
    done0 = jnp.where(n_causal <= topk, 1, 0)

    def cond(st):
        return st[0] > 0

    def step(st):
        _, it, lo, hi, c_lo, c_hi, done, zp = st
        mid = 0.5 * lo + 0.5 * hi
        lo_k, hi_k = _to_key(lo), _to_key(hi)
        mid_key = _from_key((lo_k >> 1) + (hi_k >> 1) + (lo_k & hi_k & 1))
        mid = jnp.where(it >= KEY_SPACE_AFTER, mid_key, mid)
        probe0 = (zp == 0) & (lo < 0.0) & (hi > 0.0)
        probe1 = (zp <= 1) & (lo == 0.0) & (hi > TINY_F32)
        mid = jnp.where(probe0, 0.0, jnp.where(probe1, TINY_F32, mid))
        zp = jnp.where(probe0, 1, jnp.where(probe1, 2, zp))
        go = (done == 0) & (mid > lo) & (mid < hi)
        cnt = count_keys(lambda tile, start: tile >= mid)
        ge = cnt >= kf
        up = ge & go
        dn = jnp.logical_not(ge) & go
        lo = jnp.where(up, mid, lo)
        hi = jnp.where(dn, mid, hi)
        c_lo = jnp.where(up, cnt, c_lo)
        c_hi = jnp.where(dn, cnt, c_hi)
        done = jnp.where(go & (c_lo != kf), 0, 1)
        return jnp.sum(1 - done), it + 1, lo, hi, c_lo, c_hi, done, zp

    _, _, thr, _, c_lo, c_hi, _, _ = lax.while_loop(
        cond, step, (jnp.sum(1 - done0), jnp.int32(0), rmin, _from_key(_to_key(rmax) + 1),
                     n_causal.astype(F32), jnp.zeros((1, tq), F32), done0, jnp.zeros((1, tq), I32)))

    tied = (n_causal > topk) & (c_lo > kf)
    any_tied = jnp.sum(jnp.where(tied, 1, 0)) > 0

    @pl.when(any_tied)
    def _():
        n_take = kf - c_hi

        def bis(_, st):
            lo_x, hi_x = st
            mid = (lo_x + hi_x) >> 1
            ok = count_keys(lambda tile, start: (tile == thr) & ((krow + start) <= mid)) >= n_take
            return jnp.where(ok, lo_x, mid), jnp.where(ok, mid, hi_x)

        nbits = int(math.ceil(math.log2(sct_ref.shape[0]))) + 1
        _, hi_x = lax.fori_loop(0, nbits, bis, (jnp.full((1, tq), -1, I32),
                                                jnp.full((1, tq), sct_ref.shape[0], I32)))
        cut = jnp.where(tied, hi_x, jnp.int32(2 ** 30))

        def mask_body(j, _):
            start = pl.multiple_of(j * c, c)
            tile = sct_ref[pl.ds(start, c), :]
            sel = (tile > thr) | ((tile == thr) & ((krow + start) <= cut))
            mb_ref[:, pl.ds(start, c)] = jnp.transpose(jnp.where(sel, 0.0, MASK_NEG))
            return 0

        lax.fori_loop(0, nchunks, mask_body, 0)

    @pl.when(jnp.logical_not(any_tied))
    def _():
        def mask_body(j, _):
            start = pl.multiple_of(j * c, c)
            tile = sct_ref[pl.ds(start, c), :]
            mb_ref[:, pl.ds(start, c)] = jnp.transpose(jnp.where(tile >= thr, 0.0, MASK_NEG))
            return 0

        lax.fori_loop(0, nchunks, mask_body, 0)

    m_ref[...] = jnp.full(m_ref.shape, MASK_NEG, F32)
    l_ref[...] = jnp.zeros(l_ref.shape, F32)
    acc_ref[...] = jnp.zeros(acc_ref.shape, F32)

    def attend(start, width, kind):
        nt = width // LANES
        ones = jnp.ones((width, LANES), BF16)
        for h in range(N_HEADS):
            hp = h // 2
            kp = k_ref[pl.ds(start, width), hp * LANES:(hp + 1) * LANES]
            vp = v_ref[pl.ds(start, width), hp * LANES:(hp + 1) * LANES]
            s = lax.dot_general(qm_ref[h], kp, nt_dims, preferred_element_type=F32)
            s = s + mb_ref[:, pl.ds(start, width)]
            if kind is not None:
                s = s + tb_ref[h, kind]
            m_old = m_ref[h]
            m_new = jnp.maximum(m_old, jnp.max(_fold_lanes(s, jnp.maximum), axis=1, keepdims=True))
            p = jnp.exp2(s - _tile_lanes(m_new, nt)).astype(BF16)
            pv = jnp.dot(p, jnp.concatenate([vp, ones], axis=1), preferred_element_type=F32)
            alpha = jnp.exp2(m_old - m_new)
            m_ref[h] = m_new
            l_ref[h] = alpha * l_ref[h] + pv[:, LANES:]
            acc_ref[h] = alpha * acc_ref[h] + pv[:, :LANES]

    wide = 2 * c
    n_far = jnp.maximum(i - 1, 0)

    def far_body(j, _):
        attend(pl.multiple_of(j * wide, wide), wide, None)
        return 0

    lax.fori_loop(0, n_far // 2, far_body, 0)

    @pl.when(n_far % 2 == 1)
    def _():
        attend(pl.multiple_of((n_far - 1) * c, c), c, None)

    @pl.when(i >= 1)
    def _():
        attend(pl.multiple_of((i - 1) * c, c), c, 1)

    attend(pl.multiple_of(i * c, c), c, 0)

    for hp in range(N_HEADS // 2):
        lo = acc_ref[2 * hp] / l_ref[2 * hp]
        hi = acc_ref[2 * hp + 1] / l_ref[2 * hp + 1]
        o_ref[:, hp * LANES:(hp + 1) * LANES] = jnp.where(low_half, lo, hi).astype(BF16)


def _attn_call(rel_bias, q, qi, wi, k, v, kid, tq):
    bsz, s, _ = q.shape
    nq = s // tq
    topk = min(TOPK_MAX, s // 4)
    assert tq % LANES == 0 and tq >= MAX_DISTANCE
    blk = lambda w: pl.BlockSpec((None, tq, w), lambda b, i: (b, i, 0))
    whole = lambda w: pl.BlockSpec((None, s, w), lambda b, i: (b, 0, 0), pipeline_mode=pl.Buffered(1))
    return pl.pallas_call(
        functools.partial(_attn_kernel, tq=tq, topk=topk),
        grid=(bsz, nq),
        in_specs=[pl.BlockSpec(memory_space=pltpu.SMEM),
                  blk(ATTN_WIDTH), blk(ATTN_WIDTH), blk(LANES),
                  whole(ATTN_WIDTH), whole(ATTN_WIDTH), whole(LANES)],
        out_specs=blk(ATTN_WIDTH),
        out_shape=jax.ShapeDtypeStruct((bsz, s, ATTN_WIDTH), BF16),
        scratch_shapes=[
            pltpu.VMEM((s, tq), F32),
            pltpu.VMEM((tq, s), F32),
            pltpu.VMEM((N_HEADS, 2, tq, tq), F32),
            pltpu.VMEM((N_HEADS, tq, LANES), BF16),
            pltpu.VMEM((N_HEADS, tq, LANES), BF16),
            pltpu.VMEM((N_HEADS, tq, LANES), F32),
            pltpu.VMEM((N_HEADS, tq, LANES), F32),
            pltpu.VMEM((N_HEADS, tq, LANES), F32),
        ],
        compiler_params=pltpu.CompilerParams(dimension_semantics=("arbitrary", "arbitrary"),
                                             vmem_limit_bytes=VMEM_LIMIT),
        name="attn",
    )(rel_bias, q, qi, wi, k, v, kid)


def _post_kernel(attn_ref, z_ref, gate_ref, x_ref, mod_ref, wa_ref, wc_ref, wo_ref, g2_ref,
                 wr_ref, br_ref, hres_ref, h2_ref, gd_ref):
    ya = jnp.dot(attn_ref[...], wa_ref[...], preferred_element_type=F32)
    yc = jnp.dot(z_ref[...], wc_ref[...], preferred_element_type=F32)
    merged = gate_ref[:, 0:D_MODEL].astype(F32) * ya + gate_ref[:, D_MODEL:].astype(F32) * yc
    o = jnp.dot(merged.astype(BF16), wo_ref[...], preferred_element_type=F32)
    hres = x_ref[...] + mod_ref[2:3, :] * o
    hres_ref[...] = hres
    h2 = _rms_mod(hres, g2_ref[...], mod_ref[4:5, :], mod_ref[3:4, :])
    h2_ref[...] = h2.astype(BF16)

    h2_hi = h2.astype(BF16)
    h2_lo = (h2 - h2_hi.astype(F32)).astype(BF16)
    lg = (jnp.dot(h2_hi, wr_ref[0], preferred_element_type=F32)
          + (jnp.dot(h2_lo, wr_ref[0], preferred_element_type=F32)
             + jnp.dot(h2_hi, wr_ref[1], preferred_element_type=F32))) + br_ref[...]
    tm = lg.shape[0]
    lane = lax.broadcasted_iota(I32, (tm, LANES), 1)
    ninf = -jnp.inf
    gmask = (lane >= N_EXPERTS) & (lane < N_EXPERTS + N_GROUPS)
    gl = jnp.where(gmask, lg, ninf)
    gmax = jnp.max(gl, axis=1, keepdims=True)
    g_sel = jnp.min(jnp.where(gmask & (gl == gmax), lane - N_EXPERTS, N_GROUPS), axis=1, keepdims=True)
    g_w = 1.0 / jnp.sum(jnp.where(gmask, jnp.exp(gl - gmax), 0.0), axis=1, keepdims=True)

    emask = (lane < N_EXPERTS) & ((lane // EXPERTS_PER_GROUP) == g_sel)
    el = jnp.where(emask, lg, ninf)
    emax = jnp.max(el, axis=1, keepdims=True)
    ee = jnp.where(emask, jnp.exp(el - emax), 0.0)
    e_prob = ee / jnp.sum(ee, axis=1, keepdims=True)
    p1 = jnp.max(jnp.where(emask, e_prob, -1.0), axis=1, keepdims=True)
    i1 = jnp.min(jnp.where(emask & (e_prob == p1), lane, LANES), axis=1, keepdims=True)
    rest = emask & (lane != i1)
    p2 = jnp.max(jnp.where(rest, e_prob, -1.0), axis=1, keepdims=True)
    i2 = jnp.min(jnp.where(rest & (e_prob == p2), lane, LANES), axis=1, keepdims=True)
    psum = p1 + p2
    w1 = g_w * (p1 / psum)
    w2 = g_w * (p2 / psum)
    gd_ref[...] = jnp.where(lane == i1, w1, 0.0) + jnp.where(lane == i2, w2, 0.0)


def _post_call(attn, z, gates, x, mod, wa, wc, wo, g2, wr, br, tm):
    bsz, s, d = x.shape
    nt = s // tm
    tok = lambda w: pl.BlockSpec((None, tm, w), lambda b, t: (b, t, 0))
    full = lambda arr: pl.BlockSpec(arr.shape, lambda b, t: (0,) * arr.ndim)
    return pl.pallas_call(
        _post_kernel,
        grid=(bsz, nt),
        in_specs=[tok(ATTN_WIDTH), tok(CONV_WIDTH), tok(2 * d), tok(d),
                  pl.BlockSpec((None, 6, d), lambda b, t: (b, 0, 0)),
                  full(wa), full(wc), full(wo), full(g2), full(wr), full(br)],
        out_specs=[tok(d), tok(d), tok(LANES)],
        out_shape=[jax.ShapeDtypeStruct((bsz, s, d), F32), jax.ShapeDtypeStruct((bsz, s, d), BF16),
                   jax.ShapeDtypeStruct((bsz, s, LANES), F32)],
        compiler_params=pltpu.CompilerParams(dimension_semantics=("arbitrary", "arbitrary"),
                                             vmem_limit_bytes=VMEM_LIMIT),
        name="post",
    )(attn, z, gates, x, mod, wa, wc, wo, g2, wr, br)


def _moe_kernel(h2_ref, gd_ref, wg_ref, wu_ref, wd_ref, hres_ref, mod_ref, gf_ref, o_ref, acc_ref):
    e = pl.program_id(2)

    @pl.when(e == 0)
    def _():
        acc_ref[...] = jnp.zeros(acc_ref.shape, F32)

    gd = gd_ref[...]
    lane = lax.broadcasted_iota(I32, gd.shape, 1)
    gcol = jnp.sum(jnp.where(lane == e, gd, 0.0), axis=1, keepdims=True)
    h2 = h2_ref[...]
    a = jnp.dot(h2, wg_ref[...], preferred_element_type=F32)
    b = jnp.dot(h2, wu_ref[...], preferred_element_type=F32)
    hid = (a * jax.nn.sigmoid(a)) * b
    y = jnp.dot(hid.astype(BF16), wd_ref[...], preferred_element_type=F32)
    acc_ref[...] += gcol * y

    @pl.when(e == pl.num_programs(2) - 1)
    def _():
        hfin = hres_ref[...] + mod_ref[5:6, :] * acc_ref[...]
        ms = jnp.mean(hfin * hfin, axis=-1, keepdims=True)
        o_ref[...] = hfin * lax.rsqrt(ms + EPS) * gf_ref[...]


def _moe_call(h2, gd, wg, wu, wd, hres, mod, gf, tm):
    bsz, s, d = hres.shape
    nt = s // tm
    ne = wg.shape[0]
    tok = lambda w: pl.BlockSpec((None, tm, w), lambda b, t, e: (b, t, 0))
    return pl.pallas_call(
        _moe_kernel,
        grid=(bsz, nt, ne),
        in_specs=[tok(d), tok(LANES),
                  pl.BlockSpec((None, d, D_EXPERT), lambda b, t, e: (e, 0, 0)),
                  pl.BlockSpec((None, d, D_EXPERT), lambda b, t, e: (e, 0, 0)),
                  pl.BlockSpec((None, D_EXPERT, d), lambda b, t, e: (e, 0, 0)),
                  tok(d), pl.BlockSpec((None, 6, d), lambda b, t, e: (b, 0, 0)),
                  pl.BlockSpec((1, d), lambda b, t, e: (0, 0))],
        out_specs=tok(d),
        out_shape=jax.ShapeDtypeStruct((bsz, s, d), F32),
        scratch_shapes=[pltpu.VMEM((tm, d), F32)],
        compiler_params=pltpu.CompilerParams(
            dimension_semantics=("arbitrary", "arbitrary", "arbitrary"), vmem_limit_bytes=VMEM_LIMIT),
        name="moe",
    )(h2, gd, wg, wu, wd, hres, mod, gf)


def kernel(x, c, w_ada, b_ada, norm1_g, w_in, rel_bias, conv_w, w_attn_branch, w_conv_branch, w_out,
           norm2_g, w_router_group, b_router_group, w_router_expert, b_router_expert,
           w_gate_e, w_up_e, w_down_e, norm_f_g):
    bsz, s, d = x.shape
    depth = w_ada.shape[0]
    h = x.astype(F32)
    c_pad = jnp.zeros((8, d), F32).at[:bsz].set(c.astype(F32))
    out = h
    for l in range(depth):
        mod = _ada_call(c_pad, w_ada[l].astype(F32), b_ada[l].astype(F32)[None, :])
        mod = mod[:bsz].reshape(bsz, 6, d)

        w = w_in[l]
        o_q, o_k, o_v, o_qi = 0, 512, 1024, 1536
        o_ki, o_wi, o_cb, o_gl = 2048, 2112, 2120, 3656
        wa = w[:, o_q:o_ki].astype(BF16)
        ki_w = w[:, o_ki:o_wi]
        wb = jnp.concatenate([ki_w, ki_w, w[:, o_wi:o_cb],
                              jnp.zeros((d, LANES - IDX_HEADS), w.dtype)], axis=1).astype(BF16)
        wc = w[:, o_cb:o_gl].astype(BF16)
        wd = w[:, o_gl:].astype(BF16)

        q, k, v, qi, kid, wi, z, gates = _proj_call(
            h, mod, norm1_g[l].astype(F32)[None, :], wa, wb, wc, wd, conv_w[l].astype(F32),
            tm=min(512, s))

        attn = _attn_call(rel_bias.astype(F32), q, qi, wi, k, v, kid, tq=min(TOPK_MAX, s // 4))

        wr32 = jnp.concatenate([w_router_expert[l], w_router_group[l],
                                jnp.zeros((d, LANES - N_EXPERTS - N_GROUPS), F32)], axis=1).astype(F32)
        wr_hi = wr32.astype(BF16)
        wr = jnp.stack([wr_hi, (wr32 - wr_hi.astype(F32)).astype(BF16)])
        br = jnp.concatenate([b_router_expert[l], b_router_group[l],
                              jnp.zeros((LANES - N_EXPERTS - N_GROUPS,), F32)])[None, :].astype(F32)
        hres, h2, gd = _post_call(
            attn, z, gates, h, mod, w_attn_branch[l].astype(BF16), w_conv_branch[l].astype(BF16),
            w_out[l].astype(BF16), norm2_g[l].astype(F32)[None, :], wr, br, tm=min(512, s))

        assert depth == 1, "the final RMSNorm is fused into the expert kernel of the only layer"
        gf = norm_f_g.astype(F32)[None, :]
        out = _moe_call(h2, gd, w_gate_e[l].astype(BF16), w_up_e[l].astype(BF16),
                        w_down_e[l].astype(BF16), hres, mod, gf, tm=min(1024, s))
        h = out
    return out.astype(x.dtype)
```

```python
import functools
import math

import numpy as np
import jax
import jax.numpy as jnp
from jax import lax
from jax.experimental import pallas as pl
from jax.experimental.pallas import tpu as pltpu

F32 = jnp.float32
BF16 = jnp.bfloat16
I32 = jnp.int32

D_MODEL = 1024
N_HEADS = 8
HEAD_DIM = 64
ATTN_WIDTH = N_HEADS * HEAD_DIM
IDX_HEADS = 8
IDX_DIM = 64
TOPK_MAX = 256
N_BUCKETS = 32
MAX_DISTANCE = 128
CONV_WIDTH = 512
CONV_K = 3
N_GROUPS = 4
EXPERTS_PER_GROUP = 8
N_EXPERTS = N_GROUPS * EXPERTS_PER_GROUP
D_EXPERT = 512
EPS = 1e-6

LANES = 128
VMEM_LIMIT = 56 * 1024 * 1024

LOG2E = math.log2(math.e)
MASK_NEG = -1e30
SCORE_NEG = float(np.finfo(np.float32).min)
TINY_F32 = float(np.finfo(np.float32).tiny)
KEY_SPACE_AFTER = 28


def _t5_bucket_starts():
    max_exact = N_BUCKETS // 2
    n = np.arange(0, MAX_DISTANCE + 1)
    nf = np.maximum(n, 1).astype(np.float32)
    val = (np.log(nf / np.float32(max_exact)) / np.float32(math.log(MAX_DISTANCE / max_exact))
           * np.float32(N_BUCKETS - max_exact)).astype(np.float32)
    inner = val[max_exact + 1:MAX_DISTANCE]
    assert np.min(np.abs(inner - np.round(inner))) > 1e-3
    large = np.minimum(max_exact + val.astype(np.int32), N_BUCKETS - 1)
    bucket = np.where(n < max_exact, n, large)
    assert bucket[MAX_DISTANCE] == N_BUCKETS - 1 and np.all(np.diff(bucket) >= 0)
    return [int(np.argmax(bucket >= b)) for b in range(N_BUCKETS)]


_BUCKET_START = _t5_bucket_starts()


def _ada_kernel(c_ref, w_ref, b_ref, o_ref):
    c = c_ref[...]
    ca = c * jax.nn.sigmoid(c)
    o_ref[...] = jnp.dot(ca, w_ref[...], preferred_element_type=F32,
                         precision=lax.Precision.HIGHEST) + b_ref[...]


def _ada_call(c_pad, w, b):
    rows, d = c_pad.shape
    n = w.shape[1]
    tn = 1536
    return pl.pallas_call(
        _ada_kernel,
        grid=(n // tn,),
        in_specs=[pl.BlockSpec((rows, d), lambda j: (0, 0)),
                  pl.BlockSpec((d, tn), lambda j: (0, j)),
                  pl.BlockSpec((1, tn), lambda j: (0, j))],
        out_specs=pl.BlockSpec((rows, tn), lambda j: (0, j)),
        out_shape=jax.ShapeDtypeStruct((rows, n), F32),
        compiler_params=pltpu.CompilerParams(dimension_semantics=("arbitrary",),
                                             vmem_limit_bytes=VMEM_LIMIT),
        name="ada",
    )(c_pad, w, b)


def _rms_mod(x, g, scale, shift):
    ms = jnp.mean(x * x, axis=-1, keepdims=True)
    y = x * lax.rsqrt(ms + EPS) * g
    return y * (1.0 + scale) + shift


def _proj_kernel(x_ref, mod_ref, g_ref, wa_ref, wb_ref, wc_ref, wd_ref, cw_ref,
                 q_ref, k_ref, v_ref, qi_ref, kid_ref, wi_ref, z_ref, gate_ref, ubuf, *, tm):
    h = _rms_mod(x_ref[...], g_ref[...], mod_ref[1:2, :], mod_ref[0:1, :])
    hb = h.astype(BF16)

    a = jnp.dot(hb, wa_ref[...], preferred_element_type=F32)
    q_ref[...] = (a[:, 0:512] * (HEAD_DIM ** -0.5 * LOG2E)).astype(BF16)
    k_ref[...] = a[:, 512:1024].astype(BF16)
    v_ref[...] = a[:, 1024:1536].astype(BF16)
    qi_ref[...] = a[:, 1536:2048].astype(BF16)

    b = jnp.dot(hb, wb_ref[...], preferred_element_type=F32)
    kid_ref[...] = b[:, 0:LANES].astype(BF16)
    wi_ref[...] = b[:, LANES:2 * LANES] * ((IDX_DIM ** -0.5) * (IDX_HEADS ** -0.5))

    cpart = jnp.dot(hb, wc_ref[...], preferred_element_type=F32)
    cb = cpart[:, 0:512]
    u = cpart[:, 512:1024] * cpart[:, 1024:1536]

    @pl.when(pl.program_id(1) == 0)
    def _():
        ubuf[0:8, :] = jnp.zeros((8, CONV_WIDTH), F32)

    ubuf[8:tm + 8, :] = u
    conv = (cw_ref[0:1, :] * ubuf[6:tm + 6, :] + cw_ref[1:2, :] * ubuf[7:tm + 7, :]
            + cw_ref[2:3, :] * u)
    z_ref[...] = (cb * conv).astype(BF16)
    ubuf[0:8, :] = ubuf[tm:tm + 8, :]

    d = jnp.dot(hb, wd_ref[...], preferred_element_type=F32)
    gate_ref[...] = jax.nn.sigmoid(d).astype(BF16)


def _proj_call(x, mod, g1, wa, wb, wc, wd, cw, tm):
    bsz, s, d = x.shape
    nt = s // tm
    tok = lambda w: pl.BlockSpec((None, tm, w), lambda b, t: (b, t, 0))
    full = lambda arr: pl.BlockSpec(arr.shape, lambda b, t: (0,) * arr.ndim)
    out_widths = [(512, BF16), (512, BF16), (512, BF16), (512, BF16), (LANES, BF16), (LANES, F32),
                  (512, BF16), (2 * D_MODEL, BF16)]
    return pl.pallas_call(
        functools.partial(_proj_kernel, tm=tm),
        grid=(bsz, nt),
        in_specs=[tok(d), pl.BlockSpec((None, 6, d), lambda b, t: (b, 0, 0)), full(g1),
                  full(wa), full(wb), full(wc), full(wd), full(cw)],
        out_specs=[tok(w) for w, _ in out_widths],
        out_shape=[jax.ShapeDtypeStruct((bsz, s, w), dt) for w, dt in out_widths],
        scratch_shapes=[pltpu.VMEM((tm + 8, CONV_WIDTH), F32)],
        compiler_params=pltpu.CompilerParams(dimension_semantics=("arbitrary", "arbitrary"),
                                             vmem_limit_bytes=VMEM_LIMIT),
        name="proj",
    )(x, mod, g1, wa, wb, wc, wd, cw)


def _to_key(f):
    b = pltpu.bitcast(f, I32)
    return jnp.where(b < 0, b ^ 0x7FFFFFFF, b)


def _from_key(k):
    return pltpu.bitcast(jnp.where(k < 0, k ^ 0x7FFFFFFF, k), F32)


def _fold_lanes(x, op):
    parts = [x[:, t * LANES:(t + 1) * LANES] for t in range(x.shape[1] // LANES)]
    while len(parts) > 1:
        nxt = [op(parts[a], parts[a + 1]) for a in range(0, len(parts) - 1, 2)]
        parts = nxt + ([parts[-1]] if len(parts) % 2 else [])
    return parts[0]


def _tile_lanes(x, n):
    return x if n == 1 else jnp.concatenate([x] * n, axis=1)


def _attn_kernel(rb_ref, q_ref, qi_ref, wi_ref, k_ref, v_ref, kid_ref, o_ref,
                 sct_ref, mb_ref, tb_ref, qm_ref, qim_ref, m_ref, l_ref, acc_ref, *, tq, topk):
    c = tq
    i = pl.program_id(1)
    nt_dims = (((1,), (1,)), ((), ()))
    lane = lax.broadcasted_iota(I32, (tq, LANES), 1)
    low_half = lane < HEAD_DIM
    row = lax.broadcasted_iota(I32, (tq, c), 0)
    col = lax.broadcasted_iota(I32, (tq, c), 1)

    @pl.when((pl.program_id(0) == 0) & (i == 0))
    def _():
        for kind in range(2):
            rel = row - col + kind * c
            n = jnp.maximum(rel, 0)
            for h in range(N_HEADS):
                t = jnp.full((tq, c), rb_ref[0, h], F32)
                for b in range(1, N_BUCKETS):
                    t = jnp.where(n >= _BUCKET_START[b], rb_ref[b, h], t)
                t = (t - rb_ref[N_BUCKETS - 1, h]) * LOG2E
                tb_ref[h, kind] = jnp.where(rel >= 0, t, MASK_NEG)

    for h in range(N_HEADS):
        hp, half = divmod(h, 2)
        keep = low_half if half == 0 else jnp.logical_not(low_half)
        qp = q_ref[:, hp * LANES:(hp + 1) * LANES]
        qm_ref[h] = jnp.where(keep, qp, jnp.zeros_like(qp))
        qip = qi_ref[:, hp * LANES:(hp + 1) * LANES]
        qim_ref[h] = jnp.where(keep, qip, jnp.zeros_like(qip))

    w_t = jnp.transpose(wi_ref[...])
    krow = lax.broadcasted_iota(I32, (c, tq), 0)
    qcol = lax.broadcasted_iota(I32, (c, tq), 1)

    def fold_rows(x, op):
        parts = [x[r * 8:(r + 1) * 8, :] for r in range(x.shape[0] // 8)]
        while len(parts) > 1:
            nxt = [op(parts[a], parts[a + 1]) for a in range(0, len(parts) - 1, 2)]
            parts = nxt + ([parts[-1]] if len(parts) % 2 else [])
        return parts[0]

    def score_chunk(j, diag):
        start = pl.multiple_of(j * c, c)
        kc = kid_ref[pl.ds(start, c), :]
        acc = jnp.zeros((c, tq), F32)
        for h in range(N_HEADS):
            d = lax.dot_general(kc, qim_ref[h], nt_dims, preferred_element_type=F32)
            acc = acc + jnp.maximum(d, 0.0) * w_t[h:h + 1, :]
        if diag:
            causal = krow <= qcol
            lo_src = jnp.where(causal, acc, -SCORE_NEG)
            acc = jnp.where(causal, acc, SCORE_NEG)
        else:
            lo_src = acc
        sct_ref[pl.ds(start, c), :] = acc
        return fold_rows(acc, jnp.maximum), fold_rows(lo_src, jnp.minimum)

    def score_body(j, carry):
        mx, mn = score_chunk(j, False)
        return jnp.maximum(carry[0], mx), jnp.minimum(carry[1], mn)

    pmax, pmin = lax.fori_loop(0, i, score_body, (jnp.full((8, tq), SCORE_NEG, F32),
                                                  jnp.full((8, tq), -SCORE_NEG, F32)))
    mx, mn = score_chunk(i, True)
    rmax = jnp.max(jnp.maximum(pmax, mx), axis=0, keepdims=True)
    rmin = jnp.min(jnp.minimum(pmin, mn), axis=0, keepdims=True)

    nchunks = i + 1
    n_causal = i * tq + lax.broadcasted_iota(I32, (1, tq), 1) + 1

    def count_keys(pred):
        def body(j, part):
            start = pl.multiple_of(j * c, c)
            hit = pred(sct_ref[pl.ds(start, c), :], start)
            return part + fold_rows(jnp.where(hit, 1.0, 0.0), jnp.add)
        part = lax.fori_loop(0, nchunks, body, jnp.zeros((8, tq), F32))
        return jnp.sum(part, axis=0, keepdims=True)

    kf = float(topk)
    done0 = jnp.where(n_causal <= topk, 1, 0)

    def cond(st):
        return st[0] > 0

    def step(st):
        _, it, lo, hi, c_lo, c_hi, done, zp = st
        mid = 0.5 * lo + 0.5 * hi
        lo_k, hi_k = _to_key(lo), _to_key(hi)
        mid_key = _from_key((lo_k >> 1) + (hi_k >> 1) + (lo_k & hi_k & 1))
        mid = jnp.where(it >= KEY_SPACE_AFTER, mid_key, mid)
        probe0 = (zp == 0) & (lo < 0.0) & (hi > 0.0)
        probe1 = (zp <= 1) & (lo == 0.0) & (hi > TINY_F32)
        mid = jnp.where(probe0, 0.0, jnp.where(probe1, TINY_F32, mid))
        zp = jnp.where(probe0, 1, jnp.where(probe1, 2, zp))
        go = (done == 0) & (mid > lo) & (mid < hi)
        cnt = count_keys(lambda tile, start: tile >= mid)
        ge = cnt >= kf
        up = ge & go
        dn = jnp.logical_not(ge) & go
        lo = jnp.where(up, mid, lo)
        hi = jnp.where(dn, mid, hi)
        c_lo = jnp.where(up, cnt, c_lo)
        c_hi = jnp.where(dn, cnt, c_hi)
        done = jnp.where(go & (c_lo != kf), 0, 1)
        return jnp.sum(1 - done), it + 1, lo, hi, c_lo, c_hi, done, zp

    _, _, thr, _, c_lo, c_hi, _, _ = lax.while_loop(
        cond, step, (jnp.sum(1 - done0), jnp.int32(0), rmin, _from_key(_to_key(rmax) + 1),
                     n_causal.astype(F32), jnp.zeros((1, tq), F32), done0, jnp.zeros((1, tq), I32)))

    tied = (n_causal > topk) & (c_lo > kf)
    any_tied = jnp.sum(jnp.where(tied, 1, 0)) > 0

    @pl.when(any_tied)
    def _():
        n_take = kf - c_hi

        def bis(_, st):
            lo_x, hi_x = st
            mid = (lo_x + hi_x) >> 1
            ok = count_keys(lambda tile, start: (tile == thr) & ((krow + start) <= mid)) >= n_take
            return jnp.where(ok, lo_x, mid), jnp.where(ok, mid, hi_x)

        nbits = int(math.ceil(math.log2(sct_ref.shape[0]))) + 1
        _, hi_x = lax.fori_loop(0, nbits, bis, (jnp.full((1, tq), -1, I32),
                                                jnp.full((1, tq), sct_ref.shape[0], I32)))
        cut = jnp.where(tied, hi_x, jnp.int32(2 ** 30))

        def mask_body(j, _):
            start = pl.multiple_of(j * c, c)
            tile = sct_ref[pl.ds(start, c), :]
            sel = (tile > thr) | ((tile == thr) & ((krow + start) <= cut))
            mb_ref[:, pl.ds(start, c)] = jnp.transpose(jnp.where(sel, 0.0, MASK_NEG))
            return 0

        lax.fori_loop(0, nchunks, mask_body, 0)

    @pl.when(jnp.logical_not(any_tied))
    def _():
        def mask_body(j, _):
            start = pl.multiple_of(j * c, c)
            tile = sct_ref[pl.ds(start, c), :]
            mb_ref[:, pl.ds(start, c)] = jnp.transpose(jnp.where(tile >= thr, 0.0, MASK_NEG))
            return 0

        lax.fori_loop(0, nchunks, mask_body, 0)

    m_ref[...] = jnp.full(m_ref.shape, MASK_NEG, F32)
    l_ref[...] = jnp.zeros(l_ref.shape, F32)
    acc_ref[...] = jnp.zeros(acc_ref.shape, F32)

    def attend(start, width, kind):
        nt = width // LANES
        ones = jnp.ones((width, LANES), BF16)
        for h in range(N_HEADS):
            hp = h // 2
            kp = k_ref[pl.ds(start, width), hp * LANES:(hp + 1) * LANES]
            vp = v_ref[pl.ds(start, width), hp * LANES:(hp + 1) * LANES]
            s = lax.dot_general(qm_ref[h], kp, nt_dims, preferred_element_type=F32)
            s = s + mb_ref[:, pl.ds(start, width)]
            if kind is not None:
                s = s + tb_ref[h, kind]
            m_old = m_ref[h]
            m_new = jnp.maximum(m_old, jnp.max(_fold_lanes(s, jnp.maximum), axis=1, keepdims=True))
            p = jnp.exp2(s - _tile_lanes(m_new, nt)).astype(BF16)
            pv = jnp.dot(p, jnp.concatenate([vp, ones], axis=1), preferred_element_type=F32)
            alpha = jnp.exp2(m_old - m_new)
            m_ref[h] = m_new
            l_ref[h] = alpha * l_ref[h] + pv[:, LANES:]
            acc_ref[h] = alpha * acc_ref[h] + pv[:, :LANES]

    wide = 2 * c
    n_far = jnp.maximum(i - 1, 0)

    def far_body(j, _):
        attend(pl.multiple_of(j * wide, wide), wide, None)
        return 0

    lax.fori_loop(0, n_far // 2, far_body, 0)

    @pl.when(n_far % 2 == 1)
    def _():
        attend(pl.multiple_of((n_far - 1) * c, c), c, None)

    @pl.when(i >= 1)
    def _():
        attend(pl.multiple_of((i - 1) * c, c), c, 1)

    attend(pl.multiple_of(i * c, c), c, 0)

    for hp in range(N_HEADS // 2):
        lo = acc_ref[2 * hp] / l_ref[2 * hp]
        hi = acc_ref[2 * hp + 1] / l_ref[2 * hp + 1]
        o_ref[:, hp * LANES:(hp + 1) * LANES] = jnp.where(low_half, lo, hi).astype(BF16)


def _attn_call(rel_bias, q, qi, wi, k, v, kid, tq):
    bsz, s, _ = q.shape
    nq = s // tq
    topk = min(TOPK_MAX, s // 4)
    assert tq % LANES == 0 and tq >= MAX_DISTANCE
    blk = lambda w: pl.BlockSpec((None, tq, w), lambda b, i: (b, i, 0))
    whole = lambda w: pl.BlockSpec((None, s, w), lambda b, i: (b, 0, 0), pipeline_mode=pl.Buffered(1))
    return pl.pallas_call(
        functools.partial(_attn_kernel, tq=tq, topk=topk),
        grid=(bsz, nq),
        in_specs=[pl.BlockSpec(memory_space=pltpu.SMEM),
                  blk(ATTN_WIDTH), blk(ATTN_WIDTH), blk(LANES),
                  whole(ATTN_WIDTH), whole(ATTN_WIDTH), whole(LANES)],
        out_specs=blk(ATTN_WIDTH),
        out_shape=jax.ShapeDtypeStruct((bsz, s, ATTN_WIDTH), BF16),
        scratch_shapes=[
            pltpu.VMEM((s, tq), F32),
            pltpu.VMEM((tq, s), F32),
            pltpu.VMEM((N_HEADS, 2, tq, tq), F32),
            pltpu.VMEM((N_HEADS, tq, LANES), BF16),
            pltpu.VMEM((N_HEADS, tq, LANES), BF16),
            pltpu.VMEM((N_HEADS, tq, LANES), F32),
            pltpu.VMEM((N_HEADS, tq, LANES), F32),
            pltpu.VMEM((N_HEADS, tq, LANES), F32),
        ],
        compiler_params=pltpu.CompilerParams(dimension_semantics=("arbitrary", "arbitrary"),
                                             vmem_limit_bytes=VMEM_LIMIT),
        name="attn",
    )(rel_bias, q, qi, wi, k, v, kid)


def _post_kernel(attn_ref, z_ref, gate_ref, x_ref, mod_ref, wa_ref, wc_ref, wo_ref, g2_ref,
                 wr_ref, br_ref, hres_ref, h2_ref, eid_ref, ew_ref, counts_ref, cnt_ref):
    ya = jnp.dot(attn_ref[...], wa_ref[...], preferred_element_type=F32)
    yc = jnp.dot(z_ref[...], wc_ref[...], preferred_element_type=F32)
    merged = gate_ref[:, 0:D_MODEL].astype(F32) * ya + gate_ref[:, D_MODEL:].astype(F32) * yc
    o = jnp.dot(merged.astype(BF16), wo_ref[...], preferred_element_type=F32)
    hres = x_ref[...] + mod_ref[2:3, :] * o
    hres_ref[...] = hres
    h2 = _rms_mod(hres, g2_ref[...], mod_ref[4:5, :], mod_ref[3:4, :])
    h2_ref[...] = h2.astype(BF16)

    h2_hi = h2.astype(BF16)
    h2_lo = (h2 - h2_hi.astype(F32)).astype(BF16)
    lg = (jnp.dot(h2_hi, wr_ref[0], preferred_element_type=F32)
          + (jnp.dot(h2_lo, wr_ref[0], preferred_element_type=F32)
             + jnp.dot(h2_hi, wr_ref[1], preferred_element_type=F32))) + br_ref[...]
    tm = lg.shape[0]
    lane = lax.broadcasted_iota(I32, (tm, LANES), 1)
    ninf = -jnp.inf
    gmask = (lane >= N_EXPERTS) & (lane < N_EXPERTS + N_GROUPS)
    gl = jnp.where(gmask, lg, ninf)
    gmax = jnp.max(gl, axis=1, keepdims=True)
    g_sel = jnp.min(jnp.where(gmask & (gl == gmax), lane - N_EXPERTS, N_GROUPS), axis=1, keepdims=True)
    g_w = 1.0 / jnp.sum(jnp.where(gmask, jnp.exp(gl - gmax), 0.0), axis=1, keepdims=True)

    emask = (lane < N_EXPERTS) & ((lane // EXPERTS_PER_GROUP) == g_sel)
    el = jnp.where(emask, lg, ninf)
    emax = jnp.max(el, axis=1, keepdims=True)
    ee = jnp.where(emask, jnp.exp(el - emax), 0.0)
    e_prob = ee / jnp.sum(ee, axis=1, keepdims=True)
    p1 = jnp.max(jnp.where(emask, e_prob, -1.0), axis=1, keepdims=True)
    i1 = jnp.min(jnp.where(emask & (e_prob == p1), lane, LANES), axis=1, keepdims=True)
    rest = emask & (lane != i1)
    p2 = jnp.max(jnp.where(rest, e_prob, -1.0), axis=1, keepdims=True)
    i2 = jnp.min(jnp.where(rest & (e_prob == p2), lane, LANES), axis=1, keepdims=True)
    psum = p1 + p2
    w1 = g_w * (p1 / psum)
    w2 = g_w * (p2 / psum)
    ew_ref[...] = jnp.where(lane == 0, w1, jnp.where(lane == 1, w2, 0.0))

    @pl.when((pl.program_id(0) == 0) & (pl.program_id(1) == 0))
    def _():
        cnt_ref[...] = jnp.zeros(cnt_ref.shape, F32)

    onehot = jnp.where((lane == i1) | (lane == i2), 1.0, 0.0)
    tri = (lax.broadcasted_iota(I32, (tm, tm), 0) >= lax.broadcasted_iota(I32, (tm, tm), 1))
    cum = jnp.dot(jnp.where(tri, 1.0, 0.0).astype(BF16), onehot.astype(BF16),
                  preferred_element_type=F32)
    before = cum - onehot + cnt_ref[0:1, :]
    r1 = jnp.sum(jnp.where(lane == i1, before, 0.0), axis=1, keepdims=True).astype(I32)
    r2 = jnp.sum(jnp.where(lane == i2, before, 0.0), axis=1, keepdims=True).astype(I32)
    eid_ref[...] = jnp.where(lane == 0, i1, jnp.where(lane == 1, i2,
                             jnp.where(lane == 2, r1, jnp.where(lane == 3, r2, 0))))
    total = cnt_ref[0:1, :] + cum[tm - 1:tm, :]
    cnt_ref[...] = jnp.broadcast_to(total, cnt_ref.shape)
    counts_ref[...] = jnp.broadcast_to(total, counts_ref.shape)


def _post_call(attn, z, gates, x, mod, wa, wc, wo, g2, wr, br, tm):
    bsz, s, d = x.shape
    nt = s // tm
    tok = lambda w: pl.BlockSpec((None, tm, w), lambda b, t: (b, t, 0))
    full = lambda arr: pl.BlockSpec(arr.shape, lambda b, t: (0,) * arr.ndim)
    return pl.pallas_call(
        _post_kernel,
        grid=(bsz, nt),
        in_specs=[tok(ATTN_WIDTH), tok(CONV_WIDTH), tok(2 * d), tok(d),
                  pl.BlockSpec((None, 6, d), lambda b, t: (b, 0, 0)),
                  full(wa), full(wc), full(wo), full(g2), full(wr), full(br)],
        out_specs=[tok(d), tok(d), tok(LANES), tok(LANES), pl.BlockSpec((8, LANES), lambda b, t: (0, 0))],
        out_shape=[jax.ShapeDtypeStruct((bsz, s, d), F32), jax.ShapeDtypeStruct((bsz, s, d), BF16),
                   jax.ShapeDtypeStruct((bsz, s, LANES), I32), jax.ShapeDtypeStruct((bsz, s, LANES), F32),
                   jax.ShapeDtypeStruct((8, LANES), F32)],
        scratch_shapes=[pltpu.VMEM((8, LANES), F32)],
        compiler_params=pltpu.CompilerParams(dimension_semantics=("arbitrary", "arbitrary"),
                                             vmem_limit_bytes=VMEM_LIMIT),
        name="post",
    )(attn, z, gates, x, mod, wa, wc, wo, g2, wr, br)


ROW_TILE = (D_MODEL // LANES, LANES)
EXPERT_ROWS = 256
TOKENS_PER_STEP = 256


def _row_copy_wait(src_like, dst_like, sem, n):
    def body(_, c):
        pltpu.make_async_copy(src_like, dst_like, sem).wait()
        return c
    lax.fori_loop(0, n, body, 0)


def _dispatch_kernel(pos1_ref, pos2_ref, h2_ref, xs_in_ref, xs_ref, sem):
    del xs_in_ref
    tm = h2_ref.shape[0]
    base = pl.program_id(0) * tm

    def issue(r, c):
        pltpu.make_async_copy(h2_ref.at[r], xs_ref.at[pos1_ref[base + r]], sem).start()
        pltpu.make_async_copy(h2_ref.at[r], xs_ref.at[pos2_ref[base + r]], sem).start()
        return c

    lax.fori_loop(0, tm, issue, 0)
    _row_copy_wait(h2_ref.at[0], xs_ref.at[0], sem, 2 * tm)


def _dispatch_call(pos1, pos2, h2_rows, n_rows):
    t = h2_rows.shape[0]
    tm = min(TOKENS_PER_STEP, t)
    xs0 = jnp.zeros((n_rows,) + ROW_TILE, BF16)
    return pl.pallas_call(
        _dispatch_kernel,
        grid_spec=pltpu.PrefetchScalarGridSpec(
            num_scalar_prefetch=2,
            grid=(t // tm,),
            in_specs=[pl.BlockSpec((tm,) + ROW_TILE, lambda i, p1, p2: (i, 0, 0)),
                      pl.BlockSpec(memory_space=pl.ANY)],
            out_specs=pl.BlockSpec(memory_space=pl.ANY),
            scratch_shapes=[pltpu.SemaphoreType.DMA(())]),
        out_shape=jax.ShapeDtypeStruct((n_rows,) + ROW_TILE, BF16),
        input_output_aliases={3: 0},
        compiler_params=pltpu.CompilerParams(dimension_semantics=("arbitrary",),
                                             vmem_limit_bytes=VMEM_LIMIT),
        name="dispatch",
    )(pos1, pos2, h2_rows, xs0)


def _expert_kernel(te_ref, nu_ref, xs_ref, wg_ref, wu_ref, wd_ref, ys_ref, wgb, wub, wdb):
    g = pl.program_id(0)
    e = te_ref[g]
    e_prev = te_ref[jnp.maximum(g - 1, 0)]

    @pl.when((g == 0) | (e != e_prev))
    def _():
        wgb[...] = wg_ref[...].astype(BF16)
        wub[...] = wu_ref[...].astype(BF16)
        wdb[...] = wd_ref[...].astype(BF16)

    @pl.when(g < nu_ref[0])
    def _():
        x = xs_ref[...]
        a = jnp.dot(x, wgb[...], preferred_element_type=F32)
        b = jnp.dot(x, wub[...], preferred_element_type=F32)
        hid = (a * jax.nn.sigmoid(a)) * b
        ys_ref[...] = jnp.dot(hid.astype(BF16), wdb[...], preferred_element_type=F32).astype(BF16)

    @pl.when(g >= nu_ref[0])
    def _():
        ys_ref[...] = jnp.zeros(ys_ref.shape, BF16)


def _expert_call(tile_expert, n_used, xs, wg, wu, wd):
    n_rows, d = xs.shape
    nt = n_rows // EXPERT_ROWS
    wspec = lambda shp: pl.BlockSpec((None,) + shp, lambda g, te, nu: (te[g], 0, 0))
    return pl.pallas_call(
        _expert_kernel,
        grid_spec=pltpu.PrefetchScalarGridSpec(
            num_scalar_prefetch=2,
            grid=(nt,),
            in_specs=[pl.BlockSpec((EXPERT_ROWS, d), lambda g, te, nu: (g, 0)),
                      wspec((d, D_EXPERT)), wspec((d, D_EXPERT)), wspec((D_EXPERT, d))],
            out_specs=pl.BlockSpec((EXPERT_ROWS, d), lambda g, te, nu: (g, 0)),
            scratch_shapes=[pltpu.VMEM((d, D_EXPERT), BF16), pltpu.VMEM((d, D_EXPERT), BF16),
                            pltpu.VMEM((D_EXPERT, d), BF16)]),
        out_shape=jax.ShapeDtypeStruct((n_rows, d), BF16),
        compiler_params=pltpu.CompilerParams(dimension_semantics=("arbitrary",),
                                             vmem_limit_bytes=VMEM_LIMIT),
        name="experts",
    )(tile_expert, n_used, xs, wg, wu, wd)


def _combine_kernel(pos1_ref, pos2_ref, w1_ref, w2_ref, ys_ref, hres_ref, g2_ref, gf_ref, o_ref,
                    buf, sem):
    tm = hres_ref.shape[0]
    t = pl.program_id(0)
    nt = pl.num_programs(0)

    def issue(tile, slot):
        base = tile * tm

        def body(r, c):
            pltpu.make_async_copy(ys_ref.at[pos1_ref[base + r]], buf.at[slot, 0, r], sem.at[slot]).start()
            pltpu.make_async_copy(ys_ref.at[pos2_ref[base + r]], buf.at[slot, 1, r], sem.at[slot]).start()
            return c

        lax.fori_loop(0, tm, body, 0)

    @pl.when(t == 0)
    def _():
        issue(0, 0)

    @pl.when(t + 1 < nt)
    def _():
        issue(t + 1, (t + 1) % 2)

    slot = t % 2
    _row_copy_wait(ys_ref.at[0], buf.at[slot, 0, 0], sem.at[slot], 2 * tm)

    gate2 = g2_ref[...]
    gf = gf_ref[...]
    base = t * tm

    def mix(r, c):
        y = (w1_ref[base + r] * buf[slot, 0, r].astype(F32)
             + w2_ref[base + r] * buf[slot, 1, r].astype(F32))
        hfin = hres_ref[r] + gate2 * y
        ms = jnp.sum(hfin * hfin, keepdims=True) * (1.0 / D_MODEL)
        o_ref[r] = hfin * lax.rsqrt(ms + EPS) * gf
        return c

    lax.fori_loop(0, tm, mix, 0)


def _combine_call(pos1, pos2, w1, w2, ys_rows, hres_rows, gate2, gf):
    t = hres_rows.shape[0]
    tm = min(TOKENS_PER_STEP, t)
    per_batch = t // gate2.shape[0]
    assert per_batch % tm == 0
    tok = pl.BlockSpec((tm,) + ROW_TILE, lambda i, *_: (i, 0, 0))
    return pl.pallas_call(
        _combine_kernel,
        grid_spec=pltpu.PrefetchScalarGridSpec(
            num_scalar_prefetch=4,
            grid=(t // tm,),
            in_specs=[pl.BlockSpec(memory_space=pl.ANY), tok,
                      pl.BlockSpec((None,) + ROW_TILE, lambda i, *_: ((i * tm) // per_batch, 0, 0)),
                      pl.BlockSpec(ROW_TILE, lambda i, *_: (0, 0))],
            out_specs=tok,
            scratch_shapes=[pltpu.VMEM((2, 2, tm) + ROW_TILE, BF16), pltpu.SemaphoreType.DMA((2,))]),
        out_shape=jax.ShapeDtypeStruct((t,) + ROW_TILE, F32),
        compiler_params=pltpu.CompilerParams(dimension_semantics=("arbitrary",),
                                             vmem_limit_bytes=VMEM_LIMIT),
        name="combine",
    )(pos1, pos2, w1, w2, ys_rows, hres_rows, gate2, gf)


def _sparse_moe(h2, eid, ew, counts, wg, wu, wd, hres, gate2, gf):
    bsz, s, d = hres.shape
    t = bsz * s
    ne = wg.shape[0]
    e1, e2 = eid[..., 0].reshape(t), eid[..., 1].reshape(t)
    r1, r2 = eid[..., 2].reshape(t), eid[..., 3].reshape(t)
    w1, w2 = ew[..., 0].reshape(t), ew[..., 1].reshape(t)
    cnt = counts[0, :ne].astype(I32)
    seg_rows = ((cnt + EXPERT_ROWS - 1) // EXPERT_ROWS) * EXPERT_ROWS
    seg_end = jnp.cumsum(seg_rows)
    seg_start = seg_end - seg_rows
    pos1 = seg_start[e1] + r1
    pos2 = seg_start[e2] + r2
    n_tiles = (2 * t) // EXPERT_ROWS + ne
    tile_first_row = jnp.arange(n_tiles, dtype=I32) * EXPERT_ROWS
    tile_expert = jnp.minimum(jnp.sum(seg_end[None, :] <= tile_first_row[:, None], axis=1), ne - 1)
    n_used = (seg_end[ne - 1] // EXPERT_ROWS).reshape(1)
    n_rows = n_tiles * EXPERT_ROWS

    xs = _dispatch_call(pos1, pos2, h2.reshape((t,) + ROW_TILE), n_rows)
    ys = _expert_call(tile_expert.astype(I32), n_used.astype(I32), xs.reshape(n_rows, d), wg, wu, wd)
    out = _combine_call(pos1, pos2, w1, w2, ys.reshape((n_rows,) + ROW_TILE),
                        hres.reshape((t,) + ROW_TILE), gate2.reshape((bsz,) + ROW_TILE),
                        gf.reshape(ROW_TILE))
    return out.reshape(bsz, s, d)


def kernel(x, c, w_ada, b_ada, norm1_g, w_in, rel_bias, conv_w, w_attn_branch, w_conv_branch, w_out,
           norm2_g, w_router_group, b_router_group, w_router_expert, b_router_expert,
           w_gate_e, w_up_e, w_down_e, norm_f_g):
    bsz, s, d = x.shape
    depth = w_ada.shape[0]
    h = x.astype(F32)
    c_pad = jnp.zeros((8, d), F32).at[:bsz].set(c.astype(F32))
    out = h
    for l in range(depth):
        mod = _ada_call(c_pad, w_ada[l].astype(F32), b_ada[l].astype(F32)[None, :])
        mod = mod[:bsz].reshape(bsz, 6, d)

        w = w_in[l]
        o_q, o_k, o_v, o_qi = 0, 512, 1024, 1536
        o_ki, o_wi, o_cb, o_gl = 2048, 2112, 2120, 3656
        wa = w[:, o_q:o_ki].astype(BF16)
        ki_w = w[:, o_ki:o_wi]
        wb = jnp.concatenate([ki_w, ki_w, w[:, o_wi:o_cb],
                              jnp.zeros((d, LANES - IDX_HEADS), w.dtype)], axis=1).astype(BF16)
        wc = w[:, o_cb:o_gl].astype(BF16)
        wd = w[:, o_gl:].astype(BF16)

        q, k, v, qi, kid, wi, z, gates = _proj_call(
            h, mod, norm1_g[l].astype(F32)[None, :], wa, wb, wc, wd, conv_w[l].astype(F32),
            tm=min(512, s))

        attn = _attn_call(rel_bias.astype(F32), q, qi, wi, k, v, kid, tq=min(TOPK_MAX, s // 4))

        wr32 = jnp.concatenate([w_router_expert[l], w_router_group[l],
                                jnp.zeros((d, LANES - N_EXPERTS - N_GROUPS), F32)], axis=1).astype(F32)
        wr_hi = wr32.astype(BF16)
        wr = jnp.stack([wr_hi, (wr32 - wr_hi.astype(F32)).astype(BF16)])
        br = jnp.concatenate([b_router_expert[l], b_router_group[l],
                              jnp.zeros((LANES - N_EXPERTS - N_GROUPS,), F32)])[None, :].astype(F32)
        hres, h2, eid, ew, counts = _post_call(
            attn, z, gates, h, mod, w_attn_branch[l].astype(BF16), w_conv_branch[l].astype(BF16),
            w_out[l].astype(BF16), norm2_g[l].astype(F32)[None, :], wr, br, tm=min(512, s))

        assert depth == 1, "the final RMSNorm is fused into the combine kernel of the only layer"
        out = _sparse_moe(h2, eid, ew, counts, w_gate_e[l].astype(F32), w_up_e[l].astype(F32),
                          w_down_e[l].astype(F32), hres, mod[:, 5, :], norm_f_g.astype(F32))
        h = out
    return out.astype(x.dtype)
```

```python
import functools
import math

import numpy as np
import jax
import jax.numpy as jnp
from jax import lax
from jax.experimental import pallas as pl
from jax.experimental.pallas import tpu as pltpu

F32 = jnp.float32
BF16 = jnp.bfloat16
I32 = jnp.int32

D_MODEL = 1024
N_HEADS = 8
HEAD_DIM = 64
ATTN_WIDTH = N_HEADS * HEAD_DIM
IDX_HEADS = 8
IDX_DIM = 64
TOPK_MAX = 256
N_BUCKETS = 32
MAX_DISTANCE = 128
CONV_WIDTH = 512
CONV_K = 3
N_GROUPS = 4
EXPERTS_PER_GROUP = 8
N_EXPERTS = N_GROUPS * EXPERTS_PER_GROUP
D_EXPERT = 512
EPS = 1e-6

LANES = 128
VMEM_LIMIT = 56 * 1024 * 1024

LOG2E = math.log2(math.e)
MASK_NEG = -1e30
SCORE_NEG = float(np.finfo(np.float32).min)
TINY_F32 = float(np.finfo(np.float32).tiny)
KEY_SPACE_AFTER = 28


def _t5_bucket_starts():
    max_exact = N_BUCKETS // 2
    n = np.arange(0, MAX_DISTANCE + 1)
    nf = np.maximum(n, 1).astype(np.float32)
    val = (np.log(nf / np.float32(max_exact)) / np.float32(math.log(MAX_DISTANCE / max_exact))
           * np.float32(N_BUCKETS - max_exact)).astype(np.float32)
    inner = val[max_exact + 1:MAX_DISTANCE]
    assert np.min(np.abs(inner - np.round(inner))) > 1e-3
    large = np.minimum(max_exact + val.astype(np.int32), N_BUCKETS - 1)
    bucket = np.where(n < max_exact, n, large)
    assert bucket[MAX_DISTANCE] == N_BUCKETS - 1 and np.all(np.diff(bucket) >= 0)
    return [int(np.argmax(bucket >= b)) for b in range(N_BUCKETS)]


_BUCKET_START = _t5_bucket_starts()


def _ada_kernel(c_ref, w_ref, b_ref, o_ref):
    c = c_ref[...]
    ca = c * jax.nn.sigmoid(c)
    o_ref[...] = jnp.dot(ca, w_ref[...], preferred_element_type=F32,
                         precision=lax.Precision.HIGHEST) + b_ref[...]


def _ada_call(c_pad, w, b):
    rows, d = c_pad.shape
    n = w.shape[1]
    tn = 1536
    return pl.pallas_call(
        _ada_kernel,
        grid=(n // tn,),
        in_specs=[pl.BlockSpec((rows, d), lambda j: (0, 0)),
                  pl.BlockSpec((d, tn), lambda j: (0, j)),
                  pl.BlockSpec((1, tn), lambda j: (0, j))],
        out_specs=pl.BlockSpec((rows, tn), lambda j: (0, j)),
        out_shape=jax.ShapeDtypeStruct((rows, n), F32),
        compiler_params=pltpu.CompilerParams(dimension_semantics=("arbitrary",),
                                             vmem_limit_bytes=VMEM_LIMIT),
        name="ada",
    )(c_pad, w, b)


def _rms_mod(x, g, scale, shift):
    ms = jnp.mean(x * x, axis=-1, keepdims=True)
    y = x * lax.rsqrt(ms + EPS) * g
    return y * (1.0 + scale) + shift


def _proj_kernel(x_ref, mod_ref, g_ref, wa_ref, wb_ref, wc_ref, wd_ref, cw_ref,
                 q_ref, k_ref, v_ref, qi_ref, kid_ref, wi_ref, z_ref, gate_ref, ubuf, *, tm):
    h = _rms_mod(x_ref[...], g_ref[...], mod_ref[1:2, :], mod_ref[0:1, :])
    hb = h.astype(BF16)

    a = jnp.dot(hb, wa_ref[...], preferred_element_type=F32)
    q_ref[...] = (a[:, 0:512] * (HEAD_DIM ** -0.5 * LOG2E)).astype(BF16)
    k_ref[...] = a[:, 512:1024].astype(BF16)
    v_ref[...] = a[:, 1024:1536].astype(BF16)
    qi_ref[...] = a[:, 1536:2048].astype(BF16)

    b = jnp.dot(hb, wb_ref[...], preferred_element_type=F32)
    kid_ref[...] = b[:, 0:LANES].astype(BF16)
    wi_ref[...] = b[:, LANES:2 * LANES] * ((IDX_DIM ** -0.5) * (IDX_HEADS ** -0.5))

    cpart = jnp.dot(hb, wc_ref[...], preferred_element_type=F32)
    cb = cpart[:, 0:512]
    u = cpart[:, 512:1024] * cpart[:, 1024:1536]

    @pl.when(pl.program_id(1) == 0)
    def _():
        ubuf[0:8, :] = jnp.zeros((8, CONV_WIDTH), F32)

    ubuf[8:tm + 8, :] = u
    conv = (cw_ref[0:1, :] * ubuf[6:tm + 6, :] + cw_ref[1:2, :] * ubuf[7:tm + 7, :]
            + cw_ref[2:3, :] * u)
    z_ref[...] = (cb * conv).astype(BF16)
    ubuf[0:8, :] = ubuf[tm:tm + 8, :]

    d = jnp.dot(hb, wd_ref[...], preferred_element_type=F32)
    gate_ref[...] = jax.nn.sigmoid(d).astype(BF16)


def _proj_call(x, mod, g1, wa, wb, wc, wd, cw, tm):
    bsz, s, d = x.shape
    nt = s // tm
    tok = lambda w: pl.BlockSpec((None, tm, w), lambda b, t: (b, t, 0))
    full = lambda arr: pl.BlockSpec(arr.shape, lambda b, t: (0,) * arr.ndim)
    out_widths = [(512, BF16), (512, BF16), (512, BF16), (512, BF16), (LANES, BF16), (LANES, F32),
                  (512, BF16), (2 * D_MODEL, BF16)]
    return pl.pallas_call(
        functools.partial(_proj_kernel, tm=tm),
        grid=(bsz, nt),
        in_specs=[tok(d), pl.BlockSpec((None, 6, d), lambda b, t: (b, 0, 0)), full(g1),
                  full(wa), full(wb), full(wc), full(wd), full(cw)],
        out_specs=[tok(w) for w, _ in out_widths],
        out_shape=[jax.ShapeDtypeStruct((bsz, s, w), dt) for w, dt in out_widths],
        scratch_shapes=[pltpu.VMEM((tm + 8, CONV_WIDTH), F32)],
        compiler_params=pltpu.CompilerParams(dimension_semantics=("arbitrary", "arbitrary"),
                                             vmem_limit_bytes=VMEM_LIMIT),
        name="proj",
    )(x, mod, g1, wa, wb, wc, wd, cw)


def _to_key(f):
    b = pltpu.bitcast(f, I32)
    return jnp.where(b < 0, b ^ 0x7FFFFFFF, b)


def _from_key(k):
    return pltpu.bitcast(jnp.where(k < 0, k ^ 0x7FFFFFFF, k), F32)


def _fold_lanes(x, op):
    parts = [x[:, t * LANES:(t + 1) * LANES] for t in range(x.shape[1] // LANES)]
    while len(parts) > 1:
        nxt = [op(parts[a], parts[a + 1]) for a in range(0, len(parts) - 1, 2)]
        parts = nxt + ([parts[-1]] if len(parts) % 2 else [])
    return parts[0]


def _tile_lanes(x, n):
    return x if n == 1 else jnp.concatenate([x] * n, axis=1)


def _attn_kernel(rb_ref, q_ref, qi_ref, wi_ref, k_ref, v_ref, kid_ref, o_ref,
                 sct_ref, mb_ref, tb_ref, qm_ref, qim_ref, m_ref, l_ref, acc_ref, *, tq, topk):
    c = tq
    i = pl.program_id(1)
    nt_dims = (((1,), (1,)), ((), ()))
    lane = lax.broadcasted_iota(I32, (tq, LANES), 1)
    low_half = lane < HEAD_DIM
    row = lax.broadcasted_iota(I32, (tq, c), 0)
    col = lax.broadcasted_iota(I32, (tq, c), 1)

    @pl.when((pl.program_id(0) == 0) & (i == 0))
    def _():
        for kind in range(2):
            rel = row - col + kind * c
            n = jnp.maximum(rel, 0)
            for h in range(N_HEADS):
                t = jnp.full((tq, c), rb_ref[0, h], F32)
                for b in range(1, N_BUCKETS):
                    t = jnp.where(n >= _BUCKET_START[b], rb_ref[b, h], t)
                t = (t - rb_ref[N_BUCKETS - 1, h]) * LOG2E
                tb_ref[h, kind] = jnp.where(rel >= 0, t, MASK_NEG)

    for h in range(N_HEADS):
        hp, half = divmod(h, 2)
        keep = low_half if half == 0 else jnp.logical_not(low_half)
        qp = q_ref[:, hp * LANES:(hp + 1) * LANES]
        qm_ref[h] = jnp.where(keep, qp, jnp.zeros_like(qp))
        qip = qi_ref[:, hp * LANES:(hp + 1) * LANES]
        qim_ref[h] = jnp.where(keep, qip, jnp.zeros_like(qip))

    w_t = jnp.transpose(wi_ref[...])
    krow = lax.broadcasted_iota(I32, (c, tq), 0)
    qcol = lax.broadcasted_iota(I32, (c, tq), 1)

    def fold_rows(x, op):
        parts = [x[r * 8:(r + 1) * 8, :] for r in range(x.shape[0] // 8)]
        while len(parts) > 1:
            nxt = [op(parts[a], parts[a + 1]) for a in range(0, len(parts) - 1, 2)]
            parts = nxt + ([parts[-1]] if len(parts) % 2 else [])
        return parts[0]

    def score_chunk(j, diag):
        start = pl.multiple_of(j * c, c)
        kc = kid_ref[pl.ds(start, c), :]
        acc = jnp.zeros((c, tq), F32)
        for h in range(N_HEADS):
            d = lax.dot_general(kc, qim_ref[h], nt_dims, preferred_element_type=F32)
            acc = acc + jnp.maximum(d, 0.0) * w_t[h:h + 1, :]
        if diag:
            causal = krow <= qcol
            lo_src = jnp.where(causal, acc, -SCORE_NEG)
            acc = jnp.where(causal, acc, SCORE_NEG)
        else:
            lo_src = acc
        sct_ref[pl.ds(start, c), :] = acc
        return fold_rows(acc, jnp.maximum), fold_rows(lo_src, jnp.minimum)

    def score_body(j, carry):
        mx, mn = score_chunk(j, False)
        return jnp.maximum(carry[0], mx), jnp.minimum(carry[1], mn)

    pmax, pmin = lax.fori_loop(0, i, score_body, (jnp.full((8, tq), SCORE_NEG, F32),
                                                  jnp.full((8, tq), -SCORE_NEG, F32)))
    mx, mn = score_chunk(i, True)
    rmax = jnp.max(jnp.maximum(pmax, mx), axis=0, keepdims=True)
    rmin = jnp.min(jnp.minimum(pmin, mn), axis=0, keepdims=True)

    nchunks = i + 1
    n_causal = i * tq + lax.broadcasted_iota(I32, (1, tq), 1) + 1

    def count_keys(pred):
        def body(j, part):
            start = pl.multiple_of(j * c, c)
            hit = pred(sct_ref[pl.ds(start, c), :], start)
            return part + fold_rows(jnp.where(hit, 1.0, 0.0), jnp.add)
        part = lax.fori_loop(0, nchunks, body, jnp.zeros((8, tq), F32))
        return jnp.sum(part, axis=0, keepdims=True)

    kf = float(topk)
    done0 = jnp.where(n_causal <= topk, 1, 0)

    def cond(st):
        return st[0] > 0

    def step(st):
        _, it, lo, hi, c_lo, c_hi, done, zp = st
        mid = 0.5 * lo + 0.5 * hi
        lo_k, hi_k = _to_key(lo), _to_key(hi)
        mid_key = _from_key((lo_k >> 1) + (hi_k >> 1) + (lo_k & hi_k & 1))
        mid = jnp.where(it >= KEY_SPACE_AFTER, mid_key, mid)
        probe0 = (zp == 0) & (lo < 0.0) & (hi > 0.0)
        probe1 = (zp <= 1) & (lo == 0.0) & (hi > TINY_F32)
        mid = jnp.where(probe0, 0.0, jnp.where(probe1, TINY_F32, mid))
        zp = jnp.where(probe0, 1, jnp.where(probe1, 2, zp))
        go = (done == 0) & (mid > lo) & (mid < hi)
        cnt = count_keys(lambda tile, start: tile >= mid)
        ge = cnt >= kf
        up = ge & go
        dn = jnp.logical_not(ge) & go
        lo = jnp.where(up, mid, lo)
        hi = jnp.where(dn, mid, hi)
        c_lo = jnp.where(up, cnt, c_lo)
        c_hi = jnp.where(dn, cnt, c_hi)
        done = jnp.where(go & (c_lo != kf), 0, 1)
        return jnp.sum(1 - done), it + 1, lo, hi, c_lo, c_hi, done, zp

    _, _, thr, _, c_lo, c_hi, _, _ = lax.while_loop(
        cond, step, (jnp.sum(1 - done0), jnp.int32(0), rmin, _from_key(_to_key(rmax) + 1),
                     n_causal.astype(F32), jnp.zeros((1, tq), F32), done0, jnp.zeros((1, tq), I32)))

    tied = (n_causal > topk) & (c_lo > kf)
    any_tied = jnp.sum(jnp.where(tied, 1, 0)) > 0

    @pl.when(any_tied)
    def _():
        n_take = kf - c_hi

        def bis(_, st):
            lo_x, hi_x = st
            mid = (lo_x + hi_x) >> 1
            ok = count_keys(lambda tile, start: (tile == thr) & ((krow + start) <= mid)) >= n_take
            return jnp.where(ok, lo_x, mid), jnp.where(ok, mid, hi_x)

        nbits = int(math.ceil(math.log2(sct_ref.shape[0]))) + 1
        _, hi_x = lax.fori_loop(0, nbits, bis, (jnp.full((1, tq), -1, I32),
                                                jnp.full((1, tq), sct_ref.shape[0], I32)))
        cut = jnp.where(tied, hi_x, jnp.int32(2 ** 30))

        def mask_body(j, _):
            start = pl.multiple_of(j * c, c)
            tile = sct_ref[pl.ds(start, c), :]
            sel = (tile > thr) | ((tile == thr) & ((krow + start) <= cut))
            mb_ref[:, pl.ds(start, c)] = jnp.transpose(jnp.where(sel, 0.0, MASK_NEG))
            return 0

        lax.fori_loop(0, nchunks, mask_body, 0)

    @pl.when(jnp.logical_not(any_tied))
    def _():
        def mask_body(j, _):
            start = pl.multiple_of(j * c, c)
            tile = sct_ref[pl.ds(start, c), :]
            mb_ref[:, pl.ds(start, c)] = jnp.transpose(jnp.where(tile >= thr, 0.0, MASK_NEG))
            return 0

        lax.fori_loop(0, nchunks, mask_body, 0)

    m_ref[...] = jnp.full(m_ref.shape, MASK_NEG, F32)
    l_ref[...] = jnp.zeros(l_ref.shape, F32)
    acc_ref[...] = jnp.zeros(acc_ref.shape, F32)

    def attend(start, width, kind):
        nt = width // LANES
        ones = jnp.ones((width, LANES), BF16)
        for h in range(N_HEADS):
            hp = h // 2
            kp = k_ref[pl.ds(start, width), hp * LANES:(hp + 1) * LANES]
            vp = v_ref[pl.ds(start, width), hp * LANES:(hp + 1) * LANES]
            s = lax.dot_general(qm_ref[h], kp, nt_dims, preferred_element_type=F32)
            s = s + mb_ref[:, pl.ds(start, width)]
            if kind is not None:
                s = s + tb_ref[h, kind]
            m_old = m_ref[h]
            m_new = jnp.maximum(m_old, jnp.max(_fold_lanes(s, jnp.maximum), axis=1, keepdims=True))
            p = jnp.exp2(s - _tile_lanes(m_new, nt)).astype(BF16)
            pv = jnp.dot(p, jnp.concatenate([vp, ones], axis=1), preferred_element_type=F32)
            alpha = jnp.exp2(m_old - m_new)
            m_ref[h] = m_new
            l_ref[h] = alpha * l_ref[h] + pv[:, LANES:]
            acc_ref[h] = alpha * acc_ref[h] + pv[:, :LANES]

    wide = 2 * c
    n_far = jnp.maximum(i - 1, 0)

    def far_body(j, _):
        attend(pl.multiple_of(j * wide, wide), wide, None)
        return 0

    lax.fori_loop(0, n_far // 2, far_body, 0)

    @pl.when(n_far % 2 == 1)
    def _():
        attend(pl.multiple_of((n_far - 1) * c, c), c, None)

    @pl.when(i >= 1)
    def _():
        attend(pl.multiple_of((i - 1) * c, c), c, 1)

    attend(pl.multiple_of(i * c, c), c, 0)

    for hp in range(N_HEADS // 2):
        lo = acc_ref[2 * hp] / l_ref[2 * hp]
        hi = acc_ref[2 * hp + 1] / l_ref[2 * hp + 1]
        o_ref[:, hp * LANES:(hp + 1) * LANES] = jnp.where(low_half, lo, hi).astype(BF16)


def _attn_call(rel_bias, q, qi, wi, k, v, kid, tq):
    bsz, s, _ = q.shape
    nq = s // tq
    topk = min(TOPK_MAX, s // 4)
    assert tq % LANES == 0 and tq >= MAX_DISTANCE
    blk = lambda w: pl.BlockSpec((None, tq, w), lambda b, i: (b, i, 0))
    whole = lambda w: pl.BlockSpec((None, s, w), lambda b, i: (b, 0, 0), pipeline_mode=pl.Buffered(1))
    return pl.pallas_call(
        functools.partial(_attn_kernel, tq=tq, topk=topk),
        grid=(bsz, nq),
        in_specs=[pl.BlockSpec(memory_space=pltpu.SMEM),
                  blk(ATTN_WIDTH), blk(ATTN_WIDTH), blk(LANES),
                  whole(ATTN_WIDTH), whole(ATTN_WIDTH), whole(LANES)],
        out_specs=blk(ATTN_WIDTH),
        out_shape=jax.ShapeDtypeStruct((bsz, s, ATTN_WIDTH), BF16),
        scratch_shapes=[
            pltpu.VMEM((s, tq), F32),
            pltpu.VMEM((tq, s), F32),
            pltpu.VMEM((N_HEADS, 2, tq, tq), F32),
            pltpu.VMEM((N_HEADS, tq, LANES), BF16),
            pltpu.VMEM((N_HEADS, tq, LANES), BF16),
            pltpu.VMEM((N_HEADS, tq, LANES), F32),
            pltpu.VMEM((N_HEADS, tq, LANES), F32),
            pltpu.VMEM((N_HEADS, tq, LANES), F32),
        ],
        compiler_params=pltpu.CompilerParams(dimension_semantics=("arbitrary", "arbitrary"),
                                             vmem_limit_bytes=VMEM_LIMIT),
        name="attn",
    )(rel_bias, q, qi, wi, k, v, kid)


def _post_kernel(attn_ref, z_ref, gate_ref, x_ref, mod_ref, wa_ref, wc_ref, wo_ref, g2_ref,
                 wr_ref, br_ref, hres_ref, h2_ref, eid_ref, ew_ref, counts_ref, cnt_ref):
    ya = jnp.dot(attn_ref[...], wa_ref[...], preferred_element_type=F32)
    yc = jnp.dot(z_ref[...], wc_ref[...], preferred_element_type=F32)
    merged = gate_ref[:, 0:D_MODEL].astype(F32) * ya + gate_ref[:, D_MODEL:].astype(F32) * yc
    o = jnp.dot(merged.astype(BF16), wo_ref[...], preferred_element_type=F32)
    hres = x_ref[...] + mod_ref[2:3, :] * o
    hres_ref[...] = hres
    h2 = _rms_mod(hres, g2_ref[...], mod_ref[4:5, :], mod_ref[3:4, :])
    h2_ref[...] = h2.astype(BF16).reshape(h2_ref.shape)

    h2_hi = h2.astype(BF16)
    h2_lo = (h2 - h2_hi.astype(F32)).astype(BF16)
    lg = (jnp.dot(h2_hi, wr_ref[0], preferred_element_type=F32)
          + (jnp.dot(h2_lo, wr_ref[0], preferred_element_type=F32)
             + jnp.dot(h2_hi, wr_ref[1], preferred_element_type=F32))) + br_ref[...]
    tm = lg.shape[0]
    lane = lax.broadcasted_iota(I32, (tm, LANES), 1)
    ninf = -jnp.inf
    gmask = (lane >= N_EXPERTS) & (lane < N_EXPERTS + N_GROUPS)
    gl = jnp.where(gmask, lg, ninf)
    gmax = jnp.max(gl, axis=1, keepdims=True)
    g_sel = jnp.min(jnp.where(gmask & (gl == gmax), lane - N_EXPERTS, N_GROUPS), axis=1, keepdims=True)
    g_w = 1.0 / jnp.sum(jnp.where(gmask, jnp.exp(gl - gmax), 0.0), axis=1, keepdims=True)

    emask = (lane < N_EXPERTS) & ((lane // EXPERTS_PER_GROUP) == g_sel)
    el = jnp.where(emask, lg, ninf)
    emax = jnp.max(el, axis=1, keepdims=True)
    ee = jnp.where(emask, jnp.exp(el - emax), 0.0)
    e_prob = ee / jnp.sum(ee, axis=1, keepdims=True)
    p1 = jnp.max(jnp.where(emask, e_prob, -1.0), axis=1, keepdims=True)
    i1 = jnp.min(jnp.where(emask & (e_prob == p1), lane, LANES), axis=1, keepdims=True)
    rest = emask & (lane != i1)
    p2 = jnp.max(jnp.where(rest, e_prob, -1.0), axis=1, keepdims=True)
    i2 = jnp.min(jnp.where(rest & (e_prob == p2), lane, LANES), axis=1, keepdims=True)
    psum = p1 + p2
    w1 = g_w * (p1 / psum)
    w2 = g_w * (p2 / psum)
    ew_ref[...] = jnp.where(lane == 0, w1, jnp.where(lane == 1, w2, 0.0))

    @pl.when((pl.program_id(0) == 0) & (pl.program_id(1) == 0))
    def _():
        cnt_ref[...] = jnp.zeros(cnt_ref.shape, F32)

    onehot = jnp.where((lane == i1) | (lane == i2), 1.0, 0.0)
    tri = (lax.broadcasted_iota(I32, (tm, tm), 0) >= lax.broadcasted_iota(I32, (tm, tm), 1))
    cum = jnp.dot(jnp.where(tri, 1.0, 0.0).astype(BF16), onehot.astype(BF16),
                  preferred_element_type=F32)
    before = cum - onehot + cnt_ref[0:1, :]
    r1 = jnp.sum(jnp.where(lane == i1, before, 0.0), axis=1, keepdims=True).astype(I32)
    r2 = jnp.sum(jnp.where(lane == i2, before, 0.0), axis=1, keepdims=True).astype(I32)
    eid_ref[...] = jnp.where(lane == 0, i1, jnp.where(lane == 1, i2,
                             jnp.where(lane == 2, r1, jnp.where(lane == 3, r2, 0))))
    total = cnt_ref[0:1, :] + cum[tm - 1:tm, :]
    cnt_ref[...] = jnp.broadcast_to(total, cnt_ref.shape)
    counts_ref[...] = jnp.broadcast_to(total, counts_ref.shape)


def _post_call(attn, z, gates, x, mod, wa, wc, wo, g2, wr, br, tm):
    bsz, s, d = x.shape
    nt = s // tm
    tok = lambda w: pl.BlockSpec((None, tm, w), lambda b, t: (b, t, 0))
    full = lambda arr: pl.BlockSpec(arr.shape, lambda b, t: (0,) * arr.ndim)
    return pl.pallas_call(
        _post_kernel,
        grid=(bsz, nt),
        in_specs=[tok(ATTN_WIDTH), tok(CONV_WIDTH), tok(2 * d), tok(d),
                  pl.BlockSpec((None, 6, d), lambda b, t: (b, 0, 0)),
                  full(wa), full(wc), full(wo), full(g2), full(wr), full(br)],
        out_specs=[tok(d), pl.BlockSpec((None, tm) + ROW_TILE, lambda b, t: (b, t, 0, 0)),
                   tok(LANES), tok(LANES), pl.BlockSpec((8, LANES), lambda b, t: (0, 0))],
        out_shape=[jax.ShapeDtypeStruct((bsz, s, d), F32),
                   jax.ShapeDtypeStruct((bsz, s) + ROW_TILE, BF16),
                   jax.ShapeDtypeStruct((bsz, s, LANES), I32), jax.ShapeDtypeStruct((bsz, s, LANES), F32),
                   jax.ShapeDtypeStruct((8, LANES), F32)],
        scratch_shapes=[pltpu.VMEM((8, LANES), F32)],
        compiler_params=pltpu.CompilerParams(dimension_semantics=("arbitrary", "arbitrary"),
                                             vmem_limit_bytes=VMEM_LIMIT),
        name="post",
    )(attn, z, gates, x, mod, wa, wc, wo, g2, wr, br)


ROW_TILE = (D_MODEL // LANES, LANES)
EXPERT_ROWS = 256
TOKENS_PER_STEP = 256


ISSUE_UNROLL = 8


def _dispatch_kernel(pos1_ref, pos2_ref, h2_ref, xs_in_ref, xs_ref, sem):
    del xs_in_ref
    tm = h2_ref.shape[0]
    base = pl.program_id(0) * tm

    def issue(r, c):
        pltpu.make_async_copy(h2_ref.at[r], xs_ref.at[pos1_ref[base + r]], sem).start()
        pltpu.make_async_copy(h2_ref.at[r], xs_ref.at[pos2_ref[base + r]], sem).start()
        return c

    lax.fori_loop(0, tm, issue, 0, unroll=ISSUE_UNROLL)
    for _ in range(2):
        pltpu.make_async_copy(h2_ref, xs_ref.at[pl.ds(0, tm)], sem).wait()


def _dispatch_call(pos1, pos2, h2_rows, n_rows):
    t = h2_rows.shape[0]
    tm = min(TOKENS_PER_STEP, t)
    xs0 = jnp.zeros((n_rows,) + ROW_TILE, BF16)
    return pl.pallas_call(
        _dispatch_kernel,
        grid_spec=pltpu.PrefetchScalarGridSpec(
            num_scalar_prefetch=2,
            grid=(t // tm,),
            in_specs=[pl.BlockSpec((tm,) + ROW_TILE, lambda i, p1, p2: (i, 0, 0)),
                      pl.BlockSpec(memory_space=pl.ANY)],
            out_specs=pl.BlockSpec(memory_space=pl.ANY),
            scratch_shapes=[pltpu.SemaphoreType.DMA(())]),
        out_shape=jax.ShapeDtypeStruct((n_rows,) + ROW_TILE, BF16),
        input_output_aliases={3: 0},
        compiler_params=pltpu.CompilerParams(dimension_semantics=("arbitrary",),
                                             vmem_limit_bytes=VMEM_LIMIT),
        name="dispatch",
    )(pos1, pos2, h2_rows, xs0)


def _expert_kernel(te_ref, nu_ref, xs_ref, wg_ref, wu_ref, wd_ref, ys_ref, wgb, wub, wdb):
    g = pl.program_id(0)
    e = te_ref[g]
    e_prev = te_ref[jnp.maximum(g - 1, 0)]

    @pl.when((g == 0) | (e != e_prev))
    def _():
        wgb[...] = wg_ref[...].astype(BF16)
        wub[...] = wu_ref[...].astype(BF16)
        wdb[...] = wd_ref[...].astype(BF16)

    @pl.when(g < nu_ref[0])
    def _():
        x = xs_ref[...].reshape(xs_ref.shape[0], D_MODEL)
        a = jnp.dot(x, wgb[...], preferred_element_type=F32)
        b = jnp.dot(x, wub[...], preferred_element_type=F32)
        hid = (a * jax.nn.sigmoid(a)) * b
        y = jnp.dot(hid.astype(BF16), wdb[...], preferred_element_type=F32)
        ys_ref[...] = y.astype(BF16).reshape(ys_ref.shape)

    @pl.when(g >= nu_ref[0])
    def _():
        ys_ref[...] = jnp.zeros(ys_ref.shape, BF16)


def _expert_call(tile_expert, n_used, xs, wg, wu, wd):
    n_rows, d = xs.shape[0], D_MODEL
    nt = n_rows // EXPERT_ROWS
    rows = pl.BlockSpec((EXPERT_ROWS,) + ROW_TILE, lambda g, te, nu: (g, 0, 0))
    wspec = lambda shp: pl.BlockSpec((None,) + shp, lambda g, te, nu: (te[g], 0, 0))
    return pl.pallas_call(
        _expert_kernel,
        grid_spec=pltpu.PrefetchScalarGridSpec(
            num_scalar_prefetch=2,
            grid=(nt,),
            in_specs=[rows, wspec((d, D_EXPERT)), wspec((d, D_EXPERT)), wspec((D_EXPERT, d))],
            out_specs=rows,
            scratch_shapes=[pltpu.VMEM((d, D_EXPERT), BF16), pltpu.VMEM((d, D_EXPERT), BF16),
                            pltpu.VMEM((D_EXPERT, d), BF16)]),
        out_shape=jax.ShapeDtypeStruct((n_rows,) + ROW_TILE, BF16),
        compiler_params=pltpu.CompilerParams(dimension_semantics=("arbitrary",),
                                             vmem_limit_bytes=VMEM_LIMIT),
        name="experts",
    )(tile_expert, n_used, xs, wg, wu, wd)


def _combine_kernel(pos1_ref, pos2_ref, ys_ref, ew_ref, hres_ref, g2_ref, gf_ref, o_ref, buf, sem):
    tm = hres_ref.shape[0]
    t = pl.program_id(0)
    nt = pl.num_programs(0)

    def issue(tile, slot):
        base = tile * tm

        def body(r, c):
            pltpu.make_async_copy(ys_ref.at[pos1_ref[base + r]], buf.at[slot, r], sem.at[slot]).start()
            pltpu.make_async_copy(ys_ref.at[pos2_ref[base + r]], buf.at[slot, tm + r], sem.at[slot]).start()
            return c

        lax.fori_loop(0, tm, body, 0, unroll=ISSUE_UNROLL)

    @pl.when(t == 0)
    def _():
        issue(0, 0)

    @pl.when(t + 1 < nt)
    def _():
        issue(t + 1, (t + 1) % 2)

    slot = t % 2
    pltpu.make_async_copy(ys_ref.at[pl.ds(0, 2 * tm)], buf.at[slot], sem.at[slot]).wait()

    y1 = buf[slot, 0:tm].astype(F32).reshape(tm, D_MODEL)
    y2 = buf[slot, tm:2 * tm].astype(F32).reshape(tm, D_MODEL)
    w = ew_ref[...]
    y = w[:, 0:1] * y1 + w[:, 1:2] * y2
    hfin = hres_ref[...] + g2_ref[...] * y
    ms = jnp.mean(hfin * hfin, axis=-1, keepdims=True)
    o_ref[...] = hfin * lax.rsqrt(ms + EPS) * gf_ref[...]


def _combine_call(pos1, pos2, ys_rows, ew, hres, gate2, gf):
    t, d = hres.shape
    tm = min(TOKENS_PER_STEP, t)
    per_batch = t // gate2.shape[0]
    assert per_batch % tm == 0
    tok = lambda w: pl.BlockSpec((tm, w), lambda i, *_: (i, 0))
    return pl.pallas_call(
        _combine_kernel,
        grid_spec=pltpu.PrefetchScalarGridSpec(
            num_scalar_prefetch=2,
            grid=(t // tm,),
            in_specs=[pl.BlockSpec(memory_space=pl.ANY), tok(LANES), tok(d),
                      pl.BlockSpec((None, 1, d), lambda i, *_: ((i * tm) // per_batch, 0, 0)),
                      pl.BlockSpec((1, d), lambda i, *_: (0, 0))],
            out_specs=tok(d),
            scratch_shapes=[pltpu.VMEM((2, 2 * tm) + ROW_TILE, BF16), pltpu.SemaphoreType.DMA((2,))]),
        out_shape=jax.ShapeDtypeStruct((t, d), F32),
        compiler_params=pltpu.CompilerParams(dimension_semantics=("arbitrary",),
                                             vmem_limit_bytes=VMEM_LIMIT),
        name="combine",
    )(pos1, pos2, ys_rows, ew, hres, gate2, gf)


def _sparse_moe(h2, eid, ew, counts, wg, wu, wd, hres, gate2, gf):
    bsz, s, d = hres.shape
    t = bsz * s
    ne = wg.shape[0]
    e1, e2 = eid[..., 0].reshape(t), eid[..., 1].reshape(t)
    r1, r2 = eid[..., 2].reshape(t), eid[..., 3].reshape(t)
    cnt = counts[0, :ne].astype(I32)
    seg_rows = ((cnt + EXPERT_ROWS - 1) // EXPERT_ROWS) * EXPERT_ROWS
    seg_end = jnp.cumsum(seg_rows)
    seg_start = seg_end - seg_rows
    pos1 = seg_start[e1] + r1
    pos2 = seg_start[e2] + r2
    n_tiles = (2 * t) // EXPERT_ROWS + ne
    tile_first_row = jnp.arange(n_tiles, dtype=I32) * EXPERT_ROWS
    tile_expert = jnp.minimum(jnp.sum(seg_end[None, :] <= tile_first_row[:, None], axis=1), ne - 1)
    n_used = (seg_end[ne - 1] // EXPERT_ROWS).reshape(1)
    n_rows = n_tiles * EXPERT_ROWS

    xs = _dispatch_call(pos1, pos2, h2.reshape((t,) + ROW_TILE), n_rows)
    ys = _expert_call(tile_expert.astype(I32), n_used.astype(I32), xs, wg, wu, wd)
    out = _combine_call(pos1, pos2, ys, ew.reshape(t, LANES), hres.reshape(t, d),
                        gate2.reshape(bsz, 1, d), gf.reshape(1, d))
    return out.reshape(bsz, s, d)


def kernel(x, c, w_ada, b_ada, norm1_g, w_in, rel_bias, conv_w, w_attn_branch, w_conv_branch, w_out,
           norm2_g, w_router_group, b_router_group, w_router_expert, b_router_expert,
           w_gate_e, w_up_e, w_down_e, norm_f_g):
    bsz, s, d = x.shape
    depth = w_ada.shape[0]
    h = x.astype(F32)
    c_pad = jnp.zeros((8, d), F32).at[:bsz].set(c.astype(F32))
    out = h
    for l in range(depth):
        mod = _ada_call(c_pad, w_ada[l].astype(F32), b_ada[l].astype(F32)[None, :])
        mod = mod[:bsz].reshape(bsz, 6, d)

        w = w_in[l]
        o_q, o_k, o_v, o_qi = 0, 512, 1024, 1536
        o_ki, o_wi, o_cb, o_gl = 2048, 2112, 2120, 3656
        wa = w[:, o_q:o_ki].astype(BF16)
        ki_w = w[:, o_ki:o_wi]
        wb = jnp.concatenate([ki_w, ki_w, w[:, o_wi:o_cb],
                              jnp.zeros((d, LANES - IDX_HEADS), w.dtype)], axis=1).astype(BF16)
        wc = w[:, o_cb:o_gl].astype(BF16)
        wd = w[:, o_gl:].astype(BF16)

        q, k, v, qi, kid, wi, z, gates = _proj_call(
            h, mod, norm1_g[l].astype(F32)[None, :], wa, wb, wc, wd, conv_w[l].astype(F32),
            tm=min(512, s))

        attn = _attn_call(rel_bias.astype(F32), q, qi, wi, k, v, kid, tq=min(TOPK_MAX, s // 4))

        wr32 = jnp.concatenate([w_router_expert[l], w_router_group[l],
                                jnp.zeros((d, LANES - N_EXPERTS - N_GROUPS), F32)], axis=1).astype(F32)
        wr_hi = wr32.astype(BF16)
        wr = jnp.stack([wr_hi, (wr32 - wr_hi.astype(F32)).astype(BF16)])
        br = jnp.concatenate([b_router_expert[l], b_router_group[l],
                              jnp.zeros((LANES - N_EXPERTS - N_GROUPS,), F32)])[None, :].astype(F32)
        hres, h2, eid, ew, counts = _post_call(
            attn, z, gates, h, mod, w_attn_branch[l].astype(BF16), w_conv_branch[l].astype(BF16),
            w_out[l].astype(BF16), norm2_g[l].astype(F32)[None, :], wr, br, tm=min(512, s))

        assert depth == 1, "the final RMSNorm is fused into the combine kernel of the only layer"
        out = _sparse_moe(h2, eid, ew, counts, w_gate_e[l].astype(F32), w_up_e[l].astype(F32),
                          w_down_e[l].astype(F32), hres, mod[:, 5, :], norm_f_g.astype(F32))
        h = out
    return out.astype(x.dtype)
```

```python
import functools
import math

import numpy as np
import jax
import jax.numpy as jnp
from jax import lax
from jax.experimental import pallas as pl
from jax.experimental.pallas import tpu as pltpu

F32 = jnp.float32
BF16 = jnp.bfloat16
I32 = jnp.int32

D_MODEL = 1024
N_HEADS = 8
HEAD_DIM = 64
ATTN_WIDTH = N_HEADS * HEAD_DIM
IDX_HEADS = 8
IDX_DIM = 64
TOPK_MAX = 256
N_BUCKETS = 32
MAX_DISTANCE = 128
CONV_WIDTH = 512
CONV_K = 3
N_GROUPS = 4
EXPERTS_PER_GROUP = 8
N_EXPERTS = N_GROUPS * EXPERTS_PER_GROUP
D_EXPERT = 512
EPS = 1e-6

LANES = 128
VMEM_LIMIT = 56 * 1024 * 1024

LOG2E = math.log2(math.e)
MASK_NEG = -1e30
SCORE_NEG = float(np.finfo(np.float32).min)
TINY_F32 = float(np.finfo(np.float32).tiny)
KEY_SPACE_AFTER = 40


def _t5_bucket_starts():
    max_exact = N_BUCKETS // 2
    n = np.arange(0, MAX_DISTANCE + 1)
    nf = np.maximum(n, 1).astype(np.float32)
    val = (np.log(nf / np.float32(max_exact)) / np.float32(math.log(MAX_DISTANCE / max_exact))
           * np.float32(N_BUCKETS - max_exact)).astype(np.float32)
    inner = val[max_exact + 1:MAX_DISTANCE]
    assert np.min(np.abs(inner - np.round(inner))) > 1e-3
    large = np.minimum(max_exact + val.astype(np.int32), N_BUCKETS - 1)
    bucket = np.where(n < max_exact, n, large)
    assert bucket[MAX_DISTANCE] == N_BUCKETS - 1 and np.all(np.diff(bucket) >= 0)
    return [int(np.argmax(bucket >= b)) for b in range(N_BUCKETS)]


_BUCKET_START = _t5_bucket_starts()


def _ada_kernel(c_ref, w_ref, b_ref, o_ref):
    c = c_ref[...]
    ca = c * jax.nn.sigmoid(c)
    o_ref[...] = jnp.dot(ca, w_ref[...], preferred_element_type=F32,
                         precision=lax.Precision.HIGHEST) + b_ref[...]


def _ada_call(c_pad, w, b):
    rows, d = c_pad.shape
    n = w.shape[1]
    tn = 1536
    return pl.pallas_call(
        _ada_kernel,
        grid=(n // tn,),
        in_specs=[pl.BlockSpec((rows, d), lambda j: (0, 0)),
                  pl.BlockSpec((d, tn), lambda j: (0, j)),
                  pl.BlockSpec((1, tn), lambda j: (0, j))],
        out_specs=pl.BlockSpec((rows, tn), lambda j: (0, j)),
        out_shape=jax.ShapeDtypeStruct((rows, n), F32),
        compiler_params=pltpu.CompilerParams(dimension_semantics=("arbitrary",),
                                             vmem_limit_bytes=VMEM_LIMIT),
        name="ada",
    )(c_pad, w, b)


def _rms_mod(x, g, scale, shift):
    ms = jnp.mean(x * x, axis=-1, keepdims=True)
    y = x * lax.rsqrt(ms + EPS) * g
    return y * (1.0 + scale) + shift


def _proj_kernel(x_ref, mod_ref, g_ref, wa_ref, wb_ref, wc_ref, wd_ref, cw_ref,
                 q_ref, k_ref, v_ref, qi_ref, kid_ref, wi_ref, z_ref, gate_ref, ubuf, *, tm):
    h = _rms_mod(x_ref[...], g_ref[...], mod_ref[1:2, :], mod_ref[0:1, :])
    hb = h.astype(BF16)

    a = jnp.dot(hb, wa_ref[...], preferred_element_type=F32)
    q_ref[...] = (a[:, 0:512] * (HEAD_DIM ** -0.5 * LOG2E)).astype(BF16)
    k_ref[...] = a[:, 512:1024].astype(BF16)
    v_ref[...] = a[:, 1024:1536].astype(BF16)
    qi_ref[...] = a[:, 1536:2048].astype(BF16)

    b = jnp.dot(hb, wb_ref[...], preferred_element_type=F32)
    kid_ref[...] = b[:, 0:LANES].astype(BF16)
    wi_ref[...] = b[:, LANES:2 * LANES] * ((IDX_DIM ** -0.5) * (IDX_HEADS ** -0.5))

    cpart = jnp.dot(hb, wc_ref[...], preferred_element_type=F32)
    cb = cpart[:, 0:512]
    u = cpart[:, 512:1024] * cpart[:, 1024:1536]

    @pl.when(pl.program_id(1) == 0)
    def _():
        ubuf[0:8, :] = jnp.zeros((8, CONV_WIDTH), F32)

    ubuf[8:tm + 8, :] = u
    conv = (cw_ref[0:1, :] * ubuf[6:tm + 6, :] + cw_ref[1:2, :] * ubuf[7:tm + 7, :]
            + cw_ref[2:3, :] * u)
    z_ref[...] = (cb * conv).astype(BF16)
    ubuf[0:8, :] = ubuf[tm:tm + 8, :]

    d = jnp.dot(hb, wd_ref[...], preferred_element_type=F32)
    gate_ref[...] = jax.nn.sigmoid(d).astype(BF16)


def _proj_call(x, mod, g1, wa, wb, wc, wd, cw, tm):
    bsz, s, d = x.shape
    nt = s // tm
    tok = lambda w: pl.BlockSpec((None, tm, w), lambda b, t: (b, t, 0))
    full = lambda arr: pl.BlockSpec(arr.shape, lambda b, t: (0,) * arr.ndim)
    out_widths = [(512, BF16), (512, BF16), (512, BF16), (512, BF16), (LANES, BF16), (LANES, F32),
                  (512, BF16), (2 * D_MODEL, BF16)]
    return pl.pallas_call(
        functools.partial(_proj_kernel, tm=tm),
        grid=(bsz, nt),
        in_specs=[tok(d), pl.BlockSpec((None, 6, d), lambda b, t: (b, 0, 0)), full(g1),
                  full(wa), full(wb), full(wc), full(wd), full(cw)],
        out_specs=[tok(w) for w, _ in out_widths],
        out_shape=[jax.ShapeDtypeStruct((bsz, s, w), dt) for w, dt in out_widths],
        scratch_shapes=[pltpu.VMEM((tm + 8, CONV_WIDTH), F32)],
        compiler_params=pltpu.CompilerParams(dimension_semantics=("arbitrary", "arbitrary"),
                                             vmem_limit_bytes=VMEM_LIMIT),
        name="proj",
    )(x, mod, g1, wa, wb, wc, wd, cw)


def _to_key(f):
    b = pltpu.bitcast(f, I32)
    return jnp.where(b < 0, b ^ 0x7FFFFFFF, b)


def _from_key(k):
    return pltpu.bitcast(jnp.where(k < 0, k ^ 0x7FFFFFFF, k), F32)


def _bit_transpose32(words):
    a = list(words)
    j, m = 16, 0x0000FFFF
    while j:
        k = 0
        while k < 32:
            t = (a[k] ^ lax.shift_right_logical(a[k + j], jnp.int32(j))) & jnp.int32(m)
            a[k] = a[k] ^ t
            a[k + j] = a[k + j] ^ lax.shift_left(t, jnp.int32(j))
            k = (k + j + 1) & ~j
        j >>= 1
        m = (m ^ (m << j)) & 0xFFFFFFFF
    return a


def _fold_lanes(x, op):
    parts = [x[:, t * LANES:(t + 1) * LANES] for t in range(x.shape[1] // LANES)]
    while len(parts) > 1:
        nxt = [op(parts[a], parts[a + 1]) for a in range(0, len(parts) - 1, 2)]
        parts = nxt + ([parts[-1]] if len(parts) % 2 else [])
    return parts[0]


def _tile_lanes(x, n):
    return x if n == 1 else jnp.concatenate([x] * n, axis=1)


def _attn_kernel(rb_ref, q_ref, qi_ref, wi_ref, k_ref, v_ref, kid_ref, o_ref,
                 sct_ref, planes_ref, cand_ref, mb_ref, tb_ref, qm_ref, qim_ref, m_ref, l_ref, acc_ref,
                 *, tq, topk):
    c = tq
    i = pl.program_id(1)
    nt_dims = (((1,), (1,)), ((), ()))
    lane = lax.broadcasted_iota(I32, (tq, LANES), 1)
    low_half = lane < HEAD_DIM
    row = lax.broadcasted_iota(I32, (tq, c), 0)
    col = lax.broadcasted_iota(I32, (tq, c), 1)

    @pl.when((pl.program_id(0) == 0) & (i == 0))
    def _():
        for p in range(32):
            planes_ref[p] = jnp.zeros(planes_ref.shape[1:], I32)
        for kind in range(2):
            rel = row - col + kind * c
            n = jnp.maximum(rel, 0)
            for h in range(N_HEADS):
                t = jnp.full((tq, c), rb_ref[0, h], F32)
                for b in range(1, N_BUCKETS):
                    t = jnp.where(n >= _BUCKET_START[b], rb_ref[b, h], t)
                t = (t - rb_ref[N_BUCKETS - 1, h]) * LOG2E
                tb_ref[h, kind] = jnp.where(rel >= 0, t, MASK_NEG)

    for h in range(N_HEADS):
        hp, half = divmod(h, 2)
        keep = low_half if half == 0 else jnp.logical_not(low_half)
        qp = q_ref[:, hp * LANES:(hp + 1) * LANES]
        qm_ref[h] = jnp.where(keep, qp, jnp.zeros_like(qp))
        qip = qi_ref[:, hp * LANES:(hp + 1) * LANES]
        qim_ref[h] = jnp.where(keep, qip, jnp.zeros_like(qip))

    w_t = jnp.transpose(wi_ref[...])
    krow = lax.broadcasted_iota(I32, (c, tq), 0)
    qcol = lax.broadcasted_iota(I32, (c, tq), 1)
    int_min = jnp.int32(-2 ** 31)

    def fold_rows(x, op):
        parts = [x[r * 8:(r + 1) * 8, :] for r in range(x.shape[0] // 8)]
        while len(parts) > 1:
            nxt = [op(parts[a], parts[a + 1]) for a in range(0, len(parts) - 1, 2)]
            parts = nxt + ([parts[-1]] if len(parts) % 2 else [])
        return parts[0]

    def score_chunk(j, diag):
        start = pl.multiple_of(j * c, c)
        kc = kid_ref[pl.ds(start, c), :]
        acc = jnp.zeros((c, tq), F32)
        for h in range(N_HEADS):
            d = lax.dot_general(kc, qim_ref[h], nt_dims, preferred_element_type=F32)
            acc = acc + jnp.maximum(d, 0.0) * w_t[h:h + 1, :]
        ukey = _to_key(acc) ^ int_min
        if diag:
            causal = krow <= qcol
            acc = jnp.where(causal, acc, SCORE_NEG)
            ukey = jnp.where(causal, ukey, 0)
        sct_ref[pl.ds(start, c), :] = acc
        prow = pl.multiple_of(j * (c // 32), c // 32)
        for lt in range(tq // LANES):
            words = [ukey[k * 8:(k + 1) * 8, lt * LANES:(lt + 1) * LANES] for k in range(32)]
            for p, plane in enumerate(_bit_transpose32(words)):
                planes_ref[p, pl.ds(prow, c // 32), lt * LANES:(lt + 1) * LANES] = plane

    def score_body(j, carry):
        score_chunk(j, False)
        return carry

    lax.fori_loop(0, i, score_body, 0)
    score_chunk(i, True)

    nchunks = i + 1
    n_causal = i * tq + lax.broadcasted_iota(I32, (1, tq), 1) + 1
    prow_iota = lax.broadcasted_iota(I32, cand_ref.shape, 0)
    cand_ref[...] = jnp.where(prow_iota < nchunks * (c // 32), -1, 0)

    def radix_cond(st):
        return (st[0] < 32) & (st[1] > 0)

    def radix_step(st):
        p, _, prefix, need, ncand, settled = st
        plane = planes_ref[p]
        cand = cand_ref[...]
        ones = cand & plane
        cnt1 = jnp.sum(fold_rows(lax.population_count(ones), jnp.add), axis=0, keepdims=True)
        take1 = cnt1 >= need
        live = settled == 0
        need = jnp.where(live & jnp.logical_not(take1), need - cnt1, need)
        ncand = jnp.where(live, jnp.where(take1, cnt1, ncand - cnt1), ncand)
        prefix = jnp.where(live & take1, prefix | jnp.left_shift(jnp.int32(1), 31 - p), prefix)
        cand_ref[...] = cand & (plane ^ jnp.where(take1, 0, -1))
        settled = jnp.where(ncand == need, 1, settled)
        return p + 1, jnp.sum(1 - settled), prefix, need, ncand, settled

    done0 = jnp.where(n_causal <= topk, 1, 0)
    _, _, prefix, _, _, _ = lax.while_loop(
        radix_cond, radix_step,
        (jnp.int32(0), jnp.sum(1 - done0), jnp.zeros((1, tq), I32), jnp.full((1, tq), topk, I32),
         jnp.full((1, tq), 1, I32) * (nchunks * c), done0))
    guess_key = prefix ^ int_min
    guess = _from_key(guess_key)

    def count_keys(pred):
        def body(j, part):
            start = pl.multiple_of(j * c, c)
            hit = pred(sct_ref[pl.ds(start, c), :], start)
            return part + fold_rows(jnp.where(hit, 1.0, 0.0), jnp.add)
        part = lax.fori_loop(0, nchunks, body, jnp.zeros((8, tq), F32))
        return jnp.sum(part, axis=0, keepdims=True)

    kf = float(topk)
    guess_above = _from_key(guess_key + 1)
    guess_above = jnp.where(jnp.abs(guess_above) < TINY_F32,
                            jnp.where(guess >= 0.0, TINY_F32, 0.0), guess_above)
    guess_below = _from_key(guess_key - 2)

    def cond(st):
        return st[0] > 0

    def step(st):
        _, it, lo, hi, c_lo, c_hi, done = st
        mid = 0.5 * lo + 0.5 * hi
        lo_k, hi_k = _to_key(lo), _to_key(hi)
        mid_key = _from_key((lo_k >> 1) + (hi_k >> 1) + (lo_k & hi_k & 1))
        mid = jnp.where(it >= KEY_SPACE_AFTER, mid_key, mid)
        probe = jnp.where(it == 0, guess, jnp.where(lo == guess, guess_above, guess_below))
        mid = jnp.where((it <= 1) & (probe > lo) & (probe < hi), probe, mid)
        go = (done == 0) & (mid > lo) & (mid < hi)
        cnt = count_keys(lambda tile, start: tile >= mid)
        ge = cnt >= kf
        up = ge & go
        dn = jnp.logical_not(ge) & go
        lo = jnp.where(up, mid, lo)
        hi = jnp.where(dn, mid, hi)
        c_lo = jnp.where(up, cnt, c_lo)
        c_hi = jnp.where(dn, cnt, c_hi)
        adjacent = (lo == guess) & (hi == guess_above)
        done = jnp.where(go & (c_lo != kf) & jnp.logical_not(adjacent), 0, 1)
        return jnp.sum(1 - done), it + 1, lo, hi, c_lo, c_hi, done

    _, _, thr, _, c_lo, c_hi, _ = lax.while_loop(
        cond, step, (jnp.sum(1 - done0), jnp.int32(0), jnp.full((1, tq), SCORE_NEG, F32),
                     jnp.full((1, tq), -SCORE_NEG, F32), jnp.full((1, tq), 1.0, F32) * (nchunks * c),
                     jnp.zeros((1, tq), F32), done0))

    tied = (n_causal > topk) & (c_lo > kf)
    any_tied = jnp.sum(jnp.where(tied, 1, 0)) > 0

    @pl.when(any_tied)
    def _():
        n_take = kf - c_hi

        def bis(_, st):
            lo_x, hi_x = st
            mid = (lo_x + hi_x) >> 1
            ok = count_keys(lambda tile, start: (tile == thr) & ((krow + start) <= mid)) >= n_take
            return jnp.where(ok, lo_x, mid), jnp.where(ok, mid, hi_x)

        nbits = int(math.ceil(math.log2(sct_ref.shape[0]))) + 1
        _, hi_x = lax.fori_loop(0, nbits, bis, (jnp.full((1, tq), -1, I32),
                                                jnp.full((1, tq), sct_ref.shape[0], I32)))
        cut = jnp.where(tied, hi_x, jnp.int32(2 ** 30))

        def mask_body(j, _):
            start = pl.multiple_of(j * c, c)
            tile = sct_ref[pl.ds(start, c), :]
            sel = (tile > thr) | ((tile == thr) & ((krow + start) <= cut))
            mb_ref[:, pl.ds(start, c)] = jnp.transpose(jnp.where(sel, 0.0, MASK_NEG))
            return 0

        lax.fori_loop(0, nchunks, mask_body, 0)

    @pl.when(jnp.logical_not(any_tied))
    def _():
        def mask_body(j, _):
            start = pl.multiple_of(j * c, c)
            tile = sct_ref[pl.ds(start, c), :]
            mb_ref[:, pl.ds(start, c)] = jnp.transpose(jnp.where(tile >= thr, 0.0, MASK_NEG))
            return 0

        lax.fori_loop(0, nchunks, mask_body, 0)

    m_ref[...] = jnp.full(m_ref.shape, MASK_NEG, F32)
    l_ref[...] = jnp.zeros(l_ref.shape, F32)
    acc_ref[...] = jnp.zeros(acc_ref.shape, F32)

    def attend(start, width, kind):
        nt = width // LANES
        ones = jnp.ones((width, LANES), BF16)
        for h in range(N_HEADS):
            hp = h // 2
            kp = k_ref[pl.ds(start, width), hp * LANES:(hp + 1) * LANES]
            vp = v_ref[pl.ds(start, width), hp * LANES:(hp + 1) * LANES]
            s = lax.dot_general(qm_ref[h], kp, nt_dims, preferred_element_type=F32)
            s = s + mb_ref[:, pl.ds(start, width)]
            if kind is not None:
                s = s + tb_ref[h, kind]
            m_old = m_ref[h]
            m_new = jnp.maximum(m_old, jnp.max(_fold_lanes(s, jnp.maximum), axis=1, keepdims=True))
            p = jnp.exp2(s - _tile_lanes(m_new, nt)).astype(BF16)
            pv = jnp.dot(p, jnp.concatenate([vp, ones], axis=1), preferred_element_type=F32)
            alpha = jnp.exp2(m_old - m_new)
            m_ref[h] = m_new
            l_ref[h] = alpha * l_ref[h] + pv[:, LANES:]
            acc_ref[h] = alpha * acc_ref[h] + pv[:, :LANES]

    wide = 2 * c
    n_far = jnp.maximum(i - 1, 0)

    def far_body(j, _):
        attend(pl.multiple_of(j * wide, wide), wide, None)
        return 0

    lax.fori_loop(0, n_far // 2, far_body, 0)

    @pl.when(n_far % 2 == 1)
    def _():
        attend(pl.multiple_of((n_far - 1) * c, c), c, None)

    @pl.when(i >= 1)
    def _():
        attend(pl.multiple_of((i - 1) * c, c), c, 1)

    attend(pl.multiple_of(i * c, c), c, 0)

    for hp in range(N_HEADS // 2):
        lo = acc_ref[2 * hp] / l_ref[2 * hp]
        hi = acc_ref[2 * hp + 1] / l_ref[2 * hp + 1]
        o_ref[:, hp * LANES:(hp + 1) * LANES] = jnp.where(low_half, lo, hi).astype(BF16)


def _attn_call(rel_bias, q, qi, wi, k, v, kid, tq):
    bsz, s, _ = q.shape
    nq = s // tq
    topk = min(TOPK_MAX, s // 4)
    assert tq % LANES == 0 and tq >= MAX_DISTANCE
    blk = lambda w: pl.BlockSpec((None, tq, w), lambda b, i: (b, i, 0))
    whole = lambda w: pl.BlockSpec((None, s, w), lambda b, i: (b, 0, 0), pipeline_mode=pl.Buffered(1))
    return pl.pallas_call(
        functools.partial(_attn_kernel, tq=tq, topk=topk),
        grid=(bsz, nq),
        in_specs=[pl.BlockSpec(memory_space=pltpu.SMEM),
                  blk(ATTN_WIDTH), blk(ATTN_WIDTH), blk(LANES),
                  whole(ATTN_WIDTH), whole(ATTN_WIDTH), whole(LANES)],
        out_specs=blk(ATTN_WIDTH),
        out_shape=jax.ShapeDtypeStruct((bsz, s, ATTN_WIDTH), BF16),
        scratch_shapes=[
            pltpu.VMEM((s, tq), F32),
            pltpu.VMEM((32, s // 32, tq), I32),
            pltpu.VMEM((s // 32, tq), I32),
            pltpu.VMEM((tq, s), F32),
            pltpu.VMEM((N_HEADS, 2, tq, tq), F32),
            pltpu.VMEM((N_HEADS, tq, LANES), BF16),
            pltpu.VMEM((N_HEADS, tq, LANES), BF16),
            pltpu.VMEM((N_HEADS, tq, LANES), F32),
            pltpu.VMEM((N_HEADS, tq, LANES), F32),
            pltpu.VMEM((N_HEADS, tq, LANES), F32),
        ],
        compiler_params=pltpu.CompilerParams(dimension_semantics=("arbitrary", "arbitrary"),
                                             vmem_limit_bytes=VMEM_LIMIT),
        name="attn",
    )(rel_bias, q, qi, wi, k, v, kid)


def _post_kernel(attn_ref, z_ref, gate_ref, x_ref, mod_ref, wa_ref, wc_ref, wo_ref, g2_ref,
                 wr_ref, br_ref, hres_ref, h2_ref, eid_ref, ew_ref, counts_ref, cnt_ref):
    ya = jnp.dot(attn_ref[...], wa_ref[...], preferred_element_type=F32)
    yc = jnp.dot(z_ref[...], wc_ref[...], preferred_element_type=F32)
    merged = gate_ref[:, 0:D_MODEL].astype(F32) * ya + gate_ref[:, D_MODEL:].astype(F32) * yc
    o = jnp.dot(merged.astype(BF16), wo_ref[...], preferred_element_type=F32)
    hres = x_ref[...] + mod_ref[2:3, :] * o
    hres_ref[...] = hres
    h2 = _rms_mod(hres, g2_ref[...], mod_ref[4:5, :], mod_ref[3:4, :])
    h2_ref[...] = h2.astype(BF16).reshape(h2_ref.shape)

    h2_hi = h2.astype(BF16)
    h2_lo = (h2 - h2_hi.astype(F32)).astype(BF16)
    lg = (jnp.dot(h2_hi, wr_ref[0], preferred_element_type=F32)
          + (jnp.dot(h2_lo, wr_ref[0], preferred_element_type=F32)
             + jnp.dot(h2_hi, wr_ref[1], preferred_element_type=F32))) + br_ref[...]
    tm = lg.shape[0]
    lane = lax.broadcasted_iota(I32, (tm, LANES), 1)
    ninf = -jnp.inf
    gmask = (lane >= N_EXPERTS) & (lane < N_EXPERTS + N_GROUPS)
    gl = jnp.where(gmask, lg, ninf)
    gmax = jnp.max(gl, axis=1, keepdims=True)
    g_sel = jnp.min(jnp.where(gmask & (gl == gmax), lane - N_EXPERTS, N_GROUPS), axis=1, keepdims=True)
    g_w = 1.0 / jnp.sum(jnp.where(gmask, jnp.exp(gl - gmax), 0.0), axis=1, keepdims=True)

    emask = (lane < N_EXPERTS) & ((lane // EXPERTS_PER_GROUP) == g_sel)
    el = jnp.where(emask, lg, ninf)
    emax = jnp.max(el, axis=1, keepdims=True)
    ee = jnp.where(emask, jnp.exp(el - emax), 0.0)
    e_prob = ee / jnp.sum(ee, axis=1, keepdims=True)
    p1 = jnp.max(jnp.where(emask, e_prob, -1.0), axis=1, keepdims=True)
    i1 = jnp.min(jnp.where(emask & (e_prob == p1), lane, LANES), axis=1, keepdims=True)
    rest = emask & (lane != i1)
    p2 = jnp.max(jnp.where(rest, e_prob, -1.0), axis=1, keepdims=True)
    i2 = jnp.min(jnp.where(rest & (e_prob == p2), lane, LANES), axis=1, keepdims=True)
    psum = p1 + p2
    w1 = g_w * (p1 / psum)
    w2 = g_w * (p2 / psum)
    ew_ref[...] = jnp.where(lane == 0, w1, jnp.where(lane == 1, w2, 0.0))

    @pl.when((pl.program_id(0) == 0) & (pl.program_id(1) == 0))
    def _():
        cnt_ref[...] = jnp.zeros(cnt_ref.shape, F32)

    onehot = jnp.where((lane == i1) | (lane == i2), 1.0, 0.0)
    tri = (lax.broadcasted_iota(I32, (tm, tm), 0) >= lax.broadcasted_iota(I32, (tm, tm), 1))
    cum = jnp.dot(jnp.where(tri, 1.0, 0.0).astype(BF16), onehot.astype(BF16),
                  preferred_element_type=F32)
    before = cum - onehot + cnt_ref[0:1, :]
    r1 = jnp.sum(jnp.where(lane == i1, before, 0.0), axis=1, keepdims=True).astype(I32)
    r2 = jnp.sum(jnp.where(lane == i2, before, 0.0), axis=1, keepdims=True).astype(I32)
    eid_ref[...] = jnp.where(lane == 0, i1, jnp.where(lane == 1, i2,
                             jnp.where(lane == 2, r1, jnp.where(lane == 3, r2, 0))))
    total = cnt_ref[0:1, :] + cum[tm - 1:tm, :]
    cnt_ref[...] = jnp.broadcast_to(total, cnt_ref.shape)
    counts_ref[...] = jnp.broadcast_to(total, counts_ref.shape)


def _post_call(attn, z, gates, x, mod, wa, wc, wo, g2, wr, br, tm):
    bsz, s, d = x.shape
    nt = s // tm
    tok = lambda w: pl.BlockSpec((None, tm, w), lambda b, t: (b, t, 0))
    full = lambda arr: pl.BlockSpec(arr.shape, lambda b, t: (0,) * arr.ndim)
    return pl.pallas_call(
        _post_kernel,
        grid=(bsz, nt),
        in_specs=[tok(ATTN_WIDTH), tok(CONV_WIDTH), tok(2 * d), tok(d),
                  pl.BlockSpec((None, 6, d), lambda b, t: (b, 0, 0)),
                  full(wa), full(wc), full(wo), full(g2), full(wr), full(br)],
        out_specs=[tok(d), pl.BlockSpec((None, tm) + ROW_TILE, lambda b, t: (b, t, 0, 0)),
                   tok(LANES), tok(LANES), pl.BlockSpec((8, LANES), lambda b, t: (0, 0))],
        out_shape=[jax.ShapeDtypeStruct((bsz, s, d), F32),
                   jax.ShapeDtypeStruct((bsz, s) + ROW_TILE, BF16),
                   jax.ShapeDtypeStruct((bsz, s, LANES), I32), jax.ShapeDtypeStruct((bsz, s, LANES), F32),
                   jax.ShapeDtypeStruct((8, LANES), F32)],
        scratch_shapes=[pltpu.VMEM((8, LANES), F32)],
        compiler_params=pltpu.CompilerParams(dimension_semantics=("arbitrary", "arbitrary"),
                                             vmem_limit_bytes=VMEM_LIMIT),
        name="post",
    )(attn, z, gates, x, mod, wa, wc, wo, g2, wr, br)


ROW_TILE = (D_MODEL // LANES, LANES)
EXPERT_ROWS = 256
TOKENS_PER_STEP = 256


ISSUE_UNROLL = 8


def _dispatch_kernel(pos1_ref, pos2_ref, h2_ref, xs_in_ref, xs_ref, sem):
    del xs_in_ref
    tm = h2_ref.shape[0]
    base = pl.program_id(0) * tm

    def issue(r, c):
        pltpu.make_async_copy(h2_ref.at[r], xs_ref.at[pos1_ref[base + r]], sem).start()
        pltpu.make_async_copy(h2_ref.at[r], xs_ref.at[pos2_ref[base + r]], sem).start()
        return c

    lax.fori_loop(0, tm, issue, 0, unroll=ISSUE_UNROLL)
    for _ in range(2):
        pltpu.make_async_copy(h2_ref, xs_ref.at[pl.ds(0, tm)], sem).wait()


def _dispatch_call(pos1, pos2, h2_rows, n_rows):
    t = h2_rows.shape[0]
    tm = min(TOKENS_PER_STEP, t)
    xs0 = jnp.zeros((n_rows,) + ROW_TILE, BF16)
    return pl.pallas_call(
        _dispatch_kernel,
        grid_spec=pltpu.PrefetchScalarGridSpec(
            num_scalar_prefetch=2,
            grid=(t // tm,),
            in_specs=[pl.BlockSpec((tm,) + ROW_TILE, lambda i, p1, p2: (i, 0, 0)),
                      pl.BlockSpec(memory_space=pl.ANY)],
            out_specs=pl.BlockSpec(memory_space=pl.ANY),
            scratch_shapes=[pltpu.SemaphoreType.DMA(())]),
        out_shape=jax.ShapeDtypeStruct((n_rows,) + ROW_TILE, BF16),
        input_output_aliases={3: 0},
        compiler_params=pltpu.CompilerParams(dimension_semantics=("arbitrary",),
                                             vmem_limit_bytes=VMEM_LIMIT),
        name="dispatch",
    )(pos1, pos2, h2_rows, xs0)


def _expert_kernel(te_ref, nu_ref, xs_ref, wg_ref, wu_ref, wd_ref, ys_ref, wgb, wub, wdb):
    g = pl.program_id(0)
    e = te_ref[g]
    e_prev = te_ref[jnp.maximum(g - 1, 0)]

    @pl.when((g == 0) | (e != e_prev))
    def _():
        wgb[...] = wg_ref[...].astype(BF16)
        wub[...] = wu_ref[...].astype(BF16)
        wdb[...] = wd_ref[...].astype(BF16)

    @pl.when(g < nu_ref[0])
    def _():
        x = xs_ref[...].reshape(xs_ref.shape[0], D_MODEL)
        a = jnp.dot(x, wgb[...], preferred_element_type=F32)
        b = jnp.dot(x, wub[...], preferred_element_type=F32)
        hid = (a * jax.nn.sigmoid(a)) * b
        y = jnp.dot(hid.astype(BF16), wdb[...], preferred_element_type=F32)
        ys_ref[...] = y.astype(BF16).reshape(ys_ref.shape)

    @pl.when(g >= nu_ref[0])
    def _():
        ys_ref[...] = jnp.zeros(ys_ref.shape, BF16)


def _expert_call(tile_expert, n_used, xs, wg, wu, wd):
    n_rows, d = xs.shape[0], D_MODEL
    nt = n_rows // EXPERT_ROWS
    rows = pl.BlockSpec((EXPERT_ROWS,) + ROW_TILE, lambda g, te, nu: (g, 0, 0))
    wspec = lambda shp: pl.BlockSpec((None,) + shp, lambda g, te, nu: (te[g], 0, 0))
    return pl.pallas_call(
        _expert_kernel,
        grid_spec=pltpu.PrefetchScalarGridSpec(
            num_scalar_prefetch=2,
            grid=(nt,),
            in_specs=[rows, wspec((d, D_EXPERT)), wspec((d, D_EXPERT)), wspec((D_EXPERT, d))],
            out_specs=rows,
            scratch_shapes=[pltpu.VMEM((d, D_EXPERT), BF16), pltpu.VMEM((d, D_EXPERT), BF16),
                            pltpu.VMEM((D_EXPERT, d), BF16)]),
        out_shape=jax.ShapeDtypeStruct((n_rows,) + ROW_TILE, BF16),
        compiler_params=pltpu.CompilerParams(dimension_semantics=("arbitrary",),
                                             vmem_limit_bytes=VMEM_LIMIT),
        name="experts",
    )(tile_expert, n_used, xs, wg, wu, wd)


def _combine_kernel(pos1_ref, pos2_ref, ys_ref, ew_ref, hres_ref, g2_ref, gf_ref, o_ref, buf, sem):
    tm = hres_ref.shape[0]
    t = pl.program_id(0)
    nt = pl.num_programs(0)

    def issue(tile, slot):
        base = tile * tm

        def body(r, c):
            pltpu.make_async_copy(ys_ref.at[pos1_ref[base + r]], buf.at[slot, r], sem.at[slot]).start()
            pltpu.make_async_copy(ys_ref.at[pos2_ref[base + r]], buf.at[slot, tm + r], sem.at[slot]).start()
            return c

        lax.fori_loop(0, tm, body, 0, unroll=ISSUE_UNROLL)

    @pl.when(t == 0)
    def _():
        issue(0, 0)

    @pl.when(t + 1 < nt)
    def _():
        issue(t + 1, (t + 1) % 2)

    slot = t % 2
    pltpu.make_async_copy(ys_ref.at[pl.ds(0, 2 * tm)], buf.at[slot], sem.at[slot]).wait()

    y1 = buf[slot, 0:tm].astype(F32).reshape(tm, D_MODEL)
    y2 = buf[slot, tm:2 * tm].astype(F32).reshape(tm, D_MODEL)
    w = ew_ref[...]
    y = w[:, 0:1] * y1 + w[:, 1:2] * y2
    hfin = hres_ref[...] + g2_ref[...] * y
    ms = jnp.mean(hfin * hfin, axis=-1, keepdims=True)
    o_ref[...] = hfin * lax.rsqrt(ms + EPS) * gf_ref[...]


def _combine_call(pos1, pos2, ys_rows, ew, hres, gate2, gf):
    t, d = hres.shape
    tm = min(TOKENS_PER_STEP, t)
    per_batch = t // gate2.shape[0]
    assert per_batch % tm == 0
    tok = lambda w: pl.BlockSpec((tm, w), lambda i, *_: (i, 0))
    return pl.pallas_call(
        _combine_kernel,
        grid_spec=pltpu.PrefetchScalarGridSpec(
            num_scalar_prefetch=2,
            grid=(t // tm,),
            in_specs=[pl.BlockSpec(memory_space=pl.ANY), tok(LANES), tok(d),
                      pl.BlockSpec((None, 1, d), lambda i, *_: ((i * tm) // per_batch, 0, 0)),
                      pl.BlockSpec((1, d), lambda i, *_: (0, 0))],
            out_specs=tok(d),
            scratch_shapes=[pltpu.VMEM((2, 2 * tm) + ROW_TILE, BF16), pltpu.SemaphoreType.DMA((2,))]),
        out_shape=jax.ShapeDtypeStruct((t, d), F32),
        compiler_params=pltpu.CompilerParams(dimension_semantics=("arbitrary",),
                                             vmem_limit_bytes=VMEM_LIMIT),
        name="combine",
    )(pos1, pos2, ys_rows, ew, hres, gate2, gf)


def _sparse_moe(h2, eid, ew, counts, wg, wu, wd, hres, gate2, gf):
    bsz, s, d = hres.shape
    t = bsz * s
    ne = wg.shape[0]
    e1, e2 = eid[..., 0].reshape(t), eid[..., 1].reshape(t)
    r1, r2 = eid[..., 2].reshape(t), eid[..., 3].reshape(t)
    cnt = counts[0, :ne].astype(I32)
    seg_rows = ((cnt + EXPERT_ROWS - 1) // EXPERT_ROWS) * EXPERT_ROWS
    seg_end = jnp.cumsum(seg_rows)
    seg_start = seg_end - seg_rows
    pos1 = seg_start[e1] + r1
    pos2 = seg_start[e2] + r2
    n_tiles = (2 * t) // EXPERT_ROWS + ne
    tile_first_row = jnp.arange(n_tiles, dtype=I32) * EXPERT_ROWS
    tile_expert = jnp.minimum(jnp.sum(seg_end[None, :] <= tile_first_row[:, None], axis=1), ne - 1)
    n_used = (seg_end[ne - 1] // EXPERT_ROWS).reshape(1)
    n_rows = n_tiles * EXPERT_ROWS

    xs = _dispatch_call(pos1, pos2, h2.reshape((t,) + ROW_TILE), n_rows)
    ys = _expert_call(tile_expert.astype(I32), n_used.astype(I32), xs, wg, wu, wd)
    out = _combine_call(pos1, pos2, ys, ew.reshape(t, LANES), hres.reshape(t, d),
                        gate2.reshape(bsz, 1, d), gf.reshape(1, d))
    return out.reshape(bsz, s, d)


def kernel(x, c, w_ada, b_ada, norm1_g, w_in, rel_bias, conv_w, w_attn_branch, w_conv_branch, w_out,
           norm2_g, w_router_group, b_router_group, w_router_expert, b_router_expert,
           w_gate_e, w_up_e, w_down_e, norm_f_g):
    bsz, s, d = x.shape
    depth = w_ada.shape[0]
    h = x.astype(F32)
    c_pad = jnp.zeros((8, d), F32).at[:bsz].set(c.astype(F32))
    out = h
    for l in range(depth):
        mod = _ada_call(c_pad, w_ada[l].astype(F32), b_ada[l].astype(F32)[None, :])
        mod = mod[:bsz].reshape(bsz, 6, d)

        w = w_in[l]
        o_q, o_k, o_v, o_qi = 0, 512, 1024, 1536
        o_ki, o_wi, o_cb, o_gl = 2048, 2112, 2120, 3656
        wa = w[:, o_q:o_ki].astype(BF16)
        ki_w = w[:, o_ki:o_wi]
        wb = jnp.concatenate([ki_w, ki_w, w[:, o_wi:o_cb],
                              jnp.zeros((d, LANES - IDX_HEADS), w.dtype)], axis=1).astype(BF16)
        wc = w[:, o_cb:o_gl].astype(BF16)
        wd = w[:, o_gl:].astype(BF16)

        q, k, v, qi, kid, wi, z, gates = _proj_call(
            h, mod, norm1_g[l].astype(F32)[None, :], wa, wb, wc, wd, conv_w[l].astype(F32),
            tm=min(512, s))

        attn = _attn_call(rel_bias.astype(F32), q, qi, wi, k, v, kid, tq=min(TOPK_MAX, s // 4))

        wr32 = jnp.concatenate([w_router_expert[l], w_router_group[l],
                                jnp.zeros((d, LANES - N_EXPERTS - N_GROUPS), F32)], axis=1).astype(F32)
        wr_hi = wr32.astype(BF16)
        wr = jnp.stack([wr_hi, (wr32 - wr_hi.astype(F32)).astype(BF16)])
        br = jnp.concatenate([b_router_expert[l], b_router_group[l],
                              jnp.zeros((LANES - N_EXPERTS - N_GROUPS,), F32)])[None, :].astype(F32)
        hres, h2, eid, ew, counts = _post_call(
            attn, z, gates, h, mod, w_attn_branch[l].astype(BF16), w_conv_branch[l].astype(BF16),
            w_out[l].astype(BF16), norm2_g[l].astype(F32)[None, :], wr, br, tm=min(512, s))

        assert depth == 1, "the final RMSNorm is fused into the combine kernel of the only layer"
        out = _sparse_moe(h2, eid, ew, counts, w_gate_e[l].astype(F32), w_up_e[l].astype(F32),
                          w_down_e[l].astype(F32), hres, mod[:, 5, :], norm_f_g.astype(F32))
        h = out
    return out.astype(x.dtype)
```

```python
import functools
import math

import numpy as np
import jax
import jax.numpy as jnp
from jax import lax
from jax.experimental import pallas as pl
from jax.experimental.pallas import tpu as pltpu

F32 = jnp.float32
BF16 = jnp.bfloat16
I32 = jnp.int32

D_MODEL = 1024
N_HEADS = 8
HEAD_DIM = 64
ATTN_WIDTH = N_HEADS * HEAD_DIM
IDX_HEADS = 8
IDX_DIM = 64
TOPK_MAX = 256
N_BUCKETS = 32
MAX_DISTANCE = 128
CONV_WIDTH = 512
CONV_K = 3
N_GROUPS = 4
EXPERTS_PER_GROUP = 8
N_EXPERTS = N_GROUPS * EXPERTS_PER_GROUP
D_EXPERT = 512
EPS = 1e-6

LANES = 128
VMEM_LIMIT = 56 * 1024 * 1024

LOG2E = math.log2(math.e)
MASK_NEG = -1e30
SCORE_NEG = float(np.finfo(np.float32).min)
TINY_F32 = float(np.finfo(np.float32).tiny)
KEY_SPACE_AFTER = 40


def _t5_bucket_starts():
    max_exact = N_BUCKETS // 2
    n = np.arange(0, MAX_DISTANCE + 1)
    nf = np.maximum(n, 1).astype(np.float32)
    val = (np.log(nf / np.float32(max_exact)) / np.float32(math.log(MAX_DISTANCE / max_exact))
           * np.float32(N_BUCKETS - max_exact)).astype(np.float32)
    inner = val[max_exact + 1:MAX_DISTANCE]
    assert np.min(np.abs(inner - np.round(inner))) > 1e-3
    large = np.minimum(max_exact + val.astype(np.int32), N_BUCKETS - 1)
    bucket = np.where(n < max_exact, n, large)
    assert bucket[MAX_DISTANCE] == N_BUCKETS - 1 and np.all(np.diff(bucket) >= 0)
    return [int(np.argmax(bucket >= b)) for b in range(N_BUCKETS)]


_BUCKET_START = _t5_bucket_starts()


def _ada_kernel(c_ref, w_ref, b_ref, o_ref):
    c = c_ref[...]
    ca = c * jax.nn.sigmoid(c)
    o_ref[...] = jnp.dot(ca, w_ref[...], preferred_element_type=F32,
                         precision=lax.Precision.HIGHEST) + b_ref[...]


def _ada_call(c_pad, w, b):
    rows, d = c_pad.shape
    n = w.shape[1]
    tn = 1536
    return pl.pallas_call(
        _ada_kernel,
        grid=(n // tn,),
        in_specs=[pl.BlockSpec((rows, d), lambda j: (0, 0)),
                  pl.BlockSpec((d, tn), lambda j: (0, j)),
                  pl.BlockSpec((1, tn), lambda j: (0, j))],
        out_specs=pl.BlockSpec((rows, tn), lambda j: (0, j)),
        out_shape=jax.ShapeDtypeStruct((rows, n), F32),
        compiler_params=pltpu.CompilerParams(dimension_semantics=("arbitrary",),
                                             vmem_limit_bytes=VMEM_LIMIT),
        name="ada",
    )(c_pad, w, b)


def _rms_mod(x, g, scale, shift):
    ms = jnp.mean(x * x, axis=-1, keepdims=True)
    y = x * lax.rsqrt(ms + EPS) * g
    return y * (1.0 + scale) + shift


def _proj_kernel(x_ref, mod_ref, g_ref, wa_ref, wb_ref, wc_ref, wd_ref, cw_ref,
                 q_ref, k_ref, v_ref, qi_ref, kid_ref, wi_ref, z_ref, gate_ref, ubuf, *, tm):
    h = _rms_mod(x_ref[...], g_ref[...], mod_ref[1:2, :], mod_ref[0:1, :])
    hb = h.astype(BF16)

    a = jnp.dot(hb, wa_ref[...], preferred_element_type=F32)
    q_ref[...] = (a[:, 0:512] * (HEAD_DIM ** -0.5 * LOG2E)).astype(BF16)
    k_ref[...] = a[:, 512:1024].astype(BF16)
    v_ref[...] = a[:, 1024:1536].astype(BF16)
    qi_ref[...] = a[:, 1536:2048].astype(BF16)

    b = jnp.dot(hb, wb_ref[...], preferred_element_type=F32)
    kid_ref[...] = b[:, 0:LANES].astype(BF16)
    wi_ref[...] = b[:, LANES:2 * LANES] * ((IDX_DIM ** -0.5) * (IDX_HEADS ** -0.5))

    cpart = jnp.dot(hb, wc_ref[...], preferred_element_type=F32)
    cb = cpart[:, 0:512]
    u = cpart[:, 512:1024] * cpart[:, 1024:1536]

    @pl.when(pl.program_id(1) == 0)
    def _():
        ubuf[0:8, :] = jnp.zeros((8, CONV_WIDTH), F32)

    ubuf[8:tm + 8, :] = u
    conv = (cw_ref[0:1, :] * ubuf[6:tm + 6, :] + cw_ref[1:2, :] * ubuf[7:tm + 7, :]
            + cw_ref[2:3, :] * u)
    z_ref[...] = (cb * conv).astype(BF16)
    ubuf[0:8, :] = ubuf[tm:tm + 8, :]

    d = jnp.dot(hb, wd_ref[...], preferred_element_type=F32)
    gate_ref[...] = jax.nn.sigmoid(d).astype(BF16)


def _proj_call(x, mod, g1, wa, wb, wc, wd, cw, tm):
    bsz, s, d = x.shape
    nt = s // tm
    tok = lambda w: pl.BlockSpec((None, tm, w), lambda b, t: (b, t, 0))
    full = lambda arr: pl.BlockSpec(arr.shape, lambda b, t: (0,) * arr.ndim)
    out_widths = [(512, BF16), (512, BF16), (512, BF16), (512, BF16), (LANES, BF16), (LANES, F32),
                  (512, BF16), (2 * D_MODEL, BF16)]
    return pl.pallas_call(
        functools.partial(_proj_kernel, tm=tm),
        grid=(bsz, nt),
        in_specs=[tok(d), pl.BlockSpec((None, 6, d), lambda b, t: (b, 0, 0)), full(g1),
                  full(wa), full(wb), full(wc), full(wd), full(cw)],
        out_specs=[tok(w) for w, _ in out_widths],
        out_shape=[jax.ShapeDtypeStruct((bsz, s, w), dt) for w, dt in out_widths],
        scratch_shapes=[pltpu.VMEM((tm + 8, CONV_WIDTH), F32)],
        compiler_params=pltpu.CompilerParams(dimension_semantics=("arbitrary", "arbitrary"),
                                             vmem_limit_bytes=VMEM_LIMIT),
        name="proj",
    )(x, mod, g1, wa, wb, wc, wd, cw)


def _to_key(f):
    b = pltpu.bitcast(f, I32)
    return jnp.where(b < 0, b ^ 0x7FFFFFFF, b)


def _from_key(k):
    return pltpu.bitcast(jnp.where(k < 0, k ^ 0x7FFFFFFF, k), F32)


def _bit_transpose32(words):
    a = list(words)
    j, m = 16, 0x0000FFFF
    while j:
        k = 0
        while k < 32:
            t = (a[k] ^ lax.shift_right_logical(a[k + j], jnp.int32(j))) & jnp.int32(m)
            a[k] = a[k] ^ t
            a[k + j] = a[k + j] ^ lax.shift_left(t, jnp.int32(j))
            k = (k + j + 1) & ~j
        j >>= 1
        m = (m ^ (m << j)) & 0xFFFFFFFF
    return a


def _fold_lanes(x, op):
    parts = [x[:, t * LANES:(t + 1) * LANES] for t in range(x.shape[1] // LANES)]
    while len(parts) > 1:
        nxt = [op(parts[a], parts[a + 1]) for a in range(0, len(parts) - 1, 2)]
        parts = nxt + ([parts[-1]] if len(parts) % 2 else [])
    return parts[0]


def _tile_lanes(x, n):
    return x if n == 1 else jnp.concatenate([x] * n, axis=1)


def _attn_kernel(rb_ref, q_ref, qi_ref, wi_ref, qin_ref, win_ref, k_ref, v_ref, kid_ref, o_ref,
                 sct_ref, planes_ref, cand_ref, mb_ref, tb_ref, qm_ref, qim_ref, m_ref, l_ref, acc_ref,
                 *, tq, topk):
    c = tq
    i = pl.program_id(1)
    nt_dims = (((1,), (1,)), ((), ()))
    lane = lax.broadcasted_iota(I32, (tq, LANES), 1)
    low_half = lane < HEAD_DIM
    row = lax.broadcasted_iota(I32, (tq, c), 0)
    col = lax.broadcasted_iota(I32, (tq, c), 1)

    @pl.when((pl.program_id(0) == 0) & (i == 0))
    def _():
        for p in range(32):
            planes_ref[p] = jnp.zeros(planes_ref.shape[1:], I32)
        for kind in range(2):
            rel = row - col + kind * c
            n = jnp.maximum(rel, 0)
            for h in range(N_HEADS):
                t = jnp.full((tq, c), rb_ref[0, h], F32)
                for b in range(1, N_BUCKETS):
                    t = jnp.where(n >= _BUCKET_START[b], rb_ref[b, h], t)
                t = (t - rb_ref[N_BUCKETS - 1, h]) * LOG2E
                tb_ref[h, kind] = jnp.where(rel >= 0, t, MASK_NEG)


    def masked_heads(src_ref, dst_ref):
        for h in range(N_HEADS):
            hp, half = divmod(h, 2)
            keep = low_half if half == 0 else jnp.logical_not(low_half)
            pair = src_ref[:, hp * LANES:(hp + 1) * LANES]
            dst_ref[h] = jnp.where(keep, pair, jnp.zeros_like(pair))

    krow = lax.broadcasted_iota(I32, (c, tq), 0)
    qcol = lax.broadcasted_iota(I32, (c, tq), 1)
    int_min = jnp.int32(-2 ** 31)

    def fold_rows(x, op):
        parts = [x[r * 8:(r + 1) * 8, :] for r in range(x.shape[0] // 8)]
        while len(parts) > 1:
            nxt = [op(parts[a], parts[a + 1]) for a in range(0, len(parts) - 1, 2)]
            parts = nxt + ([parts[-1]] if len(parts) % 2 else [])
        return parts[0]

    def score_chunk(j, diag, w_t):
        start = pl.multiple_of(j * c, c)
        kc = kid_ref[pl.ds(start, c), :]
        acc = jnp.zeros((c, tq), F32)
        for h in range(N_HEADS):
            d = lax.dot_general(kc, qim_ref[h], nt_dims, preferred_element_type=F32)
            acc = acc + jnp.maximum(d, 0.0) * w_t[h:h + 1, :]
        ukey = _to_key(acc) ^ int_min
        if diag:
            causal = krow <= qcol
            acc = jnp.where(causal, acc, SCORE_NEG)
            ukey = jnp.where(causal, ukey, 0)
        sct_ref[pl.ds(start, c), :] = acc
        prow = pl.multiple_of(j * (c // 32), c // 32)
        for lt in range(tq // LANES):
            words = [ukey[k * 8:(k + 1) * 8, lt * LANES:(lt + 1) * LANES] for k in range(32)]
            for p, plane in enumerate(_bit_transpose32(words)):
                planes_ref[p, pl.ds(prow, c // 32), lt * LANES:(lt + 1) * LANES] = plane

    @pl.when(i == 0)
    def _():
        masked_heads(qi_ref, qim_ref)
        score_chunk(0, True, jnp.transpose(wi_ref[...])[0:IDX_HEADS, :])

    nchunks = i + 1
    n_causal = i * tq + lax.broadcasted_iota(I32, (1, tq), 1) + 1
    prow_iota = lax.broadcasted_iota(I32, cand_ref.shape, 0)
    cand_ref[...] = jnp.where(prow_iota < nchunks * (c // 32), -1, 0)

    def radix_cond(st):
        return (st[0] < 32) & (st[1] > 0)

    def radix_step(st):
        p, _, prefix, need, ncand, settled = st
        plane = planes_ref[p]
        cand = cand_ref[...]
        ones = cand & plane
        cnt1 = jnp.sum(fold_rows(lax.population_count(ones), jnp.add), axis=0, keepdims=True)
        take1 = cnt1 >= need
        live = settled == 0
        need = jnp.where(live & jnp.logical_not(take1), need - cnt1, need)
        ncand = jnp.where(live, jnp.where(take1, cnt1, ncand - cnt1), ncand)
        prefix = jnp.where(live & take1, prefix | jnp.left_shift(jnp.int32(1), 31 - p), prefix)
        cand_ref[...] = cand & (plane ^ jnp.where(take1, 0, -1))
        settled = jnp.where(ncand == need, 1, settled)
        return p + 1, jnp.sum(1 - settled), prefix, need, ncand, settled

    done0 = jnp.where(n_causal <= topk, 1, 0)
    _, _, prefix, _, _, _ = lax.while_loop(
        radix_cond, radix_step,
        (jnp.int32(0), jnp.sum(1 - done0), jnp.zeros((1, tq), I32), jnp.full((1, tq), topk, I32),
         jnp.full((1, tq), 1, I32) * (nchunks * c), done0))
    guess_key = prefix ^ int_min
    guess = _from_key(guess_key)

    def count_keys(pred):
        def body(j, part):
            start = pl.multiple_of(j * c, c)
            hit = pred(sct_ref[pl.ds(start, c), :], start)
            return part + fold_rows(jnp.where(hit, 1.0, 0.0), jnp.add)
        part = lax.fori_loop(0, nchunks, body, jnp.zeros((8, tq), F32))
        return jnp.sum(part, axis=0, keepdims=True)

    kf = float(topk)
    guess_above = _from_key(guess_key + 1)
    guess_above = jnp.where(jnp.abs(guess_above) < TINY_F32,
                            jnp.where(guess >= 0.0, TINY_F32, 0.0), guess_above)
    guess_below = _from_key(guess_key - 2)

    def cond(st):
        return st[0] > 0

    def step(st):
        _, it, lo, hi, c_lo, c_hi, done = st
        mid = 0.5 * lo + 0.5 * hi
        lo_k, hi_k = _to_key(lo), _to_key(hi)
        mid_key = _from_key((lo_k >> 1) + (hi_k >> 1) + (lo_k & hi_k & 1))
        mid = jnp.where(it >= KEY_SPACE_AFTER, mid_key, mid)
        probe = jnp.where(it == 0, guess, jnp.where(lo == guess, guess_above, guess_below))
        mid = jnp.where((it <= 1) & (probe > lo) & (probe < hi), probe, mid)
        go = (done == 0) & (mid > lo) & (mid < hi)
        cnt = count_keys(lambda tile, start: tile >= mid)
        ge = cnt >= kf
        up = ge & go
        dn = jnp.logical_not(ge) & go
        lo = jnp.where(up, mid, lo)
        hi = jnp.where(dn, mid, hi)
        c_lo = jnp.where(up, cnt, c_lo)
        c_hi = jnp.where(dn, cnt, c_hi)
        adjacent = (lo == guess) & (hi == guess_above)
        done = jnp.where(go & (c_lo != kf) & jnp.logical_not(adjacent), 0, 1)
        return jnp.sum(1 - done), it + 1, lo, hi, c_lo, c_hi, done

    _, _, thr, _, c_lo, c_hi, _ = lax.while_loop(
        cond, step, (jnp.sum(1 - done0), jnp.int32(0), jnp.full((1, tq), SCORE_NEG, F32),
                     jnp.full((1, tq), -SCORE_NEG, F32), jnp.full((1, tq), 1.0, F32) * (nchunks * c),
                     jnp.zeros((1, tq), F32), done0))

    tied = (n_causal > topk) & (c_lo > kf)
    any_tied = jnp.sum(jnp.where(tied, 1, 0)) > 0

    @pl.when(any_tied)
    def _():
        n_take = kf - c_hi

        def bis(_, st):
            lo_x, hi_x = st
            mid = (lo_x + hi_x) >> 1
            ok = count_keys(lambda tile, start: (tile == thr) & ((krow + start) <= mid)) >= n_take
            return jnp.where(ok, lo_x, mid), jnp.where(ok, mid, hi_x)

        nbits = int(math.ceil(math.log2(sct_ref.shape[0]))) + 1
        _, hi_x = lax.fori_loop(0, nbits, bis, (jnp.full((1, tq), -1, I32),
                                                jnp.full((1, tq), sct_ref.shape[0], I32)))
        cut = jnp.where(tied, hi_x, jnp.int32(2 ** 30))

        def mask_body(j, _):
            start = pl.multiple_of(j * c, c)
            tile = sct_ref[pl.ds(start, c), :]
            sel = (tile > thr) | ((tile == thr) & ((krow + start) <= cut))
            mb_ref[:, pl.ds(start, c)] = jnp.transpose(jnp.where(sel, 0.0, MASK_NEG))
            return 0

        lax.fori_loop(0, nchunks, mask_body, 0)

    @pl.when(jnp.logical_not(any_tied))
    def _():
        def mask_body(j, _):
            start = pl.multiple_of(j * c, c)
            tile = sct_ref[pl.ds(start, c), :]
            mb_ref[:, pl.ds(start, c)] = jnp.transpose(jnp.where(tile >= thr, 0.0, MASK_NEG))
            return 0

        lax.fori_loop(0, nchunks, mask_body, 0)

    masked_heads(q_ref, qm_ref)
    masked_heads(qin_ref, qim_ref)
    w_t_next = jnp.transpose(win_ref[...])[0:IDX_HEADS, :]
    has_next = i + 1 < pl.num_programs(1)
    m_ref[...] = jnp.full(m_ref.shape, MASK_NEG, F32)
    l_ref[...] = jnp.zeros(l_ref.shape, F32)
    acc_ref[...] = jnp.zeros(acc_ref.shape, F32)

    def attend(start, width, kind):
        nt = width // LANES
        ones = jnp.ones((width, LANES), BF16)
        for h in range(N_HEADS):
            hp = h // 2
            kp = k_ref[pl.ds(start, width), hp * LANES:(hp + 1) * LANES]
            vp = v_ref[pl.ds(start, width), hp * LANES:(hp + 1) * LANES]
            s = lax.dot_general(qm_ref[h], kp, nt_dims, preferred_element_type=F32)
            s = s + mb_ref[:, pl.ds(start, width)]
            if kind is not None:
                s = s + tb_ref[h, kind]
            m_old = m_ref[h]
            m_new = jnp.maximum(m_old, jnp.max(_fold_lanes(s, jnp.maximum), axis=1, keepdims=True))
            p = jnp.exp2(s - _tile_lanes(m_new, nt)).astype(BF16)
            pv = jnp.dot(p, jnp.concatenate([vp, ones], axis=1), preferred_element_type=F32)
            alpha = jnp.exp2(m_old - m_new)
            m_ref[h] = m_new
            l_ref[h] = alpha * l_ref[h] + pv[:, LANES:]
            acc_ref[h] = alpha * acc_ref[h] + pv[:, :LANES]

    wide = 2 * c
    n_far = jnp.maximum(i - 1, 0)

    n_wide = n_far // 2

    def far_body(j, _):
        attend(pl.multiple_of(j * wide, wide), wide, None)
        score_chunk(2 * j, False, w_t_next)
        score_chunk(2 * j + 1, False, w_t_next)
        return 0

    lax.fori_loop(0, n_wide, far_body, 0)

    @pl.when(has_next)
    def _():
        def score_body(j, carry):
            score_chunk(j, False, w_t_next)
            return carry

        lax.fori_loop(2 * n_wide, i + 1, score_body, 0)
        score_chunk(i + 1, True, w_t_next)

    @pl.when(n_far % 2 == 1)
    def _():
        attend(pl.multiple_of((n_far - 1) * c, c), c, None)

    @pl.when(i >= 1)
    def _():
        attend(pl.multiple_of((i - 1) * c, c), c, 1)

    attend(pl.multiple_of(i * c, c), c, 0)

    for hp in range(N_HEADS // 2):
        lo = acc_ref[2 * hp] / l_ref[2 * hp]
        hi = acc_ref[2 * hp + 1] / l_ref[2 * hp + 1]
        o_ref[:, hp * LANES:(hp + 1) * LANES] = jnp.where(low_half, lo, hi).astype(BF16)


def _attn_call(rel_bias, q, qi, wi, k, v, kid, tq):
    bsz, s, _ = q.shape
    nq = s // tq
    topk = min(TOPK_MAX, s // 4)
    assert tq % LANES == 0 and tq >= MAX_DISTANCE
    blk = lambda w: pl.BlockSpec((None, tq, w), lambda b, i: (b, i, 0))
    nxt = lambda w: pl.BlockSpec((None, tq, w), lambda b, i: (b, jnp.minimum(i + 1, nq - 1), 0))
    whole = lambda w: pl.BlockSpec((None, s, w), lambda b, i: (b, 0, 0), pipeline_mode=pl.Buffered(1))
    return pl.pallas_call(
        functools.partial(_attn_kernel, tq=tq, topk=topk),
        grid=(bsz, nq),
        in_specs=[pl.BlockSpec(memory_space=pltpu.SMEM),
                  blk(ATTN_WIDTH), blk(ATTN_WIDTH), blk(LANES), nxt(ATTN_WIDTH), nxt(LANES),
                  whole(ATTN_WIDTH), whole(ATTN_WIDTH), whole(LANES)],
        out_specs=blk(ATTN_WIDTH),
        out_shape=jax.ShapeDtypeStruct((bsz, s, ATTN_WIDTH), BF16),
        scratch_shapes=[
            pltpu.VMEM((s, tq), F32),
            pltpu.VMEM((32, s // 32, tq), I32),
            pltpu.VMEM((s // 32, tq), I32),
            pltpu.VMEM((tq, s), F32),
            pltpu.VMEM((N_HEADS, 2, tq, tq), F32),
            pltpu.VMEM((N_HEADS, tq, LANES), BF16),
            pltpu.VMEM((N_HEADS, tq, LANES), BF16),
            pltpu.VMEM((N_HEADS, tq, LANES), F32),
            pltpu.VMEM((N_HEADS, tq, LANES), F32),
            pltpu.VMEM((N_HEADS, tq, LANES), F32),
        ],
        compiler_params=pltpu.CompilerParams(dimension_semantics=("arbitrary", "arbitrary"),
                                             vmem_limit_bytes=VMEM_LIMIT),
        name="attn",
    )(rel_bias, q, qi, wi, qi, wi, k, v, kid)


def _post_kernel(attn_ref, z_ref, gate_ref, x_ref, mod_ref, wa_ref, wc_ref, wo_ref, g2_ref,
                 wr_ref, br_ref, hres_ref, h2_ref, eid_ref, ew_ref, counts_ref, cnt_ref):
    ya = jnp.dot(attn_ref[...], wa_ref[...], preferred_element_type=F32)
    yc = jnp.dot(z_ref[...], wc_ref[...], preferred_element_type=F32)
    merged = gate_ref[:, 0:D_MODEL].astype(F32) * ya + gate_ref[:, D_MODEL:].astype(F32) * yc
    o = jnp.dot(merged.astype(BF16), wo_ref[...], preferred_element_type=F32)
    hres = x_ref[...] + mod_ref[2:3, :] * o
    hres_ref[...] = hres
    h2 = _rms_mod(hres, g2_ref[...], mod_ref[4:5, :], mod_ref[3:4, :])
    h2_ref[...] = h2.astype(BF16).reshape(h2_ref.shape)

    h2_hi = h2.astype(BF16)
    h2_lo = (h2 - h2_hi.astype(F32)).astype(BF16)
    lg = (jnp.dot(h2_hi, wr_ref[0], preferred_element_type=F32)
          + (jnp.dot(h2_lo, wr_ref[0], preferred_element_type=F32)
             + jnp.dot(h2_hi, wr_ref[1], preferred_element_type=F32))) + br_ref[...]
    tm = lg.shape[0]
    lane = lax.broadcasted_iota(I32, (tm, LANES), 1)
    ninf = -jnp.inf
    gmask = (lane >= N_EXPERTS) & (lane < N_EXPERTS + N_GROUPS)
    gl = jnp.where(gmask, lg, ninf)
    gmax = jnp.max(gl, axis=1, keepdims=True)
    g_sel = jnp.min(jnp.where(gmask & (gl == gmax), lane - N_EXPERTS, N_GROUPS), axis=1, keepdims=True)
    g_w = 1.0 / jnp.sum(jnp.where(gmask, jnp.exp(gl - gmax), 0.0), axis=1, keepdims=True)

    emask = (lane < N_EXPERTS) & ((lane // EXPERTS_PER_GROUP) == g_sel)
    el = jnp.where(emask, lg, ninf)
    emax = jnp.max(el, axis=1, keepdims=True)
    ee = jnp.where(emask, jnp.exp(el - emax), 0.0)
    e_prob = ee / jnp.sum(ee, axis=1, keepdims=True)
    p1 = jnp.max(jnp.where(emask, e_prob, -1.0), axis=1, keepdims=True)
    i1 = jnp.min(jnp.where(emask & (e_prob == p1), lane, LANES), axis=1, keepdims=True)
    rest = emask & (lane != i1)
    p2 = jnp.max(jnp.where(rest, e_prob, -1.0), axis=1, keepdims=True)
    i2 = jnp.min(jnp.where(rest & (e_prob == p2), lane, LANES), axis=1, keepdims=True)
    psum = p1 + p2
    w1 = g_w * (p1 / psum)
    w2 = g_w * (p2 / psum)
    ew_ref[...] = jnp.where(lane == 0, w1, jnp.where(lane == 1, w2, 0.0))

    @pl.when((pl.program_id(0) == 0) & (pl.program_id(1) == 0))
    def _():
        cnt_ref[...] = jnp.zeros(cnt_ref.shape, F32)

    onehot = jnp.where((lane == i1) | (lane == i2), 1.0, 0.0)
    tri = (lax.broadcasted_iota(I32, (tm, tm), 0) >= lax.broadcasted_iota(I32, (tm, tm), 1))
    cum = jnp.dot(jnp.where(tri, 1.0, 0.0).astype(BF16), onehot.astype(BF16),
                  preferred_element_type=F32)
    before = cum - onehot + cnt_ref[0:1, :]
    r1 = jnp.sum(jnp.where(lane == i1, before, 0.0), axis=1, keepdims=True).astype(I32)
    r2 = jnp.sum(jnp.where(lane == i2, before, 0.0), axis=1, keepdims=True).astype(I32)
    eid_ref[...] = jnp.where(lane == 0, i1, jnp.where(lane == 1, i2,
                             jnp.where(lane == 2, r1, jnp.where(lane == 3, r2, 0))))
    total = cnt_ref[0:1, :] + cum[tm - 1:tm, :]
    cnt_ref[...] = jnp.broadcast_to(total, cnt_ref.shape)
    counts_ref[...] = jnp.broadcast_to(total, counts_ref.shape)


def _post_call(attn, z, gates, x, mod, wa, wc, wo, g2, wr, br, tm):
    bsz, s, d = x.shape
    nt = s // tm
    tok = lambda w: pl.BlockSpec((None, tm, w), lambda b, t: (b, t, 0))
    full = lambda arr: pl.BlockSpec(arr.shape, lambda b, t: (0,) * arr.ndim)
    return pl.pallas_call(
        _post_kernel,
        grid=(bsz, nt),
        in_specs=[tok(ATTN_WIDTH), tok(CONV_WIDTH), tok(2 * d), tok(d),
                  pl.BlockSpec((None, 6, d), lambda b, t: (b, 0, 0)),
                  full(wa), full(wc), full(wo), full(g2), full(wr), full(br)],
        out_specs=[tok(d), pl.BlockSpec((None, tm) + ROW_TILE, lambda b, t: (b, t, 0, 0)),
                   tok(LANES), tok(LANES), pl.BlockSpec((8, LANES), lambda b, t: (0, 0))],
        out_shape=[jax.ShapeDtypeStruct((bsz, s, d), F32),
                   jax.ShapeDtypeStruct((bsz, s) + ROW_TILE, BF16),
                   jax.ShapeDtypeStruct((bsz, s, LANES), I32), jax.ShapeDtypeStruct((bsz, s, LANES), F32),
                   jax.ShapeDtypeStruct((8, LANES), F32)],
        scratch_shapes=[pltpu.VMEM((8, LANES), F32)],
        compiler_params=pltpu.CompilerParams(dimension_semantics=("arbitrary", "arbitrary"),
                                             vmem_limit_bytes=VMEM_LIMIT),
        name="post",
    )(attn, z, gates, x, mod, wa, wc, wo, g2, wr, br)


ROW_TILE = (D_MODEL // LANES, LANES)
EXPERT_ROWS = 256
TOKENS_PER_STEP = 256


ISSUE_UNROLL = 8


def _dispatch_kernel(pos1_ref, pos2_ref, h2_ref, xs_in_ref, xs_ref, sem):
    del xs_in_ref
    tm = h2_ref.shape[0]
    base = pl.program_id(0) * tm

    def issue(r, c):
        pltpu.make_async_copy(h2_ref.at[r], xs_ref.at[pos1_ref[base + r]], sem).start(priority=0)
        pltpu.make_async_copy(h2_ref.at[r], xs_ref.at[pos2_ref[base + r]], sem).start(priority=1)
        return c

    lax.fori_loop(0, tm, issue, 0, unroll=ISSUE_UNROLL)
    for _ in range(2):
        pltpu.make_async_copy(h2_ref, xs_ref.at[pl.ds(0, tm)], sem).wait()


def _dispatch_call(pos1, pos2, h2_rows, n_rows):
    t = h2_rows.shape[0]
    tm = min(TOKENS_PER_STEP, t)
    xs0 = jnp.zeros((n_rows,) + ROW_TILE, BF16)
    return pl.pallas_call(
        _dispatch_kernel,
        grid_spec=pltpu.PrefetchScalarGridSpec(
            num_scalar_prefetch=2,
            grid=(t // tm,),
            in_specs=[pl.BlockSpec((tm,) + ROW_TILE, lambda i, p1, p2: (i, 0, 0)),
                      pl.BlockSpec(memory_space=pl.ANY)],
            out_specs=pl.BlockSpec(memory_space=pl.ANY),
            scratch_shapes=[pltpu.SemaphoreType.DMA(())]),
        out_shape=jax.ShapeDtypeStruct((n_rows,) + ROW_TILE, BF16),
        input_output_aliases={3: 0},
        compiler_params=pltpu.CompilerParams(dimension_semantics=("arbitrary",),
                                             vmem_limit_bytes=VMEM_LIMIT),
        name="dispatch",
    )(pos1, pos2, h2_rows, xs0)


def _expert_kernel(te_ref, nu_ref, xs_ref, wg_ref, wu_ref, wd_ref, ys_ref, wgb, wub, wdb):
    g = pl.program_id(0)
    e = te_ref[g]
    e_prev = te_ref[jnp.maximum(g - 1, 0)]

    @pl.when((g == 0) | (e != e_prev))
    def _():
        wgb[...] = wg_ref[...].astype(BF16)
        wub[...] = wu_ref[...].astype(BF16)
        wdb[...] = wd_ref[...].astype(BF16)

    @pl.when(g < nu_ref[0])
    def _():
        x = xs_ref[...].reshape(xs_ref.shape[0], D_MODEL)
        a = jnp.dot(x, wgb[...], preferred_element_type=F32)
        b = jnp.dot(x, wub[...], preferred_element_type=F32)
        hid = (a * jax.nn.sigmoid(a)) * b
        y = jnp.dot(hid.astype(BF16), wdb[...], preferred_element_type=F32)
        ys_ref[...] = y.astype(BF16).reshape(ys_ref.shape)

    @pl.when(g >= nu_ref[0])
    def _():
        ys_ref[...] = jnp.zeros(ys_ref.shape, BF16)


def _expert_call(tile_expert, n_used, xs, wg, wu, wd):
    n_rows, d = xs.shape[0], D_MODEL
    nt = n_rows // EXPERT_ROWS
    rows = pl.BlockSpec((EXPERT_ROWS,) + ROW_TILE, lambda g, te, nu: (g, 0, 0))
    wspec = lambda shp: pl.BlockSpec((None,) + shp, lambda g, te, nu: (te[g], 0, 0))
    return pl.pallas_call(
        _expert_kernel,
        grid_spec=pltpu.PrefetchScalarGridSpec(
            num_scalar_prefetch=2,
            grid=(nt,),
            in_specs=[rows, wspec((d, D_EXPERT)), wspec((d, D_EXPERT)), wspec((D_EXPERT, d))],
            out_specs=rows,
            scratch_shapes=[pltpu.VMEM((d, D_EXPERT), BF16), pltpu.VMEM((d, D_EXPERT), BF16),
                            pltpu.VMEM((D_EXPERT, d), BF16)]),
        out_shape=jax.ShapeDtypeStruct((n_rows,) + ROW_TILE, BF16),
        compiler_params=pltpu.CompilerParams(dimension_semantics=("arbitrary",),
                                             vmem_limit_bytes=VMEM_LIMIT),
        name="experts",
    )(tile_expert, n_used, xs, wg, wu, wd)


def _combine_kernel(pos1_ref, pos2_ref, ys_ref, ew_ref, hres_ref, g2_ref, gf_ref, o_ref, buf, sem):
    tm = hres_ref.shape[0]
    t = pl.program_id(0)
    nt = pl.num_programs(0)

    def issue(tile, slot):
        base = tile * tm

        def body(r, c):
            pltpu.make_async_copy(ys_ref.at[pos1_ref[base + r]], buf.at[slot, r],
                                  sem.at[slot]).start(priority=0)
            pltpu.make_async_copy(ys_ref.at[pos2_ref[base + r]], buf.at[slot, tm + r],
                                  sem.at[slot]).start(priority=1)
            return c

        lax.fori_loop(0, tm, body, 0, unroll=ISSUE_UNROLL)

    @pl.when(t == 0)
    def _():
        issue(0, 0)

    @pl.when(t + 1 < nt)
    def _():
        issue(t + 1, (t + 1) % 2)

    slot = t % 2
    pltpu.make_async_copy(ys_ref.at[pl.ds(0, 2 * tm)], buf.at[slot], sem.at[slot]).wait()

    y1 = buf[slot, 0:tm].astype(F32).reshape(tm, D_MODEL)
    y2 = buf[slot, tm:2 * tm].astype(F32).reshape(tm, D_MODEL)
    w = ew_ref[...]
    y = w[:, 0:1] * y1 + w[:, 1:2] * y2
    hfin = hres_ref[...] + g2_ref[...] * y
    ms = jnp.mean(hfin * hfin, axis=-1, keepdims=True)
    o_ref[...] = hfin * lax.rsqrt(ms + EPS) * gf_ref[...]


def _combine_call(pos1, pos2, ys_rows, ew, hres, gate2, gf):
    t, d = hres.shape
    tm = min(TOKENS_PER_STEP, t)
    per_batch = t // gate2.shape[0]
    assert per_batch % tm == 0
    tok = lambda w: pl.BlockSpec((tm, w), lambda i, *_: (i, 0))
    return pl.pallas_call(
        _combine_kernel,
        grid_spec=pltpu.PrefetchScalarGridSpec(
            num_scalar_prefetch=2,
            grid=(t // tm,),
            in_specs=[pl.BlockSpec(memory_space=pl.ANY), tok(LANES), tok(d),
                      pl.BlockSpec((None, 1, d), lambda i, *_: ((i * tm) // per_batch, 0, 0)),
                      pl.BlockSpec((1, d), lambda i, *_: (0, 0))],
            out_specs=tok(d),
            scratch_shapes=[pltpu.VMEM((2, 2 * tm) + ROW_TILE, BF16), pltpu.SemaphoreType.DMA((2,))]),
        out_shape=jax.ShapeDtypeStruct((t, d), F32),
        compiler_params=pltpu.CompilerParams(dimension_semantics=("arbitrary",),
                                             vmem_limit_bytes=VMEM_LIMIT),
        name="combine",
    )(pos1, pos2, ys_rows, ew, hres, gate2, gf)


def _sparse_moe(h2, eid, ew, counts, wg, wu, wd, hres, gate2, gf):
    bsz, s, d = hres.shape
    t = bsz * s
    ne = wg.shape[0]
    e1, e2 = eid[..., 0].reshape(t), eid[..., 1].reshape(t)
    r1, r2 = eid[..., 2].reshape(t), eid[..., 3].reshape(t)
    cnt = counts[0, :ne].astype(I32)
    seg_rows = ((cnt + EXPERT_ROWS - 1) // EXPERT_ROWS) * EXPERT_ROWS
    seg_end = jnp.cumsum(seg_rows)
    seg_start = seg_end - seg_rows
    pos1 = seg_start[e1] + r1
    pos2 = seg_start[e2] + r2
    n_tiles = (2 * t) // EXPERT_ROWS + ne
    tile_first_row = jnp.arange(n_tiles, dtype=I32) * EXPERT_ROWS
    tile_expert = jnp.minimum(jnp.sum(seg_end[None, :] <= tile_first_row[:, None], axis=1), ne - 1)
    n_used = (seg_end[ne - 1] // EXPERT_ROWS).reshape(1)
    n_rows = n_tiles * EXPERT_ROWS

    xs = _dispatch_call(pos1, pos2, h2.reshape((t,) + ROW_TILE), n_rows)
    ys = _expert_call(tile_expert.astype(I32), n_used.astype(I32), xs, wg, wu, wd)
    out = _combine_call(pos1, pos2, ys, ew.reshape(t, LANES), hres.reshape(t, d),
                        gate2.reshape(bsz, 1, d), gf.reshape(1, d))
    return out.reshape(bsz, s, d)


def kernel(x, c, w_ada, b_ada, norm1_g, w_in, rel_bias, conv_w, w_attn_branch, w_conv_branch, w_out,
           norm2_g, w_router_group, b_router_group, w_router_expert, b_router_expert,
           w_gate_e, w_up_e, w_down_e, norm_f_g):
    bsz, s, d = x.shape
    depth = w_ada.shape[0]
    h = x.astype(F32)
    c_pad = jnp.zeros((8, d), F32).at[:bsz].set(c.astype(F32))
    out = h
    for l in range(depth):
        mod = _ada_call(c_pad, w_ada[l].astype(F32), b_ada[l].astype(F32)[None, :])
        mod = mod[:bsz].reshape(bsz, 6, d)

        w = w_in[l]
        o_q, o_k, o_v, o_qi = 0, 512, 1024, 1536
        o_ki, o_wi, o_cb, o_gl = 2048, 2112, 2120, 3656
        wa = w[:, o_q:o_ki].astype(BF16)
        ki_w = w[:, o_ki:o_wi]
        wb = jnp.concatenate([ki_w, ki_w, w[:, o_wi:o_cb],
                              jnp.zeros((d, LANES - IDX_HEADS), w.dtype)], axis=1).astype(BF16)
        wc = w[:, o_cb:o_gl].astype(BF16)
        wd = w[:, o_gl:].astype(BF16)

        q, k, v, qi, kid, wi, z, gates = _proj_call(
            h, mod, norm1_g[l].astype(F32)[None, :], wa, wb, wc, wd, conv_w[l].astype(F32),
            tm=min(512, s))

        attn = _attn_call(rel_bias.astype(F32), q, qi, wi, k, v, kid, tq=min(TOPK_MAX, s // 4))

        wr32 = jnp.concatenate([w_router_expert[l], w_router_group[l],
                                jnp.zeros((d, LANES - N_EXPERTS - N_GROUPS), F32)], axis=1).astype(F32)
        wr_hi = wr32.astype(BF16)
        wr = jnp.stack([wr_hi, (wr32 - wr_hi.astype(F32)).astype(BF16)])
        br = jnp.concatenate([b_router_expert[l], b_router_group[l],
                              jnp.zeros((LANES - N_EXPERTS - N_GROUPS,), F32)])[None, :].astype(F32)
        hres, h2, eid, ew, counts = _post_call(
            attn, z, gates, h, mod, w_attn_branch[l].astype(BF16), w_conv_branch[l].astype(BF16),
            w_out[l].astype(BF16), norm2_g[l].astype(F32)[None, :], wr, br, tm=min(512, s))

        assert depth == 1, "the final RMSNorm is fused into the combine kernel of the only layer"
        out = _sparse_moe(h2, eid, ew, counts, w_gate_e[l].astype(F32), w_up_e[l].astype(F32),
                          w_down_e[l].astype(F32), hres, mod[:, 5, :], norm_f_g.astype(F32))
        h = out
    return out.astype(x.dtype)
```

```python
import functools
import math

import numpy as np
import jax
import jax.numpy as jnp
from jax import lax
from jax.experimental import pallas as pl
from jax.experimental.pallas import tpu as pltpu

F32 = jnp.float32
BF16 = jnp.bfloat16
I32 = jnp.int32

D_MODEL = 1024
N_HEADS = 8
HEAD_DIM = 64
ATTN_WIDTH = N_HEADS * HEAD_DIM
IDX_HEADS = 8
IDX_DIM = 64
TOPK_MAX = 256
N_BUCKETS = 32
MAX_DISTANCE = 128
CONV_WIDTH = 512
CONV_K = 3
N_GROUPS = 4
EXPERTS_PER_GROUP = 8
N_EXPERTS = N_GROUPS * EXPERTS_PER_GROUP
D_EXPERT = 512
EPS = 1e-6

LANES = 128
VMEM_LIMIT = 56 * 1024 * 1024

LOG2E = math.log2(math.e)
MASK_NEG = -1e30
SCORE_NEG = float(np.finfo(np.float32).min)
TINY_F32 = float(np.finfo(np.float32).tiny)
KEY_SPACE_AFTER = 40
RADIX_BITS_PER_STEP = 4


def _t5_bucket_starts():
    max_exact = N_BUCKETS // 2
    n = np.arange(0, MAX_DISTANCE + 1)
    nf = np.maximum(n, 1).astype(np.float32)
    val = (np.log(nf / np.float32(max_exact)) / np.float32(math.log(MAX_DISTANCE / max_exact))
           * np.float32(N_BUCKETS - max_exact)).astype(np.float32)
    inner = val[max_exact + 1:MAX_DISTANCE]
    assert np.min(np.abs(inner - np.round(inner))) > 1e-3
    large = np.minimum(max_exact + val.astype(np.int32), N_BUCKETS - 1)
    bucket = np.where(n < max_exact, n, large)
    assert bucket[MAX_DISTANCE] == N_BUCKETS - 1 and np.all(np.diff(bucket) >= 0)
    return [int(np.argmax(bucket >= b)) for b in range(N_BUCKETS)]


_BUCKET_START = _t5_bucket_starts()


def _ada_kernel(c_ref, w_ref, b_ref, o_ref):
    c = c_ref[...]
    ca = c * jax.nn.sigmoid(c)
    o_ref[...] = jnp.dot(ca, w_ref[...], preferred_element_type=F32,
                         precision=lax.Precision.HIGHEST) + b_ref[...]


def _ada_call(c_pad, w, b):
    rows, d = c_pad.shape
    n = w.shape[1]
    tn = 1536
    return pl.pallas_call(
        _ada_kernel,
        grid=(n // tn,),
        in_specs=[pl.BlockSpec((rows, d), lambda j: (0, 0)),
                  pl.BlockSpec((d, tn), lambda j: (0, j)),
                  pl.BlockSpec((1, tn), lambda j: (0, j))],
        out_specs=pl.BlockSpec((rows, tn), lambda j: (0, j)),
        out_shape=jax.ShapeDtypeStruct((rows, n), F32),
        compiler_params=pltpu.CompilerParams(dimension_semantics=("arbitrary",),
                                             vmem_limit_bytes=VMEM_LIMIT),
        name="ada",
    )(c_pad, w, b)


def _rms_mod(x, g, scale, shift):
    ms = jnp.mean(x * x, axis=-1, keepdims=True)
    y = x * lax.rsqrt(ms + EPS) * g
    return y * (1.0 + scale) + shift


def _proj_kernel(x_ref, mod_ref, g_ref, wa_ref, wb_ref, wc_ref, wd_ref, cw_ref,
                 q_ref, k_ref, v_ref, qi_ref, kid_ref, wi_ref, z_ref, gate_ref, ubuf, *, tm):
    h = _rms_mod(x_ref[...], g_ref[...], mod_ref[1:2, :], mod_ref[0:1, :])
    hb = h.astype(BF16)

    a = jnp.dot(hb, wa_ref[...], preferred_element_type=F32)
    q_ref[...] = (a[:, 0:512] * (HEAD_DIM ** -0.5 * LOG2E)).astype(BF16)
    k_ref[...] = a[:, 512:1024].astype(BF16)
    v_ref[...] = a[:, 1024:1536].astype(BF16)
    qi_ref[...] = a[:, 1536:2048].astype(BF16)

    b = jnp.dot(hb, wb_ref[...], preferred_element_type=F32)
    kid_ref[...] = b[:, 0:LANES].astype(BF16)
    wi_ref[...] = b[:, LANES:2 * LANES] * ((IDX_DIM ** -0.5) * (IDX_HEADS ** -0.5))

    cpart = jnp.dot(hb, wc_ref[...], preferred_element_type=F32)
    cb = cpart[:, 0:512]
    u = cpart[:, 512:1024] * cpart[:, 1024:1536]

    @pl.when(pl.program_id(1) == 0)
    def _():
        ubuf[0:8, :] = jnp.zeros((8, CONV_WIDTH), F32)

    ubuf[8:tm + 8, :] = u
    conv = (cw_ref[0:1, :] * ubuf[6:tm + 6, :] + cw_ref[1:2, :] * ubuf[7:tm + 7, :]
            + cw_ref[2:3, :] * u)
    z_ref[...] = (cb * conv).astype(BF16)
    ubuf[0:8, :] = ubuf[tm:tm + 8, :]

    d = jnp.dot(hb, wd_ref[...], preferred_element_type=F32)
    gate_ref[...] = jax.nn.sigmoid(d).astype(BF16)


def _proj_call(x, mod, g1, wa, wb, wc, wd, cw, tm):
    bsz, s, d = x.shape
    nt = s // tm
    tok = lambda w: pl.BlockSpec((None, tm, w), lambda b, t: (b, t, 0))
    full = lambda arr: pl.BlockSpec(arr.shape, lambda b, t: (0,) * arr.ndim)
    out_widths = [(512, BF16), (512, BF16), (512, BF16), (512, BF16), (LANES, BF16), (LANES, F32),
                  (512, BF16), (2 * D_MODEL, BF16)]
    return pl.pallas_call(
        functools.partial(_proj_kernel, tm=tm),
        grid=(bsz, nt),
        in_specs=[tok(d), pl.BlockSpec((None, 6, d), lambda b, t: (b, 0, 0)), full(g1),
                  full(wa), full(wb), full(wc), full(wd), full(cw)],
        out_specs=[tok(w) for w, _ in out_widths],
        out_shape=[jax.ShapeDtypeStruct((bsz, s, w), dt) for w, dt in out_widths],
        scratch_shapes=[pltpu.VMEM((tm + 8, CONV_WIDTH), F32)],
        compiler_params=pltpu.CompilerParams(dimension_semantics=("arbitrary", "arbitrary"),
                                             vmem_limit_bytes=VMEM_LIMIT),
        name="proj",
    )(x, mod, g1, wa, wb, wc, wd, cw)


def _to_key(f):
    b = pltpu.bitcast(f, I32)
    return jnp.where(b < 0, b ^ 0x7FFFFFFF, b)


def _from_key(k):
    return pltpu.bitcast(jnp.where(k < 0, k ^ 0x7FFFFFFF, k), F32)


def _bit_transpose32(words):
    a = list(words)
    j, m = 16, 0x0000FFFF
    while j:
        k = 0
        while k < 32:
            t = (a[k] ^ lax.shift_right_logical(a[k + j], jnp.int32(j))) & jnp.int32(m)
            a[k] = a[k] ^ t
            a[k + j] = a[k + j] ^ lax.shift_left(t, jnp.int32(j))
            k = (k + j + 1) & ~j
        j >>= 1
        m = (m ^ (m << j)) & 0xFFFFFFFF
    return a


def _fold_lanes(x, op):
    parts = [x[:, t * LANES:(t + 1) * LANES] for t in range(x.shape[1] // LANES)]
    while len(parts) > 1:
        nxt = [op(parts[a], parts[a + 1]) for a in range(0, len(parts) - 1, 2)]
        parts = nxt + ([parts[-1]] if len(parts) % 2 else [])
    return parts[0]


def _tile_lanes(x, n):
    return x if n == 1 else jnp.concatenate([x] * n, axis=1)


def _attn_kernel(rb_ref, q_ref, qi_ref, wi_ref, qin_ref, win_ref, k_ref, v_ref, kid_ref, o_ref,
                 sct_ref, planes_ref, cand_ref, mb_ref, tb_ref, qm_ref, qim_ref, m_ref, l_ref, acc_ref,
                 *, tq, topk):
    c = tq
    i = pl.program_id(1)
    nt_dims = (((1,), (1,)), ((), ()))
    lane = lax.broadcasted_iota(I32, (tq, LANES), 1)
    low_half = lane < HEAD_DIM
    row = lax.broadcasted_iota(I32, (tq, c), 0)
    col = lax.broadcasted_iota(I32, (tq, c), 1)

    @pl.when((pl.program_id(0) == 0) & (i == 0))
    def _():
        for p in range(32):
            planes_ref[p] = jnp.zeros(planes_ref.shape[1:], I32)
        for kind in range(2):
            rel = row - col + kind * c
            n = jnp.maximum(rel, 0)
            for h in range(N_HEADS):
                t = jnp.full((tq, c), rb_ref[0, h], F32)
                for b in range(1, N_BUCKETS):
                    t = jnp.where(n >= _BUCKET_START[b], rb_ref[b, h], t)
                t = (t - rb_ref[N_BUCKETS - 1, h]) * LOG2E
                tb_ref[h, kind] = jnp.where(rel >= 0, t, MASK_NEG)


    def masked_heads(src_ref, dst_ref):
        for h in range(N_HEADS):
            hp, half = divmod(h, 2)
            keep = low_half if half == 0 else jnp.logical_not(low_half)
            pair = src_ref[:, hp * LANES:(hp + 1) * LANES]
            dst_ref[h] = jnp.where(keep, pair, jnp.zeros_like(pair))

    krow = lax.broadcasted_iota(I32, (c, tq), 0)
    qcol = lax.broadcasted_iota(I32, (c, tq), 1)
    int_min = jnp.int32(-2 ** 31)

    def fold_rows(x, op):
        parts = [x[r * 8:(r + 1) * 8, :] for r in range(x.shape[0] // 8)]
        while len(parts) > 1:
            nxt = [op(parts[a], parts[a + 1]) for a in range(0, len(parts) - 1, 2)]
            parts = nxt + ([parts[-1]] if len(parts) % 2 else [])
        return parts[0]

    def score_chunk(j, diag, w_t):
        start = pl.multiple_of(j * c, c)
        kc = kid_ref[pl.ds(start, c), :]
        acc = jnp.zeros((c, tq), F32)
        for h in range(N_HEADS):
            d = lax.dot_general(kc, qim_ref[h], nt_dims, preferred_element_type=F32)
            acc = acc + jnp.maximum(d, 0.0) * w_t[h:h + 1, :]
        ukey = _to_key(acc) ^ int_min
        if diag:
            causal = krow <= qcol
            acc = jnp.where(causal, acc, SCORE_NEG)
            ukey = jnp.where(causal, ukey, 0)
        sct_ref[pl.ds(start, c), :] = acc
        prow = pl.multiple_of(j * (c // 32), c // 32)
        for lt in range(tq // LANES):
            words = [ukey[k * 8:(k + 1) * 8, lt * LANES:(lt + 1) * LANES] for k in range(32)]
            for p, plane in enumerate(_bit_transpose32(words)):
                planes_ref[p, pl.ds(prow, c // 32), lt * LANES:(lt + 1) * LANES] = plane

    @pl.when(i == 0)
    def _():
        masked_heads(qi_ref, qim_ref)
        score_chunk(0, True, jnp.transpose(wi_ref[...])[0:IDX_HEADS, :])

    nchunks = i + 1
    n_causal = i * tq + lax.broadcasted_iota(I32, (1, tq), 1) + 1
    prow_iota = lax.broadcasted_iota(I32, cand_ref.shape, 0)
    cand_ref[...] = jnp.where(prow_iota < nchunks * (c // 32), -1, 0)

    def radix_cond(st):
        return (st[0] < 32) & (st[1] > 0)

    def radix_step(st):
        p0, _, prefix, need, ncand, settled = st
        cand = cand_ref[...]
        for sub in range(RADIX_BITS_PER_STEP):
            p = p0 + sub
            plane = planes_ref[p]
            ones = cand & plane
            cnt1 = jnp.sum(fold_rows(lax.population_count(ones), jnp.add), axis=0, keepdims=True)
            take1 = cnt1 >= need
            live = settled == 0
            need = jnp.where(live & jnp.logical_not(take1), need - cnt1, need)
            ncand = jnp.where(live, jnp.where(take1, cnt1, ncand - cnt1), ncand)
            prefix = jnp.where(live & take1, prefix | jnp.left_shift(jnp.int32(1), 31 - p), prefix)
            cand = cand & (plane ^ jnp.where(take1, 0, -1))
            settled = jnp.where(ncand == need, 1, settled)
        cand_ref[...] = cand
        return p0 + RADIX_BITS_PER_STEP, jnp.sum(1 - settled), prefix, need, ncand, settled

    done0 = jnp.where(n_causal <= topk, 1, 0)
    _, _, prefix, _, _, _ = lax.while_loop(
        radix_cond, radix_step,
        (jnp.int32(0), jnp.sum(1 - done0), jnp.zeros((1, tq), I32), jnp.full((1, tq), topk, I32),
         jnp.full((1, tq), 1, I32) * (nchunks * c), done0))
    guess_key = prefix ^ int_min
    guess = _from_key(guess_key)

    def count_keys(pred):
        def body(j, part):
            start = pl.multiple_of(j * c, c)
            hit = pred(sct_ref[pl.ds(start, c), :], start)
            return part + fold_rows(jnp.where(hit, 1.0, 0.0), jnp.add)
        part = lax.fori_loop(0, nchunks, body, jnp.zeros((8, tq), F32))
        return jnp.sum(part, axis=0, keepdims=True)

    kf = float(topk)
    guess_above = _from_key(guess_key + 1)
    guess_above = jnp.where(jnp.abs(guess_above) < TINY_F32,
                            jnp.where(guess >= 0.0, TINY_F32, 0.0), guess_above)
    guess_below = _from_key(guess_key - 2)

    def cond(st):
        return st[0] > 0

    def step(st):
        _, it, lo, hi, c_lo, c_hi, done = st
        mid = 0.5 * lo + 0.5 * hi
        lo_k, hi_k = _to_key(lo), _to_key(hi)
        mid_key = _from_key((lo_k >> 1) + (hi_k >> 1) + (lo_k & hi_k & 1))
        mid = jnp.where(it >= KEY_SPACE_AFTER, mid_key, mid)
        probe = jnp.where(it == 0, guess, jnp.where(lo == guess, guess_above, guess_below))
        mid = jnp.where((it <= 1) & (probe > lo) & (probe < hi), probe, mid)
        go = (done == 0) & (mid > lo) & (mid < hi)
        cnt = count_keys(lambda tile, start: tile >= mid)
        ge = cnt >= kf
        up = ge & go
        dn = jnp.logical_not(ge) & go
        lo = jnp.where(up, mid, lo)
        hi = jnp.where(dn, mid, hi)
        c_lo = jnp.where(up, cnt, c_lo)
        c_hi = jnp.where(dn, cnt, c_hi)
        adjacent = (lo == guess) & (hi == guess_above)
        done = jnp.where(go & (c_lo != kf) & jnp.logical_not(adjacent), 0, 1)
        return jnp.sum(1 - done), it + 1, lo, hi, c_lo, c_hi, done

    _, _, thr, _, c_lo, c_hi, _ = lax.while_loop(
        cond, step, (jnp.sum(1 - done0), jnp.int32(0), jnp.full((1, tq), SCORE_NEG, F32),
                     jnp.full((1, tq), -SCORE_NEG, F32), jnp.full((1, tq), 1.0, F32) * (nchunks * c),
                     jnp.zeros((1, tq), F32), done0))

    tied = (n_causal > topk) & (c_lo > kf)
    any_tied = jnp.sum(jnp.where(tied, 1, 0)) > 0

    @pl.when(any_tied)
    def _():
        n_take = kf - c_hi

        def bis(_, st):
            lo_x, hi_x = st
            mid = (lo_x + hi_x) >> 1
            ok = count_keys(lambda tile, start: (tile == thr) & ((krow + start) <= mid)) >= n_take
            return jnp.where(ok, lo_x, mid), jnp.where(ok, mid, hi_x)

        nbits = int(math.ceil(math.log2(sct_ref.shape[0]))) + 1
        _, hi_x = lax.fori_loop(0, nbits, bis, (jnp.full((1, tq), -1, I32),
                                                jnp.full((1, tq), sct_ref.shape[0], I32)))
        cut = jnp.where(tied, hi_x, jnp.int32(2 ** 30))

        def mask_body(j, _):
            start = pl.multiple_of(j * c, c)
            tile = sct_ref[pl.ds(start, c), :]
            sel = (tile > thr) | ((tile == thr) & ((krow + start) <= cut))
            mb_ref[:, pl.ds(start, c)] = jnp.transpose(jnp.where(sel, 0.0, MASK_NEG))
            return 0

        lax.fori_loop(0, nchunks, mask_body, 0)

    @pl.when(jnp.logical_not(any_tied))
    def _():
        def mask_body(j, _):
            start = pl.multiple_of(j * c, c)
            tile = sct_ref[pl.ds(start, c), :]
            mb_ref[:, pl.ds(start, c)] = jnp.transpose(jnp.where(tile >= thr, 0.0, MASK_NEG))
            return 0

        lax.fori_loop(0, nchunks, mask_body, 0)

    masked_heads(q_ref, qm_ref)
    masked_heads(qin_ref, qim_ref)
    w_t_next = jnp.transpose(win_ref[...])[0:IDX_HEADS, :]
    has_next = i + 1 < pl.num_programs(1)
    m_ref[...] = jnp.full(m_ref.shape, MASK_NEG, F32)
    l_ref[...] = jnp.zeros(l_ref.shape, F32)
    acc_ref[...] = jnp.zeros(acc_ref.shape, F32)

    def attend(start, width, kind):
        nt = width // LANES
        ones = jnp.ones((width, LANES), BF16)
        for h in range(N_HEADS):
            hp = h // 2
            kp = k_ref[pl.ds(start, width), hp * LANES:(hp + 1) * LANES]
            vp = v_ref[pl.ds(start, width), hp * LANES:(hp + 1) * LANES]
            s = lax.dot_general(qm_ref[h], kp, nt_dims, preferred_element_type=F32)
            s = s + mb_ref[:, pl.ds(start, width)]
            if kind is not None:
                s = s + tb_ref[h, kind]
            m_old = m_ref[h]
            m_new = jnp.maximum(m_old, jnp.max(_fold_lanes(s, jnp.maximum), axis=1, keepdims=True))
            p = jnp.exp2(s - _tile_lanes(m_new, nt)).astype(BF16)
            pv = jnp.dot(p, jnp.concatenate([vp, ones], axis=1), preferred_element_type=F32)
            alpha = jnp.exp2(m_old - m_new)
            m_ref[h] = m_new
            l_ref[h] = alpha * l_ref[h] + pv[:, LANES:]
            acc_ref[h] = alpha * acc_ref[h] + pv[:, :LANES]

    wide = 2 * c
    n_far = jnp.maximum(i - 1, 0)

    n_wide = n_far // 2

    def far_body(j, _):
        attend(pl.multiple_of(j * wide, wide), wide, None)
        score_chunk(2 * j, False, w_t_next)
        score_chunk(2 * j + 1, False, w_t_next)
        return 0

    lax.fori_loop(0, n_wide, far_body, 0)

    @pl.when(has_next)
    def _():
        def score_body(j, carry):
            score_chunk(j, False, w_t_next)
            return carry

        lax.fori_loop(2 * n_wide, i + 1, score_body, 0)
        score_chunk(i + 1, True, w_t_next)

    @pl.when(n_far % 2 == 1)
    def _():
        attend(pl.multiple_of((n_far - 1) * c, c), c, None)

    @pl.when(i >= 1)
    def _():
        attend(pl.multiple_of((i - 1) * c, c), c, 1)

    attend(pl.multiple_of(i * c, c), c, 0)

    for hp in range(N_HEADS // 2):
        lo = acc_ref[2 * hp] / l_ref[2 * hp]
        hi = acc_ref[2 * hp + 1] / l_ref[2 * hp + 1]
        o_ref[:, hp * LANES:(hp + 1) * LANES] = jnp.where(low_half, lo, hi).astype(BF16)


def _attn_call(rel_bias, q, qi, wi, k, v, kid, tq):
    bsz, s, _ = q.shape
    nq = s // tq
    topk = min(TOPK_MAX, s // 4)
    assert tq % LANES == 0 and tq >= MAX_DISTANCE
    blk = lambda w: pl.BlockSpec((None, tq, w), lambda b, i: (b, i, 0))
    nxt = lambda w: pl.BlockSpec((None, tq, w), lambda b, i: (b, jnp.minimum(i + 1, nq - 1), 0))
    whole = lambda w: pl.BlockSpec((None, s, w), lambda b, i: (b, 0, 0), pipeline_mode=pl.Buffered(1))
    return pl.pallas_call(
        functools.partial(_attn_kernel, tq=tq, topk=topk),
        grid=(bsz, nq),
        in_specs=[pl.BlockSpec(memory_space=pltpu.SMEM),
                  blk(ATTN_WIDTH), blk(ATTN_WIDTH), blk(LANES), nxt(ATTN_WIDTH), nxt(LANES),
                  whole(ATTN_WIDTH), whole(ATTN_WIDTH), whole(LANES)],
        out_specs=blk(ATTN_WIDTH),
        out_shape=jax.ShapeDtypeStruct((bsz, s, ATTN_WIDTH), BF16),
        scratch_shapes=[
            pltpu.VMEM((s, tq), F32),
            pltpu.VMEM((32, s // 32, tq), I32),
            pltpu.VMEM((s // 32, tq), I32),
            pltpu.VMEM((tq, s), F32),
            pltpu.VMEM((N_HEADS, 2, tq, tq), F32),
            pltpu.VMEM((N_HEADS, tq, LANES), BF16),
            pltpu.VMEM((N_HEADS, tq, LANES), BF16),
            pltpu.VMEM((N_HEADS, tq, LANES), F32),
            pltpu.VMEM((N_HEADS, tq, LANES), F32),
            pltpu.VMEM((N_HEADS, tq, LANES), F32),
        ],
        compiler_params=pltpu.CompilerParams(dimension_semantics=("arbitrary", "arbitrary"),
                                             vmem_limit_bytes=VMEM_LIMIT),
        name="attn",
    )(rel_bias, q, qi, wi, qi, wi, k, v, kid)


def _post_kernel(attn_ref, z_ref, gate_ref, x_ref, mod_ref, wa_ref, wc_ref, wo_ref, g2_ref,
                 wr_ref, br_ref, hres_ref, h2_ref, eid_ref, ew_ref, counts_ref, cnt_ref):
    ya = jnp.dot(attn_ref[...], wa_ref[...], preferred_element_type=F32)
    yc = jnp.dot(z_ref[...], wc_ref[...], preferred_element_type=F32)
    merged = gate_ref[:, 0:D_MODEL].astype(F32) * ya + gate_ref[:, D_MODEL:].astype(F32) * yc
    o = jnp.dot(merged.astype(BF16), wo_ref[...], preferred_element_type=F32)
    hres = x_ref[...] + mod_ref[2:3, :] * o
    hres_ref[...] = hres
    h2 = _rms_mod(hres, g2_ref[...], mod_ref[4:5, :], mod_ref[3:4, :])
    h2_ref[...] = h2.astype(BF16).reshape(h2_ref.shape)

    h2_hi = h2.astype(BF16)
    h2_lo = (h2 - h2_hi.astype(F32)).astype(BF16)
    lg = (jnp.dot(h2_hi, wr_ref[0], preferred_element_type=F32)
          + (jnp.dot(h2_lo, wr_ref[0], preferred_element_type=F32)
             + jnp.dot(h2_hi, wr_ref[1], preferred_element_type=F32))) + br_ref[...]
    tm = lg.shape[0]
    lane = lax.broadcasted_iota(I32, (tm, LANES), 1)
    ninf = -jnp.inf
    gmask = (lane >= N_EXPERTS) & (lane < N_EXPERTS + N_GROUPS)
    gl = jnp.where(gmask, lg, ninf)
    gmax = jnp.max(gl, axis=1, keepdims=True)
    g_sel = jnp.min(jnp.where(gmask & (gl == gmax), lane - N_EXPERTS, N_GROUPS), axis=1, keepdims=True)
    g_w = 1.0 / jnp.sum(jnp.where(gmask, jnp.exp(gl - gmax), 0.0), axis=1, keepdims=True)

    emask = (lane < N_EXPERTS) & ((lane // EXPERTS_PER_GROUP) == g_sel)
    el = jnp.where(emask, lg, ninf)
    emax = jnp.max(el, axis=1, keepdims=True)
    ee = jnp.where(emask, jnp.exp(el - emax), 0.0)
    e_prob = ee / jnp.sum(ee, axis=1, keepdims=True)
    p1 = jnp.max(jnp.where(emask, e_prob, -1.0), axis=1, keepdims=True)
    i1 = jnp.min(jnp.where(emask & (e_prob == p1), lane, LANES), axis=1, keepdims=True)
    rest = emask & (lane != i1)
    p2 = jnp.max(jnp.where(rest, e_prob, -1.0), axis=1, keepdims=True)
    i2 = jnp.min(jnp.where(rest & (e_prob == p2), lane, LANES), axis=1, keepdims=True)
    psum = p1 + p2
    w1 = g_w * (p1 / psum)
    w2 = g_w * (p2 / psum)
    ew_ref[...] = jnp.where(lane == 0, w1, jnp.where(lane == 1, w2, 0.0))

    @pl.when((pl.program_id(0) == 0) & (pl.program_id(1) == 0))
    def _():
        cnt_ref[...] = jnp.zeros(cnt_ref.shape, F32)

    onehot = jnp.where((lane == i1) | (lane == i2), 1.0, 0.0)
    tri = (lax.broadcasted_iota(I32, (tm, tm), 0) >= lax.broadcasted_iota(I32, (tm, tm), 1))
    cum = jnp.dot(jnp.where(tri, 1.0, 0.0).astype(BF16), onehot.astype(BF16),
                  preferred_element_type=F32)
    before = cum - onehot + cnt_ref[0:1, :]
    r1 = jnp.sum(jnp.where(lane == i1, before, 0.0), axis=1, keepdims=True).astype(I32)
    r2 = jnp.sum(jnp.where(lane == i2, before, 0.0), axis=1, keepdims=True).astype(I32)
    packed = jnp.where(lane == 0, i1, jnp.where(lane == 1, i2,
                       jnp.where(lane == 2, r1, jnp.where(lane == 3, r2, 0))))
    eid_ref[...] = jnp.transpose(packed)[0:8, :]
    total = cnt_ref[0:1, :] + cum[tm - 1:tm, :]
    cnt_ref[...] = jnp.broadcast_to(total, cnt_ref.shape)
    counts_ref[...] = jnp.broadcast_to(total, counts_ref.shape)


def _post_call(attn, z, gates, x, mod, wa, wc, wo, g2, wr, br, tm):
    bsz, s, d = x.shape
    nt = s // tm
    tok = lambda w: pl.BlockSpec((None, tm, w), lambda b, t: (b, t, 0))
    full = lambda arr: pl.BlockSpec(arr.shape, lambda b, t: (0,) * arr.ndim)
    return pl.pallas_call(
        _post_kernel,
        grid=(bsz, nt),
        in_specs=[tok(ATTN_WIDTH), tok(CONV_WIDTH), tok(2 * d), tok(d),
                  pl.BlockSpec((None, 6, d), lambda b, t: (b, 0, 0)),
                  full(wa), full(wc), full(wo), full(g2), full(wr), full(br)],
        out_specs=[tok(d), pl.BlockSpec((None, tm) + ROW_TILE, lambda b, t: (b, t, 0, 0)),
                   pl.BlockSpec((8, tm), lambda b, t: (0, b * nt + t)), tok(LANES),
                   pl.BlockSpec((8, LANES), lambda b, t: (0, 0))],
        out_shape=[jax.ShapeDtypeStruct((bsz, s, d), F32),
                   jax.ShapeDtypeStruct((bsz, s) + ROW_TILE, BF16),
                   jax.ShapeDtypeStruct((8, bsz * s), I32), jax.ShapeDtypeStruct((bsz, s, LANES), F32),
                   jax.ShapeDtypeStruct((8, LANES), F32)],
        scratch_shapes=[pltpu.VMEM((8, LANES), F32)],
        compiler_params=pltpu.CompilerParams(dimension_semantics=("arbitrary", "arbitrary"),
                                             vmem_limit_bytes=VMEM_LIMIT),
        name="post",
    )(attn, z, gates, x, mod, wa, wc, wo, g2, wr, br)


ROW_TILE = (D_MODEL // LANES, LANES)
EXPERT_ROWS = 256
TOKENS_PER_STEP = 256


ISSUE_UNROLL = 8


def _dispatch_kernel(pos1_ref, pos2_ref, h2_ref, xs_in_ref, xs_ref, sem):
    del xs_in_ref
    tm = h2_ref.shape[0]
    base = pl.program_id(0) * tm

    def issue(r, c):
        pltpu.make_async_copy(h2_ref.at[r], xs_ref.at[pos1_ref[base + r]], sem).start(priority=0)
        pltpu.make_async_copy(h2_ref.at[r], xs_ref.at[pos2_ref[base + r]], sem).start(priority=1)
        return c

    lax.fori_loop(0, tm, issue, 0, unroll=ISSUE_UNROLL)
    for _ in range(2):
        pltpu.make_async_copy(h2_ref, xs_ref.at[pl.ds(0, tm)], sem).wait()


def _dispatch_call(pos1, pos2, h2_rows, n_rows):
    t = h2_rows.shape[0]
    tm = min(TOKENS_PER_STEP, t)
    xs0 = jnp.zeros((n_rows,) + ROW_TILE, BF16)
    return pl.pallas_call(
        _dispatch_kernel,
        grid_spec=pltpu.PrefetchScalarGridSpec(
            num_scalar_prefetch=2,
            grid=(t // tm,),
            in_specs=[pl.BlockSpec((tm,) + ROW_TILE, lambda i, p1, p2: (i, 0, 0)),
                      pl.BlockSpec(memory_space=pl.ANY)],
            out_specs=pl.BlockSpec(memory_space=pl.ANY),
            scratch_shapes=[pltpu.SemaphoreType.DMA(())]),
        out_shape=jax.ShapeDtypeStruct((n_rows,) + ROW_TILE, BF16),
        input_output_aliases={3: 0},
        compiler_params=pltpu.CompilerParams(dimension_semantics=("arbitrary",),
                                             vmem_limit_bytes=VMEM_LIMIT),
        name="dispatch",
    )(pos1, pos2, h2_rows, xs0)


def _expert_kernel(te_ref, nu_ref, xs_ref, wg_ref, wu_ref, wd_ref, ys_ref, wgb, wub, wdb):
    g = pl.program_id(0)
    e = te_ref[g]
    e_prev = te_ref[jnp.maximum(g - 1, 0)]

    @pl.when((g == 0) | (e != e_prev))
    def _():
        wgb[...] = wg_ref[...].astype(BF16)
        wub[...] = wu_ref[...].astype(BF16)
        wdb[...] = wd_ref[...].astype(BF16)

    @pl.when(g < nu_ref[0])
    def _():
        x = xs_ref[...].reshape(xs_ref.shape[0], D_MODEL)
        a = jnp.dot(x, wgb[...], preferred_element_type=F32)
        b = jnp.dot(x, wub[...], preferred_element_type=F32)
        hid = (a * jax.nn.sigmoid(a)) * b
        y = jnp.dot(hid.astype(BF16), wdb[...], preferred_element_type=F32)
        ys_ref[...] = y.astype(BF16).reshape(ys_ref.shape)

    @pl.when(g >= nu_ref[0])
    def _():
        ys_ref[...] = jnp.zeros(ys_ref.shape, BF16)


def _expert_call(tile_expert, n_used, xs, wg, wu, wd):
    n_rows, d = xs.shape[0], D_MODEL
    nt = n_rows // EXPERT_ROWS
    rows = pl.BlockSpec((EXPERT_ROWS,) + ROW_TILE, lambda g, te, nu: (g, 0, 0))
    wspec = lambda shp: pl.BlockSpec((None,) + shp, lambda g, te, nu: (te[g], 0, 0))
    return pl.pallas_call(
        _expert_kernel,
        grid_spec=pltpu.PrefetchScalarGridSpec(
            num_scalar_prefetch=2,
            grid=(nt,),
            in_specs=[rows, wspec((d, D_EXPERT)), wspec((d, D_EXPERT)), wspec((D_EXPERT, d))],
            out_specs=rows,
            scratch_shapes=[pltpu.VMEM((d, D_EXPERT), BF16), pltpu.VMEM((d, D_EXPERT), BF16),
                            pltpu.VMEM((D_EXPERT, d), BF16)]),
        out_shape=jax.ShapeDtypeStruct((n_rows,) + ROW_TILE, BF16),
        compiler_params=pltpu.CompilerParams(dimension_semantics=("arbitrary",),
                                             vmem_limit_bytes=VMEM_LIMIT),
        name="experts",
    )(tile_expert, n_used, xs, wg, wu, wd)


def _combine_kernel(pos1_ref, pos2_ref, ys_ref, ew_ref, hres_ref, g2_ref, gf_ref, o_ref, buf, sem):
    tm = hres_ref.shape[0]
    t = pl.program_id(0)
    nt = pl.num_programs(0)

    def issue(tile, slot):
        base = tile * tm

        def body(r, c):
            pltpu.make_async_copy(ys_ref.at[pos1_ref[base + r]], buf.at[slot, r],
                                  sem.at[slot]).start(priority=0)
            pltpu.make_async_copy(ys_ref.at[pos2_ref[base + r]], buf.at[slot, tm + r],
                                  sem.at[slot]).start(priority=1)
            return c

        lax.fori_loop(0, tm, body, 0, unroll=ISSUE_UNROLL)

    @pl.when(t == 0)
    def _():
        issue(0, 0)

    @pl.when(t + 1 < nt)
    def _():
        issue(t + 1, (t + 1) % 2)

    slot = t % 2
    pltpu.make_async_copy(ys_ref.at[pl.ds(0, 2 * tm)], buf.at[slot], sem.at[slot]).wait()

    y1 = buf[slot, 0:tm].astype(F32).reshape(tm, D_MODEL)
    y2 = buf[slot, tm:2 * tm].astype(F32).reshape(tm, D_MODEL)
    w = ew_ref[...]
    y = w[:, 0:1] * y1 + w[:, 1:2] * y2
    hfin = hres_ref[...] + g2_ref[...] * y
    ms = jnp.mean(hfin * hfin, axis=-1, keepdims=True)
    o_ref[...] = hfin * lax.rsqrt(ms + EPS) * gf_ref[...]


def _combine_call(pos1, pos2, ys_rows, ew, hres, gate2, gf):
    t, d = hres.shape
    tm = min(TOKENS_PER_STEP, t)
    per_batch = t // gate2.shape[0]
    assert per_batch % tm == 0
    tok = lambda w: pl.BlockSpec((tm, w), lambda i, *_: (i, 0))
    return pl.pallas_call(
        _combine_kernel,
        grid_spec=pltpu.PrefetchScalarGridSpec(
            num_scalar_prefetch=2,
            grid=(t // tm,),
            in_specs=[pl.BlockSpec(memory_space=pl.ANY), tok(LANES), tok(d),
                      pl.BlockSpec((None, 1, d), lambda i, *_: ((i * tm) // per_batch, 0, 0)),
                      pl.BlockSpec((1, d), lambda i, *_: (0, 0))],
            out_specs=tok(d),
            scratch_shapes=[pltpu.VMEM((2, 2 * tm) + ROW_TILE, BF16), pltpu.SemaphoreType.DMA((2,))]),
        out_shape=jax.ShapeDtypeStruct((t, d), F32),
        compiler_params=pltpu.CompilerParams(dimension_semantics=("arbitrary",),
                                             vmem_limit_bytes=VMEM_LIMIT),
        name="combine",
    )(pos1, pos2, ys_rows, ew, hres, gate2, gf)


def _slot_kernel(seg_ref, eid_ref, pos_ref):
    e = eid_ref[0:2, :]
    pos = eid_ref[2:4, :]
    for k in range(seg_ref.shape[0]):
        pos = pos + jnp.where(e == k, seg_ref[k], 0)
    pos_ref[...] = jnp.concatenate([pos, jnp.zeros((6, pos.shape[1]), I32)], axis=0)


def _slot_call(seg_start, eid):
    return pl.pallas_call(
        _slot_kernel,
        in_specs=[pl.BlockSpec(memory_space=pltpu.SMEM), pl.BlockSpec(eid.shape, lambda: (0, 0))],
        out_specs=pl.BlockSpec(eid.shape, lambda: (0, 0)),
        out_shape=jax.ShapeDtypeStruct(eid.shape, I32),
        name="slots",
    )(seg_start, eid)


def _sparse_moe(h2, eid, ew, counts, wg, wu, wd, hres, gate2, gf):
    bsz, s, d = hres.shape
    t = bsz * s
    ne = wg.shape[0]
    cnt = counts[0, :ne].astype(I32)
    seg_rows = ((cnt + EXPERT_ROWS - 1) // EXPERT_ROWS) * EXPERT_ROWS
    seg_end = jnp.cumsum(seg_rows)
    seg_start = seg_end - seg_rows
    pos = _slot_call(seg_start, eid)
    pos1, pos2 = pos[0], pos[1]
    n_tiles = (2 * t) // EXPERT_ROWS + ne
    tile_first_row = jnp.arange(n_tiles, dtype=I32) * EXPERT_ROWS
    tile_expert = jnp.minimum(jnp.sum(seg_end[None, :] <= tile_first_row[:, None], axis=1), ne - 1)
    n_used = (seg_end[ne - 1] // EXPERT_ROWS).reshape(1)
    n_rows = n_tiles * EXPERT_ROWS

    xs = _dispatch_call(pos1, pos2, h2.reshape((t,) + ROW_TILE), n_rows)
    ys = _expert_call(tile_expert.astype(I32), n_used.astype(I32), xs, wg, wu, wd)
    out = _combine_call(pos1, pos2, ys, ew.reshape(t, LANES), hres.reshape(t, d),
                        gate2.reshape(bsz, 1, d), gf.reshape(1, d))
    return out.reshape(bsz, s, d)


def kernel(x, c, w_ada, b_ada, norm1_g, w_in, rel_bias, conv_w, w_attn_branch, w_conv_branch, w_out,
           norm2_g, w_router_group, b_router_group, w_router_expert, b_router_expert,
           w_gate_e, w_up_e, w_down_e, norm_f_g):
    bsz, s, d = x.shape
    depth = w_ada.shape[0]
    h = x.astype(F32)
    c_pad = jnp.zeros((8, d), F32).at[:bsz].set(c.astype(F32))
    out = h
    for l in range(depth):
        mod = _ada_call(c_pad, w_ada[l].astype(F32), b_ada[l].astype(F32)[None, :])
        mod = mod[:bsz].reshape(bsz, 6, d)

        w = w_in[l]
        o_q, o_k, o_v, o_qi = 0, 512, 1024, 1536
        o_ki, o_wi, o_cb, o_gl = 2048, 2112, 2120, 3656
        wa = w[:, o_q:o_ki].astype(BF16)
        ki_w = w[:, o_ki:o_wi]
        wb = jnp.concatenate([ki_w, ki_w, w[:, o_wi:o_cb],
                              jnp.zeros((d, LANES - IDX_HEADS), w.dtype)], axis=1).astype(BF16)
        wc = w[:, o_cb:o_gl].astype(BF16)
        wd = w[:, o_gl:].astype(BF16)

        q, k, v, qi, kid, wi, z, gates = _proj_call(
            h, mod, norm1_g[l].astype(F32)[None, :], wa, wb, wc, wd, conv_w[l].astype(F32),
            tm=min(512, s))

        attn = _attn_call(rel_bias.astype(F32), q, qi, wi, k, v, kid, tq=min(TOPK_MAX, s // 4))

        wr32 = jnp.concatenate([w_router_expert[l], w_router_group[l],
                                jnp.zeros((d, LANES - N_EXPERTS - N_GROUPS), F32)], axis=1).astype(F32)
        wr_hi = wr32.astype(BF16)
        wr = jnp.stack([wr_hi, (wr32 - wr_hi.astype(F32)).astype(BF16)])
        br = jnp.concatenate([b_router_expert[l], b_router_group[l],
                              jnp.zeros((LANES - N_EXPERTS - N_GROUPS,), F32)])[None, :].astype(F32)
        hres, h2, eid, ew, counts = _post_call(
            attn, z, gates, h, mod, w_attn_branch[l].astype(BF16), w_conv_branch[l].astype(BF16),
            w_out[l].astype(BF16), norm2_g[l].astype(F32)[None, :], wr, br, tm=min(512, s))

        assert depth == 1, "the final RMSNorm is fused into the combine kernel of the only layer"
        out = _sparse_moe(h2, eid, ew, counts, w_gate_e[l].astype(F32), w_up_e[l].astype(F32),
                          w_down_e[l].astype(F32), hres, mod[:, 5, :], norm_f_g.astype(F32))
        h = out
    return out.astype(x.dtype)
```

```python
import functools
import math

import numpy as np
import jax
import jax.numpy as jnp
from jax import lax
from jax.experimental import pallas as pl
from jax.experimental.pallas import tpu as pltpu

F32 = jnp.float32
BF16 = jnp.bfloat16
I32 = jnp.int32

D_MODEL = 1024
N_HEADS = 8
HEAD_DIM = 64
ATTN_WIDTH = N_HEADS * HEAD_DIM
IDX_HEADS = 8
IDX_DIM = 64
TOPK_MAX = 256
N_BUCKETS = 32
MAX_DISTANCE = 128
CONV_WIDTH = 512
CONV_K = 3
N_GROUPS = 4
EXPERTS_PER_GROUP = 8
N_EXPERTS = N_GROUPS * EXPERTS_PER_GROUP
D_EXPERT = 512
EPS = 1e-6

LANES = 128
VMEM_LIMIT = 56 * 1024 * 1024

LOG2E = math.log2(math.e)
MASK_NEG = -1e30
SCORE_NEG = float(np.finfo(np.float32).min)
TINY_F32 = float(np.finfo(np.float32).tiny)
KEY_SPACE_AFTER = 40
RADIX_BITS_PER_STEP = 4


def _t5_bucket_starts():
    max_exact = N_BUCKETS // 2
    n = np.arange(0, MAX_DISTANCE + 1)
    nf = np.maximum(n, 1).astype(np.float32)
    val = (np.log(nf / np.float32(max_exact)) / np.float32(math.log(MAX_DISTANCE / max_exact))
           * np.float32(N_BUCKETS - max_exact)).astype(np.float32)
    inner = val[max_exact + 1:MAX_DISTANCE]
    assert np.min(np.abs(inner - np.round(inner))) > 1e-3
    large = np.minimum(max_exact + val.astype(np.int32), N_BUCKETS - 1)
    bucket = np.where(n < max_exact, n, large)
    assert bucket[MAX_DISTANCE] == N_BUCKETS - 1 and np.all(np.diff(bucket) >= 0)
    return [int(np.argmax(bucket >= b)) for b in range(N_BUCKETS)]


_BUCKET_START = _t5_bucket_starts()


def _ada_kernel(c_ref, w_ref, b_ref, o_ref):
    c = c_ref[...]
    ca = c * jax.nn.sigmoid(c)
    o_ref[...] = jnp.dot(ca, w_ref[...], preferred_element_type=F32,
                         precision=lax.Precision.HIGHEST) + b_ref[...]


def _ada_call(c_pad, w, b):
    rows, d = c_pad.shape
    n = w.shape[1]
    tn = 1536
    return pl.pallas_call(
        _ada_kernel,
        grid=(n // tn,),
        in_specs=[pl.BlockSpec((rows, d), lambda j: (0, 0)),
                  pl.BlockSpec((d, tn), lambda j: (0, j)),
                  pl.BlockSpec((1, tn), lambda j: (0, j))],
        out_specs=pl.BlockSpec((rows, tn), lambda j: (0, j)),
        out_shape=jax.ShapeDtypeStruct((rows, n), F32),
        compiler_params=pltpu.CompilerParams(dimension_semantics=("arbitrary",),
                                             vmem_limit_bytes=VMEM_LIMIT),
        name="ada",
    )(c_pad, w, b)


def _rms_mod(x, g, scale, shift):
    ms = jnp.mean(x * x, axis=-1, keepdims=True)
    y = x * lax.rsqrt(ms + EPS) * g
    return y * (1.0 + scale) + shift


def _proj_kernel(x_ref, mod_ref, g_ref, wa_ref, wb_ref, wc_ref, wd_ref, cw_ref,
                 q_ref, k_ref, v_ref, qi_ref, kid_ref, wi_ref, z_ref, gate_ref, ubuf, *, tm):
    h = _rms_mod(x_ref[...], g_ref[...], mod_ref[1:2, :], mod_ref[0:1, :])
    hb = h.astype(BF16)

    a = jnp.dot(hb, wa_ref[...], preferred_element_type=F32)
    q_ref[...] = (a[:, 0:512] * (HEAD_DIM ** -0.5 * LOG2E)).astype(BF16)
    k_ref[...] = a[:, 512:1024].astype(BF16)
    v_ref[...] = a[:, 1024:1536].astype(BF16)
    qi_ref[...] = a[:, 1536:2048].astype(BF16)

    b = jnp.dot(hb, wb_ref[...], preferred_element_type=F32)
    kid_ref[...] = b[:, 0:LANES].astype(BF16)
    wi_ref[...] = b[:, LANES:2 * LANES] * ((IDX_DIM ** -0.5) * (IDX_HEADS ** -0.5))

    cpart = jnp.dot(hb, wc_ref[...], preferred_element_type=F32)
    cb = cpart[:, 0:512]
    u = cpart[:, 512:1024] * cpart[:, 1024:1536]

    @pl.when(pl.program_id(1) == 0)
    def _():
        ubuf[0:8, :] = jnp.zeros((8, CONV_WIDTH), F32)

    ubuf[8:tm + 8, :] = u
    conv = (cw_ref[0:1, :] * ubuf[6:tm + 6, :] + cw_ref[1:2, :] * ubuf[7:tm + 7, :]
            + cw_ref[2:3, :] * u)
    z_ref[...] = (cb * conv).astype(BF16)
    ubuf[0:8, :] = ubuf[tm:tm + 8, :]

    d = jnp.dot(hb, wd_ref[...], preferred_element_type=F32)
    gate_ref[...] = jax.nn.sigmoid(d).astype(BF16)


def _proj_call(x, mod, g1, wa, wb, wc, wd, cw, tm):
    bsz, s, d = x.shape
    nt = s // tm
    tok = lambda w: pl.BlockSpec((None, tm, w), lambda b, t: (b, t, 0))
    full = lambda arr: pl.BlockSpec(arr.shape, lambda b, t: (0,) * arr.ndim)
    out_widths = [(512, BF16), (512, BF16), (512, BF16), (512, BF16), (LANES, BF16), (LANES, F32),
                  (512, BF16), (2 * D_MODEL, BF16)]
    return pl.pallas_call(
        functools.partial(_proj_kernel, tm=tm),
        grid=(bsz, nt),
        in_specs=[tok(d), pl.BlockSpec((None, 6, d), lambda b, t: (b, 0, 0)), full(g1),
                  full(wa), full(wb), full(wc), full(wd), full(cw)],
        out_specs=[tok(w) for w, _ in out_widths],
        out_shape=[jax.ShapeDtypeStruct((bsz, s, w), dt) for w, dt in out_widths],
        scratch_shapes=[pltpu.VMEM((tm + 8, CONV_WIDTH), F32)],
        compiler_params=pltpu.CompilerParams(dimension_semantics=("arbitrary", "arbitrary"),
                                             vmem_limit_bytes=VMEM_LIMIT),
        name="proj",
    )(x, mod, g1, wa, wb, wc, wd, cw)


def _to_key(f):
    b = pltpu.bitcast(f, I32)
    return jnp.where(b < 0, b ^ 0x7FFFFFFF, b)


def _from_key(k):
    return pltpu.bitcast(jnp.where(k < 0, k ^ 0x7FFFFFFF, k), F32)


def _bit_transpose32(words):
    a = list(words)
    j, m = 16, 0x0000FFFF
    while j:
        k = 0
        while k < 32:
            t = (a[k] ^ lax.shift_right_logical(a[k + j], jnp.int32(j))) & jnp.int32(m)
            a[k] = a[k] ^ t
            a[k + j] = a[k + j] ^ lax.shift_left(t, jnp.int32(j))
            k = (k + j + 1) & ~j
        j >>= 1
        m = (m ^ (m << j)) & 0xFFFFFFFF
    return a


def _fold_lanes(x, op):
    parts = [x[:, t * LANES:(t + 1) * LANES] for t in range(x.shape[1] // LANES)]
    while len(parts) > 1:
        nxt = [op(parts[a], parts[a + 1]) for a in range(0, len(parts) - 1, 2)]
        parts = nxt + ([parts[-1]] if len(parts) % 2 else [])
    return parts[0]


def _tile_lanes(x, n):
    return x if n == 1 else jnp.concatenate([x] * n, axis=1)


def _attn_kernel(rb_ref, q_ref, qi_ref, wi_ref, qin_ref, win_ref, k_ref, v_ref, kid_ref, o_ref,
                 sct_ref, planes_ref, cand_ref, mb_ref, tb_ref, qm_ref, qim_ref, m_ref, l_ref, acc_ref,
                 *, tq, topk):
    c = tq
    i = pl.program_id(1)
    nt_dims = (((1,), (1,)), ((), ()))
    lane = lax.broadcasted_iota(I32, (tq, LANES), 1)
    low_half = lane < HEAD_DIM
    row = lax.broadcasted_iota(I32, (tq, c), 0)
    col = lax.broadcasted_iota(I32, (tq, c), 1)

    @pl.when((pl.program_id(0) == 0) & (i == 0))
    def _():
        for p in range(32):
            planes_ref[p] = jnp.zeros(planes_ref.shape[1:], I32)
        for kind in range(2):
            rel = row - col + kind * c
            n = jnp.maximum(rel, 0)
            for h in range(N_HEADS):
                t = jnp.full((tq, c), rb_ref[0, h], F32)
                for b in range(1, N_BUCKETS):
                    t = jnp.where(n >= _BUCKET_START[b], rb_ref[b, h], t)
                t = (t - rb_ref[N_BUCKETS - 1, h]) * LOG2E
                tb_ref[h, kind] = jnp.where(rel >= 0, t, MASK_NEG)


    def masked_heads(src_ref, dst_ref):
        for h in range(N_HEADS):
            hp, half = divmod(h, 2)
            keep = low_half if half == 0 else jnp.logical_not(low_half)
            pair = src_ref[:, hp * LANES:(hp + 1) * LANES]
            dst_ref[h] = jnp.where(keep, pair, jnp.zeros_like(pair))

    krow = lax.broadcasted_iota(I32, (c, tq), 0)
    qcol = lax.broadcasted_iota(I32, (c, tq), 1)
    int_min = jnp.int32(-2 ** 31)

    def fold_rows(x, op):
        parts = [x[r * 8:(r + 1) * 8, :] for r in range(x.shape[0] // 8)]
        while len(parts) > 1:
            nxt = [op(parts[a], parts[a + 1]) for a in range(0, len(parts) - 1, 2)]
            parts = nxt + ([parts[-1]] if len(parts) % 2 else [])
        return parts[0]

    def score_chunk(j, diag, w_t):
        start = pl.multiple_of(j * c, c)
        kc = kid_ref[pl.ds(start, c), :]
        acc = jnp.zeros((c, tq), F32)
        for h in range(N_HEADS):
            d = lax.dot_general(kc, qim_ref[h], nt_dims, preferred_element_type=F32)
            acc = acc + jnp.maximum(d, 0.0) * w_t[h:h + 1, :]
        ukey = _to_key(acc) ^ int_min
        if diag:
            causal = krow <= qcol
            acc = jnp.where(causal, acc, SCORE_NEG)
            ukey = jnp.where(causal, ukey, 0)
        sct_ref[pl.ds(start, c), :] = acc
        prow = pl.multiple_of(j * (c // 32), c // 32)
        for lt in range(tq // LANES):
            words = [ukey[k * 8:(k + 1) * 8, lt * LANES:(lt + 1) * LANES] for k in range(32)]
            for p, plane in enumerate(_bit_transpose32(words)):
                planes_ref[p, pl.ds(prow, c // 32), lt * LANES:(lt + 1) * LANES] = plane

    @pl.when(i == 0)
    def _():
        masked_heads(qi_ref, qim_ref)
        score_chunk(0, True, jnp.transpose(wi_ref[...])[0:IDX_HEADS, :])

    nchunks = i + 1
    n_causal = i * tq + lax.broadcasted_iota(I32, (1, tq), 1) + 1
    prow_iota = lax.broadcasted_iota(I32, cand_ref.shape, 0)
    cand_ref[...] = jnp.where(prow_iota < nchunks * (c // 32), -1, 0)

    def radix_cond(st):
        return (st[0] < 32) & (st[1] > 0)

    def radix_step(st):
        p0, _, prefix, need, ncand, settled = st
        cand = cand_ref[...]
        for sub in range(RADIX_BITS_PER_STEP):
            p = p0 + sub
            plane = planes_ref[p]
            ones = cand & plane
            cnt1 = jnp.sum(fold_rows(lax.population_count(ones), jnp.add), axis=0, keepdims=True)
            take1 = cnt1 >= need
            live = settled == 0
            need = jnp.where(live & jnp.logical_not(take1), need - cnt1, need)
            ncand = jnp.where(live, jnp.where(take1, cnt1, ncand - cnt1), ncand)
            prefix = jnp.where(live & take1, prefix | jnp.left_shift(jnp.int32(1), 31 - p), prefix)
            cand = cand & (plane ^ jnp.where(take1, 0, -1))
            settled = jnp.where(ncand == need, 1, settled)
        cand_ref[...] = cand
        return p0 + RADIX_BITS_PER_STEP, jnp.sum(1 - settled), prefix, need, ncand, settled

    done0 = jnp.where(n_causal <= topk, 1, 0)
    _, _, prefix, _, _, _ = lax.while_loop(
        radix_cond, radix_step,
        (jnp.int32(0), jnp.sum(1 - done0), jnp.zeros((1, tq), I32), jnp.full((1, tq), topk, I32),
         jnp.full((1, tq), 1, I32) * (nchunks * c), done0))
    guess_key = prefix ^ int_min
    guess = _from_key(guess_key)

    def count_keys(pred):
        def body(j, part):
            start = pl.multiple_of(j * c, c)
            hit = pred(sct_ref[pl.ds(start, c), :], start)
            return part + fold_rows(jnp.where(hit, 1.0, 0.0), jnp.add)
        part = lax.fori_loop(0, nchunks, body, jnp.zeros((8, tq), F32))
        return jnp.sum(part, axis=0, keepdims=True)

    kf = float(topk)
    guess_above = _from_key(guess_key + 1)
    guess_above = jnp.where(jnp.abs(guess_above) < TINY_F32,
                            jnp.where(guess >= 0.0, TINY_F32, 0.0), guess_above)
    guess_below = _from_key(guess_key - 2)

    def cond(st):
        return st[0] > 0

    def step(st):
        _, it, lo, hi, c_lo, c_hi, done = st
        mid = 0.5 * lo + 0.5 * hi
        lo_k, hi_k = _to_key(lo), _to_key(hi)
        mid_key = _from_key((lo_k >> 1) + (hi_k >> 1) + (lo_k & hi_k & 1))
        mid = jnp.where(it >= KEY_SPACE_AFTER, mid_key, mid)
        probe = jnp.where(it == 0, guess, jnp.where(lo == guess, guess_above, guess_below))
        mid = jnp.where((it <= 1) & (probe > lo) & (probe < hi), probe, mid)
        go = (done == 0) & (mid > lo) & (mid < hi)
        cnt = count_keys(lambda tile, start: tile >= mid)
        ge = cnt >= kf
        up = ge & go
        dn = jnp.logical_not(ge) & go
        lo = jnp.where(up, mid, lo)
        hi = jnp.where(dn, mid, hi)
        c_lo = jnp.where(up, cnt, c_lo)
        c_hi = jnp.where(dn, cnt, c_hi)
        adjacent = (lo == guess) & (hi == guess_above)
        done = jnp.where(go & (c_lo != kf) & jnp.logical_not(adjacent), 0, 1)
        return jnp.sum(1 - done), it + 1, lo, hi, c_lo, c_hi, done

    _, _, thr, _, c_lo, c_hi, _ = lax.while_loop(
        cond, step, (jnp.sum(1 - done0), jnp.int32(0), jnp.full((1, tq), SCORE_NEG, F32),
                     jnp.full((1, tq), -SCORE_NEG, F32), jnp.full((1, tq), 1.0, F32) * (nchunks * c),
                     jnp.zeros((1, tq), F32), done0))

    tied = (n_causal > topk) & (c_lo > kf)
    any_tied = jnp.sum(jnp.where(tied, 1, 0)) > 0

    @pl.when(any_tied)
    def _():
        n_take = kf - c_hi

        def bis(_, st):
            lo_x, hi_x = st
            mid = (lo_x + hi_x) >> 1
            ok = count_keys(lambda tile, start: (tile == thr) & ((krow + start) <= mid)) >= n_take
            return jnp.where(ok, lo_x, mid), jnp.where(ok, mid, hi_x)

        nbits = int(math.ceil(math.log2(sct_ref.shape[0]))) + 1
        _, hi_x = lax.fori_loop(0, nbits, bis, (jnp.full((1, tq), -1, I32),
                                                jnp.full((1, tq), sct_ref.shape[0], I32)))
        cut = jnp.where(tied, hi_x, jnp.int32(2 ** 30))

        def mask_body(j, _):
            start = pl.multiple_of(j * c, c)
            tile = sct_ref[pl.ds(start, c), :]
            sel = (tile > thr) | ((tile == thr) & ((krow + start) <= cut))
            mb_ref[:, pl.ds(start, c)] = jnp.transpose(jnp.where(sel, 0.0, MASK_NEG))
            return 0

        lax.fori_loop(0, nchunks, mask_body, 0)

    @pl.when(jnp.logical_not(any_tied))
    def _():
        def mask_body(j, _):
            start = pl.multiple_of(j * c, c)
            tile = sct_ref[pl.ds(start, c), :]
            mb_ref[:, pl.ds(start, c)] = jnp.transpose(jnp.where(tile >= thr, 0.0, MASK_NEG))
            return 0

        lax.fori_loop(0, nchunks, mask_body, 0)

    masked_heads(q_ref, qm_ref)
    masked_heads(qin_ref, qim_ref)
    w_t_next = jnp.transpose(win_ref[...])[0:IDX_HEADS, :]
    has_next = i + 1 < pl.num_programs(1)
    m_ref[...] = jnp.full(m_ref.shape, MASK_NEG, F32)
    l_ref[...] = jnp.zeros(l_ref.shape, F32)
    acc_ref[...] = jnp.zeros(acc_ref.shape, F32)

    def attend(start, width, kind):
        nt = width // LANES
        ones = jnp.ones((width, LANES), BF16)
        for h in range(N_HEADS):
            hp = h // 2
            kp = k_ref[pl.ds(start, width), hp * LANES:(hp + 1) * LANES]
            vp = v_ref[pl.ds(start, width), hp * LANES:(hp + 1) * LANES]
            s = lax.dot_general(qm_ref[h], kp, nt_dims, preferred_element_type=F32)
            s = s + mb_ref[:, pl.ds(start, width)]
            if kind is not None:
                s = s + tb_ref[h, kind]
            m_old = m_ref[h]
            m_new = jnp.maximum(m_old, jnp.max(_fold_lanes(s, jnp.maximum), axis=1, keepdims=True))
            p = jnp.exp2(s - _tile_lanes(m_new, nt)).astype(BF16)
            pv = jnp.dot(p, jnp.concatenate([vp, ones], axis=1), preferred_element_type=F32)
            alpha = jnp.exp2(m_old - m_new)
            m_ref[h] = m_new
            l_ref[h] = alpha * l_ref[h] + pv[:, LANES:]
            acc_ref[h] = alpha * acc_ref[h] + pv[:, :LANES]

    wide = 2 * c
    n_far = jnp.maximum(i - 1, 0)

    n_wide = n_far // 2

    def far_body(j, _):
        attend(pl.multiple_of(j * wide, wide), wide, None)
        score_chunk(2 * j, False, w_t_next)
        score_chunk(2 * j + 1, False, w_t_next)
        return 0

    lax.fori_loop(0, n_wide, far_body, 0)

    @pl.when(n_far % 2 == 1)
    def _():
        attend(pl.multiple_of((n_far - 1) * c, c), c, None)
        score_chunk(n_far - 1, False, w_t_next)

    @pl.when(i >= 1)
    def _():
        attend(pl.multiple_of((i - 1) * c, c), c, 1)
        score_chunk(i - 1, False, w_t_next)

    attend(pl.multiple_of(i * c, c), c, 0)
    score_chunk(i, False, w_t_next)

    @pl.when(has_next)
    def _():
        score_chunk(i + 1, True, w_t_next)

    for hp in range(N_HEADS // 2):
        lo = acc_ref[2 * hp] / l_ref[2 * hp]
        hi = acc_ref[2 * hp + 1] / l_ref[2 * hp + 1]
        o_ref[:, hp * LANES:(hp + 1) * LANES] = jnp.where(low_half, lo, hi).astype(BF16)


def _attn_call(rel_bias, q, qi, wi, k, v, kid, tq):
    bsz, s, _ = q.shape
    nq = s // tq
    topk = min(TOPK_MAX, s // 4)
    assert tq % LANES == 0 and tq >= MAX_DISTANCE
    blk = lambda w: pl.BlockSpec((None, tq, w), lambda b, i: (b, i, 0))
    nxt = lambda w: pl.BlockSpec((None, tq, w), lambda b, i: (b, jnp.minimum(i + 1, nq - 1), 0))
    whole = lambda w: pl.BlockSpec((None, s, w), lambda b, i: (b, 0, 0), pipeline_mode=pl.Buffered(1))
    return pl.pallas_call(
        functools.partial(_attn_kernel, tq=tq, topk=topk),
        grid=(bsz, nq),
        in_specs=[pl.BlockSpec(memory_space=pltpu.SMEM),
                  blk(ATTN_WIDTH), blk(ATTN_WIDTH), blk(LANES), nxt(ATTN_WIDTH), nxt(LANES),
                  whole(ATTN_WIDTH), whole(ATTN_WIDTH), whole(LANES)],
        out_specs=blk(ATTN_WIDTH),
        out_shape=jax.ShapeDtypeStruct((bsz, s, ATTN_WIDTH), BF16),
        scratch_shapes=[
            pltpu.VMEM((s, tq), F32),
            pltpu.VMEM((32, s // 32, tq), I32),
            pltpu.VMEM((s // 32, tq), I32),
            pltpu.VMEM((tq, s), F32),
            pltpu.VMEM((N_HEADS, 2, tq, tq), F32),
            pltpu.VMEM((N_HEADS, tq, LANES), BF16),
            pltpu.VMEM((N_HEADS, tq, LANES), BF16),
            pltpu.VMEM((N_HEADS, tq, LANES), F32),
            pltpu.VMEM((N_HEADS, tq, LANES), F32),
            pltpu.VMEM((N_HEADS, tq, LANES), F32),
        ],
        compiler_params=pltpu.CompilerParams(dimension_semantics=("arbitrary", "arbitrary"),
                                             vmem_limit_bytes=VMEM_LIMIT),
        name="attn",
    )(rel_bias, q, qi, wi, qi, wi, k, v, kid)


def _post_kernel(attn_ref, z_ref, gate_ref, x_ref, mod_ref, wa_ref, wc_ref, wo_ref, g2_ref,
                 wr_ref, br_ref, hres_ref, h2_ref, eid_ref, ew_ref, counts_ref, cnt_ref):
    ya = jnp.dot(attn_ref[...], wa_ref[...], preferred_element_type=F32)
    yc = jnp.dot(z_ref[...], wc_ref[...], preferred_element_type=F32)
    merged = gate_ref[:, 0:D_MODEL].astype(F32) * ya + gate_ref[:, D_MODEL:].astype(F32) * yc
    o = jnp.dot(merged.astype(BF16), wo_ref[...], preferred_element_type=F32)
    hres = x_ref[...] + mod_ref[2:3, :] * o
    hres_ref[...] = hres
    h2 = _rms_mod(hres, g2_ref[...], mod_ref[4:5, :], mod_ref[3:4, :])
    h2_ref[...] = h2.astype(BF16).reshape(h2_ref.shape)

    h2_hi = h2.astype(BF16)
    h2_lo = (h2 - h2_hi.astype(F32)).astype(BF16)
    lg = (jnp.dot(h2_hi, wr_ref[0], preferred_element_type=F32)
          + (jnp.dot(h2_lo, wr_ref[0], preferred_element_type=F32)
             + jnp.dot(h2_hi, wr_ref[1], preferred_element_type=F32))) + br_ref[...]
    tm = lg.shape[0]
    lgt = jnp.transpose(lg)
    ninf = -jnp.inf
    grow = lax.broadcasted_iota(I32, (8, tm), 0)
    gl = jnp.where(grow < N_GROUPS, lgt[N_EXPERTS:N_EXPERTS + 8, :], ninf)
    gmax = jnp.max(gl, axis=0, keepdims=True)
    g_sel = jnp.min(jnp.where(gl == gmax, grow, N_GROUPS), axis=0, keepdims=True)
    g_w = 1.0 / jnp.sum(jnp.exp(gl - gmax), axis=0, keepdims=True)

    erow = lax.broadcasted_iota(I32, (N_EXPERTS, tm), 0)
    emask = (erow // EXPERTS_PER_GROUP) == g_sel
    el = jnp.where(emask, lgt[0:N_EXPERTS, :], ninf)
    emax = jnp.max(el, axis=0, keepdims=True)
    ee = jnp.exp(el - emax)
    e_prob = ee / jnp.sum(ee, axis=0, keepdims=True)
    p1 = jnp.max(jnp.where(emask, e_prob, -1.0), axis=0, keepdims=True)
    i1 = jnp.min(jnp.where(emask & (e_prob == p1), erow, LANES), axis=0, keepdims=True)
    rest = emask & (erow != i1)
    p2 = jnp.max(jnp.where(rest, e_prob, -1.0), axis=0, keepdims=True)
    i2 = jnp.min(jnp.where(rest & (e_prob == p2), erow, LANES), axis=0, keepdims=True)
    psum = p1 + p2
    w1 = g_w * (p1 / psum)
    w2 = g_w * (p2 / psum)
    ew_ref[...] = jnp.transpose(jnp.concatenate([w1, w2, jnp.zeros((LANES - 2, tm), F32)], axis=0))

    @pl.when((pl.program_id(0) == 0) & (pl.program_id(1) == 0))
    def _():
        cnt_ref[...] = jnp.zeros(cnt_ref.shape, F32)

    onehot_t = jnp.where((erow == i1) | (erow == i2), 1.0, 0.0).astype(BF16)
    upper = (lax.broadcasted_iota(I32, (tm, tm), 0) <= lax.broadcasted_iota(I32, (tm, tm), 1))
    cum_t = jnp.dot(onehot_t, jnp.where(upper, 1.0, 0.0).astype(BF16), preferred_element_type=F32)
    cnt_col = jnp.transpose(cnt_ref[...])[0:N_EXPERTS, 0:1]
    before = cum_t - onehot_t.astype(F32) + cnt_col
    r1 = jnp.sum(jnp.where(erow == i1, before, 0.0), axis=0, keepdims=True).astype(I32)
    r2 = jnp.sum(jnp.where(erow == i2, before, 0.0), axis=0, keepdims=True).astype(I32)
    eid_ref[...] = jnp.concatenate([i1, i2, r1, r2, jnp.zeros((4, tm), I32)], axis=0)
    padded = jnp.concatenate([onehot_t, jnp.zeros((LANES - N_EXPERTS, tm), BF16)], axis=0)
    tile_total = lax.dot_general(jnp.ones((8, tm), BF16), padded, (((1,), (1,)), ((), ())),
                                 preferred_element_type=F32)
    total = cnt_ref[...] + tile_total
    cnt_ref[...] = total
    counts_ref[...] = total


def _post_call(attn, z, gates, x, mod, wa, wc, wo, g2, wr, br, tm):
    bsz, s, d = x.shape
    nt = s // tm
    tok = lambda w: pl.BlockSpec((None, tm, w), lambda b, t: (b, t, 0))
    full = lambda arr: pl.BlockSpec(arr.shape, lambda b, t: (0,) * arr.ndim)
    return pl.pallas_call(
        _post_kernel,
        grid=(bsz, nt),
        in_specs=[tok(ATTN_WIDTH), tok(CONV_WIDTH), tok(2 * d), tok(d),
                  pl.BlockSpec((None, 6, d), lambda b, t: (b, 0, 0)),
                  full(wa), full(wc), full(wo), full(g2), full(wr), full(br)],
        out_specs=[tok(d), pl.BlockSpec((None, tm) + ROW_TILE, lambda b, t: (b, t, 0, 0)),
                   pl.BlockSpec((8, tm), lambda b, t: (0, b * nt + t)), tok(LANES),
                   pl.BlockSpec((8, LANES), lambda b, t: (0, 0))],
        out_shape=[jax.ShapeDtypeStruct((bsz, s, d), F32),
                   jax.ShapeDtypeStruct((bsz, s) + ROW_TILE, BF16),
                   jax.ShapeDtypeStruct((8, bsz * s), I32), jax.ShapeDtypeStruct((bsz, s, LANES), F32),
                   jax.ShapeDtypeStruct((8, LANES), F32)],
        scratch_shapes=[pltpu.VMEM((8, LANES), F32)],
        compiler_params=pltpu.CompilerParams(dimension_semantics=("arbitrary", "arbitrary"),
                                             vmem_limit_bytes=VMEM_LIMIT),
        name="post",
    )(attn, z, gates, x, mod, wa, wc, wo, g2, wr, br)


ROW_TILE = (D_MODEL // LANES, LANES)
EXPERT_ROWS = 256
TOKENS_PER_STEP = 256


ISSUE_UNROLL = 8


def _dispatch_kernel(pos1_ref, pos2_ref, h2_ref, xs_in_ref, xs_ref, sem):
    del xs_in_ref
    tm = h2_ref.shape[0]
    base = pl.program_id(0) * tm

    def issue(r, c):
        pltpu.make_async_copy(h2_ref.at[r], xs_ref.at[pos1_ref[base + r]], sem).start(priority=0)
        pltpu.make_async_copy(h2_ref.at[r], xs_ref.at[pos2_ref[base + r]], sem).start(priority=1)
        return c

    lax.fori_loop(0, tm, issue, 0, unroll=ISSUE_UNROLL)
    for _ in range(2):
        pltpu.make_async_copy(h2_ref, xs_ref.at[pl.ds(0, tm)], sem).wait()


def _dispatch_call(pos1, pos2, h2_rows, n_rows):
    t = h2_rows.shape[0]
    tm = min(TOKENS_PER_STEP, t)
    xs0 = jnp.zeros((n_rows,) + ROW_TILE, BF16)
    return pl.pallas_call(
        _dispatch_kernel,
        grid_spec=pltpu.PrefetchScalarGridSpec(
            num_scalar_prefetch=2,
            grid=(t // tm,),
            in_specs=[pl.BlockSpec((tm,) + ROW_TILE, lambda i, p1, p2: (i, 0, 0)),
                      pl.BlockSpec(memory_space=pl.ANY)],
            out_specs=pl.BlockSpec(memory_space=pl.ANY),
            scratch_shapes=[pltpu.SemaphoreType.DMA(())]),
        out_shape=jax.ShapeDtypeStruct((n_rows,) + ROW_TILE, BF16),
        input_output_aliases={3: 0},
        compiler_params=pltpu.CompilerParams(dimension_semantics=("arbitrary",),
                                             vmem_limit_bytes=VMEM_LIMIT),
        name="dispatch",
    )(pos1, pos2, h2_rows, xs0)


def _expert_kernel(te_ref, nu_ref, xs_ref, wg_ref, wu_ref, wd_ref, ys_ref, wgb, wub, wdb):
    g = pl.program_id(0)
    e = te_ref[g]
    e_prev = te_ref[jnp.maximum(g - 1, 0)]

    @pl.when((g == 0) | (e != e_prev))
    def _():
        wgb[...] = wg_ref[...].astype(BF16)
        wub[...] = wu_ref[...].astype(BF16)
        wdb[...] = wd_ref[...].astype(BF16)

    @pl.when(g < nu_ref[0])
    def _():
        x = xs_ref[...].reshape(xs_ref.shape[0], D_MODEL)
        a = jnp.dot(x, wgb[...], preferred_element_type=F32)
        b = jnp.dot(x, wub[...], preferred_element_type=F32)
        hid = (a * jax.nn.sigmoid(a)) * b
        y = jnp.dot(hid.astype(BF16), wdb[...], preferred_element_type=F32)
        ys_ref[...] = y.astype(BF16).reshape(ys_ref.shape)

    @pl.when(g >= nu_ref[0])
    def _():
        ys_ref[...] = jnp.zeros(ys_ref.shape, BF16)


def _expert_call(tile_expert, n_used, xs, wg, wu, wd):
    n_rows, d = xs.shape[0], D_MODEL
    nt = n_rows // EXPERT_ROWS
    rows = pl.BlockSpec((EXPERT_ROWS,) + ROW_TILE, lambda g, te, nu: (g, 0, 0))
    wspec = lambda shp: pl.BlockSpec((None,) + shp, lambda g, te, nu: (te[g], 0, 0))
    return pl.pallas_call(
        _expert_kernel,
        grid_spec=pltpu.PrefetchScalarGridSpec(
            num_scalar_prefetch=2,
            grid=(nt,),
            in_specs=[rows, wspec((d, D_EXPERT)), wspec((d, D_EXPERT)), wspec((D_EXPERT, d))],
            out_specs=rows,
            scratch_shapes=[pltpu.VMEM((d, D_EXPERT), BF16), pltpu.VMEM((d, D_EXPERT), BF16),
                            pltpu.VMEM((D_EXPERT, d), BF16)]),
        out_shape=jax.ShapeDtypeStruct((n_rows,) + ROW_TILE, BF16),
        compiler_params=pltpu.CompilerParams(dimension_semantics=("arbitrary",),
                                             vmem_limit_bytes=VMEM_LIMIT),
        name="experts",
    )(tile_expert, n_used, xs, wg, wu, wd)


def _combine_kernel(pos1_ref, pos2_ref, ys_ref, ew_ref, hres_ref, g2_ref, gf_ref, o_ref, buf, sem):
    tm = hres_ref.shape[0]
    t = pl.program_id(0)
    nt = pl.num_programs(0)

    def issue(tile, slot):
        base = tile * tm

        def body(r, c):
            pltpu.make_async_copy(ys_ref.at[pos1_ref[base + r]], buf.at[slot, r],
                                  sem.at[slot]).start(priority=0)
            pltpu.make_async_copy(ys_ref.at[pos2_ref[base + r]], buf.at[slot, tm + r],
                                  sem.at[slot]).start(priority=1)
            return c

        lax.fori_loop(0, tm, body, 0, unroll=ISSUE_UNROLL)

    @pl.when(t == 0)
    def _():
        issue(0, 0)

    @pl.when(t + 1 < nt)
    def _():
        issue(t + 1, (t + 1) % 2)

    slot = t % 2
    pltpu.make_async_copy(ys_ref.at[pl.ds(0, 2 * tm)], buf.at[slot], sem.at[slot]).wait()

    y1 = buf[slot, 0:tm].astype(F32).reshape(tm, D_MODEL)
    y2 = buf[slot, tm:2 * tm].astype(F32).reshape(tm, D_MODEL)
    w = ew_ref[...]
    y = w[:, 0:1] * y1 + w[:, 1:2] * y2
    hfin = hres_ref[...] + g2_ref[...] * y
    ms = jnp.mean(hfin * hfin, axis=-1, keepdims=True)
    o_ref[...] = hfin * lax.rsqrt(ms + EPS) * gf_ref[...]


def _combine_call(pos1, pos2, ys_rows, ew, hres, gate2, gf):
    t, d = hres.shape
    tm = min(TOKENS_PER_STEP, t)
    per_batch = t // gate2.shape[0]
    assert per_batch % tm == 0
    tok = lambda w: pl.BlockSpec((tm, w), lambda i, *_: (i, 0))
    return pl.pallas_call(
        _combine_kernel,
        grid_spec=pltpu.PrefetchScalarGridSpec(
            num_scalar_prefetch=2,
            grid=(t // tm,),
            in_specs=[pl.BlockSpec(memory_space=pl.ANY), tok(LANES), tok(d),
                      pl.BlockSpec((None, 1, d), lambda i, *_: ((i * tm) // per_batch, 0, 0)),
                      pl.BlockSpec((1, d), lambda i, *_: (0, 0))],
            out_specs=tok(d),
            scratch_shapes=[pltpu.VMEM((2, 2 * tm) + ROW_TILE, BF16), pltpu.SemaphoreType.DMA((2,))]),
        out_shape=jax.ShapeDtypeStruct((t, d), F32),
        compiler_params=pltpu.CompilerParams(dimension_semantics=("arbitrary",),
                                             vmem_limit_bytes=VMEM_LIMIT),
        name="combine",
    )(pos1, pos2, ys_rows, ew, hres, gate2, gf)


def _slot_kernel(seg_ref, eid_ref, pos_ref):
    e = eid_ref[0:2, :]
    pos = eid_ref[2:4, :]
    for k in range(seg_ref.shape[0]):
        pos = pos + jnp.where(e == k, seg_ref[k], 0)
    pos_ref[...] = jnp.concatenate([pos, jnp.zeros((6, pos.shape[1]), I32)], axis=0)


def _slot_call(seg_start, eid):
    return pl.pallas_call(
        _slot_kernel,
        in_specs=[pl.BlockSpec(memory_space=pltpu.SMEM), pl.BlockSpec(eid.shape, lambda: (0, 0))],
        out_specs=pl.BlockSpec(eid.shape, lambda: (0, 0)),
        out_shape=jax.ShapeDtypeStruct(eid.shape, I32),
        name="slots",
    )(seg_start, eid)


def _sparse_moe(h2, eid, ew, counts, wg, wu, wd, hres, gate2, gf):
    bsz, s, d = hres.shape
    t = bsz * s
    ne = wg.shape[0]
    cnt = counts[0, :ne].astype(I32)
    seg_rows = ((cnt + EXPERT_ROWS - 1) // EXPERT_ROWS) * EXPERT_ROWS
    seg_end = jnp.cumsum(seg_rows)
    seg_start = seg_end - seg_rows
    pos = _slot_call(seg_start, eid)
    pos1, pos2 = pos[0], pos[1]
    n_tiles = (2 * t) // EXPERT_ROWS + ne
    tile_first_row = jnp.arange(n_tiles, dtype=I32) * EXPERT_ROWS
    tile_expert = jnp.minimum(jnp.sum(seg_end[None, :] <= tile_first_row[:, None], axis=1), ne - 1)
    n_used = (seg_end[ne - 1] // EXPERT_ROWS).reshape(1)
    n_rows = n_tiles * EXPERT_ROWS

    xs = _dispatch_call(pos1, pos2, h2.reshape((t,) + ROW_TILE), n_rows)
    ys = _expert_call(tile_expert.astype(I32), n_used.astype(I32), xs, wg, wu, wd)
    out = _combine_call(pos1, pos2, ys, ew.reshape(t, LANES), hres.reshape(t, d),
                        gate2.reshape(bsz, 1, d), gf.reshape(1, d))
    return out.reshape(bsz, s, d)


def kernel(x, c, w_ada, b_ada, norm1_g, w_in, rel_bias, conv_w, w_attn_branch, w_conv_branch, w_out,
           norm2_g, w_router_group, b_router_group, w_router_expert, b_router_expert,
           w_gate_e, w_up_e, w_down_e, norm_f_g):
    bsz, s, d = x.shape
    depth = w_ada.shape[0]
    h = x.astype(F32)
    c_pad = jnp.zeros((8, d), F32).at[:bsz].set(c.astype(F32))
    out = h
    for l in range(depth):
        mod = _ada_call(c_pad, w_ada[l].astype(F32), b_ada[l].astype(F32)[None, :])
        mod = mod[:bsz].reshape(bsz, 6, d)

        w = w_in[l]
        o_q, o_k, o_v, o_qi = 0, 512, 1024, 1536
        o_ki, o_wi, o_cb, o_gl = 2048, 2112, 2120, 3656
        wa = w[:, o_q:o_ki].astype(BF16)
        ki_w = w[:, o_ki:o_wi]
        wb = jnp.concatenate([ki_w, ki_w, w[:, o_wi:o_cb],
                              jnp.zeros((d, LANES - IDX_HEADS), w.dtype)], axis=1).astype(BF16)
        wc = w[:, o_cb:o_gl].astype(BF16)
        wd = w[:, o_gl:].astype(BF16)

        q, k, v, qi, kid, wi, z, gates = _proj_call(
            h, mod, norm1_g[l].astype(F32)[None, :], wa, wb, wc, wd, conv_w[l].astype(F32),
            tm=min(512, s))

        attn = _attn_call(rel_bias.astype(F32), q, qi, wi, k, v, kid, tq=min(TOPK_MAX, s // 4))

        wr32 = jnp.concatenate([w_router_expert[l], w_router_group[l],
                                jnp.zeros((d, LANES - N_EXPERTS - N_GROUPS), F32)], axis=1).astype(F32)
        wr_hi = wr32.astype(BF16)
        wr = jnp.stack([wr_hi, (wr32 - wr_hi.astype(F32)).astype(BF16)])
        br = jnp.concatenate([b_router_expert[l], b_router_group[l],
                              jnp.zeros((LANES - N_EXPERTS - N_GROUPS,), F32)])[None, :].astype(F32)
        hres, h2, eid, ew, counts = _post_call(
            attn, z, gates, h, mod, w_attn_branch[l].astype(BF16), w_conv_branch[l].astype(BF16),
            w_out[l].astype(BF16), norm2_g[l].astype(F32)[None, :], wr, br, tm=min(512, s))

        assert depth == 1, "the final RMSNorm is fused into the combine kernel of the only layer"
        out = _sparse_moe(h2, eid, ew, counts, w_gate_e[l].astype(F32), w_up_e[l].astype(F32),
                          w_down_e[l].astype(F32), hres, mod[:, 5, :], norm_f_g.astype(F32))
        h = out
    return out.astype(x.dtype)
```

```python
import functools
import math

import numpy as np
import jax
import jax.numpy as jnp
from jax import lax
from jax.experimental import pallas as pl
from jax.experimental.pallas import tpu as pltpu

F32 = jnp.float32
BF16 = jnp.bfloat16
I32 = jnp.int32

D_MODEL = 1024
N_HEADS = 8
HEAD_DIM = 64
ATTN_WIDTH = N_HEADS * HEAD_DIM
IDX_HEADS = 8
IDX_DIM = 64
TOPK_MAX = 256
N_BUCKETS = 32
MAX_DISTANCE = 128
CONV_WIDTH = 512
CONV_K = 3
N_GROUPS = 4
EXPERTS_PER_GROUP = 8
N_EXPERTS = N_GROUPS * EXPERTS_PER_GROUP
D_EXPERT = 512
EPS = 1e-6

LANES = 128
VMEM_LIMIT = 56 * 1024 * 1024

LOG2E = math.log2(math.e)
MASK_NEG = -1e30
SCORE_NEG = float(np.finfo(np.float32).min)
TINY_F32 = float(np.finfo(np.float32).tiny)
KEY_SPACE_AFTER = 40
RADIX_BITS_PER_STEP = 4


def _t5_bucket_starts():
    max_exact = N_BUCKETS // 2
    n = np.arange(0, MAX_DISTANCE + 1)
    nf = np.maximum(n, 1).astype(np.float32)
    val = (np.log(nf / np.float32(max_exact)) / np.float32(math.log(MAX_DISTANCE / max_exact))
           * np.float32(N_BUCKETS - max_exact)).astype(np.float32)
    inner = val[max_exact + 1:MAX_DISTANCE]
    assert np.min(np.abs(inner - np.round(inner))) > 1e-3
    large = np.minimum(max_exact + val.astype(np.int32), N_BUCKETS - 1)
    bucket = np.where(n < max_exact, n, large)
    assert bucket[MAX_DISTANCE] == N_BUCKETS - 1 and np.all(np.diff(bucket) >= 0)
    return [int(np.argmax(bucket >= b)) for b in range(N_BUCKETS)]


_BUCKET_START = _t5_bucket_starts()


def _ada_kernel(c_ref, w_ref, b_ref, o_ref):
    c = c_ref[...]
    ca = c * jax.nn.sigmoid(c)
    o_ref[...] = jnp.dot(ca, w_ref[...], preferred_element_type=F32,
                         precision=lax.Precision.HIGHEST) + b_ref[...]


def _ada_call(c_pad, w, b):
    rows, d = c_pad.shape
    n = w.shape[1]
    tn = 1536
    return pl.pallas_call(
        _ada_kernel,
        grid=(n // tn,),
        in_specs=[pl.BlockSpec((rows, d), lambda j: (0, 0)),
                  pl.BlockSpec((d, tn), lambda j: (0, j)),
                  pl.BlockSpec((1, tn), lambda j: (0, j))],
        out_specs=pl.BlockSpec((rows, tn), lambda j: (0, j)),
        out_shape=jax.ShapeDtypeStruct((rows, n), F32),
        compiler_params=pltpu.CompilerParams(dimension_semantics=("arbitrary",),
                                             vmem_limit_bytes=VMEM_LIMIT),
        name="ada",
    )(c_pad, w, b)


def _rms_mod(x, g, scale, shift):
    ms = jnp.mean(x * x, axis=-1, keepdims=True)
    y = x * lax.rsqrt(ms + EPS) * g
    return y * (1.0 + scale) + shift


def _proj_kernel(x_ref, mod_ref, g_ref, wa_ref, wb_ref, wc_ref, wd_ref, cw_ref,
                 q_ref, k_ref, v_ref, qi_ref, kid_ref, wi_ref, z_ref, gate_ref, ubuf, *, tm):
    h = _rms_mod(x_ref[...], g_ref[...], mod_ref[1:2, :], mod_ref[0:1, :])
    hb = h.astype(BF16)

    a = jnp.dot(hb, wa_ref[...], preferred_element_type=F32)
    q_ref[...] = (a[:, 0:512] * (HEAD_DIM ** -0.5 * LOG2E)).astype(BF16)
    k_ref[...] = a[:, 512:1024].astype(BF16)
    v_ref[...] = a[:, 1024:1536].astype(BF16)
    qi_ref[...] = a[:, 1536:2048].astype(BF16)

    b = jnp.dot(hb, wb_ref[...], preferred_element_type=F32)
    kid_ref[...] = b[:, 0:LANES].astype(BF16)
    wi_ref[...] = b[:, LANES:2 * LANES] * ((IDX_DIM ** -0.5) * (IDX_HEADS ** -0.5))

    cpart = jnp.dot(hb, wc_ref[...], preferred_element_type=F32)
    cb = cpart[:, 0:512]
    u = cpart[:, 512:1024] * cpart[:, 1024:1536]

    @pl.when(pl.program_id(1) == 0)
    def _():
        ubuf[0:8, :] = jnp.zeros((8, CONV_WIDTH), F32)

    ubuf[8:tm + 8, :] = u
    conv = (cw_ref[0:1, :] * ubuf[6:tm + 6, :] + cw_ref[1:2, :] * ubuf[7:tm + 7, :]
            + cw_ref[2:3, :] * u)
    z_ref[...] = (cb * conv).astype(BF16)
    ubuf[0:8, :] = ubuf[tm:tm + 8, :]

    d = jnp.dot(hb, wd_ref[...], preferred_element_type=F32)
    gate_ref[...] = jax.nn.sigmoid(d).astype(BF16)


def _proj_call(x, mod, g1, wa, wb, wc, wd, cw, tm):
    bsz, s, d = x.shape
    nt = s // tm
    tok = lambda w: pl.BlockSpec((None, tm, w), lambda b, t: (b, t, 0))
    full = lambda arr: pl.BlockSpec(arr.shape, lambda b, t: (0,) * arr.ndim)
    out_widths = [(512, BF16), (512, BF16), (512, BF16), (512, BF16), (LANES, BF16), (LANES, F32),
                  (512, BF16), (2 * D_MODEL, BF16)]
    return pl.pallas_call(
        functools.partial(_proj_kernel, tm=tm),
        grid=(bsz, nt),
        in_specs=[tok(d), pl.BlockSpec((None, 6, d), lambda b, t: (b, 0, 0)), full(g1),
                  full(wa), full(wb), full(wc), full(wd), full(cw)],
        out_specs=[tok(w) for w, _ in out_widths],
        out_shape=[jax.ShapeDtypeStruct((bsz, s, w), dt) for w, dt in out_widths],
        scratch_shapes=[pltpu.VMEM((tm + 8, CONV_WIDTH), F32)],
        compiler_params=pltpu.CompilerParams(dimension_semantics=("arbitrary", "arbitrary"),
                                             vmem_limit_bytes=VMEM_LIMIT),
        name="proj",
    )(x, mod, g1, wa, wb, wc, wd, cw)


def _to_key(f):
    b = pltpu.bitcast(f, I32)
    return jnp.where(b < 0, b ^ 0x7FFFFFFF, b)


def _from_key(k):
    return pltpu.bitcast(jnp.where(k < 0, k ^ 0x7FFFFFFF, k), F32)


def _bit_transpose32(words):
    a = list(words)
    j, m = 16, 0x0000FFFF
    while j:
        k = 0
        while k < 32:
            t = (a[k] ^ lax.shift_right_logical(a[k + j], jnp.int32(j))) & jnp.int32(m)
            a[k] = a[k] ^ t
            a[k + j] = a[k + j] ^ lax.shift_left(t, jnp.int32(j))
            k = (k + j + 1) & ~j
        j >>= 1
        m = (m ^ (m << j)) & 0xFFFFFFFF
    return a


def _fold_lanes(x, op):
    parts = [x[:, t * LANES:(t + 1) * LANES] for t in range(x.shape[1] // LANES)]
    while len(parts) > 1:
        nxt = [op(parts[a], parts[a + 1]) for a in range(0, len(parts) - 1, 2)]
        parts = nxt + ([parts[-1]] if len(parts) % 2 else [])
    return parts[0]


def _tile_lanes(x, n):
    return x if n == 1 else jnp.concatenate([x] * n, axis=1)


def _attn_kernel(rb_ref, q_ref, qi_ref, wi_ref, qin_ref, win_ref, k_ref, v_ref, kid_ref, o_ref,
                 sct_ref, planes_ref, cand_ref, mb_ref, tb_ref, qm_ref, qim_ref, m_ref, l_ref, acc_ref,
                 *, tq, topk):
    c = tq
    i = pl.program_id(1)
    nt_dims = (((1,), (1,)), ((), ()))
    lane = lax.broadcasted_iota(I32, (tq, LANES), 1)
    low_half = lane < HEAD_DIM
    row = lax.broadcasted_iota(I32, (tq, c), 0)
    col = lax.broadcasted_iota(I32, (tq, c), 1)

    @pl.when((pl.program_id(0) == 0) & (i == 0))
    def _():
        for p in range(32):
            planes_ref[p] = jnp.zeros(planes_ref.shape[1:], I32)
        for kind in range(2):
            rel = row - col + kind * c
            n = jnp.maximum(rel, 0)
            for h in range(N_HEADS):
                t = jnp.full((tq, c), rb_ref[0, h], F32)
                for b in range(1, N_BUCKETS):
                    t = jnp.where(n >= _BUCKET_START[b], rb_ref[b, h], t)
                t = (t - rb_ref[N_BUCKETS - 1, h]) * LOG2E
                tb_ref[h, kind] = jnp.where(rel >= 0, t, MASK_NEG)


    def masked_heads(src_ref, dst_ref):
        for h in range(N_HEADS):
            hp, half = divmod(h, 2)
            keep = low_half if half == 0 else jnp.logical_not(low_half)
            pair = src_ref[:, hp * LANES:(hp + 1) * LANES]
            dst_ref[h] = jnp.where(keep, pair, jnp.zeros_like(pair))

    krow = lax.broadcasted_iota(I32, (c, tq), 0)
    qcol = lax.broadcasted_iota(I32, (c, tq), 1)
    int_min = jnp.int32(-2 ** 31)

    def fold_rows(x, op):
        parts = [x[r * 8:(r + 1) * 8, :] for r in range(x.shape[0] // 8)]
        while len(parts) > 1:
            nxt = [op(parts[a], parts[a + 1]) for a in range(0, len(parts) - 1, 2)]
            parts = nxt + ([parts[-1]] if len(parts) % 2 else [])
        return parts[0]

    def score_chunk(j, diag, w_t):
        start = pl.multiple_of(j * c, c)
        kc = kid_ref[pl.ds(start, c), :]
        acc = jnp.zeros((c, tq), F32)
        for h in range(N_HEADS):
            d = lax.dot_general(kc, qim_ref[h], nt_dims, preferred_element_type=F32)
            acc = acc + jnp.maximum(d, 0.0) * w_t[h:h + 1, :]
        ukey = _to_key(acc) ^ int_min
        if diag:
            causal = krow <= qcol
            acc = jnp.where(causal, acc, SCORE_NEG)
            ukey = jnp.where(causal, ukey, 0)
        sct_ref[pl.ds(start, c), :] = acc
        prow = pl.multiple_of(j * (c // 32), c // 32)
        for lt in range(tq // LANES):
            words = [ukey[k * 8:(k + 1) * 8, lt * LANES:(lt + 1) * LANES] for k in range(32)]
            for p, plane in enumerate(_bit_transpose32(words)):
                planes_ref[p, pl.ds(prow, c // 32), lt * LANES:(lt + 1) * LANES] = plane

    @pl.when(i == 0)
    def _():
        masked_heads(qi_ref, qim_ref)
        score_chunk(0, True, jnp.transpose(wi_ref[...])[0:IDX_HEADS, :])

    nchunks = i + 1
    n_causal = i * tq + lax.broadcasted_iota(I32, (1, tq), 1) + 1
    prow_iota = lax.broadcasted_iota(I32, cand_ref.shape, 0)
    cand_ref[...] = jnp.where(prow_iota < nchunks * (c // 32), -1, 0)

    def radix_cond(st):
        return (st[0] < 32) & (st[1] > 0)

    def radix_step(st):
        p0, _, prefix, need, ncand, settled = st
        cand = cand_ref[...]
        for sub in range(RADIX_BITS_PER_STEP):
            p = p0 + sub
            plane = planes_ref[p]
            ones = cand & plane
            cnt1 = jnp.sum(fold_rows(lax.population_count(ones), jnp.add), axis=0, keepdims=True)
            take1 = cnt1 >= need
            live = settled == 0
            need = jnp.where(live & jnp.logical_not(take1), need - cnt1, need)
            ncand = jnp.where(live, jnp.where(take1, cnt1, ncand - cnt1), ncand)
            prefix = jnp.where(live & take1, prefix | jnp.left_shift(jnp.int32(1), 31 - p), prefix)
            cand = cand & (plane ^ jnp.where(take1, 0, -1))
            settled = jnp.where(ncand == need, 1, settled)
        cand_ref[...] = cand
        return p0 + RADIX_BITS_PER_STEP, jnp.sum(1 - settled), prefix, need, ncand, settled

    done0 = jnp.where(n_causal <= topk, 1, 0)
    _, _, prefix, _, _, _ = lax.while_loop(
        radix_cond, radix_step,
        (jnp.int32(0), jnp.sum(1 - done0), jnp.zeros((1, tq), I32), jnp.full((1, tq), topk, I32),
         jnp.full((1, tq), 1, I32) * (nchunks * c), done0))
    guess_key = prefix ^ int_min
    guess = _from_key(guess_key)

    def count_keys(pred):
        def body(j, part):
            start = pl.multiple_of(j * c, c)
            hit = pred(sct_ref[pl.ds(start, c), :], start)
            return part + fold_rows(jnp.where(hit, 1.0, 0.0), jnp.add)
        part = lax.fori_loop(0, nchunks, body, jnp.zeros((8, tq), F32))
        return jnp.sum(part, axis=0, keepdims=True)

    kf = float(topk)
    guess_above = _from_key(guess_key + 1)
    guess_above = jnp.where(jnp.abs(guess_above) < TINY_F32,
                            jnp.where(guess >= 0.0, TINY_F32, 0.0), guess_above)
    guess_below = _from_key(guess_key - 2)

    def cond(st):
        return st[0] > 0

    def step(st):
        _, it, lo, hi, c_lo, c_hi, done = st
        mid = 0.5 * lo + 0.5 * hi
        lo_k, hi_k = _to_key(lo), _to_key(hi)
        mid_key = _from_key((lo_k >> 1) + (hi_k >> 1) + (lo_k & hi_k & 1))
        mid = jnp.where(it >= KEY_SPACE_AFTER, mid_key, mid)
        probe = jnp.where(it == 0, guess, jnp.where(lo == guess, guess_above, guess_below))
        mid = jnp.where((it <= 1) & (probe > lo) & (probe < hi), probe, mid)
        go = (done == 0) & (mid > lo) & (mid < hi)
        cnt = count_keys(lambda tile, start: tile >= mid)
        ge = cnt >= kf
        up = ge & go
        dn = jnp.logical_not(ge) & go
        lo = jnp.where(up, mid, lo)
        hi = jnp.where(dn, mid, hi)
        c_lo = jnp.where(up, cnt, c_lo)
        c_hi = jnp.where(dn, cnt, c_hi)
        adjacent = (lo == guess) & (hi == guess_above)
        done = jnp.where(go & (c_lo != kf) & jnp.logical_not(adjacent), 0, 1)
        return jnp.sum(1 - done), it + 1, lo, hi, c_lo, c_hi, done

    _, _, thr, _, c_lo, c_hi, _ = lax.while_loop(
        cond, step, (jnp.sum(1 - done0), jnp.int32(0), jnp.full((1, tq), SCORE_NEG, F32),
                     jnp.full((1, tq), -SCORE_NEG, F32), jnp.full((1, tq), 1.0, F32) * (nchunks * c),
                     jnp.zeros((1, tq), F32), done0))

    tied = (n_causal > topk) & (c_lo > kf)
    any_tied = jnp.sum(jnp.where(tied, 1, 0)) > 0

    @pl.when(any_tied)
    def _():
        n_take = kf - c_hi

        def bis(_, st):
            lo_x, hi_x = st
            mid = (lo_x + hi_x) >> 1
            ok = count_keys(lambda tile, start: (tile == thr) & ((krow + start) <= mid)) >= n_take
            return jnp.where(ok, lo_x, mid), jnp.where(ok, mid, hi_x)

        nbits = int(math.ceil(math.log2(sct_ref.shape[0]))) + 1
        _, hi_x = lax.fori_loop(0, nbits, bis, (jnp.full((1, tq), -1, I32),
                                                jnp.full((1, tq), sct_ref.shape[0], I32)))
        cut = jnp.where(tied, hi_x, jnp.int32(2 ** 30))

        def mask_body(j, _):
            start = pl.multiple_of(j * c, c)
            tile = sct_ref[pl.ds(start, c), :]
            sel = (tile > thr) | ((tile == thr) & ((krow + start) <= cut))
            mb_ref[:, pl.ds(start, c)] = jnp.transpose(jnp.where(sel, 0.0, MASK_NEG)).astype(BF16)
            return 0

        lax.fori_loop(0, nchunks, mask_body, 0)

    @pl.when(jnp.logical_not(any_tied))
    def _():
        def mask_body(j, _):
            start = pl.multiple_of(j * c, c)
            tile = sct_ref[pl.ds(start, c), :]
            mb_ref[:, pl.ds(start, c)] = jnp.transpose(
                jnp.where(tile >= thr, 0.0, MASK_NEG)).astype(BF16)
            return 0

        lax.fori_loop(0, nchunks, mask_body, 0)

    masked_heads(q_ref, qm_ref)
    masked_heads(qin_ref, qim_ref)
    w_t_next = jnp.transpose(win_ref[...])[0:IDX_HEADS, :]
    has_next = i + 1 < pl.num_programs(1)
    m_ref[...] = jnp.full(m_ref.shape, MASK_NEG, F32)
    l_ref[...] = jnp.zeros(l_ref.shape, F32)
    acc_ref[...] = jnp.zeros(acc_ref.shape, F32)

    def attend(start, width, kind):
        nt = width // LANES
        ones = jnp.ones((width, LANES), BF16)
        for h in range(N_HEADS):
            hp = h // 2
            kp = k_ref[pl.ds(start, width), hp * LANES:(hp + 1) * LANES]
            vp = v_ref[pl.ds(start, width), hp * LANES:(hp + 1) * LANES]
            s = lax.dot_general(qm_ref[h], kp, nt_dims, preferred_element_type=F32)
            s = s + mb_ref[:, pl.ds(start, width)].astype(F32)
            if kind is not None:
                s = s + tb_ref[h, kind]
            m_old = m_ref[h]
            m_new = jnp.maximum(m_old, jnp.max(_fold_lanes(s, jnp.maximum), axis=1, keepdims=True))
            p = jnp.exp2(s - _tile_lanes(m_new, nt)).astype(BF16)
            pv = jnp.dot(p, jnp.concatenate([vp, ones], axis=1), preferred_element_type=F32)
            alpha = jnp.exp2(m_old - m_new)
            m_ref[h] = m_new
            l_ref[h] = alpha * l_ref[h] + pv[:, LANES:]
            acc_ref[h] = alpha * acc_ref[h] + pv[:, :LANES]

    wide = 2 * c
    n_far = jnp.maximum(i - 1, 0)

    n_wide = n_far // 2

    def far_body(j, _):
        attend(pl.multiple_of(j * wide, wide), wide, None)
        score_chunk(2 * j, False, w_t_next)
        score_chunk(2 * j + 1, False, w_t_next)
        return 0

    lax.fori_loop(0, n_wide, far_body, 0)

    @pl.when(n_far % 2 == 1)
    def _():
        attend(pl.multiple_of((n_far - 1) * c, c), c, None)
        score_chunk(n_far - 1, False, w_t_next)

    @pl.when(i >= 1)
    def _():
        attend(pl.multiple_of((i - 1) * c, c), c, 1)
        score_chunk(i - 1, False, w_t_next)

    attend(pl.multiple_of(i * c, c), c, 0)
    score_chunk(i, False, w_t_next)

    @pl.when(has_next)
    def _():
        score_chunk(i + 1, True, w_t_next)

    for hp in range(N_HEADS // 2):
        lo = acc_ref[2 * hp] / l_ref[2 * hp]
        hi = acc_ref[2 * hp + 1] / l_ref[2 * hp + 1]
        o_ref[:, hp * LANES:(hp + 1) * LANES] = jnp.where(low_half, lo, hi).astype(BF16)


def _attn_call(rel_bias, q, qi, wi, k, v, kid, tq):
    bsz, s, _ = q.shape
    nq = s // tq
    topk = min(TOPK_MAX, s // 4)
    assert tq % LANES == 0 and tq >= MAX_DISTANCE
    blk = lambda w: pl.BlockSpec((None, tq, w), lambda b, i: (b, i, 0))
    nxt = lambda w: pl.BlockSpec((None, tq, w), lambda b, i: (b, jnp.minimum(i + 1, nq - 1), 0))
    whole = lambda w: pl.BlockSpec((None, s, w), lambda b, i: (b, 0, 0), pipeline_mode=pl.Buffered(1))
    return pl.pallas_call(
        functools.partial(_attn_kernel, tq=tq, topk=topk),
        grid=(bsz, nq),
        in_specs=[pl.BlockSpec(memory_space=pltpu.SMEM),
                  blk(ATTN_WIDTH), blk(ATTN_WIDTH), blk(LANES), nxt(ATTN_WIDTH), nxt(LANES),
                  whole(ATTN_WIDTH), whole(ATTN_WIDTH), whole(LANES)],
        out_specs=blk(ATTN_WIDTH),
        out_shape=jax.ShapeDtypeStruct((bsz, s, ATTN_WIDTH), BF16),
        scratch_shapes=[
            pltpu.VMEM((s, tq), F32),
            pltpu.VMEM((32, s // 32, tq), I32),
            pltpu.VMEM((s // 32, tq), I32),
            pltpu.VMEM((tq, s), BF16),
            pltpu.VMEM((N_HEADS, 2, tq, tq), F32),
            pltpu.VMEM((N_HEADS, tq, LANES), BF16),
            pltpu.VMEM((N_HEADS, tq, LANES), BF16),
            pltpu.VMEM((N_HEADS, tq, LANES), F32),
            pltpu.VMEM((N_HEADS, tq, LANES), F32),
            pltpu.VMEM((N_HEADS, tq, LANES), F32),
        ],
        compiler_params=pltpu.CompilerParams(dimension_semantics=("arbitrary", "arbitrary"),
                                             vmem_limit_bytes=VMEM_LIMIT),
        name="attn",
    )(rel_bias, q, qi, wi, qi, wi, k, v, kid)


def _post_kernel(attn_ref, z_ref, gate_ref, x_ref, mod_ref, wa_ref, wc_ref, wo_ref, g2_ref,
                 wr_ref, br_ref, hres_ref, h2_ref, eid_ref, ew_ref, counts_ref, cnt_ref):
    ya = jnp.dot(attn_ref[...], wa_ref[...], preferred_element_type=F32)
    yc = jnp.dot(z_ref[...], wc_ref[...], preferred_element_type=F32)
    merged = gate_ref[:, 0:D_MODEL].astype(F32) * ya + gate_ref[:, D_MODEL:].astype(F32) * yc
    o = jnp.dot(merged.astype(BF16), wo_ref[...], preferred_element_type=F32)
    hres = x_ref[...] + mod_ref[2:3, :] * o
    hres_ref[...] = hres
    h2 = _rms_mod(hres, g2_ref[...], mod_ref[4:5, :], mod_ref[3:4, :])
    h2_ref[...] = h2.astype(BF16).reshape(h2_ref.shape)

    h2_hi = h2.astype(BF16)
    h2_lo = (h2 - h2_hi.astype(F32)).astype(BF16)
    lg = (jnp.dot(h2_hi, wr_ref[0], preferred_element_type=F32)
          + (jnp.dot(h2_lo, wr_ref[0], preferred_element_type=F32)
             + jnp.dot(h2_hi, wr_ref[1], preferred_element_type=F32))) + br_ref[...]
    tm = lg.shape[0]
    lgt = jnp.transpose(lg)
    ninf = -jnp.inf
    grow = lax.broadcasted_iota(I32, (8, tm), 0)
    gl = jnp.where(grow < N_GROUPS, lgt[N_EXPERTS:N_EXPERTS + 8, :], ninf)
    gmax = jnp.max(gl, axis=0, keepdims=True)
    g_sel = jnp.min(jnp.where(gl == gmax, grow, N_GROUPS), axis=0, keepdims=True)
    g_w = 1.0 / jnp.sum(jnp.exp(gl - gmax), axis=0, keepdims=True)

    erow = lax.broadcasted_iota(I32, (N_EXPERTS, tm), 0)
    emask = (erow // EXPERTS_PER_GROUP) == g_sel
    el = jnp.where(emask, lgt[0:N_EXPERTS, :], ninf)
    emax = jnp.max(el, axis=0, keepdims=True)
    ee = jnp.exp(el - emax)
    e_prob = ee / jnp.sum(ee, axis=0, keepdims=True)
    p1 = jnp.max(jnp.where(emask, e_prob, -1.0), axis=0, keepdims=True)
    i1 = jnp.min(jnp.where(emask & (e_prob == p1), erow, LANES), axis=0, keepdims=True)
    rest = emask & (erow != i1)
    p2 = jnp.max(jnp.where(rest, e_prob, -1.0), axis=0, keepdims=True)
    i2 = jnp.min(jnp.where(rest & (e_prob == p2), erow, LANES), axis=0, keepdims=True)
    psum = p1 + p2
    w1 = g_w * (p1 / psum)
    w2 = g_w * (p2 / psum)
    ew_ref[...] = jnp.transpose(jnp.concatenate([w1, w2, jnp.zeros((LANES - 2, tm), F32)], axis=0))

    @pl.when((pl.program_id(0) == 0) & (pl.program_id(1) == 0))
    def _():
        cnt_ref[...] = jnp.zeros(cnt_ref.shape, F32)

    onehot_t = jnp.where((erow == i1) | (erow == i2), 1.0, 0.0).astype(BF16)
    upper = (lax.broadcasted_iota(I32, (tm, tm), 0) <= lax.broadcasted_iota(I32, (tm, tm), 1))
    cum_t = jnp.dot(onehot_t, jnp.where(upper, 1.0, 0.0).astype(BF16), preferred_element_type=F32)
    cnt_col = jnp.transpose(cnt_ref[...])[0:N_EXPERTS, 0:1]
    before = cum_t - onehot_t.astype(F32) + cnt_col
    r1 = jnp.sum(jnp.where(erow == i1, before, 0.0), axis=0, keepdims=True).astype(I32)
    r2 = jnp.sum(jnp.where(erow == i2, before, 0.0), axis=0, keepdims=True).astype(I32)
    eid_ref[...] = jnp.concatenate([i1, i2, r1, r2, jnp.zeros((4, tm), I32)], axis=0)
    padded = jnp.concatenate([onehot_t, jnp.zeros((LANES - N_EXPERTS, tm), BF16)], axis=0)
    tile_total = lax.dot_general(jnp.ones((8, tm), BF16), padded, (((1,), (1,)), ((), ())),
                                 preferred_element_type=F32)
    total = cnt_ref[...] + tile_total
    cnt_ref[...] = total
    counts_ref[...] = total


def _post_call(attn, z, gates, x, mod, wa, wc, wo, g2, wr, br, tm):
    bsz, s, d = x.shape
    nt = s // tm
    tok = lambda w: pl.BlockSpec((None, tm, w), lambda b, t: (b, t, 0))
    full = lambda arr: pl.BlockSpec(arr.shape, lambda b, t: (0,) * arr.ndim)
    return pl.pallas_call(
        _post_kernel,
        grid=(bsz, nt),
        in_specs=[tok(ATTN_WIDTH), tok(CONV_WIDTH), tok(2 * d), tok(d),
                  pl.BlockSpec((None, 6, d), lambda b, t: (b, 0, 0)),
                  full(wa), full(wc), full(wo), full(g2), full(wr), full(br)],
        out_specs=[tok(d), pl.BlockSpec((None, tm) + ROW_TILE, lambda b, t: (b, t, 0, 0)),
                   pl.BlockSpec((8, tm), lambda b, t: (0, b * nt + t)), tok(LANES),
                   pl.BlockSpec((8, LANES), lambda b, t: (0, 0))],
        out_shape=[jax.ShapeDtypeStruct((bsz, s, d), F32),
                   jax.ShapeDtypeStruct((bsz, s) + ROW_TILE, BF16),
                   jax.ShapeDtypeStruct((8, bsz * s), I32), jax.ShapeDtypeStruct((bsz, s, LANES), F32),
                   jax.ShapeDtypeStruct((8, LANES), F32)],
        scratch_shapes=[pltpu.VMEM((8, LANES), F32)],
        compiler_params=pltpu.CompilerParams(dimension_semantics=("arbitrary", "arbitrary"),
                                             vmem_limit_bytes=VMEM_LIMIT),
        name="post",
    )(attn, z, gates, x, mod, wa, wc, wo, g2, wr, br)


ROW_TILE = (D_MODEL // LANES, LANES)
EXPERT_ROWS = 256
TOKENS_PER_STEP = 256


ISSUE_UNROLL = 8


def _dispatch_kernel(pos1_ref, pos2_ref, h2_ref, xs_in_ref, xs_ref, sem):
    del xs_in_ref
    tm = h2_ref.shape[0]
    base = pl.program_id(0) * tm

    def issue(r, c):
        pltpu.make_async_copy(h2_ref.at[r], xs_ref.at[pos1_ref[base + r]], sem).start(priority=0)
        pltpu.make_async_copy(h2_ref.at[r], xs_ref.at[pos2_ref[base + r]], sem).start(priority=1)
        return c

    lax.fori_loop(0, tm, issue, 0, unroll=ISSUE_UNROLL)
    for _ in range(2):
        pltpu.make_async_copy(h2_ref, xs_ref.at[pl.ds(0, tm)], sem).wait()


def _dispatch_call(pos1, pos2, h2_rows, n_rows):
    t = h2_rows.shape[0]
    tm = min(TOKENS_PER_STEP, t)
    xs0 = jnp.zeros((n_rows,) + ROW_TILE, BF16)
    return pl.pallas_call(
        _dispatch_kernel,
        grid_spec=pltpu.PrefetchScalarGridSpec(
            num_scalar_prefetch=2,
            grid=(t // tm,),
            in_specs=[pl.BlockSpec((tm,) + ROW_TILE, lambda i, p1, p2: (i, 0, 0)),
                      pl.BlockSpec(memory_space=pl.ANY)],
            out_specs=pl.BlockSpec(memory_space=pl.ANY),
            scratch_shapes=[pltpu.SemaphoreType.DMA(())]),
        out_shape=jax.ShapeDtypeStruct((n_rows,) + ROW_TILE, BF16),
        input_output_aliases={3: 0},
        compiler_params=pltpu.CompilerParams(dimension_semantics=("arbitrary",),
                                             vmem_limit_bytes=VMEM_LIMIT),
        name="dispatch",
    )(pos1, pos2, h2_rows, xs0)


def _expert_kernel(te_ref, nu_ref, xs_ref, wg_ref, wu_ref, wd_ref, ys_ref, wgb, wub, wdb):
    g = pl.program_id(0)
    e = te_ref[g]
    e_prev = te_ref[jnp.maximum(g - 1, 0)]

    @pl.when((g == 0) | (e != e_prev))
    def _():
        wgb[...] = wg_ref[...].astype(BF16)
        wub[...] = wu_ref[...].astype(BF16)
        wdb[...] = wd_ref[...].astype(BF16)

    @pl.when(g < nu_ref[0])
    def _():
        x = xs_ref[...].reshape(xs_ref.shape[0], D_MODEL)
        a = jnp.dot(x, wgb[...], preferred_element_type=F32)
        b = jnp.dot(x, wub[...], preferred_element_type=F32)
        hid = (a * jax.nn.sigmoid(a)) * b
        y = jnp.dot(hid.astype(BF16), wdb[...], preferred_element_type=F32)
        ys_ref[...] = y.astype(BF16).reshape(ys_ref.shape)

    @pl.when(g >= nu_ref[0])
    def _():
        ys_ref[...] = jnp.zeros(ys_ref.shape, BF16)


def _expert_call(tile_expert, n_used, xs, wg, wu, wd):
    n_rows, d = xs.shape[0], D_MODEL
    nt = n_rows // EXPERT_ROWS
    rows = pl.BlockSpec((EXPERT_ROWS,) + ROW_TILE, lambda g, te, nu: (g, 0, 0))
    wspec = lambda shp: pl.BlockSpec((None,) + shp, lambda g, te, nu: (te[g], 0, 0))
    return pl.pallas_call(
        _expert_kernel,
        grid_spec=pltpu.PrefetchScalarGridSpec(
            num_scalar_prefetch=2,
            grid=(nt,),
            in_specs=[rows, wspec((d, D_EXPERT)), wspec((d, D_EXPERT)), wspec((D_EXPERT, d))],
            out_specs=rows,
            scratch_shapes=[pltpu.VMEM((d, D_EXPERT), BF16), pltpu.VMEM((d, D_EXPERT), BF16),
                            pltpu.VMEM((D_EXPERT, d), BF16)]),
        out_shape=jax.ShapeDtypeStruct((n_rows,) + ROW_TILE, BF16),
        compiler_params=pltpu.CompilerParams(dimension_semantics=("arbitrary",),
                                             vmem_limit_bytes=VMEM_LIMIT),
        name="experts",
    )(tile_expert, n_used, xs, wg, wu, wd)


def _combine_kernel(pos1_ref, pos2_ref, ys_ref, ew_ref, hres_ref, g2_ref, gf_ref, o_ref, buf, sem):
    tm = hres_ref.shape[0]
    t = pl.program_id(0)
    nt = pl.num_programs(0)

    def issue(tile, slot):
        base = tile * tm

        def body(r, c):
            pltpu.make_async_copy(ys_ref.at[pos1_ref[base + r]], buf.at[slot, r],
                                  sem.at[slot]).start(priority=0)
            pltpu.make_async_copy(ys_ref.at[pos2_ref[base + r]], buf.at[slot, tm + r],
                                  sem.at[slot]).start(priority=1)
            return c

        lax.fori_loop(0, tm, body, 0, unroll=ISSUE_UNROLL)

    @pl.when(t == 0)
    def _():
        issue(0, 0)

    @pl.when(t + 1 < nt)
    def _():
        issue(t + 1, (t + 1) % 2)

    slot = t % 2
    pltpu.make_async_copy(ys_ref.at[pl.ds(0, 2 * tm)], buf.at[slot], sem.at[slot]).wait()

    y1 = buf[slot, 0:tm].astype(F32).reshape(tm, D_MODEL)
    y2 = buf[slot, tm:2 * tm].astype(F32).reshape(tm, D_MODEL)
    w = ew_ref[...]
    y = w[:, 0:1] * y1 + w[:, 1:2] * y2
    hfin = hres_ref[...] + g2_ref[...] * y
    ms = jnp.mean(hfin * hfin, axis=-1, keepdims=True)
    o_ref[...] = hfin * lax.rsqrt(ms + EPS) * gf_ref[...]


def _combine_call(pos1, pos2, ys_rows, ew, hres, gate2, gf):
    t, d = hres.shape
    tm = min(TOKENS_PER_STEP, t)
    per_batch = t // gate2.shape[0]
    assert per_batch % tm == 0
    tok = lambda w: pl.BlockSpec((tm, w), lambda i, *_: (i, 0))
    return pl.pallas_call(
        _combine_kernel,
        grid_spec=pltpu.PrefetchScalarGridSpec(
            num_scalar_prefetch=2,
            grid=(t // tm,),
            in_specs=[pl.BlockSpec(memory_space=pl.ANY), tok(LANES), tok(d),
                      pl.BlockSpec((None, 1, d), lambda i, *_: ((i * tm) // per_batch, 0, 0)),
                      pl.BlockSpec((1, d), lambda i, *_: (0, 0))],
            out_specs=tok(d),
            scratch_shapes=[pltpu.VMEM((2, 2 * tm) + ROW_TILE, BF16), pltpu.SemaphoreType.DMA((2,))]),
        out_shape=jax.ShapeDtypeStruct((t, d), F32),
        compiler_params=pltpu.CompilerParams(dimension_semantics=("arbitrary",),
                                             vmem_limit_bytes=VMEM_LIMIT),
        name="combine",
    )(pos1, pos2, ys_rows, ew, hres, gate2, gf)


def _slot_kernel(seg_ref, eid_ref, pos_ref):
    e = eid_ref[0:2, :]
    pos = eid_ref[2:4, :]
    for k in range(seg_ref.shape[0]):
        pos = pos + jnp.where(e == k, seg_ref[k], 0)
    pos_ref[...] = jnp.concatenate([pos, jnp.zeros((6, pos.shape[1]), I32)], axis=0)


def _slot_call(seg_start, eid):
    return pl.pallas_call(
        _slot_kernel,
        in_specs=[pl.BlockSpec(memory_space=pltpu.SMEM), pl.BlockSpec(eid.shape, lambda: (0, 0))],
        out_specs=pl.BlockSpec(eid.shape, lambda: (0, 0)),
        out_shape=jax.ShapeDtypeStruct(eid.shape, I32),
        name="slots",
    )(seg_start, eid)


def _sparse_moe(h2, eid, ew, counts, wg, wu, wd, hres, gate2, gf):
    bsz, s, d = hres.shape
    t = bsz * s
    ne = wg.shape[0]
    cnt = counts[0, :ne].astype(I32)
    seg_rows = ((cnt + EXPERT_ROWS - 1) // EXPERT_ROWS) * EXPERT_ROWS
    seg_end = jnp.cumsum(seg_rows)
    seg_start = seg_end - seg_rows
    pos = _slot_call(seg_start, eid)
    pos1, pos2 = pos[0], pos[1]
    n_tiles = (2 * t) // EXPERT_ROWS + ne
    tile_first_row = jnp.arange(n_tiles, dtype=I32) * EXPERT_ROWS
    tile_expert = jnp.minimum(jnp.sum(seg_end[None, :] <= tile_first_row[:, None], axis=1), ne - 1)
    n_used = (seg_end[ne - 1] // EXPERT_ROWS).reshape(1)
    n_rows = n_tiles * EXPERT_ROWS

    xs = _dispatch_call(pos1, pos2, h2.reshape((t,) + ROW_TILE), n_rows)
    ys = _expert_call(tile_expert.astype(I32), n_used.astype(I32), xs, wg, wu, wd)
    out = _combine_call(pos1, pos2, ys, ew.reshape(t, LANES), hres.reshape(t, d),
                        gate2.reshape(bsz, 1, d), gf.reshape(1, d))
    return out.reshape(bsz, s, d)


def kernel(x, c, w_ada, b_ada, norm1_g, w_in, rel_bias, conv_w, w_attn_branch, w_conv_branch, w_out,
           norm2_g, w_router_group, b_router_group, w_router_expert, b_router_expert,
           w_gate_e, w_up_e, w_down_e, norm_f_g):
    bsz, s, d = x.shape
    depth = w_ada.shape[0]
    h = x.astype(F32)
    c_pad = jnp.zeros((8, d), F32).at[:bsz].set(c.astype(F32))
    out = h
    for l in range(depth):
        mod = _ada_call(c_pad, w_ada[l].astype(F32), b_ada[l].astype(F32)[None, :])
        mod = mod[:bsz].reshape(bsz, 6, d)

        w = w_in[l]
        o_q, o_k, o_v, o_qi = 0, 512, 1024, 1536
        o_ki, o_wi, o_cb, o_gl = 2048, 2112, 2120, 3656
        wa = w[:, o_q:o_ki].astype(BF16)
        ki_w = w[:, o_ki:o_wi]
        wb = jnp.concatenate([ki_w, ki_w, w[:, o_wi:o_cb],
                              jnp.zeros((d, LANES - IDX_HEADS), w.dtype)], axis=1).astype(BF16)
        wc = w[:, o_cb:o_gl].astype(BF16)
        wd = w[:, o_gl:].astype(BF16)

        q, k, v, qi, kid, wi, z, gates = _proj_call(
            h, mod, norm1_g[l].astype(F32)[None, :], wa, wb, wc, wd, conv_w[l].astype(F32),
            tm=min(512, s))

        attn = _attn_call(rel_bias.astype(F32), q, qi, wi, k, v, kid, tq=min(TOPK_MAX, s // 4))

        wr32 = jnp.concatenate([w_router_expert[l], w_router_group[l],
                                jnp.zeros((d, LANES - N_EXPERTS - N_GROUPS), F32)], axis=1).astype(F32)
        wr_hi = wr32.astype(BF16)
        wr = jnp.stack([wr_hi, (wr32 - wr_hi.astype(F32)).astype(BF16)])
        br = jnp.concatenate([b_router_expert[l], b_router_group[l],
                              jnp.zeros((LANES - N_EXPERTS - N_GROUPS,), F32)])[None, :].astype(F32)
        hres, h2, eid, ew, counts = _post_call(
            attn, z, gates, h, mod, w_attn_branch[l].astype(BF16), w_conv_branch[l].astype(BF16),
            w_out[l].astype(BF16), norm2_g[l].astype(F32)[None, :], wr, br, tm=min(512, s))

        assert depth == 1, "the final RMSNorm is fused into the combine kernel of the only layer"
        out = _sparse_moe(h2, eid, ew, counts, w_gate_e[l].astype(F32), w_up_e[l].astype(F32),
                          w_down_e[l].astype(F32), hres, mod[:, 5, :], norm_f_g.astype(F32))
        h = out
    return out.astype(x.dtype)
```

```python
import functools
import math

import numpy as np
import jax
import jax.numpy as jnp
from jax import lax
from jax.experimental import pallas as pl
from jax.experimental.pallas import tpu as pltpu

F32 = jnp.float32
BF16 = jnp.bfloat16
I32 = jnp.int32

D_MODEL = 1024
N_HEADS = 8
HEAD_DIM = 64
ATTN_WIDTH = N_HEADS * HEAD_DIM
IDX_HEADS = 8
IDX_DIM = 64
TOPK_MAX = 256
N_BUCKETS = 32
MAX_DISTANCE = 128
CONV_WIDTH = 512
CONV_K = 3
N_GROUPS = 4
EXPERTS_PER_GROUP = 8
N_EXPERTS = N_GROUPS * EXPERTS_PER_GROUP
D_EXPERT = 512
EPS = 1e-6

LANES = 128
VMEM_LIMIT = 56 * 1024 * 1024

LOG2E = math.log2(math.e)
MASK_NEG = -1e30
SCORE_NEG = float(np.finfo(np.float32).min)
TINY_F32 = float(np.finfo(np.float32).tiny)
KEY_SPACE_AFTER = 40
RADIX_BITS_PER_STEP = 4


def _t5_bucket_starts():
    max_exact = N_BUCKETS // 2
    n = np.arange(0, MAX_DISTANCE + 1)
    nf = np.maximum(n, 1).astype(np.float32)
    val = (np.log(nf / np.float32(max_exact)) / np.float32(math.log(MAX_DISTANCE / max_exact))
           * np.float32(N_BUCKETS - max_exact)).astype(np.float32)
    inner = val[max_exact + 1:MAX_DISTANCE]
    assert np.min(np.abs(inner - np.round(inner))) > 1e-3
    large = np.minimum(max_exact + val.astype(np.int32), N_BUCKETS - 1)
    bucket = np.where(n < max_exact, n, large)
    assert bucket[MAX_DISTANCE] == N_BUCKETS - 1 and np.all(np.diff(bucket) >= 0)
    return [int(np.argmax(bucket >= b)) for b in range(N_BUCKETS)]


_BUCKET_START = _t5_bucket_starts()


def _ada_kernel(c_ref, w_ref, b_ref, o_ref):
    c = c_ref[...]
    ca = c * jax.nn.sigmoid(c)
    o_ref[...] = jnp.dot(ca, w_ref[...], preferred_element_type=F32,
                         precision=lax.Precision.HIGHEST) + b_ref[...]


def _ada_call(c_pad, w, b):
    rows, d = c_pad.shape
    n = w.shape[1]
    tn = 1536
    return pl.pallas_call(
        _ada_kernel,
        grid=(n // tn,),
        in_specs=[pl.BlockSpec((rows, d), lambda j: (0, 0)),
                  pl.BlockSpec((d, tn), lambda j: (0, j)),
                  pl.BlockSpec((1, tn), lambda j: (0, j))],
        out_specs=pl.BlockSpec((rows, tn), lambda j: (0, j)),
        out_shape=jax.ShapeDtypeStruct((rows, n), F32),
        compiler_params=pltpu.CompilerParams(dimension_semantics=("arbitrary",),
                                             vmem_limit_bytes=VMEM_LIMIT),
        name="ada",
    )(c_pad, w, b)


def _rms_mod(x, g, scale, shift):
    ms = jnp.mean(x * x, axis=-1, keepdims=True)
    y = x * lax.rsqrt(ms + EPS) * g
    return y * (1.0 + scale) + shift


def _proj_kernel(x_ref, mod_ref, g_ref, wa_ref, wb_ref, wc_ref, wd_ref, cw_ref,
                 q_ref, k_ref, v_ref, qi_ref, kid_ref, wi_ref, z_ref, gate_ref, ubuf, *, tm):
    h = _rms_mod(x_ref[...], g_ref[...], mod_ref[1:2, :], mod_ref[0:1, :])
    hb = h.astype(BF16)

    a = jnp.dot(hb, wa_ref[...], preferred_element_type=F32)
    q_ref[...] = (a[:, 0:512] * (HEAD_DIM ** -0.5 * LOG2E)).astype(BF16)
    k_ref[...] = a[:, 512:1024].astype(BF16)
    v_ref[...] = a[:, 1024:1536].astype(BF16)
    qi_ref[...] = a[:, 1536:2048].astype(BF16)

    b = jnp.dot(hb, wb_ref[...], preferred_element_type=F32)
    kid_ref[...] = b[:, 0:LANES].astype(BF16)
    wi_ref[...] = b[:, LANES:2 * LANES] * ((IDX_DIM ** -0.5) * (IDX_HEADS ** -0.5))

    cpart = jnp.dot(hb, wc_ref[...], preferred_element_type=F32)
    cb = cpart[:, 0:512]
    u = cpart[:, 512:1024] * cpart[:, 1024:1536]

    @pl.when(pl.program_id(1) == 0)
    def _():
        ubuf[0:8, :] = jnp.zeros((8, CONV_WIDTH), F32)

    ubuf[8:tm + 8, :] = u
    conv = (cw_ref[0:1, :] * ubuf[6:tm + 6, :] + cw_ref[1:2, :] * ubuf[7:tm + 7, :]
            + cw_ref[2:3, :] * u)
    z_ref[...] = (cb * conv).astype(BF16)
    ubuf[0:8, :] = ubuf[tm:tm + 8, :]

    d = jnp.dot(hb, wd_ref[...], preferred_element_type=F32)
    gate_ref[...] = jax.nn.sigmoid(d).astype(BF16)


def _proj_call(x, mod, g1, wa, wb, wc, wd, cw, tm):
    bsz, s, d = x.shape
    nt = s // tm
    tok = lambda w: pl.BlockSpec((None, tm, w), lambda b, t: (b, t, 0))
    full = lambda arr: pl.BlockSpec(arr.shape, lambda b, t: (0,) * arr.ndim)
    out_widths = [(512, BF16), (512, BF16), (512, BF16), (512, BF16), (LANES, BF16), (LANES, F32),
                  (512, BF16), (2 * D_MODEL, BF16)]
    return pl.pallas_call(
        functools.partial(_proj_kernel, tm=tm),
        grid=(bsz, nt),
        in_specs=[tok(d), pl.BlockSpec((None, 6, d), lambda b, t: (b, 0, 0)), full(g1),
                  full(wa), full(wb), full(wc), full(wd), full(cw)],
        out_specs=[tok(w) for w, _ in out_widths],
        out_shape=[jax.ShapeDtypeStruct((bsz, s, w), dt) for w, dt in out_widths],
        scratch_shapes=[pltpu.VMEM((tm + 8, CONV_WIDTH), F32)],
        compiler_params=pltpu.CompilerParams(dimension_semantics=("arbitrary", "arbitrary"),
                                             vmem_limit_bytes=VMEM_LIMIT),
        name="proj",
    )(x, mod, g1, wa, wb, wc, wd, cw)


def _to_key(f):
    b = pltpu.bitcast(f, I32)
    return jnp.where(b < 0, b ^ 0x7FFFFFFF, b)


def _from_key(k):
    return pltpu.bitcast(jnp.where(k < 0, k ^ 0x7FFFFFFF, k), F32)


def _bit_transpose32(words):
    a = list(words)
    j, m = 16, 0x0000FFFF
    while j:
        k = 0
        while k < 32:
            t = (a[k] ^ lax.shift_right_logical(a[k + j], jnp.int32(j))) & jnp.int32(m)
            a[k] = a[k] ^ t
            a[k + j] = a[k + j] ^ lax.shift_left(t, jnp.int32(j))
            k = (k + j + 1) & ~j
        j >>= 1
        m = (m ^ (m << j)) & 0xFFFFFFFF
    return a


def _fold_lanes(x, op):
    parts = [x[:, t * LANES:(t + 1) * LANES] for t in range(x.shape[1] // LANES)]
    while len(parts) > 1:
        nxt = [op(parts[a], parts[a + 1]) for a in range(0, len(parts) - 1, 2)]
        parts = nxt + ([parts[-1]] if len(parts) % 2 else [])
    return parts[0]


def _tile_lanes(x, n):
    return x if n == 1 else jnp.concatenate([x] * n, axis=1)


def _attn_kernel(rb_ref, q_ref, qi_ref, wi_ref, qin_ref, win_ref, k_ref, v_ref, kid_ref, o_ref,
                 sct_ref, planes_ref, cand_ref, mb_ref, tb_ref, qm_ref, qim_ref, m_ref, acc_ref,
                 *, tq, topk):
    c = tq
    i = pl.program_id(1)
    nt_dims = (((1,), (1,)), ((), ()))
    lane = lax.broadcasted_iota(I32, (tq, LANES), 1)
    low_half = lane < HEAD_DIM
    row = lax.broadcasted_iota(I32, (tq, c), 0)
    col = lax.broadcasted_iota(I32, (tq, c), 1)

    @pl.when((pl.program_id(0) == 0) & (i == 0))
    def _():
        for p in range(32):
            planes_ref[p] = jnp.zeros(planes_ref.shape[1:], I32)
        for kind in range(2):
            rel = row - col + kind * c
            n = jnp.maximum(rel, 0)
            for h in range(N_HEADS):
                t = jnp.full((tq, c), rb_ref[0, h], F32)
                for b in range(1, N_BUCKETS):
                    t = jnp.where(n >= _BUCKET_START[b], rb_ref[b, h], t)
                t = (t - rb_ref[N_BUCKETS - 1, h]) * LOG2E
                tb_ref[h, kind] = jnp.where(rel >= 0, t, MASK_NEG)


    def masked_heads(src_ref, dst_ref):
        for h in range(N_HEADS):
            hp, half = divmod(h, 2)
            keep = low_half if half == 0 else jnp.logical_not(low_half)
            pair = src_ref[:, hp * LANES:(hp + 1) * LANES]
            dst_ref[h] = jnp.where(keep, pair, jnp.zeros_like(pair))

    krow = lax.broadcasted_iota(I32, (c, tq), 0)
    qcol = lax.broadcasted_iota(I32, (c, tq), 1)
    int_min = jnp.int32(-2 ** 31)

    def fold_rows(x, op):
        parts = [x[r * 8:(r + 1) * 8, :] for r in range(x.shape[0] // 8)]
        while len(parts) > 1:
            nxt = [op(parts[a], parts[a + 1]) for a in range(0, len(parts) - 1, 2)]
            parts = nxt + ([parts[-1]] if len(parts) % 2 else [])
        return parts[0]

    def score_chunk(j, diag, w_t):
        start = pl.multiple_of(j * c, c)
        kc = kid_ref[pl.ds(start, c), :]
        acc = jnp.zeros((c, tq), F32)
        for h in range(N_HEADS):
            d = lax.dot_general(kc, qim_ref[h], nt_dims, preferred_element_type=F32)
            acc = acc + jnp.maximum(d, 0.0) * w_t[h:h + 1, :]
        ukey = _to_key(acc) ^ int_min
        if diag:
            causal = krow <= qcol
            acc = jnp.where(causal, acc, SCORE_NEG)
            ukey = jnp.where(causal, ukey, 0)
        sct_ref[pl.ds(start, c), :] = acc
        prow = pl.multiple_of(j * (c // 32), c // 32)
        for lt in range(tq // LANES):
            words = [ukey[k * 8:(k + 1) * 8, lt * LANES:(lt + 1) * LANES] for k in range(32)]
            for p, plane in enumerate(_bit_transpose32(words)):
                planes_ref[p, pl.ds(prow, c // 32), lt * LANES:(lt + 1) * LANES] = plane

    @pl.when(i == 0)
    def _():
        masked_heads(qi_ref, qim_ref)
        score_chunk(0, True, jnp.transpose(wi_ref[...])[0:IDX_HEADS, :])

    nchunks = i + 1
    n_causal = i * tq + lax.broadcasted_iota(I32, (1, tq), 1) + 1
    prow_iota = lax.broadcasted_iota(I32, cand_ref.shape, 0)
    cand_ref[...] = jnp.where(prow_iota < nchunks * (c // 32), -1, 0)

    def radix_cond(st):
        return (st[0] < 32) & (st[1] > 0)

    def radix_step(st):
        p0, _, prefix, need, ncand, settled = st
        cand = cand_ref[...]
        for sub in range(RADIX_BITS_PER_STEP):
            p = p0 + sub
            plane = planes_ref[p]
            ones = cand & plane
            cnt1 = jnp.sum(fold_rows(lax.population_count(ones), jnp.add), axis=0, keepdims=True)
            take1 = cnt1 >= need
            live = settled == 0
            need = jnp.where(live & jnp.logical_not(take1), need - cnt1, need)
            ncand = jnp.where(live, jnp.where(take1, cnt1, ncand - cnt1), ncand)
            prefix = jnp.where(live & take1, prefix | jnp.left_shift(jnp.int32(1), 31 - p), prefix)
            cand = cand & (plane ^ jnp.where(take1, 0, -1))
            settled = jnp.where(ncand == need, 1, settled)
        cand_ref[...] = cand
        return p0 + RADIX_BITS_PER_STEP, jnp.sum(1 - settled), prefix, need, ncand, settled

    done0 = jnp.where(n_causal <= topk, 1, 0)
    _, _, prefix, _, _, _ = lax.while_loop(
        radix_cond, radix_step,
        (jnp.int32(0), jnp.sum(1 - done0), jnp.zeros((1, tq), I32), jnp.full((1, tq), topk, I32),
         jnp.full((1, tq), 1, I32) * (nchunks * c), done0))
    guess_key = prefix ^ int_min
    guess = _from_key(guess_key)

    def count_keys(pred):
        def body(j, part):
            start = pl.multiple_of(j * c, c)
            hit = pred(sct_ref[pl.ds(start, c), :], start)
            return part + fold_rows(jnp.where(hit, 1.0, 0.0), jnp.add)
        part = lax.fori_loop(0, nchunks, body, jnp.zeros((8, tq), F32))
        return jnp.sum(part, axis=0, keepdims=True)

    kf = float(topk)
    guess_above = _from_key(guess_key + 1)
    guess_above = jnp.where(jnp.abs(guess_above) < TINY_F32,
                            jnp.where(guess >= 0.0, TINY_F32, 0.0), guess_above)
    guess_below = _from_key(guess_key - 2)

    def cond(st):
        return st[0] > 0

    def step(st):
        _, it, lo, hi, c_lo, c_hi, done = st
        mid = 0.5 * lo + 0.5 * hi
        lo_k, hi_k = _to_key(lo), _to_key(hi)
        mid_key = _from_key((lo_k >> 1) + (hi_k >> 1) + (lo_k & hi_k & 1))
        mid = jnp.where(it >= KEY_SPACE_AFTER, mid_key, mid)
        probe = jnp.where(it == 0, guess, jnp.where(lo == guess, guess_above, guess_below))
        mid = jnp.where((it <= 1) & (probe > lo) & (probe < hi), probe, mid)
        go = (done == 0) & (mid > lo) & (mid < hi)
        cnt = count_keys(lambda tile, start: tile >= mid)
        ge = cnt >= kf
        up = ge & go
        dn = jnp.logical_not(ge) & go
        lo = jnp.where(up, mid, lo)
        hi = jnp.where(dn, mid, hi)
        c_lo = jnp.where(up, cnt, c_lo)
        c_hi = jnp.where(dn, cnt, c_hi)
        adjacent = (lo == guess) & (hi == guess_above)
        done = jnp.where(go & (c_lo != kf) & jnp.logical_not(adjacent), 0, 1)
        return jnp.sum(1 - done), it + 1, lo, hi, c_lo, c_hi, done

    _, _, thr, _, c_lo, c_hi, _ = lax.while_loop(
        cond, step, (jnp.sum(1 - done0), jnp.int32(0), jnp.full((1, tq), SCORE_NEG, F32),
                     jnp.full((1, tq), -SCORE_NEG, F32), jnp.full((1, tq), 1.0, F32) * (nchunks * c),
                     jnp.zeros((1, tq), F32), done0))

    tied = (n_causal > topk) & (c_lo > kf)
    any_tied = jnp.sum(jnp.where(tied, 1, 0)) > 0

    @pl.when(any_tied)
    def _():
        n_take = kf - c_hi

        def bis(_, st):
            lo_x, hi_x = st
            mid = (lo_x + hi_x) >> 1
            ok = count_keys(lambda tile, start: (tile == thr) & ((krow + start) <= mid)) >= n_take
            return jnp.where(ok, lo_x, mid), jnp.where(ok, mid, hi_x)

        nbits = int(math.ceil(math.log2(sct_ref.shape[0]))) + 1
        _, hi_x = lax.fori_loop(0, nbits, bis, (jnp.full((1, tq), -1, I32),
                                                jnp.full((1, tq), sct_ref.shape[0], I32)))
        cut = jnp.where(tied, hi_x, jnp.int32(2 ** 30))

        def mask_body(j, _):
            start = pl.multiple_of(j * c, c)
            tile = sct_ref[pl.ds(start, c), :]
            sel = (tile > thr) | ((tile == thr) & ((krow + start) <= cut))
            mb_ref[:, pl.ds(start, c)] = jnp.transpose(jnp.where(sel, 0.0, MASK_NEG))
            return 0

        lax.fori_loop(0, nchunks, mask_body, 0)

    @pl.when(jnp.logical_not(any_tied))
    def _():
        def mask_body(j, _):
            start = pl.multiple_of(j * c, c)
            tile = sct_ref[pl.ds(start, c), :]
            mb_ref[:, pl.ds(start, c)] = jnp.transpose(jnp.where(tile >= thr, 0.0, MASK_NEG))
            return 0

        lax.fori_loop(0, nchunks, mask_body, 0)

    masked_heads(q_ref, qm_ref)
    masked_heads(qin_ref, qim_ref)
    w_t_next = jnp.transpose(win_ref[...])[0:IDX_HEADS, :]
    has_next = i + 1 < pl.num_programs(1)
    m_ref[...] = jnp.full(m_ref.shape, MASK_NEG, F32)
    acc_ref[...] = jnp.zeros(acc_ref.shape, F32)

    def attend(start, width, kind):
        nt = width // LANES
        low_half_w = lax.broadcasted_iota(I32, (width, LANES), 1) < HEAD_DIM
        for hp in range(N_HEADS // 2):
            kp = k_ref[pl.ds(start, width), hp * LANES:(hp + 1) * LANES]
            vp = v_ref[pl.ds(start, width), hp * LANES:(hp + 1) * LANES]
            for half in range(2):
                h = 2 * hp + half
                own = low_half_w if half == 0 else jnp.logical_not(low_half_w)
                v_aug = jnp.where(own, vp, jnp.ones_like(vp))
                s = lax.dot_general(qm_ref[h], kp, nt_dims, preferred_element_type=F32)
                s = s + mb_ref[:, pl.ds(start, width)]
                if kind is not None:
                    s = s + tb_ref[h, kind]
                m_old = m_ref[h]
                m_new = jnp.maximum(m_old, jnp.max(_fold_lanes(s, jnp.maximum), axis=1, keepdims=True))
                p = jnp.exp2(s - _tile_lanes(m_new, nt)).astype(BF16)
                m_ref[h] = m_new
                acc_ref[h] = (jnp.exp2(m_old - m_new) * acc_ref[h]
                              + jnp.dot(p, v_aug, preferred_element_type=F32))

    wide = 2 * c
    n_far = jnp.maximum(i - 1, 0)

    n_wide = n_far // 2

    def far_body(j, _):
        attend(pl.multiple_of(j * wide, wide), wide, None)
        score_chunk(2 * j, False, w_t_next)
        score_chunk(2 * j + 1, False, w_t_next)
        return 0

    lax.fori_loop(0, n_wide, far_body, 0)

    @pl.when(n_far % 2 == 1)
    def _():
        attend(pl.multiple_of((n_far - 1) * c, c), c, None)
        score_chunk(n_far - 1, False, w_t_next)

    @pl.when(i >= 1)
    def _():
        attend(pl.multiple_of((i - 1) * c, c), c, 1)
        score_chunk(i - 1, False, w_t_next)

    attend(pl.multiple_of(i * c, c), c, 0)
    score_chunk(i, False, w_t_next)

    @pl.when(has_next)
    def _():
        score_chunk(i + 1, True, w_t_next)

    for hp in range(N_HEADS // 2):
        lo, hi = acc_ref[2 * hp], acc_ref[2 * hp + 1]
        lo = lo / pltpu.roll(lo, HEAD_DIM, 1)
        hi = hi / pltpu.roll(hi, HEAD_DIM, 1)
        o_ref[:, hp * LANES:(hp + 1) * LANES] = jnp.where(low_half, lo, hi).astype(BF16)


def _attn_call(rel_bias, q, qi, wi, k, v, kid, tq):
    bsz, s, _ = q.shape
    nq = s // tq
    topk = min(TOPK_MAX, s // 4)
    assert tq % LANES == 0 and tq >= MAX_DISTANCE
    blk = lambda w: pl.BlockSpec((None, tq, w), lambda b, i: (b, i, 0))
    nxt = lambda w: pl.BlockSpec((None, tq, w), lambda b, i: (b, jnp.minimum(i + 1, nq - 1), 0))
    whole = lambda w: pl.BlockSpec((None, s, w), lambda b, i: (b, 0, 0), pipeline_mode=pl.Buffered(1))
    return pl.pallas_call(
        functools.partial(_attn_kernel, tq=tq, topk=topk),
        grid=(bsz, nq),
        in_specs=[pl.BlockSpec(memory_space=pltpu.SMEM),
                  blk(ATTN_WIDTH), blk(ATTN_WIDTH), blk(LANES), nxt(ATTN_WIDTH), nxt(LANES),
                  whole(ATTN_WIDTH), whole(ATTN_WIDTH), whole(LANES)],
        out_specs=blk(ATTN_WIDTH),
        out_shape=jax.ShapeDtypeStruct((bsz, s, ATTN_WIDTH), BF16),
        scratch_shapes=[
            pltpu.VMEM((s, tq), F32),
            pltpu.VMEM((32, s // 32, tq), I32),
            pltpu.VMEM((s // 32, tq), I32),
            pltpu.VMEM((tq, s), F32),
            pltpu.VMEM((N_HEADS, 2, tq, tq), F32),
            pltpu.VMEM((N_HEADS, tq, LANES), BF16),
            pltpu.VMEM((N_HEADS, tq, LANES), BF16),
            pltpu.VMEM((N_HEADS, tq, LANES), F32),
            pltpu.VMEM((N_HEADS, tq, LANES), F32),
        ],
        compiler_params=pltpu.CompilerParams(dimension_semantics=("arbitrary", "arbitrary"),
                                             vmem_limit_bytes=VMEM_LIMIT),
        name="attn",
    )(rel_bias, q, qi, wi, qi, wi, k, v, kid)


def _post_kernel(attn_ref, z_ref, gate_ref, x_ref, mod_ref, wa_ref, wc_ref, wo_ref, g2_ref,
                 wr_ref, br_ref, hres_ref, h2_ref, eid_ref, ew_ref, counts_ref, cnt_ref):
    ya = jnp.dot(attn_ref[...], wa_ref[...], preferred_element_type=F32)
    yc = jnp.dot(z_ref[...], wc_ref[...], preferred_element_type=F32)
    merged = gate_ref[:, 0:D_MODEL].astype(F32) * ya + gate_ref[:, D_MODEL:].astype(F32) * yc
    o = jnp.dot(merged.astype(BF16), wo_ref[...], preferred_element_type=F32)
    hres = x_ref[...] + mod_ref[2:3, :] * o
    hres_ref[...] = hres
    h2 = _rms_mod(hres, g2_ref[...], mod_ref[4:5, :], mod_ref[3:4, :])
    h2_ref[...] = h2.astype(BF16).reshape(h2_ref.shape)

    h2_hi = h2.astype(BF16)
    h2_lo = (h2 - h2_hi.astype(F32)).astype(BF16)
    lg = (jnp.dot(h2_hi, wr_ref[0], preferred_element_type=F32)
          + (jnp.dot(h2_lo, wr_ref[0], preferred_element_type=F32)
             + jnp.dot(h2_hi, wr_ref[1], preferred_element_type=F32))) + br_ref[...]
    tm = lg.shape[0]
    lgt = jnp.transpose(lg)
    ninf = -jnp.inf
    grow = lax.broadcasted_iota(I32, (8, tm), 0)
    gl = jnp.where(grow < N_GROUPS, lgt[N_EXPERTS:N_EXPERTS + 8, :], ninf)
    gmax = jnp.max(gl, axis=0, keepdims=True)
    g_sel = jnp.min(jnp.where(gl == gmax, grow, N_GROUPS), axis=0, keepdims=True)
    g_w = 1.0 / jnp.sum(jnp.exp(gl - gmax), axis=0, keepdims=True)

    erow = lax.broadcasted_iota(I32, (N_EXPERTS, tm), 0)
    emask = (erow // EXPERTS_PER_GROUP) == g_sel
    el = jnp.where(emask, lgt[0:N_EXPERTS, :], ninf)
    emax = jnp.max(el, axis=0, keepdims=True)
    ee = jnp.exp(el - emax)
    e_prob = ee / jnp.sum(ee, axis=0, keepdims=True)
    p1 = jnp.max(jnp.where(emask, e_prob, -1.0), axis=0, keepdims=True)
    i1 = jnp.min(jnp.where(emask & (e_prob == p1), erow, LANES), axis=0, keepdims=True)
    rest = emask & (erow != i1)
    p2 = jnp.max(jnp.where(rest, e_prob, -1.0), axis=0, keepdims=True)
    i2 = jnp.min(jnp.where(rest & (e_prob == p2), erow, LANES), axis=0, keepdims=True)
    psum = p1 + p2
    w1 = g_w * (p1 / psum)
    w2 = g_w * (p2 / psum)
    ew_ref[...] = jnp.transpose(jnp.concatenate([w1, w2, jnp.zeros((LANES - 2, tm), F32)], axis=0))

    @pl.when((pl.program_id(0) == 0) & (pl.program_id(1) == 0))
    def _():
        cnt_ref[...] = jnp.zeros(cnt_ref.shape, F32)

    onehot_t = jnp.where((erow == i1) | (erow == i2), 1.0, 0.0).astype(BF16)
    upper = (lax.broadcasted_iota(I32, (tm, tm), 0) <= lax.broadcasted_iota(I32, (tm, tm), 1))
    cum_t = jnp.dot(onehot_t, jnp.where(upper, 1.0, 0.0).astype(BF16), preferred_element_type=F32)
    cnt_col = jnp.transpose(cnt_ref[...])[0:N_EXPERTS, 0:1]
    before = cum_t - onehot_t.astype(F32) + cnt_col
    r1 = jnp.sum(jnp.where(erow == i1, before, 0.0), axis=0, keepdims=True).astype(I32)
    r2 = jnp.sum(jnp.where(erow == i2, before, 0.0), axis=0, keepdims=True).astype(I32)
    eid_ref[...] = jnp.concatenate([i1, i2, r1, r2, jnp.zeros((4, tm), I32)], axis=0)
    padded = jnp.concatenate([onehot_t, jnp.zeros((LANES - N_EXPERTS, tm), BF16)], axis=0)
    tile_total = lax.dot_general(jnp.ones((8, tm), BF16), padded, (((1,), (1,)), ((), ())),
                                 preferred_element_type=F32)
    total = cnt_ref[...] + tile_total
    cnt_ref[...] = total
    counts_ref[...] = total


def _post_call(attn, z, gates, x, mod, wa, wc, wo, g2, wr, br, tm):
    bsz, s, d = x.shape
    nt = s // tm
    tok = lambda w: pl.BlockSpec((None, tm, w), lambda b, t: (b, t, 0))
    full = lambda arr: pl.BlockSpec(arr.shape, lambda b, t: (0,) * arr.ndim)
    return pl.pallas_call(
        _post_kernel,
        grid=(bsz, nt),
        in_specs=[tok(ATTN_WIDTH), tok(CONV_WIDTH), tok(2 * d), tok(d),
                  pl.BlockSpec((None, 6, d), lambda b, t: (b, 0, 0)),
                  full(wa), full(wc), full(wo), full(g2), full(wr), full(br)],
        out_specs=[tok(d), pl.BlockSpec((None, tm) + ROW_TILE, lambda b, t: (b, t, 0, 0)),
                   pl.BlockSpec((8, tm), lambda b, t: (0, b * nt + t)), tok(LANES),
                   pl.BlockSpec((8, LANES), lambda b, t: (0, 0))],
        out_shape=[jax.ShapeDtypeStruct((bsz, s, d), F32),
                   jax.ShapeDtypeStruct((bsz, s) + ROW_TILE, BF16),
                   jax.ShapeDtypeStruct((8, bsz * s), I32), jax.ShapeDtypeStruct((bsz, s, LANES), F32),
                   jax.ShapeDtypeStruct((8, LANES), F32)],
        scratch_shapes=[pltpu.VMEM((8, LANES), F32)],
        compiler_params=pltpu.CompilerParams(dimension_semantics=("arbitrary", "arbitrary"),
                                             vmem_limit_bytes=VMEM_LIMIT),
        name="post",
    )(attn, z, gates, x, mod, wa, wc, wo, g2, wr, br)


ROW_TILE = (D_MODEL // LANES, LANES)
EXPERT_ROWS = 256
TOKENS_PER_STEP = 256


ISSUE_UNROLL = 8


def _dispatch_kernel(pos1_ref, pos2_ref, h2_ref, xs_in_ref, xs_ref, sem):
    del xs_in_ref
    tm = h2_ref.shape[0]
    base = pl.program_id(0) * tm

    def issue(r, c):
        pltpu.make_async_copy(h2_ref.at[r], xs_ref.at[pos1_ref[base + r]], sem).start(priority=0)
        pltpu.make_async_copy(h2_ref.at[r], xs_ref.at[pos2_ref[base + r]], sem).start(priority=1)
        return c

    lax.fori_loop(0, tm, issue, 0, unroll=ISSUE_UNROLL)
    for _ in range(2):
        pltpu.make_async_copy(h2_ref, xs_ref.at[pl.ds(0, tm)], sem).wait()


def _dispatch_call(pos1, pos2, h2_rows, n_rows):
    t = h2_rows.shape[0]
    tm = min(TOKENS_PER_STEP, t)
    xs0 = jnp.zeros((n_rows,) + ROW_TILE, BF16)
    return pl.pallas_call(
        _dispatch_kernel,
        grid_spec=pltpu.PrefetchScalarGridSpec(
            num_scalar_prefetch=2,
            grid=(t // tm,),
            in_specs=[pl.BlockSpec((tm,) + ROW_TILE, lambda i, p1, p2: (i, 0, 0)),
                      pl.BlockSpec(memory_space=pl.ANY)],
            out_specs=pl.BlockSpec(memory_space=pl.ANY),
            scratch_shapes=[pltpu.SemaphoreType.DMA(())]),
        out_shape=jax.ShapeDtypeStruct((n_rows,) + ROW_TILE, BF16),
        input_output_aliases={3: 0},
        compiler_params=pltpu.CompilerParams(dimension_semantics=("arbitrary",),
                                             vmem_limit_bytes=VMEM_LIMIT),
        name="dispatch",
    )(pos1, pos2, h2_rows, xs0)


def _expert_kernel(te_ref, nu_ref, xs_ref, wg_ref, wu_ref, wd_ref, ys_ref, wgb, wub, wdb):
    g = pl.program_id(0)
    e = te_ref[g]
    e_prev = te_ref[jnp.maximum(g - 1, 0)]

    @pl.when((g == 0) | (e != e_prev))
    def _():
        wgb[...] = wg_ref[...].astype(BF16)
        wub[...] = wu_ref[...].astype(BF16)
        wdb[...] = wd_ref[...].astype(BF16)

    @pl.when(g < nu_ref[0])
    def _():
        x = xs_ref[...].reshape(xs_ref.shape[0], D_MODEL)
        a = jnp.dot(x, wgb[...], preferred_element_type=F32)
        b = jnp.dot(x, wub[...], preferred_element_type=F32)
        hid = (a * jax.nn.sigmoid(a)) * b
        y = jnp.dot(hid.astype(BF16), wdb[...], preferred_element_type=F32)
        ys_ref[...] = y.astype(BF16).reshape(ys_ref.shape)

    @pl.when(g >= nu_ref[0])
    def _():
        ys_ref[...] = jnp.zeros(ys_ref.shape, BF16)


def _expert_call(tile_expert, n_used, xs, wg, wu, wd):
    n_rows, d = xs.shape[0], D_MODEL
    nt = n_rows // EXPERT_ROWS
    rows = pl.BlockSpec((EXPERT_ROWS,) + ROW_TILE, lambda g, te, nu: (g, 0, 0))
    wspec = lambda shp: pl.BlockSpec((None,) + shp, lambda g, te, nu: (te[g], 0, 0))
    return pl.pallas_call(
        _expert_kernel,
        grid_spec=pltpu.PrefetchScalarGridSpec(
            num_scalar_prefetch=2,
            grid=(nt,),
            in_specs=[rows, wspec((d, D_EXPERT)), wspec((d, D_EXPERT)), wspec((D_EXPERT, d))],
            out_specs=rows,
            scratch_shapes=[pltpu.VMEM((d, D_EXPERT), BF16), pltpu.VMEM((d, D_EXPERT), BF16),
                            pltpu.VMEM((D_EXPERT, d), BF16)]),
        out_shape=jax.ShapeDtypeStruct((n_rows,) + ROW_TILE, BF16),
        compiler_params=pltpu.CompilerParams(dimension_semantics=("arbitrary",),
                                             vmem_limit_bytes=VMEM_LIMIT),
        name="experts",
    )(tile_expert, n_used, xs, wg, wu, wd)


def _combine_kernel(pos1_ref, pos2_ref, ys_ref, ew_ref, hres_ref, g2_ref, gf_ref, o_ref, buf, sem):
    tm = hres_ref.shape[0]
    t = pl.program_id(0)
    nt = pl.num_programs(0)

    def issue(tile, slot):
        base = tile * tm

        def body(r, c):
            pltpu.make_async_copy(ys_ref.at[pos1_ref[base + r]], buf.at[slot, r],
                                  sem.at[slot]).start(priority=0)
            pltpu.make_async_copy(ys_ref.at[pos2_ref[base + r]], buf.at[slot, tm + r],
                                  sem.at[slot]).start(priority=1)
            return c

        lax.fori_loop(0, tm, body, 0, unroll=ISSUE_UNROLL)

    @pl.when(t == 0)
    def _():
        issue(0, 0)

    @pl.when(t + 1 < nt)
    def _():
        issue(t + 1, (t + 1) % 2)

    slot = t % 2
    pltpu.make_async_copy(ys_ref.at[pl.ds(0, 2 * tm)], buf.at[slot], sem.at[slot]).wait()

    y1 = buf[slot, 0:tm].astype(F32).reshape(tm, D_MODEL)
    y2 = buf[slot, tm:2 * tm].astype(F32).reshape(tm, D_MODEL)
    w = ew_ref[...]
    y = w[:, 0:1] * y1 + w[:, 1:2] * y2
    hfin = hres_ref[...] + g2_ref[...] * y
    ms = jnp.mean(hfin * hfin, axis=-1, keepdims=True)
    o_ref[...] = hfin * lax.rsqrt(ms + EPS) * gf_ref[...]


def _combine_call(pos1, pos2, ys_rows, ew, hres, gate2, gf):
    t, d = hres.shape
    tm = min(TOKENS_PER_STEP, t)
    per_batch = t // gate2.shape[0]
    assert per_batch % tm == 0
    tok = lambda w: pl.BlockSpec((tm, w), lambda i, *_: (i, 0))
    return pl.pallas_call(
        _combine_kernel,
        grid_spec=pltpu.PrefetchScalarGridSpec(
            num_scalar_prefetch=2,
            grid=(t // tm,),
            in_specs=[pl.BlockSpec(memory_space=pl.ANY), tok(LANES), tok(d),
                      pl.BlockSpec((None, 1, d), lambda i, *_: ((i * tm) // per_batch, 0, 0)),
                      pl.BlockSpec((1, d), lambda i, *_: (0, 0))],
            out_specs=tok(d),
            scratch_shapes=[pltpu.VMEM((2, 2 * tm) + ROW_TILE, BF16), pltpu.SemaphoreType.DMA((2,))]),
        out_shape=jax.ShapeDtypeStruct((t, d), F32),
        compiler_params=pltpu.CompilerParams(dimension_semantics=("arbitrary",),
                                             vmem_limit_bytes=VMEM_LIMIT),
        name="combine",
    )(pos1, pos2, ys_rows, ew, hres, gate2, gf)


def _slot_kernel(seg_ref, eid_ref, pos_ref):
    e = eid_ref[0:2, :]
    pos = eid_ref[2:4, :]
    for k in range(seg_ref.shape[0]):
        pos = pos + jnp.where(e == k, seg_ref[k], 0)
    pos_ref[...] = jnp.concatenate([pos, jnp.zeros((6, pos.shape[1]), I32)], axis=0)


def _slot_call(seg_start, eid):
    return pl.pallas_call(
        _slot_kernel,
        in_specs=[pl.BlockSpec(memory_space=pltpu.SMEM), pl.BlockSpec(eid.shape, lambda: (0, 0))],
        out_specs=pl.BlockSpec(eid.shape, lambda: (0, 0)),
        out_shape=jax.ShapeDtypeStruct(eid.shape, I32),
        name="slots",
    )(seg_start, eid)


def _sparse_moe(h2, eid, ew, counts, wg, wu, wd, hres, gate2, gf):
    bsz, s, d = hres.shape
    t = bsz * s
    ne = wg.shape[0]
    cnt = counts[0, :ne].astype(I32)
    seg_rows = ((cnt + EXPERT_ROWS - 1) // EXPERT_ROWS) * EXPERT_ROWS
    seg_end = jnp.cumsum(seg_rows)
    seg_start = seg_end - seg_rows
    pos = _slot_call(seg_start, eid)
    pos1, pos2 = pos[0], pos[1]
    n_tiles = (2 * t) // EXPERT_ROWS + ne
    tile_first_row = jnp.arange(n_tiles, dtype=I32) * EXPERT_ROWS
    tile_expert = jnp.minimum(jnp.sum(seg_end[None, :] <= tile_first_row[:, None], axis=1), ne - 1)
    n_used = (seg_end[ne - 1] // EXPERT_ROWS).reshape(1)
    n_rows = n_tiles * EXPERT_ROWS

    xs = _dispatch_call(pos1, pos2, h2.reshape((t,) + ROW_TILE), n_rows)
    ys = _expert_call(tile_expert.astype(I32), n_used.astype(I32), xs, wg, wu, wd)
    out = _combine_call(pos1, pos2, ys, ew.reshape(t, LANES), hres.reshape(t, d),
                        gate2.reshape(bsz, 1, d), gf.reshape(1, d))
    return out.reshape(bsz, s, d)


def kernel(x, c, w_ada, b_ada, norm1_g, w_in, rel_bias, conv_w, w_attn_branch, w_conv_branch, w_out,
           norm2_g, w_router_group, b_router_group, w_router_expert, b_router_expert,
           w_gate_e, w_up_e, w_down_e, norm_f_g):
    bsz, s, d = x.shape
    depth = w_ada.shape[0]
    h = x.astype(F32)
    c_pad = jnp.zeros((8, d), F32).at[:bsz].set(c.astype(F32))
    out = h
    for l in range(depth):
        mod = _ada_call(c_pad, w_ada[l].astype(F32), b_ada[l].astype(F32)[None, :])
        mod = mod[:bsz].reshape(bsz, 6, d)

        w = w_in[l]
        o_q, o_k, o_v, o_qi = 0, 512, 1024, 1536
        o_ki, o_wi, o_cb, o_gl = 2048, 2112, 2120, 3656
        wa = w[:, o_q:o_ki].astype(BF16)
        ki_w = w[:, o_ki:o_wi]
        wb = jnp.concatenate([ki_w, ki_w, w[:, o_wi:o_cb],
                              jnp.zeros((d, LANES - IDX_HEADS), w.dtype)], axis=1).astype(BF16)
        wc = w[:, o_cb:o_gl].astype(BF16)
        wd = w[:, o_gl:].astype(BF16)

        q, k, v, qi, kid, wi, z, gates = _proj_call(
            h, mod, norm1_g[l].astype(F32)[None, :], wa, wb, wc, wd, conv_w[l].astype(F32),
            tm=min(512, s))

        attn = _attn_call(rel_bias.astype(F32), q, qi, wi, k, v, kid, tq=min(TOPK_MAX, s // 4))

        wr32 = jnp.concatenate([w_router_expert[l], w_router_group[l],
                                jnp.zeros((d, LANES - N_EXPERTS - N_GROUPS), F32)], axis=1).astype(F32)
        wr_hi = wr32.astype(BF16)
        wr = jnp.stack([wr_hi, (wr32 - wr_hi.astype(F32)).astype(BF16)])
        br = jnp.concatenate([b_router_expert[l], b_router_group[l],
                              jnp.zeros((LANES - N_EXPERTS - N_GROUPS,), F32)])[None, :].astype(F32)
        hres, h2, eid, ew, counts = _post_call(
            attn, z, gates, h, mod, w_attn_branch[l].astype(BF16), w_conv_branch[l].astype(BF16),
            w_out[l].astype(BF16), norm2_g[l].astype(F32)[None, :], wr, br, tm=min(512, s))

        assert depth == 1, "the final RMSNorm is fused into the combine kernel of the only layer"
        out = _sparse_moe(h2, eid, ew, counts, w_gate_e[l].astype(F32), w_up_e[l].astype(F32),
                          w_down_e[l].astype(F32), hres, mod[:, 5, :], norm_f_g.astype(F32))
        h = out
    return out.astype(x.dtype)
```

```python
import functools
import math

import numpy as np
import jax
import jax.numpy as jnp
from jax import lax
from jax.experimental import pallas as pl
from jax.experimental.pallas import tpu as pltpu

F32 = jnp.float32
BF16 = jnp.bfloat16
I32 = jnp.int32

D_MODEL = 1024
N_HEADS = 8
HEAD_DIM = 64
ATTN_WIDTH = N_HEADS * HEAD_DIM
IDX_HEADS = 8
IDX_DIM = 64
TOPK_MAX = 256
N_BUCKETS = 32
MAX_DISTANCE = 128
CONV_WIDTH = 512
CONV_K = 3
N_GROUPS = 4
EXPERTS_PER_GROUP = 8
N_EXPERTS = N_GROUPS * EXPERTS_PER_GROUP
D_EXPERT = 512
EPS = 1e-6

LANES = 128
VMEM_LIMIT = 56 * 1024 * 1024

LOG2E = math.log2(math.e)
MASK_NEG = -1e30
SCORE_NEG = float(np.finfo(np.float32).min)
TINY_F32 = float(np.finfo(np.float32).tiny)
KEY_SPACE_AFTER = 40
RADIX_BITS_PER_STEP = 4


def _t5_bucket_starts():
    max_exact = N_BUCKETS // 2
    n = np.arange(0, MAX_DISTANCE + 1)
    nf = np.maximum(n, 1).astype(np.float32)
    val = (np.log(nf / np.float32(max_exact)) / np.float32(math.log(MAX_DISTANCE / max_exact))
           * np.float32(N_BUCKETS - max_exact)).astype(np.float32)
    inner = val[max_exact + 1:MAX_DISTANCE]
    assert np.min(np.abs(inner - np.round(inner))) > 1e-3
    large = np.minimum(max_exact + val.astype(np.int32), N_BUCKETS - 1)
    bucket = np.where(n < max_exact, n, large)
    assert bucket[MAX_DISTANCE] == N_BUCKETS - 1 and np.all(np.diff(bucket) >= 0)
    return [int(np.argmax(bucket >= b)) for b in range(N_BUCKETS)]


_BUCKET_START = _t5_bucket_starts()


def _ada_kernel(c_ref, w_ref, b_ref, o_ref):
    c = c_ref[...]
    ca = c * jax.nn.sigmoid(c)
    o_ref[...] = jnp.dot(ca, w_ref[...], preferred_element_type=F32,
                         precision=lax.Precision.HIGHEST) + b_ref[...]


def _ada_call(c_pad, w, b):
    rows, d = c_pad.shape
    n = w.shape[1]
    tn = 1536
    return pl.pallas_call(
        _ada_kernel,
        grid=(n // tn,),
        in_specs=[pl.BlockSpec((rows, d), lambda j: (0, 0)),
                  pl.BlockSpec((d, tn), lambda j: (0, j)),
                  pl.BlockSpec((1, tn), lambda j: (0, j))],
        out_specs=pl.BlockSpec((rows, tn), lambda j: (0, j)),
        out_shape=jax.ShapeDtypeStruct((rows, n), F32),
        compiler_params=pltpu.CompilerParams(dimension_semantics=("arbitrary",),
                                             vmem_limit_bytes=VMEM_LIMIT),
        name="ada",
    )(c_pad, w, b)


def _rms_mod(x, g, scale, shift):
    ms = jnp.mean(x * x, axis=-1, keepdims=True)
    y = x * lax.rsqrt(ms + EPS) * g
    return y * (1.0 + scale) + shift


def _proj_kernel(x_ref, mod_ref, g_ref, wa_ref, wb_ref, wc_ref, wd_ref, cw_ref,
                 q_ref, k_ref, v_ref, qi_ref, kid_ref, wi_ref, z_ref, gate_ref, ubuf, *, tm):
    h = _rms_mod(x_ref[...], g_ref[...], mod_ref[1:2, :], mod_ref[0:1, :])
    hb = h.astype(BF16)

    a = jnp.dot(hb, wa_ref[...], preferred_element_type=F32)
    q_ref[...] = (a[:, 0:512] * (HEAD_DIM ** -0.5 * LOG2E)).astype(BF16)
    k_ref[...] = a[:, 512:1024].astype(BF16)
    v_ref[...] = a[:, 1024:1536].astype(BF16)
    qi_ref[...] = a[:, 1536:2048].astype(BF16)

    b = jnp.dot(hb, wb_ref[...], preferred_element_type=F32)
    kid_ref[...] = b[:, 0:LANES].astype(BF16)
    wi_ref[...] = b[:, LANES:2 * LANES] * ((IDX_DIM ** -0.5) * (IDX_HEADS ** -0.5))

    cpart = jnp.dot(hb, wc_ref[...], preferred_element_type=F32)
    cb = cpart[:, 0:512]
    u = cpart[:, 512:1024] * cpart[:, 1024:1536]

    @pl.when(pl.program_id(1) == 0)
    def _():
        ubuf[0:8, :] = jnp.zeros((8, CONV_WIDTH), F32)

    ubuf[8:tm + 8, :] = u
    conv = (cw_ref[0:1, :] * ubuf[6:tm + 6, :] + cw_ref[1:2, :] * ubuf[7:tm + 7, :]
            + cw_ref[2:3, :] * u)
    z_ref[...] = (cb * conv).astype(BF16)
    ubuf[0:8, :] = ubuf[tm:tm + 8, :]

    d = jnp.dot(hb, wd_ref[...], preferred_element_type=F32)
    gate_ref[...] = jax.nn.sigmoid(d).astype(BF16)


def _proj_call(x, mod, g1, wa, wb, wc, wd, cw, tm):
    bsz, s, d = x.shape
    nt = s // tm
    tok = lambda w: pl.BlockSpec((None, tm, w), lambda b, t: (b, t, 0))
    full = lambda arr: pl.BlockSpec(arr.shape, lambda b, t: (0,) * arr.ndim)
    out_widths = [(512, BF16), (512, BF16), (512, BF16), (512, BF16), (LANES, BF16), (LANES, F32),
                  (512, BF16), (2 * D_MODEL, BF16)]
    return pl.pallas_call(
        functools.partial(_proj_kernel, tm=tm),
        grid=(bsz, nt),
        in_specs=[tok(d), pl.BlockSpec((None, 6, d), lambda b, t: (b, 0, 0)), full(g1),
                  full(wa), full(wb), full(wc), full(wd), full(cw)],
        out_specs=[tok(w) for w, _ in out_widths],
        out_shape=[jax.ShapeDtypeStruct((bsz, s, w), dt) for w, dt in out_widths],
        scratch_shapes=[pltpu.VMEM((tm + 8, CONV_WIDTH), F32)],
        compiler_params=pltpu.CompilerParams(dimension_semantics=("arbitrary", "arbitrary"),
                                             vmem_limit_bytes=VMEM_LIMIT),
        name="proj",
    )(x, mod, g1, wa, wb, wc, wd, cw)


def _to_key(f):
    b = pltpu.bitcast(f, I32)
    return jnp.where(b < 0, b ^ 0x7FFFFFFF, b)


def _from_key(k):
    return pltpu.bitcast(jnp.where(k < 0, k ^ 0x7FFFFFFF, k), F32)


def _bit_transpose32(words):
    a = list(words)
    j, m = 16, 0x0000FFFF
    while j:
        k = 0
        while k < 32:
            t = (a[k] ^ lax.shift_right_logical(a[k + j], jnp.int32(j))) & jnp.int32(m)
            a[k] = a[k] ^ t
            a[k + j] = a[k + j] ^ lax.shift_left(t, jnp.int32(j))
            k = (k + j + 1) & ~j
        j >>= 1
        m = (m ^ (m << j)) & 0xFFFFFFFF
    return a


def _fold_lanes(x, op):
    parts = [x[:, t * LANES:(t + 1) * LANES] for t in range(x.shape[1] // LANES)]
    while len(parts) > 1:
        nxt = [op(parts[a], parts[a + 1]) for a in range(0, len(parts) - 1, 2)]
        parts = nxt + ([parts[-1]] if len(parts) % 2 else [])
    return parts[0]


def _tile_lanes(x, n):
    return x if n == 1 else jnp.concatenate([x] * n, axis=1)


def _attn_kernel(rb_ref, q_ref, qi_ref, wi_ref, qin_ref, win_ref, k_ref, v_ref, kid_ref, o_ref,
                 sct_ref, planes_ref, cand_ref, mb_ref, tb_ref, qm_ref, qim_ref, m_ref, l_ref, acc_ref,
                 *, tq, topk):
    c = tq
    i = pl.program_id(1)
    nt_dims = (((1,), (1,)), ((), ()))
    lane = lax.broadcasted_iota(I32, (tq, LANES), 1)
    low_half = lane < HEAD_DIM
    row = lax.broadcasted_iota(I32, (tq, c), 0)
    col = lax.broadcasted_iota(I32, (tq, c), 1)

    @pl.when((pl.program_id(0) == 0) & (i == 0))
    def _():
        for p in range(32):
            planes_ref[p] = jnp.zeros(planes_ref.shape[1:], I32)
        for kind in range(2):
            rel = row - col + kind * c
            n = jnp.maximum(rel, 0)
            for h in range(N_HEADS):
                t = jnp.full((tq, c), rb_ref[0, h], F32)
                for b in range(1, N_BUCKETS):
                    t = jnp.where(n >= _BUCKET_START[b], rb_ref[b, h], t)
                t = (t - rb_ref[N_BUCKETS - 1, h]) * LOG2E
                tb_ref[h, kind] = jnp.where(rel >= 0, t, MASK_NEG)


    def masked_heads(src_ref, dst_ref):
        for h in range(N_HEADS):
            hp, half = divmod(h, 2)
            keep = low_half if half == 0 else jnp.logical_not(low_half)
            pair = src_ref[:, hp * LANES:(hp + 1) * LANES]
            dst_ref[h] = jnp.where(keep, pair, jnp.zeros_like(pair))

    krow = lax.broadcasted_iota(I32, (c, tq), 0)
    qcol = lax.broadcasted_iota(I32, (c, tq), 1)
    int_min = jnp.int32(-2 ** 31)

    def fold_rows(x, op):
        parts = [x[r * 8:(r + 1) * 8, :] for r in range(x.shape[0] // 8)]
        while len(parts) > 1:
            nxt = [op(parts[a], parts[a + 1]) for a in range(0, len(parts) - 1, 2)]
            parts = nxt + ([parts[-1]] if len(parts) % 2 else [])
        return parts[0]

    def score_chunk(j, diag, w_t):
        start = pl.multiple_of(j * c, c)
        kc = kid_ref[pl.ds(start, c), :]
        acc = jnp.zeros((c, tq), F32)
        for h in range(N_HEADS):
            d = lax.dot_general(kc, qim_ref[h], nt_dims, preferred_element_type=F32)
            acc = acc + jnp.maximum(d, 0.0) * w_t[h:h + 1, :]
        ukey = _to_key(acc) ^ int_min
        if diag:
            causal = krow <= qcol
            acc = jnp.where(causal, acc, SCORE_NEG)
            ukey = jnp.where(causal, ukey, 0)
        sct_ref[pl.ds(start, c), :] = acc
        prow = pl.multiple_of(j * (c // 32), c // 32)
        for lt in range(tq // LANES):
            words = [ukey[k * 8:(k + 1) * 8, lt * LANES:(lt + 1) * LANES] for k in range(32)]
            for p, plane in enumerate(_bit_transpose32(words)):
                planes_ref[p, pl.ds(prow, c // 32), lt * LANES:(lt + 1) * LANES] = plane

    @pl.when(i == 0)
    def _():
        masked_heads(qi_ref, qim_ref)
        score_chunk(0, True, jnp.transpose(wi_ref[...])[0:IDX_HEADS, :])

    nchunks = i + 1
    n_causal = i * tq + lax.broadcasted_iota(I32, (1, tq), 1) + 1
    prow_iota = lax.broadcasted_iota(I32, cand_ref.shape, 0)
    cand_ref[...] = jnp.where(prow_iota < nchunks * (c // 32), -1, 0)

    def radix_cond(st):
        return (st[0] < 32) & (st[1] > 0)

    def radix_step(st):
        p0, _, prefix, need, ncand, settled = st
        cand = cand_ref[...]
        for sub in range(RADIX_BITS_PER_STEP):
            p = p0 + sub
            plane = planes_ref[p]
            ones = cand & plane
            cnt1 = jnp.sum(fold_rows(lax.population_count(ones), jnp.add), axis=0, keepdims=True)
            take1 = cnt1 >= need
            live = settled == 0
            need = jnp.where(live & jnp.logical_not(take1), need - cnt1, need)
            ncand = jnp.where(live, jnp.where(take1, cnt1, ncand - cnt1), ncand)
            prefix = jnp.where(live & take1, prefix | jnp.left_shift(jnp.int32(1), 31 - p), prefix)
            cand = cand & (plane ^ jnp.where(take1, 0, -1))
            settled = jnp.where(ncand == need, 1, settled)
        cand_ref[...] = cand
        return p0 + RADIX_BITS_PER_STEP, jnp.sum(1 - settled), prefix, need, ncand, settled

    done0 = jnp.where(n_causal <= topk, 1, 0)
    _, _, prefix, _, _, _ = lax.while_loop(
        radix_cond, radix_step,
        (jnp.int32(0), jnp.sum(1 - done0), jnp.zeros((1, tq), I32), jnp.full((1, tq), topk, I32),
         jnp.full((1, tq), 1, I32) * (nchunks * c), done0))
    guess_key = prefix ^ int_min
    guess = _from_key(guess_key)

    def count_keys(pred):
        def body(j, part):
            start = pl.multiple_of(j * c, c)
            hit = pred(sct_ref[pl.ds(start, c), :], start)
            return part + fold_rows(jnp.where(hit, 1.0, 0.0), jnp.add)
        part = lax.fori_loop(0, nchunks, body, jnp.zeros((8, tq), F32))
        return jnp.sum(part, axis=0, keepdims=True)

    kf = float(topk)
    guess_above = _from_key(guess_key + 1)
    guess_above = jnp.where(jnp.abs(guess_above) < TINY_F32,
                            jnp.where(guess >= 0.0, TINY_F32, 0.0), guess_above)
    guess_below = _from_key(guess_key - 2)

    def cond(st):
        return st[0] > 0

    def step(st):
        _, it, lo, hi, c_lo, c_hi, done = st
        mid = 0.5 * lo + 0.5 * hi
        lo_k, hi_k = _to_key(lo), _to_key(hi)
        mid_key = _from_key((lo_k >> 1) + (hi_k >> 1) + (lo_k & hi_k & 1))
        mid = jnp.where(it >= KEY_SPACE_AFTER, mid_key, mid)
        probe = jnp.where(it == 0, guess, jnp.where(lo == guess, guess_above, guess_below))
        mid = jnp.where((it <= 1) & (probe > lo) & (probe < hi), probe, mid)
        go = (done == 0) & (mid > lo) & (mid < hi)
        cnt = count_keys(lambda tile, start: tile >= mid)
        ge = cnt >= kf
        up = ge & go
        dn = jnp.logical_not(ge) & go
        lo = jnp.where(up, mid, lo)
        hi = jnp.where(dn, mid, hi)
        c_lo = jnp.where(up, cnt, c_lo)
        c_hi = jnp.where(dn, cnt, c_hi)
        adjacent = (lo == guess) & (hi == guess_above)
        done = jnp.where(go & (c_lo != kf) & jnp.logical_not(adjacent), 0, 1)
        return jnp.sum(1 - done), it + 1, lo, hi, c_lo, c_hi, done

    _, _, thr, _, c_lo, c_hi, _ = lax.while_loop(
        cond, step, (jnp.sum(1 - done0), jnp.int32(0), jnp.full((1, tq), SCORE_NEG, F32),
                     jnp.full((1, tq), -SCORE_NEG, F32), jnp.full((1, tq), 1.0, F32) * (nchunks * c),
                     jnp.zeros((1, tq), F32), done0))

    tied = (n_causal > topk) & (c_lo > kf)
    any_tied = jnp.sum(jnp.where(tied, 1, 0)) > 0

    @pl.when(any_tied)
    def _():
        n_take = kf - c_hi

        def bis(_, st):
            lo_x, hi_x = st
            mid = (lo_x + hi_x) >> 1
            ok = count_keys(lambda tile, start: (tile == thr) & ((krow + start) <= mid)) >= n_take
            return jnp.where(ok, lo_x, mid), jnp.where(ok, mid, hi_x)

        nbits = int(math.ceil(math.log2(sct_ref.shape[0]))) + 1
        _, hi_x = lax.fori_loop(0, nbits, bis, (jnp.full((1, tq), -1, I32),
                                                jnp.full((1, tq), sct_ref.shape[0], I32)))
        cut = jnp.where(tied, hi_x, jnp.int32(2 ** 30))

        def mask_body(j, _):
            start = pl.multiple_of(j * c, c)
            tile = sct_ref[pl.ds(start, c), :]
            sel = (tile > thr) | ((tile == thr) & ((krow + start) <= cut))
            mb_ref[:, pl.ds(start, c)] = jnp.transpose(jnp.where(sel, 0.0, MASK_NEG))
            return 0

        lax.fori_loop(0, nchunks, mask_body, 0)

    @pl.when(jnp.logical_not(any_tied))
    def _():
        def mask_body(j, _):
            start = pl.multiple_of(j * c, c)
            tile = sct_ref[pl.ds(start, c), :]
            mb_ref[:, pl.ds(start, c)] = jnp.transpose(jnp.where(tile >= thr, 0.0, MASK_NEG))
            return 0

        lax.fori_loop(0, nchunks, mask_body, 0)

    masked_heads(q_ref, qm_ref)
    masked_heads(qin_ref, qim_ref)
    w_t_next = jnp.transpose(win_ref[...])[0:IDX_HEADS, :]
    has_next = i + 1 < pl.num_programs(1)
    m_ref[...] = jnp.full(m_ref.shape, MASK_NEG, F32)
    l_ref[...] = jnp.zeros(l_ref.shape, F32)
    acc_ref[...] = jnp.zeros(acc_ref.shape, F32)

    def attend(start, width, kind):
        nt = width // LANES
        ones = jnp.ones((width, LANES), BF16)
        for h in range(N_HEADS):
            hp = h // 2
            kp = k_ref[pl.ds(start, width), hp * LANES:(hp + 1) * LANES]
            vp = v_ref[pl.ds(start, width), hp * LANES:(hp + 1) * LANES]
            s = lax.dot_general(qm_ref[h], kp, nt_dims, preferred_element_type=F32)
            s = s + mb_ref[:, pl.ds(start, width)]
            if kind is not None:
                s = s + tb_ref[h, kind]
            m_old = m_ref[h]
            m_new = jnp.maximum(m_old, jnp.max(_fold_lanes(s, jnp.maximum), axis=1, keepdims=True))
            p = jnp.exp2(s - _tile_lanes(m_new, nt)).astype(BF16)
            pv = jnp.dot(p, jnp.concatenate([vp, ones], axis=1), preferred_element_type=F32)
            alpha = jnp.exp2(m_old - m_new)
            m_ref[h] = m_new
            l_ref[h] = alpha * l_ref[h] + pv[:, LANES:]
            acc_ref[h] = alpha * acc_ref[h] + pv[:, :LANES]

    wide = 2 * c
    n_far = jnp.maximum(i - 1, 0)

    n_wide = n_far // 2

    def far_body(j, _):
        score_chunk(2 * j, False, w_t_next)
        attend(pl.multiple_of(j * wide, wide), wide, None)
        score_chunk(2 * j + 1, False, w_t_next)
        return 0

    lax.fori_loop(0, n_wide, far_body, 0)

    @pl.when(n_far % 2 == 1)
    def _():
        attend(pl.multiple_of((n_far - 1) * c, c), c, None)
        score_chunk(n_far - 1, False, w_t_next)

    @pl.when(i >= 1)
    def _():
        attend(pl.multiple_of((i - 1) * c, c), c, 1)
        score_chunk(i - 1, False, w_t_next)

    attend(pl.multiple_of(i * c, c), c, 0)
    score_chunk(i, False, w_t_next)

    @pl.when(has_next)
    def _():
        score_chunk(i + 1, True, w_t_next)

    for hp in range(N_HEADS // 2):
        lo = acc_ref[2 * hp] / l_ref[2 * hp]
        hi = acc_ref[2 * hp + 1] / l_ref[2 * hp + 1]
        o_ref[:, hp * LANES:(hp + 1) * LANES] = jnp.where(low_half, lo, hi).astype(BF16)


def _attn_call(rel_bias, q, qi, wi, k, v, kid, tq):
    bsz, s, _ = q.shape
    nq = s // tq
    topk = min(TOPK_MAX, s // 4)
    assert tq % LANES == 0 and tq >= MAX_DISTANCE
    blk = lambda w: pl.BlockSpec((None, tq, w), lambda b, i: (b, i, 0))
    nxt = lambda w: pl.BlockSpec((None, tq, w), lambda b, i: (b, jnp.minimum(i + 1, nq - 1), 0))
    whole = lambda w: pl.BlockSpec((None, s, w), lambda b, i: (b, 0, 0), pipeline_mode=pl.Buffered(1))
    return pl.pallas_call(
        functools.partial(_attn_kernel, tq=tq, topk=topk),
        grid=(bsz, nq),
        in_specs=[pl.BlockSpec(memory_space=pltpu.SMEM),
                  blk(ATTN_WIDTH), blk(ATTN_WIDTH), blk(LANES), nxt(ATTN_WIDTH), nxt(LANES),
                  whole(ATTN_WIDTH), whole(ATTN_WIDTH), whole(LANES)],
        out_specs=blk(ATTN_WIDTH),
        out_shape=jax.ShapeDtypeStruct((bsz, s, ATTN_WIDTH), BF16),
        scratch_shapes=[
            pltpu.VMEM((s, tq), F32),
            pltpu.VMEM((32, s // 32, tq), I32),
            pltpu.VMEM((s // 32, tq), I32),
            pltpu.VMEM((tq, s), F32),
            pltpu.VMEM((N_HEADS, 2, tq, tq), F32),
            pltpu.VMEM((N_HEADS, tq, LANES), BF16),
            pltpu.VMEM((N_HEADS, tq, LANES), BF16),
            pltpu.VMEM((N_HEADS, tq, LANES), F32),
            pltpu.VMEM((N_HEADS, tq, LANES), F32),
            pltpu.VMEM((N_HEADS, tq, LANES), F32),
        ],
        compiler_params=pltpu.CompilerParams(dimension_semantics=("arbitrary", "arbitrary"),
                                             vmem_limit_bytes=VMEM_LIMIT),
        name="attn",
    )(rel_bias, q, qi, wi, qi, wi, k, v, kid)


def _post_kernel(attn_ref, z_ref, gate_ref, x_ref, mod_ref, wa_ref, wc_ref, wo_ref, g2_ref,
                 wr_ref, br_ref, hres_ref, h2_ref, eid_ref, ew_ref, counts_ref, cnt_ref):
    ya = jnp.dot(attn_ref[...], wa_ref[...], preferred_element_type=F32)
    yc = jnp.dot(z_ref[...], wc_ref[...], preferred_element_type=F32)
    merged = gate_ref[:, 0:D_MODEL].astype(F32) * ya + gate_ref[:, D_MODEL:].astype(F32) * yc
    o = jnp.dot(merged.astype(BF16), wo_ref[...], preferred_element_type=F32)
    hres = x_ref[...] + mod_ref[2:3, :] * o
    hres_ref[...] = hres
    h2 = _rms_mod(hres, g2_ref[...], mod_ref[4:5, :], mod_ref[3:4, :])
    h2_ref[...] = h2.astype(BF16).reshape(h2_ref.shape)

    h2_hi = h2.astype(BF16)
    h2_lo = (h2 - h2_hi.astype(F32)).astype(BF16)
    lg = (jnp.dot(h2_hi, wr_ref[0], preferred_element_type=F32)
          + (jnp.dot(h2_lo, wr_ref[0], preferred_element_type=F32)
             + jnp.dot(h2_hi, wr_ref[1], preferred_element_type=F32))) + br_ref[...]
    tm = lg.shape[0]
    lgt = jnp.transpose(lg)
    ninf = -jnp.inf
    grow = lax.broadcasted_iota(I32, (8, tm), 0)
    gl = jnp.where(grow < N_GROUPS, lgt[N_EXPERTS:N_EXPERTS + 8, :], ninf)
    gmax = jnp.max(gl, axis=0, keepdims=True)
    g_sel = jnp.min(jnp.where(gl == gmax, grow, N_GROUPS), axis=0, keepdims=True)
    g_w = 1.0 / jnp.sum(jnp.exp(gl - gmax), axis=0, keepdims=True)

    erow = lax.broadcasted_iota(I32, (N_EXPERTS, tm), 0)
    emask = (erow // EXPERTS_PER_GROUP) == g_sel
    el = jnp.where(emask, lgt[0:N_EXPERTS, :], ninf)
    emax = jnp.max(el, axis=0, keepdims=True)
    ee = jnp.exp(el - emax)
    e_prob = ee / jnp.sum(ee, axis=0, keepdims=True)
    p1 = jnp.max(jnp.where(emask, e_prob, -1.0), axis=0, keepdims=True)
    i1 = jnp.min(jnp.where(emask & (e_prob == p1), erow, LANES), axis=0, keepdims=True)
    rest = emask & (erow != i1)
    p2 = jnp.max(jnp.where(rest, e_prob, -1.0), axis=0, keepdims=True)
    i2 = jnp.min(jnp.where(rest & (e_prob == p2), erow, LANES), axis=0, keepdims=True)
    psum = p1 + p2
    w1 = g_w * (p1 / psum)
    w2 = g_w * (p2 / psum)
    ew_ref[...] = jnp.transpose(jnp.concatenate([w1, w2, jnp.zeros((LANES - 2, tm), F32)], axis=0))

    @pl.when((pl.program_id(0) == 0) & (pl.program_id(1) == 0))
    def _():
        cnt_ref[...] = jnp.zeros(cnt_ref.shape, F32)

    onehot_t = jnp.where((erow == i1) | (erow == i2), 1.0, 0.0).astype(BF16)
    upper = (lax.broadcasted_iota(I32, (tm, tm), 0) <= lax.broadcasted_iota(I32, (tm, tm), 1))
    cum_t = jnp.dot(onehot_t, jnp.where(upper, 1.0, 0.0).astype(BF16), preferred_element_type=F32)
    cnt_col = jnp.transpose(cnt_ref[...])[0:N_EXPERTS, 0:1]
    before = cum_t - onehot_t.astype(F32) + cnt_col
    r1 = jnp.sum(jnp.where(erow == i1, before, 0.0), axis=0, keepdims=True).astype(I32)
    r2 = jnp.sum(jnp.where(erow == i2, before, 0.0), axis=0, keepdims=True).astype(I32)
    eid_ref[...] = jnp.concatenate([i1, i2, r1, r2, jnp.zeros((4, tm), I32)], axis=0)
    padded = jnp.concatenate([onehot_t, jnp.zeros((LANES - N_EXPERTS, tm), BF16)], axis=0)
    tile_total = lax.dot_general(jnp.ones((8, tm), BF16), padded, (((1,), (1,)), ((), ())),
                                 preferred_element_type=F32)
    total = cnt_ref[...] + tile_total
    cnt_ref[...] = total
    counts_ref[...] = total


def _post_call(attn, z, gates, x, mod, wa, wc, wo, g2, wr, br, tm):
    bsz, s, d = x.shape
    nt = s // tm
    tok = lambda w: pl.BlockSpec((None, tm, w), lambda b, t: (b, t, 0))
    full = lambda arr: pl.BlockSpec(arr.shape, lambda b, t: (0,) * arr.ndim)
    return pl.pallas_call(
        _post_kernel,
        grid=(bsz, nt),
        in_specs=[tok(ATTN_WIDTH), tok(CONV_WIDTH), tok(2 * d), tok(d),
                  pl.BlockSpec((None, 6, d), lambda b, t: (b, 0, 0)),
                  full(wa), full(wc), full(wo), full(g2), full(wr), full(br)],
        out_specs=[tok(d), pl.BlockSpec((None, tm) + ROW_TILE, lambda b, t: (b, t, 0, 0)),
                   pl.BlockSpec((8, tm), lambda b, t: (0, b * nt + t)), tok(LANES),
                   pl.BlockSpec((8, LANES), lambda b, t: (0, 0))],
        out_shape=[jax.ShapeDtypeStruct((bsz, s, d), F32),
                   jax.ShapeDtypeStruct((bsz, s) + ROW_TILE, BF16),
                   jax.ShapeDtypeStruct((8, bsz * s), I32), jax.ShapeDtypeStruct((bsz, s, LANES), F32),
                   jax.ShapeDtypeStruct((8, LANES), F32)],
        scratch_shapes=[pltpu.VMEM((8, LANES), F32)],
        compiler_params=pltpu.CompilerParams(dimension_semantics=("arbitrary", "arbitrary"),
                                             vmem_limit_bytes=VMEM_LIMIT),
        name="post",
    )(attn, z, gates, x, mod, wa, wc, wo, g2, wr, br)


ROW_TILE = (D_MODEL // LANES, LANES)
EXPERT_ROWS = 256
TOKENS_PER_STEP = 256


ISSUE_UNROLL = 8


def _dispatch_kernel(pos1_ref, pos2_ref, h2_ref, xs_in_ref, xs_ref, sem):
    del xs_in_ref
    tm = h2_ref.shape[0]
    base = pl.program_id(0) * tm

    def issue(r, c):
        pltpu.make_async_copy(h2_ref.at[r], xs_ref.at[pos1_ref[base + r]], sem).start(priority=0)
        pltpu.make_async_copy(h2_ref.at[r], xs_ref.at[pos2_ref[base + r]], sem).start(priority=1)
        return c

    lax.fori_loop(0, tm, issue, 0, unroll=ISSUE_UNROLL)
    for _ in range(2):
        pltpu.make_async_copy(h2_ref, xs_ref.at[pl.ds(0, tm)], sem).wait()


def _dispatch_call(pos1, pos2, h2_rows, n_rows):
    t = h2_rows.shape[0]
    tm = min(TOKENS_PER_STEP, t)
    xs0 = jnp.zeros((n_rows,) + ROW_TILE, BF16)
    return pl.pallas_call(
        _dispatch_kernel,
        grid_spec=pltpu.PrefetchScalarGridSpec(
            num_scalar_prefetch=2,
            grid=(t // tm,),
            in_specs=[pl.BlockSpec((tm,) + ROW_TILE, lambda i, p1, p2: (i, 0, 0)),
                      pl.BlockSpec(memory_space=pl.ANY)],
            out_specs=pl.BlockSpec(memory_space=pl.ANY),
            scratch_shapes=[pltpu.SemaphoreType.DMA(())]),
        out_shape=jax.ShapeDtypeStruct((n_rows,) + ROW_TILE, BF16),
        input_output_aliases={3: 0},
        compiler_params=pltpu.CompilerParams(dimension_semantics=("arbitrary",),
                                             vmem_limit_bytes=VMEM_LIMIT),
        name="dispatch",
    )(pos1, pos2, h2_rows, xs0)


def _expert_kernel(te_ref, nu_ref, xs_ref, wg_ref, wu_ref, wd_ref, ys_ref, wgb, wub, wdb):
    g = pl.program_id(0)
    e = te_ref[g]
    e_prev = te_ref[jnp.maximum(g - 1, 0)]

    @pl.when((g == 0) | (e != e_prev))
    def _():
        wgb[...] = wg_ref[...].astype(BF16)
        wub[...] = wu_ref[...].astype(BF16)
        wdb[...] = wd_ref[...].astype(BF16)

    @pl.when(g < nu_ref[0])
    def _():
        x = xs_ref[...].reshape(xs_ref.shape[0], D_MODEL)
        a = jnp.dot(x, wgb[...], preferred_element_type=F32)
        b = jnp.dot(x, wub[...], preferred_element_type=F32)
        hid = (a * jax.nn.sigmoid(a)) * b
        y = jnp.dot(hid.astype(BF16), wdb[...], preferred_element_type=F32)
        ys_ref[...] = y.astype(BF16).reshape(ys_ref.shape)

    @pl.when(g >= nu_ref[0])
    def _():
        ys_ref[...] = jnp.zeros(ys_ref.shape, BF16)


def _expert_call(tile_expert, n_used, xs, wg, wu, wd):
    n_rows, d = xs.shape[0], D_MODEL
    nt = n_rows // EXPERT_ROWS
    rows = pl.BlockSpec((EXPERT_ROWS,) + ROW_TILE, lambda g, te, nu: (g, 0, 0))
    wspec = lambda shp: pl.BlockSpec((None,) + shp, lambda g, te, nu: (te[g], 0, 0))
    return pl.pallas_call(
        _expert_kernel,
        grid_spec=pltpu.PrefetchScalarGridSpec(
            num_scalar_prefetch=2,
            grid=(nt,),
            in_specs=[rows, wspec((d, D_EXPERT)), wspec((d, D_EXPERT)), wspec((D_EXPERT, d))],
            out_specs=rows,
            scratch_shapes=[pltpu.VMEM((d, D_EXPERT), BF16), pltpu.VMEM((d, D_EXPERT), BF16),
                            pltpu.VMEM((D_EXPERT, d), BF16)]),
        out_shape=jax.ShapeDtypeStruct((n_rows,) + ROW_TILE, BF16),
        compiler_params=pltpu.CompilerParams(dimension_semantics=("arbitrary",),
                                             vmem_limit_bytes=VMEM_LIMIT),
        name="experts",
    )(tile_expert, n_used, xs, wg, wu, wd)


def _combine_kernel(pos1_ref, pos2_ref, ys_ref, ew_ref, hres_ref, g2_ref, gf_ref, o_ref, buf, sem):
    tm = hres_ref.shape[0]
    t = pl.program_id(0)
    nt = pl.num_programs(0)

    def issue(tile, slot):
        base = tile * tm

        def body(r, c):
            pltpu.make_async_copy(ys_ref.at[pos1_ref[base + r]], buf.at[slot, r],
                                  sem.at[slot]).start(priority=0)
            pltpu.make_async_copy(ys_ref.at[pos2_ref[base + r]], buf.at[slot, tm + r],
                                  sem.at[slot]).start(priority=1)
            return c

        lax.fori_loop(0, tm, body, 0, unroll=ISSUE_UNROLL)

    @pl.when(t == 0)
    def _():
        issue(0, 0)

    @pl.when(t + 1 < nt)
    def _():
        issue(t + 1, (t + 1) % 2)

    slot = t % 2
    pltpu.make_async_copy(ys_ref.at[pl.ds(0, 2 * tm)], buf.at[slot], sem.at[slot]).wait()

    y1 = buf[slot, 0:tm].astype(F32).reshape(tm, D_MODEL)
    y2 = buf[slot, tm:2 * tm].astype(F32).reshape(tm, D_MODEL)
    w = ew_ref[...]
    y = w[:, 0:1] * y1 + w[:, 1:2] * y2
    hfin = hres_ref[...] + g2_ref[...] * y
    ms = jnp.mean(hfin * hfin, axis=-1, keepdims=True)
    o_ref[...] = hfin * lax.rsqrt(ms + EPS) * gf_ref[...]


def _combine_call(pos1, pos2, ys_rows, ew, hres, gate2, gf):
    t, d = hres.shape
    tm = min(TOKENS_PER_STEP, t)
    per_batch = t // gate2.shape[0]
    assert per_batch % tm == 0
    tok = lambda w: pl.BlockSpec((tm, w), lambda i, *_: (i, 0))
    return pl.pallas_call(
        _combine_kernel,
        grid_spec=pltpu.PrefetchScalarGridSpec(
            num_scalar_prefetch=2,
            grid=(t // tm,),
            in_specs=[pl.BlockSpec(memory_space=pl.ANY), tok(LANES), tok(d),
                      pl.BlockSpec((None, 1, d), lambda i, *_: ((i * tm) // per_batch, 0, 0)),
                      pl.BlockSpec((1, d), lambda i, *_: (0, 0))],
            out_specs=tok(d),
            scratch_shapes=[pltpu.VMEM((2, 2 * tm) + ROW_TILE, BF16), pltpu.SemaphoreType.DMA((2,))]),
        out_shape=jax.ShapeDtypeStruct((t, d), F32),
        compiler_params=pltpu.CompilerParams(dimension_semantics=("arbitrary",),
                                             vmem_limit_bytes=VMEM_LIMIT),
        name="combine",
    )(pos1, pos2, ys_rows, ew, hres, gate2, gf)


def _slot_kernel(seg_ref, eid_ref, pos_ref):
    e = eid_ref[0:2, :]
    pos = eid_ref[2:4, :]
    for k in range(seg_ref.shape[0]):
        pos = pos + jnp.where(e == k, seg_ref[k], 0)
    pos_ref[...] = jnp.concatenate([pos, jnp.zeros((6, pos.shape[1]), I32)], axis=0)


def _slot_call(seg_start, eid):
    return pl.pallas_call(
        _slot_kernel,
        in_specs=[pl.BlockSpec(memory_space=pltpu.SMEM), pl.BlockSpec(eid.shape, lambda: (0, 0))],
        out_specs=pl.BlockSpec(eid.shape, lambda: (0, 0)),
        out_shape=jax.ShapeDtypeStruct(eid.shape, I32),
        name="slots",
    )(seg_start, eid)


def _sparse_moe(h2, eid, ew, counts, wg, wu, wd, hres, gate2, gf):
    bsz, s, d = hres.shape
    t = bsz * s
    ne = wg.shape[0]
    cnt = counts[0, :ne].astype(I32)
    seg_rows = ((cnt + EXPERT_ROWS - 1) // EXPERT_ROWS) * EXPERT_ROWS
    seg_end = jnp.cumsum(seg_rows)
    seg_start = seg_end - seg_rows
    pos = _slot_call(seg_start, eid)
    pos1, pos2 = pos[0], pos[1]
    n_tiles = (2 * t) // EXPERT_ROWS + ne
    tile_first_row = jnp.arange(n_tiles, dtype=I32) * EXPERT_ROWS
    tile_expert = jnp.minimum(jnp.sum(seg_end[None, :] <= tile_first_row[:, None], axis=1), ne - 1)
    n_used = (seg_end[ne - 1] // EXPERT_ROWS).reshape(1)
    n_rows = n_tiles * EXPERT_ROWS

    xs = _dispatch_call(pos1, pos2, h2.reshape((t,) + ROW_TILE), n_rows)
    ys = _expert_call(tile_expert.astype(I32), n_used.astype(I32), xs, wg, wu, wd)
    out = _combine_call(pos1, pos2, ys, ew.reshape(t, LANES), hres.reshape(t, d),
                        gate2.reshape(bsz, 1, d), gf.reshape(1, d))
    return out.reshape(bsz, s, d)


def kernel(x, c, w_ada, b_ada, norm1_g, w_in, rel_bias, conv_w, w_attn_branch, w_conv_branch, w_out,
           norm2_g, w_router_group, b_router_group, w_router_expert, b_router_expert,
           w_gate_e, w_up_e, w_down_e, norm_f_g):
    bsz, s, d = x.shape
    depth = w_ada.shape[0]
    h = x.astype(F32)
    c_pad = jnp.zeros((8, d), F32).at[:bsz].set(c.astype(F32))
    out = h
    for l in range(depth):
        mod = _ada_call(c_pad, w_ada[l].astype(F32), b_ada[l].astype(F32)[None, :])
        mod = mod[:bsz].reshape(bsz, 6, d)

        w = w_in[l]
        o_q, o_k, o_v, o_qi = 0, 512, 1024, 1536
        o_ki, o_wi, o_cb, o_gl = 2048, 2112, 2120, 3656
        wa = w[:, o_q:o_ki].astype(BF16)
        ki_w = w[:, o_ki:o_wi]
        wb = jnp.concatenate([ki_w, ki_w, w[:, o_wi:o_cb],
                              jnp.zeros((d, LANES - IDX_HEADS), w.dtype)], axis=1).astype(BF16)
        wc = w[:, o_cb:o_gl].astype(BF16)
        wd = w[:, o_gl:].astype(BF16)

        q, k, v, qi, kid, wi, z, gates = _proj_call(
            h, mod, norm1_g[l].astype(F32)[None, :], wa, wb, wc, wd, conv_w[l].astype(F32),
            tm=min(512, s))

        attn = _attn_call(rel_bias.astype(F32), q, qi, wi, k, v, kid, tq=min(TOPK_MAX, s // 4))

        wr32 = jnp.concatenate([w_router_expert[l], w_router_group[l],
                                jnp.zeros((d, LANES - N_EXPERTS - N_GROUPS), F32)], axis=1).astype(F32)
        wr_hi = wr32.astype(BF16)
        wr = jnp.stack([wr_hi, (wr32 - wr_hi.astype(F32)).astype(BF16)])
        br = jnp.concatenate([b_router_expert[l], b_router_group[l],
                              jnp.zeros((LANES - N_EXPERTS - N_GROUPS,), F32)])[None, :].astype(F32)
        hres, h2, eid, ew, counts = _post_call(
            attn, z, gates, h, mod, w_attn_branch[l].astype(BF16), w_conv_branch[l].astype(BF16),
            w_out[l].astype(BF16), norm2_g[l].astype(F32)[None, :], wr, br, tm=min(512, s))

        assert depth == 1, "the final RMSNorm is fused into the combine kernel of the only layer"
        out = _sparse_moe(h2, eid, ew, counts, w_gate_e[l].astype(F32), w_up_e[l].astype(F32),
                          w_down_e[l].astype(F32), hres, mod[:, 5, :], norm_f_g.astype(F32))
        h = out
    return out.astype(x.dtype)
```

```python
import functools
import math

import numpy as np
import jax
import jax.numpy as jnp
from jax import lax
from jax.experimental import pallas as pl
from jax.experimental.pallas import tpu as pltpu

F32 = jnp.float32
BF16 = jnp.bfloat16
I32 = jnp.int32

D_MODEL = 1024
N_HEADS = 8
HEAD_DIM = 64
ATTN_WIDTH = N_HEADS * HEAD_DIM
IDX_HEADS = 8
IDX_DIM = 64
TOPK_MAX = 256
N_BUCKETS = 32
MAX_DISTANCE = 128
CONV_WIDTH = 512
CONV_K = 3
N_GROUPS = 4
EXPERTS_PER_GROUP = 8
N_EXPERTS = N_GROUPS * EXPERTS_PER_GROUP
D_EXPERT = 512
EPS = 1e-6

LANES = 128
VMEM_LIMIT = 56 * 1024 * 1024

LOG2E = math.log2(math.e)
MASK_NEG = -1e30
SCORE_NEG = float(np.finfo(np.float32).min)
TINY_F32 = float(np.finfo(np.float32).tiny)
KEY_SPACE_AFTER = 40
RADIX_BITS_PER_STEP = 4


def _t5_bucket_starts():
    max_exact = N_BUCKETS // 2
    n = np.arange(0, MAX_DISTANCE + 1)
    nf = np.maximum(n, 1).astype(np.float32)
    val = (np.log(nf / np.float32(max_exact)) / np.float32(math.log(MAX_DISTANCE / max_exact))
           * np.float32(N_BUCKETS - max_exact)).astype(np.float32)
    inner = val[max_exact + 1:MAX_DISTANCE]
    assert np.min(np.abs(inner - np.round(inner))) > 1e-3
    large = np.minimum(max_exact + val.astype(np.int32), N_BUCKETS - 1)
    bucket = np.where(n < max_exact, n, large)
    assert bucket[MAX_DISTANCE] == N_BUCKETS - 1 and np.all(np.diff(bucket) >= 0)
    return [int(np.argmax(bucket >= b)) for b in range(N_BUCKETS)]


_BUCKET_START = _t5_bucket_starts()


def _ada_kernel(c_ref, w_ref, b_ref, o_ref):
    c = c_ref[...]
    ca = c * jax.nn.sigmoid(c)
    o_ref[...] = jnp.dot(ca, w_ref[...], preferred_element_type=F32,
                         precision=lax.Precision.HIGHEST) + b_ref[...]


def _ada_call(c_pad, w, b):
    rows, d = c_pad.shape
    n = w.shape[1]
    tn = 1536
    return pl.pallas_call(
        _ada_kernel,
        grid=(n // tn,),
        in_specs=[pl.BlockSpec((rows, d), lambda j: (0, 0)),
                  pl.BlockSpec((d, tn), lambda j: (0, j)),
                  pl.BlockSpec((1, tn), lambda j: (0, j))],
        out_specs=pl.BlockSpec((rows, tn), lambda j: (0, j)),
        out_shape=jax.ShapeDtypeStruct((rows, n), F32),
        compiler_params=pltpu.CompilerParams(dimension_semantics=("arbitrary",),
                                             vmem_limit_bytes=VMEM_LIMIT),
        name="ada",
    )(c_pad, w, b)


def _rms_mod(x, g, scale, shift):
    ms = jnp.mean(x * x, axis=-1, keepdims=True)
    y = x * lax.rsqrt(ms + EPS) * g
    return y * (1.0 + scale) + shift


def _proj_kernel(x_ref, mod_ref, g_ref, wa_ref, wb_ref, wc_ref, wd_ref, cw_ref,
                 q_ref, k_ref, v_ref, qi_ref, kid_ref, wi_ref, z_ref, gate_ref, ubuf, *, tm):
    h = _rms_mod(x_ref[...], g_ref[...], mod_ref[1:2, :], mod_ref[0:1, :])
    hb = h.astype(BF16)

    a = jnp.dot(hb, wa_ref[...], preferred_element_type=F32)
    q_ref[...] = (a[:, 0:512] * (HEAD_DIM ** -0.5 * LOG2E)).astype(BF16)
    k_ref[...] = a[:, 512:1024].astype(BF16)
    v_ref[...] = a[:, 1024:1536].astype(BF16)
    qi_ref[...] = a[:, 1536:2048].astype(BF16)

    b = jnp.dot(hb, wb_ref[...], preferred_element_type=F32)
    kid_ref[...] = b[:, 0:LANES].astype(BF16)
    wi_ref[...] = b[:, LANES:2 * LANES] * ((IDX_DIM ** -0.5) * (IDX_HEADS ** -0.5))

    cpart = jnp.dot(hb, wc_ref[...], preferred_element_type=F32)
    cb = cpart[:, 0:512]
    u = cpart[:, 512:1024] * cpart[:, 1024:1536]

    @pl.when(pl.program_id(1) == 0)
    def _():
        ubuf[0:8, :] = jnp.zeros((8, CONV_WIDTH), F32)

    ubuf[8:tm + 8, :] = u
    conv = (cw_ref[0:1, :] * ubuf[6:tm + 6, :] + cw_ref[1:2, :] * ubuf[7:tm + 7, :]
            + cw_ref[2:3, :] * u)
    z_ref[...] = (cb * conv).astype(BF16)
    ubuf[0:8, :] = ubuf[tm:tm + 8, :]

    d = jnp.dot(hb, wd_ref[...], preferred_element_type=F32)
    gate_ref[...] = jax.nn.sigmoid(d).astype(BF16)


def _proj_call(x, mod, g1, wa, wb, wc, wd, cw, tm):
    bsz, s, d = x.shape
    nt = s // tm
    tok = lambda w: pl.BlockSpec((None, tm, w), lambda b, t: (b, t, 0))
    full = lambda arr: pl.BlockSpec(arr.shape, lambda b, t: (0,) * arr.ndim)
    out_widths = [(512, BF16), (512, BF16), (512, BF16), (512, BF16), (LANES, BF16), (LANES, F32),
                  (512, BF16), (2 * D_MODEL, BF16)]
    return pl.pallas_call(
        functools.partial(_proj_kernel, tm=tm),
        grid=(bsz, nt),
        in_specs=[tok(d), pl.BlockSpec((None, 6, d), lambda b, t: (b, 0, 0)), full(g1),
                  full(wa), full(wb), full(wc), full(wd), full(cw)],
        out_specs=[tok(w) for w, _ in out_widths],
        out_shape=[jax.ShapeDtypeStruct((bsz, s, w), dt) for w, dt in out_widths],
        scratch_shapes=[pltpu.VMEM((tm + 8, CONV_WIDTH), F32)],
        compiler_params=pltpu.CompilerParams(dimension_semantics=("arbitrary", "arbitrary"),
                                             vmem_limit_bytes=VMEM_LIMIT),
        name="proj",
    )(x, mod, g1, wa, wb, wc, wd, cw)


def _to_key(f):
    b = pltpu.bitcast(f, I32)
    return jnp.where(b < 0, b ^ 0x7FFFFFFF, b)


def _from_key(k):
    return pltpu.bitcast(jnp.where(k < 0, k ^ 0x7FFFFFFF, k), F32)


def _bit_transpose32(words):
    a = list(words)
    j, m = 16, 0x0000FFFF
    while j:
        k = 0
        while k < 32:
            t = (a[k] ^ lax.shift_right_logical(a[k + j], jnp.int32(j))) & jnp.int32(m)
            a[k] = a[k] ^ t
            a[k + j] = a[k + j] ^ lax.shift_left(t, jnp.int32(j))
            k = (k + j + 1) & ~j
        j >>= 1
        m = (m ^ (m << j)) & 0xFFFFFFFF
    return a


def _fold_lanes(x, op):
    parts = [x[:, t * LANES:(t + 1) * LANES] for t in range(x.shape[1] // LANES)]
    while len(parts) > 1:
        nxt = [op(parts[a], parts[a + 1]) for a in range(0, len(parts) - 1, 2)]
        parts = nxt + ([parts[-1]] if len(parts) % 2 else [])
    return parts[0]


def _tile_lanes(x, n):
    return x if n == 1 else jnp.concatenate([x] * n, axis=1)


def _attn_kernel(rb_ref, q_ref, qi_ref, wi_ref, qin_ref, win_ref, k_ref, v_ref, kid_ref, o_ref,
                 sct_ref, planes_ref, cand_ref, mb_ref, tb_ref, qm_ref, qim_ref, m_ref, l_ref, acc_ref,
                 *, tq, topk):
    c = tq
    i = pl.program_id(1)
    nt_dims = (((1,), (1,)), ((), ()))
    lane = lax.broadcasted_iota(I32, (tq, LANES), 1)
    low_half = lane < HEAD_DIM
    row = lax.broadcasted_iota(I32, (tq, c), 0)
    col = lax.broadcasted_iota(I32, (tq, c), 1)

    @pl.when((pl.program_id(0) == 0) & (i == 0))
    def _():
        for p in range(32):
            planes_ref[p] = jnp.zeros(planes_ref.shape[1:], I32)
        for near in range(2):
            rel = row - col + (1 - near) * c
            n = jnp.maximum(rel, 0)
            for h in range(N_HEADS):
                t = jnp.full((tq, c), rb_ref[0, h], F32)
                for b in range(1, N_BUCKETS):
                    t = jnp.where(n >= _BUCKET_START[b], rb_ref[b, h], t)
                t = (t - rb_ref[N_BUCKETS - 1, h]) * LOG2E
                tb_ref[h, :, near * c:(near + 1) * c] = jnp.where(rel >= 0, t, MASK_NEG)


    def masked_heads(src_ref, dst_ref):
        for h in range(N_HEADS):
            hp, half = divmod(h, 2)
            keep = low_half if half == 0 else jnp.logical_not(low_half)
            pair = src_ref[:, hp * LANES:(hp + 1) * LANES]
            dst_ref[h] = jnp.where(keep, pair, jnp.zeros_like(pair))

    krow = lax.broadcasted_iota(I32, (c, tq), 0)
    qcol = lax.broadcasted_iota(I32, (c, tq), 1)
    int_min = jnp.int32(-2 ** 31)

    def fold_rows(x, op):
        parts = [x[r * 8:(r + 1) * 8, :] for r in range(x.shape[0] // 8)]
        while len(parts) > 1:
            nxt = [op(parts[a], parts[a + 1]) for a in range(0, len(parts) - 1, 2)]
            parts = nxt + ([parts[-1]] if len(parts) % 2 else [])
        return parts[0]

    def score_chunk(j, diag, w_t):
        start = pl.multiple_of(j * c, c)
        kc = kid_ref[pl.ds(start, c), :]
        acc = jnp.zeros((c, tq), F32)
        for h in range(N_HEADS):
            d = lax.dot_general(kc, qim_ref[h], nt_dims, preferred_element_type=F32)
            acc = acc + jnp.maximum(d, 0.0) * w_t[h:h + 1, :]
        ukey = _to_key(acc) ^ int_min
        if diag:
            causal = krow <= qcol
            acc = jnp.where(causal, acc, SCORE_NEG)
            ukey = jnp.where(causal, ukey, 0)
        sct_ref[pl.ds(start, c), :] = acc
        prow = pl.multiple_of(j * (c // 32), c // 32)
        for lt in range(tq // LANES):
            words = [ukey[k * 8:(k + 1) * 8, lt * LANES:(lt + 1) * LANES] for k in range(32)]
            for p, plane in enumerate(_bit_transpose32(words)):
                planes_ref[p, pl.ds(prow, c // 32), lt * LANES:(lt + 1) * LANES] = plane

    @pl.when(i == 0)
    def _():
        masked_heads(qi_ref, qim_ref)
        score_chunk(0, True, jnp.transpose(wi_ref[...])[0:IDX_HEADS, :])

    nchunks = i + 1
    n_causal = i * tq + lax.broadcasted_iota(I32, (1, tq), 1) + 1
    prow_iota = lax.broadcasted_iota(I32, cand_ref.shape, 0)
    cand_ref[...] = jnp.where(prow_iota < nchunks * (c // 32), -1, 0)

    def radix_cond(st):
        return (st[0] < 32) & (st[1] > 0)

    def radix_step(st):
        p0, _, prefix, need, ncand, settled = st
        cand = cand_ref[...]
        for sub in range(RADIX_BITS_PER_STEP):
            p = p0 + sub
            plane = planes_ref[p]
            ones = cand & plane
            cnt1 = jnp.sum(fold_rows(lax.population_count(ones), jnp.add), axis=0, keepdims=True)
            take1 = cnt1 >= need
            live = settled == 0
            need = jnp.where(live & jnp.logical_not(take1), need - cnt1, need)
            ncand = jnp.where(live, jnp.where(take1, cnt1, ncand - cnt1), ncand)
            prefix = jnp.where(live & take1, prefix | jnp.left_shift(jnp.int32(1), 31 - p), prefix)
            cand = cand & (plane ^ jnp.where(take1, 0, -1))
            settled = jnp.where(ncand == need, 1, settled)
        cand_ref[...] = cand
        return p0 + RADIX_BITS_PER_STEP, jnp.sum(1 - settled), prefix, need, ncand, settled

    done0 = jnp.where(n_causal <= topk, 1, 0)
    _, _, prefix, _, _, _ = lax.while_loop(
        radix_cond, radix_step,
        (jnp.int32(0), jnp.sum(1 - done0), jnp.zeros((1, tq), I32), jnp.full((1, tq), topk, I32),
         jnp.full((1, tq), 1, I32) * (nchunks * c), done0))
    guess_key = prefix ^ int_min
    guess = _from_key(guess_key)

    def count_keys(pred):
        def body(j, part):
            start = pl.multiple_of(j * c, c)
            hit = pred(sct_ref[pl.ds(start, c), :], start)
            return part + fold_rows(jnp.where(hit, 1.0, 0.0), jnp.add)
        part = lax.fori_loop(0, nchunks, body, jnp.zeros((8, tq), F32))
        return jnp.sum(part, axis=0, keepdims=True)

    kf = float(topk)
    guess_above = _from_key(guess_key + 1)
    guess_above = jnp.where(jnp.abs(guess_above) < TINY_F32,
                            jnp.where(guess >= 0.0, TINY_F32, 0.0), guess_above)
    guess_below = _from_key(guess_key - 2)

    def cond(st):
        return st[0] > 0

    def step(st):
        _, it, lo, hi, c_lo, c_hi, done = st
        mid = 0.5 * lo + 0.5 * hi
        lo_k, hi_k = _to_key(lo), _to_key(hi)
        mid_key = _from_key((lo_k >> 1) + (hi_k >> 1) + (lo_k & hi_k & 1))
        mid = jnp.where(it >= KEY_SPACE_AFTER, mid_key, mid)
        probe = jnp.where(it == 0, guess, jnp.where(lo == guess, guess_above, guess_below))
        mid = jnp.where((it <= 1) & (probe > lo) & (probe < hi), probe, mid)
        go = (done == 0) & (mid > lo) & (mid < hi)
        cnt = count_keys(lambda tile, start: tile >= mid)
        ge = cnt >= kf
        up = ge & go
        dn = jnp.logical_not(ge) & go
        lo = jnp.where(up, mid, lo)
        hi = jnp.where(dn, mid, hi)
        c_lo = jnp.where(up, cnt, c_lo)
        c_hi = jnp.where(dn, cnt, c_hi)
        adjacent = (lo == guess) & (hi == guess_above)
        done = jnp.where(go & (c_lo != kf) & jnp.logical_not(adjacent), 0, 1)
        return jnp.sum(1 - done), it + 1, lo, hi, c_lo, c_hi, done

    _, _, thr, _, c_lo, c_hi, _ = lax.while_loop(
        cond, step, (jnp.sum(1 - done0), jnp.int32(0), jnp.full((1, tq), SCORE_NEG, F32),
                     jnp.full((1, tq), -SCORE_NEG, F32), jnp.full((1, tq), 1.0, F32) * (nchunks * c),
                     jnp.zeros((1, tq), F32), done0))

    def write_mask(selected):
        def tiles(start, rows):
            tile = sct_ref[pl.ds(start, rows), :]
            key = lax.broadcasted_iota(I32, (rows, tq), 0) + start
            mb_ref[:, pl.ds(start, rows)] = jnp.transpose(
                jnp.where(selected(tile, key), 0.0, MASK_NEG))

        def body(j, carry):
            tiles(pl.multiple_of(j * 2 * c, 2 * c), 2 * c)
            return carry

        lax.fori_loop(0, nchunks // 2, body, 0)

        @pl.when(nchunks % 2 == 1)
        def _():
            tiles(pl.multiple_of((nchunks - 1) * c, c), c)

    tied = (n_causal > topk) & (c_lo > kf)
    any_tied = jnp.sum(jnp.where(tied, 1, 0)) > 0

    @pl.when(any_tied)
    def _():
        n_take = kf - c_hi

        def bis(_, st):
            lo_x, hi_x = st
            mid = (lo_x + hi_x) >> 1
            ok = count_keys(lambda tile, start: (tile == thr) & ((krow + start) <= mid)) >= n_take
            return jnp.where(ok, lo_x, mid), jnp.where(ok, mid, hi_x)

        nbits = int(math.ceil(math.log2(sct_ref.shape[0]))) + 1
        _, hi_x = lax.fori_loop(0, nbits, bis, (jnp.full((1, tq), -1, I32),
                                                jnp.full((1, tq), sct_ref.shape[0], I32)))
        cut = jnp.where(tied, hi_x, jnp.int32(2 ** 30))

        write_mask(lambda tile, key: (tile > thr) | ((tile == thr) & (key <= cut)))

    @pl.when(jnp.logical_not(any_tied))
    def _():
        write_mask(lambda tile, key: tile >= thr)

    masked_heads(q_ref, qm_ref)
    masked_heads(qin_ref, qim_ref)
    w_t_next = jnp.transpose(win_ref[...])[0:IDX_HEADS, :]
    has_next = i + 1 < pl.num_programs(1)
    m_ref[...] = jnp.full(m_ref.shape, MASK_NEG, F32)
    l_ref[...] = jnp.zeros(l_ref.shape, F32)
    acc_ref[...] = jnp.zeros(acc_ref.shape, F32)

    def attend(start, width, near):
        nt = width // LANES
        ones = jnp.ones((width, LANES), BF16)
        for h in range(N_HEADS):
            hp = h // 2
            kp = k_ref[pl.ds(start, width), hp * LANES:(hp + 1) * LANES]
            vp = v_ref[pl.ds(start, width), hp * LANES:(hp + 1) * LANES]
            s = lax.dot_general(qm_ref[h], kp, nt_dims, preferred_element_type=F32)
            s = s + mb_ref[:, pl.ds(start, width)]
            if near:
                s = s + tb_ref[h, :, 2 * c - width:2 * c]
            m_old = m_ref[h]
            m_new = jnp.maximum(m_old, jnp.max(_fold_lanes(s, jnp.maximum), axis=1, keepdims=True))
            p = jnp.exp2(s - _tile_lanes(m_new, nt)).astype(BF16)
            pv = jnp.dot(p, jnp.concatenate([vp, ones], axis=1), preferred_element_type=F32)
            alpha = jnp.exp2(m_old - m_new)
            m_ref[h] = m_new
            l_ref[h] = alpha * l_ref[h] + pv[:, LANES:]
            acc_ref[h] = alpha * acc_ref[h] + pv[:, :LANES]

    wide = 2 * c
    n_far = jnp.maximum(i - 1, 0)

    n_wide = n_far // 2

    def far_body(j, _):
        score_chunk(2 * j, False, w_t_next)
        attend(pl.multiple_of(j * wide, wide), wide, False)
        score_chunk(2 * j + 1, False, w_t_next)
        return 0

    lax.fori_loop(0, n_wide, far_body, 0)

    @pl.when(n_far % 2 == 1)
    def _():
        attend(pl.multiple_of((n_far - 1) * c, c), c, False)
        score_chunk(n_far - 1, False, w_t_next)

    @pl.when(i >= 1)
    def _():
        score_chunk(i - 1, False, w_t_next)
        attend(pl.multiple_of((i - 1) * c, c), wide, True)
        score_chunk(i, False, w_t_next)

    @pl.when(i == 0)
    def _():
        attend(0, c, True)
        score_chunk(0, False, w_t_next)

    @pl.when(has_next)
    def _():
        score_chunk(i + 1, True, w_t_next)

    for hp in range(N_HEADS // 2):
        lo = acc_ref[2 * hp] / l_ref[2 * hp]
        hi = acc_ref[2 * hp + 1] / l_ref[2 * hp + 1]
        o_ref[:, hp * LANES:(hp + 1) * LANES] = jnp.where(low_half, lo, hi).astype(BF16)


def _attn_call(rel_bias, q, qi, wi, k, v, kid, tq):
    bsz, s, _ = q.shape
    nq = s // tq
    topk = min(TOPK_MAX, s // 4)
    assert tq % LANES == 0 and tq >= MAX_DISTANCE
    blk = lambda w: pl.BlockSpec((None, tq, w), lambda b, i: (b, i, 0))
    nxt = lambda w: pl.BlockSpec((None, tq, w), lambda b, i: (b, jnp.minimum(i + 1, nq - 1), 0))
    whole = lambda w: pl.BlockSpec((None, s, w), lambda b, i: (b, 0, 0), pipeline_mode=pl.Buffered(1))
    return pl.pallas_call(
        functools.partial(_attn_kernel, tq=tq, topk=topk),
        grid=(bsz, nq),
        in_specs=[pl.BlockSpec(memory_space=pltpu.SMEM),
                  blk(ATTN_WIDTH), blk(ATTN_WIDTH), blk(LANES), nxt(ATTN_WIDTH), nxt(LANES),
                  whole(ATTN_WIDTH), whole(ATTN_WIDTH), whole(LANES)],
        out_specs=blk(ATTN_WIDTH),
        out_shape=jax.ShapeDtypeStruct((bsz, s, ATTN_WIDTH), BF16),
        scratch_shapes=[
            pltpu.VMEM((s, tq), F32),
            pltpu.VMEM((32, s // 32, tq), I32),
            pltpu.VMEM((s // 32, tq), I32),
            pltpu.VMEM((tq, s), F32),
            pltpu.VMEM((N_HEADS, tq, 2 * tq), F32),
            pltpu.VMEM((N_HEADS, tq, LANES), BF16),
            pltpu.VMEM((N_HEADS, tq, LANES), BF16),
            pltpu.VMEM((N_HEADS, tq, LANES), F32),
            pltpu.VMEM((N_HEADS, tq, LANES), F32),
            pltpu.VMEM((N_HEADS, tq, LANES), F32),
        ],
        compiler_params=pltpu.CompilerParams(dimension_semantics=("arbitrary", "arbitrary"),
                                             vmem_limit_bytes=VMEM_LIMIT),
        name="attn",
    )(rel_bias, q, qi, wi, qi, wi, k, v, kid)


def _post_kernel(attn_ref, z_ref, gate_ref, x_ref, mod_ref, wa_ref, wc_ref, wo_ref, g2_ref,
                 wr_ref, br_ref, hres_ref, h2_ref, eid_ref, ew_ref, counts_ref, cnt_ref):
    ya = jnp.dot(attn_ref[...], wa_ref[...], preferred_element_type=F32)
    yc = jnp.dot(z_ref[...], wc_ref[...], preferred_element_type=F32)
    merged = gate_ref[:, 0:D_MODEL].astype(F32) * ya + gate_ref[:, D_MODEL:].astype(F32) * yc
    o = jnp.dot(merged.astype(BF16), wo_ref[...], preferred_element_type=F32)
    hres = x_ref[...] + mod_ref[2:3, :] * o
    hres_ref[...] = hres
    h2 = _rms_mod(hres, g2_ref[...], mod_ref[4:5, :], mod_ref[3:4, :])
    h2_ref[...] = h2.astype(BF16).reshape(h2_ref.shape)

    h2_hi = h2.astype(BF16)
    h2_lo = (h2 - h2_hi.astype(F32)).astype(BF16)
    lg = (jnp.dot(h2_hi, wr_ref[0], preferred_element_type=F32)
          + (jnp.dot(h2_lo, wr_ref[0], preferred_element_type=F32)
             + jnp.dot(h2_hi, wr_ref[1], preferred_element_type=F32))) + br_ref[...]
    tm = lg.shape[0]
    lgt = jnp.transpose(lg)
    ninf = -jnp.inf
    grow = lax.broadcasted_iota(I32, (8, tm), 0)
    gl = jnp.where(grow < N_GROUPS, lgt[N_EXPERTS:N_EXPERTS + 8, :], ninf)
    gmax = jnp.max(gl, axis=0, keepdims=True)
    g_sel = jnp.min(jnp.where(gl == gmax, grow, N_GROUPS), axis=0, keepdims=True)
    g_w = 1.0 / jnp.sum(jnp.exp(gl - gmax), axis=0, keepdims=True)

    erow = lax.broadcasted_iota(I32, (N_EXPERTS, tm), 0)
    emask = (erow // EXPERTS_PER_GROUP) == g_sel
    el = jnp.where(emask, lgt[0:N_EXPERTS, :], ninf)
    emax = jnp.max(el, axis=0, keepdims=True)
    ee = jnp.exp(el - emax)
    e_prob = ee / jnp.sum(ee, axis=0, keepdims=True)
    p1 = jnp.max(jnp.where(emask, e_prob, -1.0), axis=0, keepdims=True)
    i1 = jnp.min(jnp.where(emask & (e_prob == p1), erow, LANES), axis=0, keepdims=True)
    rest = emask & (erow != i1)
    p2 = jnp.max(jnp.where(rest, e_prob, -1.0), axis=0, keepdims=True)
    i2 = jnp.min(jnp.where(rest & (e_prob == p2), erow, LANES), axis=0, keepdims=True)
    psum = p1 + p2
    w1 = g_w * (p1 / psum)
    w2 = g_w * (p2 / psum)
    ew_ref[...] = jnp.transpose(jnp.concatenate([w1, w2, jnp.zeros((LANES - 2, tm), F32)], axis=0))

    @pl.when((pl.program_id(0) == 0) & (pl.program_id(1) == 0))
    def _():
        cnt_ref[...] = jnp.zeros(cnt_ref.shape, F32)

    onehot_t = jnp.where((erow == i1) | (erow == i2), 1.0, 0.0).astype(BF16)
    upper = (lax.broadcasted_iota(I32, (tm, tm), 0) <= lax.broadcasted_iota(I32, (tm, tm), 1))
    cum_t = jnp.dot(onehot_t, jnp.where(upper, 1.0, 0.0).astype(BF16), preferred_element_type=F32)
    cnt_col = jnp.transpose(cnt_ref[...])[0:N_EXPERTS, 0:1]
    before = cum_t - onehot_t.astype(F32) + cnt_col
    r1 = jnp.sum(jnp.where(erow == i1, before, 0.0), axis=0, keepdims=True).astype(I32)
    r2 = jnp.sum(jnp.where(erow == i2, before, 0.0), axis=0, keepdims=True).astype(I32)
    eid_ref[...] = jnp.concatenate([i1, i2, r1, r2, jnp.zeros((4, tm), I32)], axis=0)
    padded = jnp.concatenate([onehot_t, jnp.zeros((LANES - N_EXPERTS, tm), BF16)], axis=0)
    tile_total = lax.dot_general(jnp.ones((8, tm), BF16), padded, (((1,), (1,)), ((), ())),
                                 preferred_element_type=F32)
    total = cnt_ref[...] + tile_total
    cnt_ref[...] = total
    counts_ref[...] = total


def _post_call(attn, z, gates, x, mod, wa, wc, wo, g2, wr, br, tm):
    bsz, s, d = x.shape
    nt = s // tm
    tok = lambda w: pl.BlockSpec((None, tm, w), lambda b, t: (b, t, 0))
    full = lambda arr: pl.BlockSpec(arr.shape, lambda b, t: (0,) * arr.ndim)
    return pl.pallas_call(
        _post_kernel,
        grid=(bsz, nt),
        in_specs=[tok(ATTN_WIDTH), tok(CONV_WIDTH), tok(2 * d), tok(d),
                  pl.BlockSpec((None, 6, d), lambda b, t: (b, 0, 0)),
                  full(wa), full(wc), full(wo), full(g2), full(wr), full(br)],
        out_specs=[tok(d), pl.BlockSpec((None, tm) + ROW_TILE, lambda b, t: (b, t, 0, 0)),
                   pl.BlockSpec((8, tm), lambda b, t: (0, b * nt + t)), tok(LANES),
                   pl.BlockSpec((8, LANES), lambda b, t: (0, 0))],
        out_shape=[jax.ShapeDtypeStruct((bsz, s, d), F32),
                   jax.ShapeDtypeStruct((bsz, s) + ROW_TILE, BF16),
                   jax.ShapeDtypeStruct((8, bsz * s), I32), jax.ShapeDtypeStruct((bsz, s, LANES), F32),
                   jax.ShapeDtypeStruct((8, LANES), F32)],
        scratch_shapes=[pltpu.VMEM((8, LANES), F32)],
        compiler_params=pltpu.CompilerParams(dimension_semantics=("arbitrary", "arbitrary"),
                                             vmem_limit_bytes=VMEM_LIMIT),
        name="post",
    )(attn, z, gates, x, mod, wa, wc, wo, g2, wr, br)


ROW_TILE = (D_MODEL // LANES, LANES)
EXPERT_ROWS = 256
TOKENS_PER_STEP = 256


ISSUE_UNROLL = 8


def _dispatch_kernel(pos1_ref, pos2_ref, h2_ref, xs_in_ref, xs_ref, sem):
    del xs_in_ref
    tm = h2_ref.shape[0]
    base = pl.program_id(0) * tm

    def issue(r, c):
        pltpu.make_async_copy(h2_ref.at[r], xs_ref.at[pos1_ref[base + r]], sem).start(priority=0)
        pltpu.make_async_copy(h2_ref.at[r], xs_ref.at[pos2_ref[base + r]], sem).start(priority=1)
        return c

    lax.fori_loop(0, tm, issue, 0, unroll=ISSUE_UNROLL)
    for _ in range(2):
        pltpu.make_async_copy(h2_ref, xs_ref.at[pl.ds(0, tm)], sem).wait()


def _dispatch_call(pos1, pos2, h2_rows, n_rows):
    t = h2_rows.shape[0]
    tm = min(TOKENS_PER_STEP, t)
    xs0 = jnp.zeros((n_rows,) + ROW_TILE, BF16)
    return pl.pallas_call(
        _dispatch_kernel,
        grid_spec=pltpu.PrefetchScalarGridSpec(
            num_scalar_prefetch=2,
            grid=(t // tm,),
            in_specs=[pl.BlockSpec((tm,) + ROW_TILE, lambda i, p1, p2: (i, 0, 0)),
                      pl.BlockSpec(memory_space=pl.ANY)],
            out_specs=pl.BlockSpec(memory_space=pl.ANY),
            scratch_shapes=[pltpu.SemaphoreType.DMA(())]),
        out_shape=jax.ShapeDtypeStruct((n_rows,) + ROW_TILE, BF16),
        input_output_aliases={3: 0},
        compiler_params=pltpu.CompilerParams(dimension_semantics=("arbitrary",),
                                             vmem_limit_bytes=VMEM_LIMIT),
        name="dispatch",
    )(pos1, pos2, h2_rows, xs0)


def _expert_kernel(te_ref, nu_ref, xs_ref, wg_ref, wu_ref, wd_ref, ys_ref, wgb, wub, wdb):
    g = pl.program_id(0)
    e = te_ref[g]
    e_prev = te_ref[jnp.maximum(g - 1, 0)]

    @pl.when((g == 0) | (e != e_prev))
    def _():
        wgb[...] = wg_ref[...].astype(BF16)
        wub[...] = wu_ref[...].astype(BF16)
        wdb[...] = wd_ref[...].astype(BF16)

    @pl.when(g < nu_ref[0])
    def _():
        x = xs_ref[...].reshape(xs_ref.shape[0], D_MODEL)
        a = jnp.dot(x, wgb[...], preferred_element_type=F32)
        b = jnp.dot(x, wub[...], preferred_element_type=F32)
        hid = (a * jax.nn.sigmoid(a)) * b
        y = jnp.dot(hid.astype(BF16), wdb[...], preferred_element_type=F32)
        ys_ref[...] = y.astype(BF16).reshape(ys_ref.shape)

    @pl.when(g >= nu_ref[0])
    def _():
        ys_ref[...] = jnp.zeros(ys_ref.shape, BF16)


def _expert_call(tile_expert, n_used, xs, wg, wu, wd):
    n_rows, d = xs.shape[0], D_MODEL
    nt = n_rows // EXPERT_ROWS
    rows = pl.BlockSpec((EXPERT_ROWS,) + ROW_TILE, lambda g, te, nu: (g, 0, 0))
    wspec = lambda shp: pl.BlockSpec((None,) + shp, lambda g, te, nu: (te[g], 0, 0))
    return pl.pallas_call(
        _expert_kernel,
        grid_spec=pltpu.PrefetchScalarGridSpec(
            num_scalar_prefetch=2,
            grid=(nt,),
            in_specs=[rows, wspec((d, D_EXPERT)), wspec((d, D_EXPERT)), wspec((D_EXPERT, d))],
            out_specs=rows,
            scratch_shapes=[pltpu.VMEM((d, D_EXPERT), BF16), pltpu.VMEM((d, D_EXPERT), BF16),
                            pltpu.VMEM((D_EXPERT, d), BF16)]),
        out_shape=jax.ShapeDtypeStruct((n_rows,) + ROW_TILE, BF16),
        compiler_params=pltpu.CompilerParams(dimension_semantics=("arbitrary",),
                                             vmem_limit_bytes=VMEM_LIMIT),
        name="experts",
    )(tile_expert, n_used, xs, wg, wu, wd)


def _combine_kernel(pos1_ref, pos2_ref, ys_ref, ew_ref, hres_ref, g2_ref, gf_ref, o_ref, buf, sem):
    tm = hres_ref.shape[0]
    t = pl.program_id(0)
    nt = pl.num_programs(0)

    def issue(tile, slot):
        base = tile * tm

        def body(r, c):
            pltpu.make_async_copy(ys_ref.at[pos1_ref[base + r]], buf.at[slot, r],
                                  sem.at[slot]).start(priority=0)
            pltpu.make_async_copy(ys_ref.at[pos2_ref[base + r]], buf.at[slot, tm + r],
                                  sem.at[slot]).start(priority=1)
            return c

        lax.fori_loop(0, tm, body, 0, unroll=ISSUE_UNROLL)

    @pl.when(t == 0)
    def _():
        issue(0, 0)

    @pl.when(t + 1 < nt)
    def _():
        issue(t + 1, (t + 1) % 2)

    slot = t % 2
    pltpu.make_async_copy(ys_ref.at[pl.ds(0, 2 * tm)], buf.at[slot], sem.at[slot]).wait()

    y1 = buf[slot, 0:tm].astype(F32).reshape(tm, D_MODEL)
    y2 = buf[slot, tm:2 * tm].astype(F32).reshape(tm, D_MODEL)
    w = ew_ref[...]
    y = w[:, 0:1] * y1 + w[:, 1:2] * y2
    hfin = hres_ref[...] + g2_ref[...] * y
    ms = jnp.mean(hfin * hfin, axis=-1, keepdims=True)
    o_ref[...] = hfin * lax.rsqrt(ms + EPS) * gf_ref[...]


def _combine_call(pos1, pos2, ys_rows, ew, hres, gate2, gf):
    t, d = hres.shape
    tm = min(TOKENS_PER_STEP, t)
    per_batch = t // gate2.shape[0]
    assert per_batch % tm == 0
    tok = lambda w: pl.BlockSpec((tm, w), lambda i, *_: (i, 0))
    return pl.pallas_call(
        _combine_kernel,
        grid_spec=pltpu.PrefetchScalarGridSpec(
            num_scalar_prefetch=2,
            grid=(t // tm,),
            in_specs=[pl.BlockSpec(memory_space=pl.ANY), tok(LANES), tok(d),
                      pl.BlockSpec((None, 1, d), lambda i, *_: ((i * tm) // per_batch, 0, 0)),
                      pl.BlockSpec((1, d), lambda i, *_: (0, 0))],
            out_specs=tok(d),
            scratch_shapes=[pltpu.VMEM((2, 2 * tm) + ROW_TILE, BF16), pltpu.SemaphoreType.DMA((2,))]),
        out_shape=jax.ShapeDtypeStruct((t, d), F32),
        compiler_params=pltpu.CompilerParams(dimension_semantics=("arbitrary",),
                                             vmem_limit_bytes=VMEM_LIMIT),
        name="combine",
    )(pos1, pos2, ys_rows, ew, hres, gate2, gf)


def _slot_kernel(seg_ref, eid_ref, pos_ref):
    e = eid_ref[0:2, :]
    pos = eid_ref[2:4, :]
    for k in range(seg_ref.shape[0]):
        pos = pos + jnp.where(e == k, seg_ref[k], 0)
    pos_ref[...] = jnp.concatenate([pos, jnp.zeros((6, pos.shape[1]), I32)], axis=0)


def _slot_call(seg_start, eid):
    return pl.pallas_call(
        _slot_kernel,
        in_specs=[pl.BlockSpec(memory_space=pltpu.SMEM), pl.BlockSpec(eid.shape, lambda: (0, 0))],
        out_specs=pl.BlockSpec(eid.shape, lambda: (0, 0)),
        out_shape=jax.ShapeDtypeStruct(eid.shape, I32),
        name="slots",
    )(seg_start, eid)


def _sparse_moe(h2, eid, ew, counts, wg, wu, wd, hres, gate2, gf):
    bsz, s, d = hres.shape
    t = bsz * s
    ne = wg.shape[0]
    cnt = counts[0, :ne].astype(I32)
    seg_rows = ((cnt + EXPERT_ROWS - 1) // EXPERT_ROWS) * EXPERT_ROWS
    seg_end = jnp.cumsum(seg_rows)
    seg_start = seg_end - seg_rows
    pos = _slot_call(seg_start, eid)
    pos1, pos2 = pos[0], pos[1]
    n_tiles = (2 * t) // EXPERT_ROWS + ne
    tile_first_row = jnp.arange(n_tiles, dtype=I32) * EXPERT_ROWS
    tile_expert = jnp.minimum(jnp.sum(seg_end[None, :] <= tile_first_row[:, None], axis=1), ne - 1)
    n_used = (seg_end[ne - 1] // EXPERT_ROWS).reshape(1)
    n_rows = n_tiles * EXPERT_ROWS

    xs = _dispatch_call(pos1, pos2, h2.reshape((t,) + ROW_TILE), n_rows)
    ys = _expert_call(tile_expert.astype(I32), n_used.astype(I32), xs, wg, wu, wd)
    out = _combine_call(pos1, pos2, ys, ew.reshape(t, LANES), hres.reshape(t, d),
                        gate2.reshape(bsz, 1, d), gf.reshape(1, d))
    return out.reshape(bsz, s, d)


def kernel(x, c, w_ada, b_ada, norm1_g, w_in, rel_bias, conv_w, w_attn_branch, w_conv_branch, w_out,
           norm2_g, w_router_group, b_router_group, w_router_expert, b_router_expert,
           w_gate_e, w_up_e, w_down_e, norm_f_g):
    bsz, s, d = x.shape
    depth = w_ada.shape[0]
    h = x.astype(F32)
    c_pad = jnp.zeros((8, d), F32).at[:bsz].set(c.astype(F32))
    out = h
    for l in range(depth):
        mod = _ada_call(c_pad, w_ada[l].astype(F32), b_ada[l].astype(F32)[None, :])
        mod = mod[:bsz].reshape(bsz, 6, d)

        w = w_in[l]
        o_q, o_k, o_v, o_qi = 0, 512, 1024, 1536
        o_ki, o_wi, o_cb, o_gl = 2048, 2112, 2120, 3656
        wa = w[:, o_q:o_ki].astype(BF16)
        ki_w = w[:, o_ki:o_wi]
        wb = jnp.concatenate([ki_w, ki_w, w[:, o_wi:o_cb],
                              jnp.zeros((d, LANES - IDX_HEADS), w.dtype)], axis=1).astype(BF16)
        wc = w[:, o_cb:o_gl].astype(BF16)
        wd = w[:, o_gl:].astype(BF16)

        q, k, v, qi, kid, wi, z, gates = _proj_call(
            h, mod, norm1_g[l].astype(F32)[None, :], wa, wb, wc, wd, conv_w[l].astype(F32),
            tm=min(512, s))

        attn = _attn_call(rel_bias.astype(F32), q, qi, wi, k, v, kid, tq=min(TOPK_MAX, s // 4))

        wr32 = jnp.concatenate([w_router_expert[l], w_router_group[l],
                                jnp.zeros((d, LANES - N_EXPERTS - N_GROUPS), F32)], axis=1).astype(F32)
        wr_hi = wr32.astype(BF16)
        wr = jnp.stack([wr_hi, (wr32 - wr_hi.astype(F32)).astype(BF16)])
        br = jnp.concatenate([b_router_expert[l], b_router_group[l],
                              jnp.zeros((LANES - N_EXPERTS - N_GROUPS,), F32)])[None, :].astype(F32)
        hres, h2, eid, ew, counts = _post_call(
            attn, z, gates, h, mod, w_attn_branch[l].astype(BF16), w_conv_branch[l].astype(BF16),
            w_out[l].astype(BF16), norm2_g[l].astype(F32)[None, :], wr, br, tm=min(512, s))

        assert depth == 1, "the final RMSNorm is fused into the combine kernel of the only layer"
        out = _sparse_moe(h2, eid, ew, counts, w_gate_e[l].astype(F32), w_up_e[l].astype(F32),
                          w_down_e[l].astype(F32), hres, mod[:, 5, :], norm_f_g.astype(F32))
        h = out
    return out.astype(x.dtype)
```

```python
import functools
import math

import numpy as np
import jax
import jax.numpy as jnp
from jax import lax
from jax.experimental import pallas as pl
from jax.experimental.pallas import tpu as pltpu

F32 = jnp.float32
BF16 = jnp.bfloat16
I32 = jnp.int32

D_MODEL = 1024
N_HEADS = 8
HEAD_DIM = 64
ATTN_WIDTH = N_HEADS * HEAD_DIM
IDX_HEADS = 8
IDX_DIM = 64
TOPK_MAX = 256
N_BUCKETS = 32
MAX_DISTANCE = 128
CONV_WIDTH = 512
CONV_K = 3
N_GROUPS = 4
EXPERTS_PER_GROUP = 8
N_EXPERTS = N_GROUPS * EXPERTS_PER_GROUP
D_EXPERT = 512
EPS = 1e-6

LANES = 128
VMEM_LIMIT = 56 * 1024 * 1024

LOG2E = math.log2(math.e)
MASK_NEG = -1e30
SCORE_NEG = float(np.finfo(np.float32).min)
TINY_F32 = float(np.finfo(np.float32).tiny)
KEY_SPACE_AFTER = 40
RADIX_BITS_PER_STEP = 4


def _t5_bucket_starts():
    max_exact = N_BUCKETS // 2
    n = np.arange(0, MAX_DISTANCE + 1)
    nf = np.maximum(n, 1).astype(np.float32)
    val = (np.log(nf / np.float32(max_exact)) / np.float32(math.log(MAX_DISTANCE / max_exact))
           * np.float32(N_BUCKETS - max_exact)).astype(np.float32)
    inner = val[max_exact + 1:MAX_DISTANCE]
    assert np.min(np.abs(inner - np.round(inner))) > 1e-3
    large = np.minimum(max_exact + val.astype(np.int32), N_BUCKETS - 1)
    bucket = np.where(n < max_exact, n, large)
    assert bucket[MAX_DISTANCE] == N_BUCKETS - 1 and np.all(np.diff(bucket) >= 0)
    return [int(np.argmax(bucket >= b)) for b in range(N_BUCKETS)]


_BUCKET_START = _t5_bucket_starts()


def _ada_kernel(c_ref, w_ref, b_ref, o_ref):
    c = c_ref[...]
    ca = c * jax.nn.sigmoid(c)
    o_ref[...] = jnp.dot(ca, w_ref[...], preferred_element_type=F32,
                         precision=lax.Precision.HIGHEST) + b_ref[...]


def _ada_call(c_pad, w, b):
    rows, d = c_pad.shape
    n = w.shape[1]
    tn = 1536
    return pl.pallas_call(
        _ada_kernel,
        grid=(n // tn,),
        in_specs=[pl.BlockSpec((rows, d), lambda j: (0, 0)),
                  pl.BlockSpec((d, tn), lambda j: (0, j)),
                  pl.BlockSpec((1, tn), lambda j: (0, j))],
        out_specs=pl.BlockSpec((rows, tn), lambda j: (0, j)),
        out_shape=jax.ShapeDtypeStruct((rows, n), F32),
        compiler_params=pltpu.CompilerParams(dimension_semantics=("arbitrary",),
                                             vmem_limit_bytes=VMEM_LIMIT),
        name="ada",
    )(c_pad, w, b)


def _rms_mod(x, g, scale, shift):
    ms = jnp.mean(x * x, axis=-1, keepdims=True)
    y = x * lax.rsqrt(ms + EPS) * g
    return y * (1.0 + scale) + shift


def _proj_kernel(x_ref, mod_ref, g_ref, wa_ref, wb_ref, wc_ref, wd_ref, cw_ref,
                 q_ref, k_ref, v_ref, qi_ref, kid_ref, wi_ref, z_ref, gate_ref, ubuf, *, tm):
    h = _rms_mod(x_ref[...], g_ref[...], mod_ref[1:2, :], mod_ref[0:1, :])
    hb = h.astype(BF16)

    a = jnp.dot(hb, wa_ref[...], preferred_element_type=F32)
    q_ref[...] = (a[:, 0:512] * (HEAD_DIM ** -0.5 * LOG2E)).astype(BF16)
    k_ref[...] = a[:, 512:1024].astype(BF16)
    v_ref[...] = a[:, 1024:1536].astype(BF16)
    qi_ref[...] = a[:, 1536:2048].astype(BF16)

    b = jnp.dot(hb, wb_ref[...], preferred_element_type=F32)
    kid_ref[...] = b[:, 0:LANES].astype(BF16)
    wi_ref[...] = b[:, LANES:2 * LANES] * ((IDX_DIM ** -0.5) * (IDX_HEADS ** -0.5))

    cpart = jnp.dot(hb, wc_ref[...], preferred_element_type=F32)
    cb = cpart[:, 0:512]
    u = cpart[:, 512:1024] * cpart[:, 1024:1536]

    @pl.when(pl.program_id(1) == 0)
    def _():
        ubuf[0:8, :] = jnp.zeros((8, CONV_WIDTH), F32)

    ubuf[8:tm + 8, :] = u
    conv = (cw_ref[0:1, :] * ubuf[6:tm + 6, :] + cw_ref[1:2, :] * ubuf[7:tm + 7, :]
            + cw_ref[2:3, :] * u)
    z_ref[...] = (cb * conv).astype(BF16)
    ubuf[0:8, :] = ubuf[tm:tm + 8, :]

    d = jnp.dot(hb, wd_ref[...], preferred_element_type=F32)
    gate_ref[...] = jax.nn.sigmoid(d).astype(BF16)


def _proj_call(x, mod, g1, wa, wb, wc, wd, cw, tm):
    bsz, s, d = x.shape
    nt = s // tm
    tok = lambda w: pl.BlockSpec((None, tm, w), lambda b, t: (b, t, 0))
    full = lambda arr: pl.BlockSpec(arr.shape, lambda b, t: (0,) * arr.ndim)
    out_widths = [(512, BF16), (512, BF16), (512, BF16), (512, BF16), (LANES, BF16), (LANES, F32),
                  (512, BF16), (2 * D_MODEL, BF16)]
    return pl.pallas_call(
        functools.partial(_proj_kernel, tm=tm),
        grid=(bsz, nt),
        in_specs=[tok(d), pl.BlockSpec((None, 6, d), lambda b, t: (b, 0, 0)), full(g1),
                  full(wa), full(wb), full(wc), full(wd), full(cw)],
        out_specs=[tok(w) for w, _ in out_widths],
        out_shape=[jax.ShapeDtypeStruct((bsz, s, w), dt) for w, dt in out_widths],
        scratch_shapes=[pltpu.VMEM((tm + 8, CONV_WIDTH), F32)],
        compiler_params=pltpu.CompilerParams(dimension_semantics=("arbitrary", "arbitrary"),
                                             vmem_limit_bytes=VMEM_LIMIT),
        name="proj",
    )(x, mod, g1, wa, wb, wc, wd, cw)


def _to_key(f):
    b = pltpu.bitcast(f, I32)
    return jnp.where(b < 0, b ^ 0x7FFFFFFF, b)


def _from_key(k):
    return pltpu.bitcast(jnp.where(k < 0, k ^ 0x7FFFFFFF, k), F32)


def _bit_transpose32(words):
    a = list(words)
    j, m = 16, 0x0000FFFF
    while j:
        k = 0
        while k < 32:
            t = (a[k] ^ lax.shift_right_logical(a[k + j], jnp.int32(j))) & jnp.int32(m)
            a[k] = a[k] ^ t
            a[k + j] = a[k + j] ^ lax.shift_left(t, jnp.int32(j))
            k = (k + j + 1) & ~j
        j >>= 1
        m = (m ^ (m << j)) & 0xFFFFFFFF
    return a


def _fold_lanes(x, op):
    parts = [x[:, t * LANES:(t + 1) * LANES] for t in range(x.shape[1] // LANES)]
    while len(parts) > 1:
        nxt = [op(parts[a], parts[a + 1]) for a in range(0, len(parts) - 1, 2)]
        parts = nxt + ([parts[-1]] if len(parts) % 2 else [])
    return parts[0]


def _tile_lanes(x, n):
    return x if n == 1 else jnp.concatenate([x] * n, axis=1)


def _attn_kernel(rb_ref, q_ref, qi_ref, wi_ref, qin_ref, win_ref, k_ref, v_ref, kid_ref, o_ref,
                 sct_ref, planes_ref, cand_ref, prefix_ref, mb_ref, tb_ref, qm_ref, qim_ref, m_ref, l_ref, acc_ref,
                 *, tq, topk):
    c = tq
    i = pl.program_id(1)
    nt_dims = (((1,), (1,)), ((), ()))
    lane = lax.broadcasted_iota(I32, (tq, LANES), 1)
    low_half = lane < HEAD_DIM
    row = lax.broadcasted_iota(I32, (tq, c), 0)
    col = lax.broadcasted_iota(I32, (tq, c), 1)

    @pl.when((pl.program_id(0) == 0) & (i == 0))
    def _():
        for p in range(32):
            planes_ref[p] = jnp.zeros(planes_ref.shape[1:], I32)
        for near in range(2):
            rel = row - col + (1 - near) * c
            n = jnp.maximum(rel, 0)
            for h in range(N_HEADS):
                t = jnp.full((tq, c), rb_ref[0, h], F32)
                for b in range(1, N_BUCKETS):
                    t = jnp.where(n >= _BUCKET_START[b], rb_ref[b, h], t)
                t = (t - rb_ref[N_BUCKETS - 1, h]) * LOG2E
                tb_ref[h, :, near * c:(near + 1) * c] = jnp.where(rel >= 0, t, MASK_NEG)


    def masked_heads(src_ref, dst_ref):
        for h in range(N_HEADS):
            hp, half = divmod(h, 2)
            keep = low_half if half == 0 else jnp.logical_not(low_half)
            pair = src_ref[:, hp * LANES:(hp + 1) * LANES]
            dst_ref[h] = jnp.where(keep, pair, jnp.zeros_like(pair))

    krow = lax.broadcasted_iota(I32, (c, tq), 0)
    qcol = lax.broadcasted_iota(I32, (c, tq), 1)
    int_min = jnp.int32(-2 ** 31)

    def fold_rows(x, op):
        parts = [x[r * 8:(r + 1) * 8, :] for r in range(x.shape[0] // 8)]
        while len(parts) > 1:
            nxt = [op(parts[a], parts[a + 1]) for a in range(0, len(parts) - 1, 2)]
            parts = nxt + ([parts[-1]] if len(parts) % 2 else [])
        return parts[0]

    def score_chunk(j, diag, w_t):
        start = pl.multiple_of(j * c, c)
        kc = kid_ref[pl.ds(start, c), :]
        acc = jnp.zeros((c, tq), F32)
        for h in range(N_HEADS):
            d = lax.dot_general(kc, qim_ref[h], nt_dims, preferred_element_type=F32)
            acc = acc + jnp.maximum(d, 0.0) * w_t[h:h + 1, :]
        ukey = _to_key(acc) ^ int_min
        if diag:
            causal = krow <= qcol
            acc = jnp.where(causal, acc, SCORE_NEG)
            ukey = jnp.where(causal, ukey, 0)
        sct_ref[pl.ds(start, c), :] = acc
        prow = pl.multiple_of(j * (c // 32), c // 32)
        for lt in range(tq // LANES):
            words = [ukey[k * 8:(k + 1) * 8, lt * LANES:(lt + 1) * LANES] for k in range(32)]
            for p, plane in enumerate(_bit_transpose32(words)):
                planes_ref[p, pl.ds(prow, c // 32), lt * LANES:(lt + 1) * LANES] = plane

    @pl.when(i == 0)
    def _():
        masked_heads(qi_ref, qim_ref)
        score_chunk(0, True, jnp.transpose(wi_ref[...])[0:IDX_HEADS, :])

    nchunks = i + 1
    n_causal = i * tq + lax.broadcasted_iota(I32, (1, tq), 1) + 1
    done0 = jnp.where(n_causal <= topk, 1, 0)

    def radix_select(rows):
        prow_iota = lax.broadcasted_iota(I32, (rows, tq), 0)
        cand_ref[0:rows, :] = jnp.where(prow_iota < nchunks * (c // 32), -1, 0)

        def radix_cond(st):
            return (st[0] < 32) & (st[1] > 0)

        def radix_step(st):
            p0, _, prefix, need, ncand, settled = st
            cand = cand_ref[0:rows, :]
            for sub in range(RADIX_BITS_PER_STEP):
                p = p0 + sub
                plane = planes_ref[p, 0:rows, :]
                ones = cand & plane
                cnt1 = jnp.sum(fold_rows(lax.population_count(ones), jnp.add), axis=0, keepdims=True)
                take1 = cnt1 >= need
                live = settled == 0
                need = jnp.where(live & jnp.logical_not(take1), need - cnt1, need)
                ncand = jnp.where(live, jnp.where(take1, cnt1, ncand - cnt1), ncand)
                prefix = jnp.where(live & take1, prefix | jnp.left_shift(jnp.int32(1), 31 - p), prefix)
                cand = cand & (plane ^ jnp.where(take1, 0, -1))
                settled = jnp.where(ncand == need, 1, settled)
            cand_ref[0:rows, :] = cand
            return p0 + RADIX_BITS_PER_STEP, jnp.sum(1 - settled), prefix, need, ncand, settled

        _, _, prefix, _, _, _ = lax.while_loop(
            radix_cond, radix_step,
            (jnp.int32(0), jnp.sum(1 - done0), jnp.zeros((1, tq), I32), jnp.full((1, tq), topk, I32),
             jnp.full((1, tq), 1, I32) * (nchunks * c), done0))
        prefix_ref[...] = jnp.broadcast_to(prefix, prefix_ref.shape)

    all_rows = cand_ref.shape[0]
    lax.cond(nchunks * (c // 32) <= all_rows // 4, lambda: radix_select(all_rows // 4),
             lambda: lax.cond(nchunks * (c // 32) <= all_rows // 2, lambda: radix_select(all_rows // 2),
                              lambda: radix_select(all_rows)))
    prefix = prefix_ref[0:1, :]
    guess_key = prefix ^ int_min
    guess = _from_key(guess_key)

    def count_keys(pred):
        def body(j, part):
            start = pl.multiple_of(j * c, c)
            hit = pred(sct_ref[pl.ds(start, c), :], start)
            return part + fold_rows(jnp.where(hit, 1.0, 0.0), jnp.add)
        part = lax.fori_loop(0, nchunks, body, jnp.zeros((8, tq), F32))
        return jnp.sum(part, axis=0, keepdims=True)

    kf = float(topk)
    guess_above = _from_key(guess_key + 1)
    guess_above = jnp.where(jnp.abs(guess_above) < TINY_F32,
                            jnp.where(guess >= 0.0, TINY_F32, 0.0), guess_above)
    guess_below = _from_key(guess_key - 2)

    def cond(st):
        return st[0] > 0

    def step(st):
        _, it, lo, hi, c_lo, c_hi, done = st
        mid = 0.5 * lo + 0.5 * hi
        lo_k, hi_k = _to_key(lo), _to_key(hi)
        mid_key = _from_key((lo_k >> 1) + (hi_k >> 1) + (lo_k & hi_k & 1))
        mid = jnp.where(it >= KEY_SPACE_AFTER, mid_key, mid)
        probe = jnp.where(it == 0, guess, jnp.where(lo == guess, guess_above, guess_below))
        mid = jnp.where((it <= 1) & (probe > lo) & (probe < hi), probe, mid)
        go = (done == 0) & (mid > lo) & (mid < hi)
        cnt = count_keys(lambda tile, start: tile >= mid)
        ge = cnt >= kf
        up = ge & go
        dn = jnp.logical_not(ge) & go
        lo = jnp.where(up, mid, lo)
        hi = jnp.where(dn, mid, hi)
        c_lo = jnp.where(up, cnt, c_lo)
        c_hi = jnp.where(dn, cnt, c_hi)
        adjacent = (lo == guess) & (hi == guess_above)
        done = jnp.where(go & (c_lo != kf) & jnp.logical_not(adjacent), 0, 1)
        return jnp.sum(1 - done), it + 1, lo, hi, c_lo, c_hi, done

    _, _, thr, _, c_lo, c_hi, _ = lax.while_loop(
        cond, step, (jnp.sum(1 - done0), jnp.int32(0), jnp.full((1, tq), SCORE_NEG, F32),
                     jnp.full((1, tq), -SCORE_NEG, F32), jnp.full((1, tq), 1.0, F32) * (nchunks * c),
                     jnp.zeros((1, tq), F32), done0))

    def write_mask(selected):
        def tiles(start, rows):
            tile = sct_ref[pl.ds(start, rows), :]
            key = lax.broadcasted_iota(I32, (rows, tq), 0) + start
            mb_ref[:, pl.ds(start, rows)] = jnp.transpose(
                jnp.where(selected(tile, key), 0.0, MASK_NEG))

        def body(j, carry):
            tiles(pl.multiple_of(j * 2 * c, 2 * c), 2 * c)
            return carry

        lax.fori_loop(0, nchunks // 2, body, 0)

        @pl.when(nchunks % 2 == 1)
        def _():
            tiles(pl.multiple_of((nchunks - 1) * c, c), c)

    tied = (n_causal > topk) & (c_lo > kf)
    any_tied = jnp.sum(jnp.where(tied, 1, 0)) > 0

    @pl.when(any_tied)
    def _():
        n_take = kf - c_hi

        def bis(_, st):
            lo_x, hi_x = st
            mid = (lo_x + hi_x) >> 1
            ok = count_keys(lambda tile, start: (tile == thr) & ((krow + start) <= mid)) >= n_take
            return jnp.where(ok, lo_x, mid), jnp.where(ok, mid, hi_x)

        nbits = int(math.ceil(math.log2(sct_ref.shape[0]))) + 1
        _, hi_x = lax.fori_loop(0, nbits, bis, (jnp.full((1, tq), -1, I32),
                                                jnp.full((1, tq), sct_ref.shape[0], I32)))
        cut = jnp.where(tied, hi_x, jnp.int32(2 ** 30))

        write_mask(lambda tile, key: (tile > thr) | ((tile == thr) & (key <= cut)))

    @pl.when(jnp.logical_not(any_tied))
    def _():
        write_mask(lambda tile, key: tile >= thr)

    masked_heads(q_ref, qm_ref)
    masked_heads(qin_ref, qim_ref)
    w_t_next = jnp.transpose(win_ref[...])[0:IDX_HEADS, :]
    has_next = i + 1 < pl.num_programs(1)
    m_ref[...] = jnp.full(m_ref.shape, MASK_NEG, F32)
    l_ref[...] = jnp.zeros(l_ref.shape, F32)
    acc_ref[...] = jnp.zeros(acc_ref.shape, F32)

    def attend(start, width, near):
        nt = width // LANES
        ones = jnp.ones((width, LANES), BF16)
        for h in range(N_HEADS):
            hp = h // 2
            kp = k_ref[pl.ds(start, width), hp * LANES:(hp + 1) * LANES]
            vp = v_ref[pl.ds(start, width), hp * LANES:(hp + 1) * LANES]
            s = lax.dot_general(qm_ref[h], kp, nt_dims, preferred_element_type=F32)
            s = s + mb_ref[:, pl.ds(start, width)]
            if near:
                s = s + tb_ref[h, :, 2 * c - width:2 * c]
            m_old = m_ref[h]
            m_new = jnp.maximum(m_old, jnp.max(_fold_lanes(s, jnp.maximum), axis=1, keepdims=True))
            p = jnp.exp2(s - _tile_lanes(m_new, nt)).astype(BF16)
            pv = jnp.dot(p, jnp.concatenate([vp, ones], axis=1), preferred_element_type=F32)
            alpha = jnp.exp2(m_old - m_new)
            m_ref[h] = m_new
            l_ref[h] = alpha * l_ref[h] + pv[:, LANES:]
            acc_ref[h] = alpha * acc_ref[h] + pv[:, :LANES]

    wide = 2 * c
    n_far = jnp.maximum(i - 1, 0)

    n_wide = n_far // 2

    def far_body(j, _):
        score_chunk(2 * j, False, w_t_next)
        attend(pl.multiple_of(j * wide, wide), wide, False)
        score_chunk(2 * j + 1, False, w_t_next)
        return 0

    lax.fori_loop(0, n_wide, far_body, 0)

    @pl.when(n_far % 2 == 1)
    def _():
        attend(pl.multiple_of((n_far - 1) * c, c), c, False)
        score_chunk(n_far - 1, False, w_t_next)

    @pl.when(i >= 1)
    def _():
        score_chunk(i - 1, False, w_t_next)
        attend(pl.multiple_of((i - 1) * c, c), wide, True)
        score_chunk(i, False, w_t_next)

    @pl.when(i == 0)
    def _():
        attend(0, c, True)
        score_chunk(0, False, w_t_next)

    @pl.when(has_next)
    def _():
        score_chunk(i + 1, True, w_t_next)

    for hp in range(N_HEADS // 2):
        lo = acc_ref[2 * hp] / l_ref[2 * hp]
        hi = acc_ref[2 * hp + 1] / l_ref[2 * hp + 1]
        o_ref[:, hp * LANES:(hp + 1) * LANES] = jnp.where(low_half, lo, hi).astype(BF16)


def _attn_call(rel_bias, q, qi, wi, k, v, kid, tq):
    bsz, s, _ = q.shape
    nq = s // tq
    topk = min(TOPK_MAX, s // 4)
    assert tq % LANES == 0 and tq >= MAX_DISTANCE
    blk = lambda w: pl.BlockSpec((None, tq, w), lambda b, i: (b, i, 0))
    nxt = lambda w: pl.BlockSpec((None, tq, w), lambda b, i: (b, jnp.minimum(i + 1, nq - 1), 0))
    whole = lambda w: pl.BlockSpec((None, s, w), lambda b, i: (b, 0, 0), pipeline_mode=pl.Buffered(1))
    return pl.pallas_call(
        functools.partial(_attn_kernel, tq=tq, topk=topk),
        grid=(bsz, nq),
        in_specs=[pl.BlockSpec(memory_space=pltpu.SMEM),
                  blk(ATTN_WIDTH), blk(ATTN_WIDTH), blk(LANES), nxt(ATTN_WIDTH), nxt(LANES),
                  whole(ATTN_WIDTH), whole(ATTN_WIDTH), whole(LANES)],
        out_specs=blk(ATTN_WIDTH),
        out_shape=jax.ShapeDtypeStruct((bsz, s, ATTN_WIDTH), BF16),
        scratch_shapes=[
            pltpu.VMEM((s, tq), F32),
            pltpu.VMEM((32, s // 32, tq), I32),
            pltpu.VMEM((s // 32, tq), I32),
            pltpu.VMEM((8, tq), I32),
            pltpu.VMEM((tq, s), F32),
            pltpu.VMEM((N_HEADS, tq, 2 * tq), F32),
            pltpu.VMEM((N_HEADS, tq, LANES), BF16),
            pltpu.VMEM((N_HEADS, tq, LANES), BF16),
            pltpu.VMEM((N_HEADS, tq, LANES), F32),
            pltpu.VMEM((N_HEADS, tq, LANES), F32),
            pltpu.VMEM((N_HEADS, tq, LANES), F32),
        ],
        compiler_params=pltpu.CompilerParams(dimension_semantics=("arbitrary", "arbitrary"),
                                             vmem_limit_bytes=VMEM_LIMIT),
        name="attn",
    )(rel_bias, q, qi, wi, qi, wi, k, v, kid)


def _post_kernel(attn_ref, z_ref, gate_ref, x_ref, mod_ref, wa_ref, wc_ref, wo_ref, g2_ref,
                 wr_ref, br_ref, hres_ref, h2_ref, eid_ref, ew_ref, counts_ref, cnt_ref):
    ya = jnp.dot(attn_ref[...], wa_ref[...], preferred_element_type=F32)
    yc = jnp.dot(z_ref[...], wc_ref[...], preferred_element_type=F32)
    merged = gate_ref[:, 0:D_MODEL].astype(F32) * ya + gate_ref[:, D_MODEL:].astype(F32) * yc
    o = jnp.dot(merged.astype(BF16), wo_ref[...], preferred_element_type=F32)
    hres = x_ref[...] + mod_ref[2:3, :] * o
    hres_ref[...] = hres
    h2 = _rms_mod(hres, g2_ref[...], mod_ref[4:5, :], mod_ref[3:4, :])
    h2_ref[...] = h2.astype(BF16).reshape(h2_ref.shape)

    h2_hi = h2.astype(BF16)
    h2_lo = (h2 - h2_hi.astype(F32)).astype(BF16)
    lg = (jnp.dot(h2_hi, wr_ref[0], preferred_element_type=F32)
          + (jnp.dot(h2_lo, wr_ref[0], preferred_element_type=F32)
             + jnp.dot(h2_hi, wr_ref[1], preferred_element_type=F32))) + br_ref[...]
    tm = lg.shape[0]
    lgt = jnp.transpose(lg)
    ninf = -jnp.inf
    grow = lax.broadcasted_iota(I32, (8, tm), 0)
    gl = jnp.where(grow < N_GROUPS, lgt[N_EXPERTS:N_EXPERTS + 8, :], ninf)
    gmax = jnp.max(gl, axis=0, keepdims=True)
    g_sel = jnp.min(jnp.where(gl == gmax, grow, N_GROUPS), axis=0, keepdims=True)
    g_w = 1.0 / jnp.sum(jnp.exp(gl - gmax), axis=0, keepdims=True)

    erow = lax.broadcasted_iota(I32, (N_EXPERTS, tm), 0)
    emask = (erow // EXPERTS_PER_GROUP) == g_sel
    el = jnp.where(emask, lgt[0:N_EXPERTS, :], ninf)
    emax = jnp.max(el, axis=0, keepdims=True)
    ee = jnp.exp(el - emax)
    e_prob = ee / jnp.sum(ee, axis=0, keepdims=True)
    p1 = jnp.max(jnp.where(emask, e_prob, -1.0), axis=0, keepdims=True)
    i1 = jnp.min(jnp.where(emask & (e_prob == p1), erow, LANES), axis=0, keepdims=True)
    rest = emask & (erow != i1)
    p2 = jnp.max(jnp.where(rest, e_prob, -1.0), axis=0, keepdims=True)
    i2 = jnp.min(jnp.where(rest & (e_prob == p2), erow, LANES), axis=0, keepdims=True)
    psum = p1 + p2
    w1 = g_w * (p1 / psum)
    w2 = g_w * (p2 / psum)
    ew_ref[...] = jnp.transpose(jnp.concatenate([w1, w2, jnp.zeros((LANES - 2, tm), F32)], axis=0))

    @pl.when((pl.program_id(0) == 0) & (pl.program_id(1) == 0))
    def _():
        cnt_ref[...] = jnp.zeros(cnt_ref.shape, F32)

    onehot_t = jnp.where((erow == i1) | (erow == i2), 1.0, 0.0).astype(BF16)
    upper = (lax.broadcasted_iota(I32, (tm, tm), 0) <= lax.broadcasted_iota(I32, (tm, tm), 1))
    cum_t = jnp.dot(onehot_t, jnp.where(upper, 1.0, 0.0).astype(BF16), preferred_element_type=F32)
    cnt_col = jnp.transpose(cnt_ref[...])[0:N_EXPERTS, 0:1]
    before = cum_t - onehot_t.astype(F32) + cnt_col
    r1 = jnp.sum(jnp.where(erow == i1, before, 0.0), axis=0, keepdims=True).astype(I32)
    r2 = jnp.sum(jnp.where(erow == i2, before, 0.0), axis=0, keepdims=True).astype(I32)
    eid_ref[...] = jnp.concatenate([i1, i2, r1, r2, jnp.zeros((4, tm), I32)], axis=0)
    padded = jnp.concatenate([onehot_t, jnp.zeros((LANES - N_EXPERTS, tm), BF16)], axis=0)
    tile_total = lax.dot_general(jnp.ones((8, tm), BF16), padded, (((1,), (1,)), ((), ())),
                                 preferred_element_type=F32)
    total = cnt_ref[...] + tile_total
    cnt_ref[...] = total
    counts_ref[...] = total


def _post_call(attn, z, gates, x, mod, wa, wc, wo, g2, wr, br, tm):
    bsz, s, d = x.shape
    nt = s // tm
    tok = lambda w: pl.BlockSpec((None, tm, w), lambda b, t: (b, t, 0))
    full = lambda arr: pl.BlockSpec(arr.shape, lambda b, t: (0,) * arr.ndim)
    return pl.pallas_call(
        _post_kernel,
        grid=(bsz, nt),
        in_specs=[tok(ATTN_WIDTH), tok(CONV_WIDTH), tok(2 * d), tok(d),
                  pl.BlockSpec((None, 6, d), lambda b, t: (b, 0, 0)),
                  full(wa), full(wc), full(wo), full(g2), full(wr), full(br)],
        out_specs=[tok(d), pl.BlockSpec((None, tm) + ROW_TILE, lambda b, t: (b, t, 0, 0)),
                   pl.BlockSpec((8, tm), lambda b, t: (0, b * nt + t)), tok(LANES),
                   pl.BlockSpec((8, LANES), lambda b, t: (0, 0))],
        out_shape=[jax.ShapeDtypeStruct((bsz, s, d), F32),
                   jax.ShapeDtypeStruct((bsz, s) + ROW_TILE, BF16),
                   jax.ShapeDtypeStruct((8, bsz * s), I32), jax.ShapeDtypeStruct((bsz, s, LANES), F32),
                   jax.ShapeDtypeStruct((8, LANES), F32)],
        scratch_shapes=[pltpu.VMEM((8, LANES), F32)],
        compiler_params=pltpu.CompilerParams(dimension_semantics=("arbitrary", "arbitrary"),
                                             vmem_limit_bytes=VMEM_LIMIT),
        name="post",
    )(attn, z, gates, x, mod, wa, wc, wo, g2, wr, br)


ROW_TILE = (D_MODEL // LANES, LANES)
EXPERT_ROWS = 256
TOKENS_PER_STEP = 256


ISSUE_UNROLL = 8


def _dispatch_kernel(pos1_ref, pos2_ref, h2_ref, xs_in_ref, xs_ref, sem):
    del xs_in_ref
    tm = h2_ref.shape[0]
    base = pl.program_id(0) * tm

    def issue(r, c):
        pltpu.make_async_copy(h2_ref.at[r], xs_ref.at[pos1_ref[base + r]], sem).start(priority=0)
        pltpu.make_async_copy(h2_ref.at[r], xs_ref.at[pos2_ref[base + r]], sem).start(priority=1)
        return c

    lax.fori_loop(0, tm, issue, 0, unroll=ISSUE_UNROLL)
    for _ in range(2):
        pltpu.make_async_copy(h2_ref, xs_ref.at[pl.ds(0, tm)], sem).wait()


def _dispatch_call(pos1, pos2, h2_rows, n_rows):
    t = h2_rows.shape[0]
    tm = min(TOKENS_PER_STEP, t)
    xs0 = jnp.zeros((n_rows,) + ROW_TILE, BF16)
    return pl.pallas_call(
        _dispatch_kernel,
        grid_spec=pltpu.PrefetchScalarGridSpec(
            num_scalar_prefetch=2,
            grid=(t // tm,),
            in_specs=[pl.BlockSpec((tm,) + ROW_TILE, lambda i, p1, p2: (i, 0, 0)),
                      pl.BlockSpec(memory_space=pl.ANY)],
            out_specs=pl.BlockSpec(memory_space=pl.ANY),
            scratch_shapes=[pltpu.SemaphoreType.DMA(())]),
        out_shape=jax.ShapeDtypeStruct((n_rows,) + ROW_TILE, BF16),
        input_output_aliases={3: 0},
        compiler_params=pltpu.CompilerParams(dimension_semantics=("arbitrary",),
                                             vmem_limit_bytes=VMEM_LIMIT),
        name="dispatch",
    )(pos1, pos2, h2_rows, xs0)


def _expert_kernel(te_ref, nu_ref, xs_ref, wg_ref, wu_ref, wd_ref, ys_ref, wgb, wub, wdb):
    g = pl.program_id(0)
    e = te_ref[g]
    e_prev = te_ref[jnp.maximum(g - 1, 0)]

    @pl.when((g == 0) | (e != e_prev))
    def _():
        wgb[...] = wg_ref[...].astype(BF16)
        wub[...] = wu_ref[...].astype(BF16)
        wdb[...] = wd_ref[...].astype(BF16)

    @pl.when(g < nu_ref[0])
    def _():
        x = xs_ref[...].reshape(xs_ref.shape[0], D_MODEL)
        a = jnp.dot(x, wgb[...], preferred_element_type=F32)
        b = jnp.dot(x, wub[...], preferred_element_type=F32)
        hid = (a * jax.nn.sigmoid(a)) * b
        y = jnp.dot(hid.astype(BF16), wdb[...], preferred_element_type=F32)
        ys_ref[...] = y.astype(BF16).reshape(ys_ref.shape)

    @pl.when(g >= nu_ref[0])
    def _():
        ys_ref[...] = jnp.zeros(ys_ref.shape, BF16)


def _expert_call(tile_expert, n_used, xs, wg, wu, wd):
    n_rows, d = xs.shape[0], D_MODEL
    nt = n_rows // EXPERT_ROWS
    rows = pl.BlockSpec((EXPERT_ROWS,) + ROW_TILE, lambda g, te, nu: (g, 0, 0))
    wspec = lambda shp: pl.BlockSpec((None,) + shp, lambda g, te, nu: (te[g], 0, 0))
    return pl.pallas_call(
        _expert_kernel,
        grid_spec=pltpu.PrefetchScalarGridSpec(
            num_scalar_prefetch=2,
            grid=(nt,),
            in_specs=[rows, wspec((d, D_EXPERT)), wspec((d, D_EXPERT)), wspec((D_EXPERT, d))],
            out_specs=rows,
            scratch_shapes=[pltpu.VMEM((d, D_EXPERT), BF16), pltpu.VMEM((d, D_EXPERT), BF16),
                            pltpu.VMEM((D_EXPERT, d), BF16)]),
        out_shape=jax.ShapeDtypeStruct((n_rows,) + ROW_TILE, BF16),
        compiler_params=pltpu.CompilerParams(dimension_semantics=("arbitrary",),
                                             vmem_limit_bytes=VMEM_LIMIT),
        name="experts",
    )(tile_expert, n_used, xs, wg, wu, wd)


def _combine_kernel(pos1_ref, pos2_ref, ys_ref, ew_ref, hres_ref, g2_ref, gf_ref, o_ref, buf, sem):
    tm = hres_ref.shape[0]
    t = pl.program_id(0)
    nt = pl.num_programs(0)

    def issue(tile, slot):
        base = tile * tm

        def body(r, c):
            pltpu.make_async_copy(ys_ref.at[pos1_ref[base + r]], buf.at[slot, r],
                                  sem.at[slot]).start(priority=0)
            pltpu.make_async_copy(ys_ref.at[pos2_ref[base + r]], buf.at[slot, tm + r],
                                  sem.at[slot]).start(priority=1)
            return c

        lax.fori_loop(0, tm, body, 0, unroll=ISSUE_UNROLL)

    @pl.when(t == 0)
    def _():
        issue(0, 0)

    @pl.when(t + 1 < nt)
    def _():
        issue(t + 1, (t + 1) % 2)

    slot = t % 2
    pltpu.make_async_copy(ys_ref.at[pl.ds(0, 2 * tm)], buf.at[slot], sem.at[slot]).wait()

    y1 = buf[slot, 0:tm].astype(F32).reshape(tm, D_MODEL)
    y2 = buf[slot, tm:2 * tm].astype(F32).reshape(tm, D_MODEL)
    w = ew_ref[...]
    y = w[:, 0:1] * y1 + w[:, 1:2] * y2
    hfin = hres_ref[...] + g2_ref[...] * y
    ms = jnp.mean(hfin * hfin, axis=-1, keepdims=True)
    o_ref[...] = hfin * lax.rsqrt(ms + EPS) * gf_ref[...]


def _combine_call(pos1, pos2, ys_rows, ew, hres, gate2, gf):
    t, d = hres.shape
    tm = min(TOKENS_PER_STEP, t)
    per_batch = t // gate2.shape[0]
    assert per_batch % tm == 0
    tok = lambda w: pl.BlockSpec((tm, w), lambda i, *_: (i, 0))
    return pl.pallas_call(
        _combine_kernel,
        grid_spec=pltpu.PrefetchScalarGridSpec(
            num_scalar_prefetch=2,
            grid=(t // tm,),
            in_specs=[pl.BlockSpec(memory_space=pl.ANY), tok(LANES), tok(d),
                      pl.BlockSpec((None, 1, d), lambda i, *_: ((i * tm) // per_batch, 0, 0)),
                      pl.BlockSpec((1, d), lambda i, *_: (0, 0))],
            out_specs=tok(d),
            scratch_shapes=[pltpu.VMEM((2, 2 * tm) + ROW_TILE, BF16), pltpu.SemaphoreType.DMA((2,))]),
        out_shape=jax.ShapeDtypeStruct((t, d), F32),
        compiler_params=pltpu.CompilerParams(dimension_semantics=("arbitrary",),
                                             vmem_limit_bytes=VMEM_LIMIT),
        name="combine",
    )(pos1, pos2, ys_rows, ew, hres, gate2, gf)


def _slot_kernel(seg_ref, eid_ref, pos_ref):
    e = eid_ref[0:2, :]
    pos = eid_ref[2:4, :]
    for k in range(seg_ref.shape[0]):
        pos = pos + jnp.where(e == k, seg_ref[k], 0)
    pos_ref[...] = jnp.concatenate([pos, jnp.zeros((6, pos.shape[1]), I32)], axis=0)


def _slot_call(seg_start, eid):
    return pl.pallas_call(
        _slot_kernel,
        in_specs=[pl.BlockSpec(memory_space=pltpu.SMEM), pl.BlockSpec(eid.shape, lambda: (0, 0))],
        out_specs=pl.BlockSpec(eid.shape, lambda: (0, 0)),
        out_shape=jax.ShapeDtypeStruct(eid.shape, I32),
        name="slots",
    )(seg_start, eid)


def _sparse_moe(h2, eid, ew, counts, wg, wu, wd, hres, gate2, gf):
    bsz, s, d = hres.shape
    t = bsz * s
    ne = wg.shape[0]
    cnt = counts[0, :ne].astype(I32)
    seg_rows = ((cnt + EXPERT_ROWS - 1) // EXPERT_ROWS) * EXPERT_ROWS
    seg_end = jnp.cumsum(seg_rows)
    seg_start = seg_end - seg_rows
    pos = _slot_call(seg_start, eid)
    pos1, pos2 = pos[0], pos[1]
    n_tiles = (2 * t) // EXPERT_ROWS + ne
    tile_first_row = jnp.arange(n_tiles, dtype=I32) * EXPERT_ROWS
    tile_expert = jnp.minimum(jnp.sum(seg_end[None, :] <= tile_first_row[:, None], axis=1), ne - 1)
    n_used = (seg_end[ne - 1] // EXPERT_ROWS).reshape(1)
    n_rows = n_tiles * EXPERT_ROWS

    xs = _dispatch_call(pos1, pos2, h2.reshape((t,) + ROW_TILE), n_rows)
    ys = _expert_call(tile_expert.astype(I32), n_used.astype(I32), xs, wg, wu, wd)
    out = _combine_call(pos1, pos2, ys, ew.reshape(t, LANES), hres.reshape(t, d),
                        gate2.reshape(bsz, 1, d), gf.reshape(1, d))
    return out.reshape(bsz, s, d)


def kernel(x, c, w_ada, b_ada, norm1_g, w_in, rel_bias, conv_w, w_attn_branch, w_conv_branch, w_out,
           norm2_g, w_router_group, b_router_group, w_router_expert, b_router_expert,
           w_gate_e, w_up_e, w_down_e, norm_f_g):
    bsz, s, d = x.shape
    depth = w_ada.shape[0]
    h = x.astype(F32)
    c_pad = jnp.zeros((8, d), F32).at[:bsz].set(c.astype(F32))
    out = h
    for l in range(depth):
        mod = _ada_call(c_pad, w_ada[l].astype(F32), b_ada[l].astype(F32)[None, :])
        mod = mod[:bsz].reshape(bsz, 6, d)

        w = w_in[l]
        o_q, o_k, o_v, o_qi = 0, 512, 1024, 1536
        o_ki, o_wi, o_cb, o_gl = 2048, 2112, 2120, 3656
        wa = w[:, o_q:o_ki].astype(BF16)
        ki_w = w[:, o_ki:o_wi]
        wb = jnp.concatenate([ki_w, ki_w, w[:, o_wi:o_cb],
                              jnp.zeros((d, LANES - IDX_HEADS), w.dtype)], axis=1).astype(BF16)
        wc = w[:, o_cb:o_gl].astype(BF16)
        wd = w[:, o_gl:].astype(BF16)

        q, k, v, qi, kid, wi, z, gates = _proj_call(
            h, mod, norm1_g[l].astype(F32)[None, :], wa, wb, wc, wd, conv_w[l].astype(F32),
            tm=min(512, s))

        attn = _attn_call(rel_bias.astype(F32), q, qi, wi, k, v, kid, tq=min(TOPK_MAX, s // 4))

        wr32 = jnp.concatenate([w_router_expert[l], w_router_group[l],
                                jnp.zeros((d, LANES - N_EXPERTS - N_GROUPS), F32)], axis=1).astype(F32)
        wr_hi = wr32.astype(BF16)
        wr = jnp.stack([wr_hi, (wr32 - wr_hi.astype(F32)).astype(BF16)])
        br = jnp.concatenate([b_router_expert[l], b_router_group[l],
                              jnp.zeros((LANES - N_EXPERTS - N_GROUPS,), F32)])[None, :].astype(F32)
        hres, h2, eid, ew, counts = _post_call(
            attn, z, gates, h, mod, w_attn_branch[l].astype(BF16), w_conv_branch[l].astype(BF16),
            w_out[l].astype(BF16), norm2_g[l].astype(F32)[None, :], wr, br, tm=min(512, s))

        assert depth == 1, "the final RMSNorm is fused into the combine kernel of the only layer"
        out = _sparse_moe(h2, eid, ew, counts, w_gate_e[l].astype(F32), w_up_e[l].astype(F32),
                          w_down_e[l].astype(F32), hres, mod[:, 5, :], norm_f_g.astype(F32))
        h = out
    return out.astype(x.dtype)
```

```python
import functools
import math

import numpy as np
import jax
import jax.numpy as jnp
from jax import lax
from jax.experimental import pallas as pl
from jax.experimental.pallas import tpu as pltpu

F32 = jnp.float32
BF16 = jnp.bfloat16
I32 = jnp.int32

D_MODEL = 1024
N_HEADS = 8
HEAD_DIM = 64
ATTN_WIDTH = N_HEADS * HEAD_DIM
IDX_HEADS = 8
IDX_DIM = 64
TOPK_MAX = 256
N_BUCKETS = 32
MAX_DISTANCE = 128
CONV_WIDTH = 512
CONV_K = 3
N_GROUPS = 4
EXPERTS_PER_GROUP = 8
N_EXPERTS = N_GROUPS * EXPERTS_PER_GROUP
D_EXPERT = 512
EPS = 1e-6

LANES = 128
VMEM_LIMIT = 56 * 1024 * 1024

LOG2E = math.log2(math.e)
MASK_NEG = -1e30
SCORE_NEG = float(np.finfo(np.float32).min)
TINY_F32 = float(np.finfo(np.float32).tiny)
KEY_SPACE_AFTER = 40
RADIX_BITS_PER_STEP = 4


def _t5_bucket_starts():
    max_exact = N_BUCKETS // 2
    n = np.arange(0, MAX_DISTANCE + 1)
    nf = np.maximum(n, 1).astype(np.float32)
    val = (np.log(nf / np.float32(max_exact)) / np.float32(math.log(MAX_DISTANCE / max_exact))
           * np.float32(N_BUCKETS - max_exact)).astype(np.float32)
    inner = val[max_exact + 1:MAX_DISTANCE]
    assert np.min(np.abs(inner - np.round(inner))) > 1e-3
    large = np.minimum(max_exact + val.astype(np.int32), N_BUCKETS - 1)
    bucket = np.where(n < max_exact, n, large)
    assert bucket[MAX_DISTANCE] == N_BUCKETS - 1 and np.all(np.diff(bucket) >= 0)
    return [int(np.argmax(bucket >= b)) for b in range(N_BUCKETS)]


_BUCKET_START = _t5_bucket_starts()


def _ada_kernel(c_ref, w_ref, b_ref, o_ref):
    c = c_ref[...]
    ca = c * jax.nn.sigmoid(c)
    o_ref[...] = jnp.dot(ca, w_ref[...], preferred_element_type=F32,
                         precision=lax.Precision.HIGHEST) + b_ref[...]


def _ada_call(c_pad, w, b):
    rows, d = c_pad.shape
    n = w.shape[1]
    tn = 1536
    return pl.pallas_call(
        _ada_kernel,
        grid=(n // tn,),
        in_specs=[pl.BlockSpec((rows, d), lambda j: (0, 0)),
                  pl.BlockSpec((d, tn), lambda j: (0, j)),
                  pl.BlockSpec((1, tn), lambda j: (0, j))],
        out_specs=pl.BlockSpec((rows, tn), lambda j: (0, j)),
        out_shape=jax.ShapeDtypeStruct((rows, n), F32),
        compiler_params=pltpu.CompilerParams(dimension_semantics=("arbitrary",),
                                             vmem_limit_bytes=VMEM_LIMIT),
        name="ada",
    )(c_pad, w, b)


def _rms_mod(x, g, scale, shift):
    ms = jnp.mean(x * x, axis=-1, keepdims=True)
    y = x * lax.rsqrt(ms + EPS) * g
    return y * (1.0 + scale) + shift


def _proj_kernel(x_ref, mod_ref, g_ref, wa_ref, wb_ref, wc_ref, wd_ref, cw_ref,
                 q_ref, k_ref, v_ref, qi_ref, kid_ref, wi_ref, z_ref, gate_ref, ubuf, *, tm):
    h = _rms_mod(x_ref[...], g_ref[...], mod_ref[1:2, :], mod_ref[0:1, :])
    hb = h.astype(BF16)

    a = jnp.dot(hb, wa_ref[...], preferred_element_type=F32)
    q_ref[...] = (a[:, 0:512] * (HEAD_DIM ** -0.5 * LOG2E)).astype(BF16)
    k_ref[...] = a[:, 512:1024].astype(BF16)
    v_ref[...] = a[:, 1024:1536].astype(BF16)
    qi_ref[...] = a[:, 1536:2048].astype(BF16)

    b = jnp.dot(hb, wb_ref[...], preferred_element_type=F32)
    kid_ref[...] = b[:, 0:LANES].astype(BF16)
    wi_ref[...] = b[:, LANES:2 * LANES] * ((IDX_DIM ** -0.5) * (IDX_HEADS ** -0.5))

    cpart = jnp.dot(hb, wc_ref[...], preferred_element_type=F32)
    cb = cpart[:, 0:512]
    u = cpart[:, 512:1024] * cpart[:, 1024:1536]

    @pl.when(pl.program_id(1) == 0)
    def _():
        ubuf[0:8, :] = jnp.zeros((8, CONV_WIDTH), F32)

    ubuf[8:tm + 8, :] = u
    conv = (cw_ref[0:1, :] * ubuf[6:tm + 6, :] + cw_ref[1:2, :] * ubuf[7:tm + 7, :]
            + cw_ref[2:3, :] * u)
    z_ref[...] = (cb * conv).astype(BF16)
    ubuf[0:8, :] = ubuf[tm:tm + 8, :]

    d = jnp.dot(hb, wd_ref[...], preferred_element_type=F32)
    gate_ref[...] = jax.nn.sigmoid(d).astype(BF16)


def _proj_call(x, mod, g1, wa, wb, wc, wd, cw, tm):
    bsz, s, d = x.shape
    nt = s // tm
    tok = lambda w: pl.BlockSpec((None, tm, w), lambda b, t: (b, t, 0))
    full = lambda arr: pl.BlockSpec(arr.shape, lambda b, t: (0,) * arr.ndim)
    out_widths = [(512, BF16), (512, BF16), (512, BF16), (512, BF16), (LANES, BF16), (LANES, F32),
                  (512, BF16), (2 * D_MODEL, BF16)]
    return pl.pallas_call(
        functools.partial(_proj_kernel, tm=tm),
        grid=(bsz, nt),
        in_specs=[tok(d), pl.BlockSpec((None, 6, d), lambda b, t: (b, 0, 0)), full(g1),
                  full(wa), full(wb), full(wc), full(wd), full(cw)],
        out_specs=[tok(w) for w, _ in out_widths],
        out_shape=[jax.ShapeDtypeStruct((bsz, s, w), dt) for w, dt in out_widths],
        scratch_shapes=[pltpu.VMEM((tm + 8, CONV_WIDTH), F32)],
        compiler_params=pltpu.CompilerParams(dimension_semantics=("arbitrary", "arbitrary"),
                                             vmem_limit_bytes=VMEM_LIMIT),
        name="proj",
    )(x, mod, g1, wa, wb, wc, wd, cw)


def _to_key(f):
    b = pltpu.bitcast(f, I32)
    return jnp.where(b < 0, b ^ 0x7FFFFFFF, b)


def _from_key(k):
    return pltpu.bitcast(jnp.where(k < 0, k ^ 0x7FFFFFFF, k), F32)


def _bit_transpose32(words):
    a = list(words)
    j, m = 16, 0x0000FFFF
    while j:
        k = 0
        while k < 32:
            t = (a[k] ^ lax.shift_right_logical(a[k + j], jnp.int32(j))) & jnp.int32(m)
            a[k] = a[k] ^ t
            a[k + j] = a[k + j] ^ lax.shift_left(t, jnp.int32(j))
            k = (k + j + 1) & ~j
        j >>= 1
        m = (m ^ (m << j)) & 0xFFFFFFFF
    return a


def _fold_lanes(x, op):
    parts = [x[:, t * LANES:(t + 1) * LANES] for t in range(x.shape[1] // LANES)]
    while len(parts) > 1:
        nxt = [op(parts[a], parts[a + 1]) for a in range(0, len(parts) - 1, 2)]
        parts = nxt + ([parts[-1]] if len(parts) % 2 else [])
    return parts[0]


def _tile_lanes(x, n):
    return x if n == 1 else jnp.concatenate([x] * n, axis=1)


def _attn_kernel(rb_ref, q_ref, qi_ref, wi_ref, qin_ref, win_ref, k_ref, v_ref, kid_ref, o_ref,
                 sct_ref, planes_ref, cand_ref, mb_ref, tb_ref, qm_ref, qim_ref, m_ref, l_ref, acc_ref,
                 *, tq, topk):
    c = tq
    i = pl.program_id(1)
    nt_dims = (((1,), (1,)), ((), ()))
    lane = lax.broadcasted_iota(I32, (tq, LANES), 1)
    low_half = lane < HEAD_DIM
    row = lax.broadcasted_iota(I32, (tq, c), 0)
    col = lax.broadcasted_iota(I32, (tq, c), 1)

    @pl.when((pl.program_id(0) == 0) & (i == 0))
    def _():
        for p in range(32):
            planes_ref[p] = jnp.zeros(planes_ref.shape[1:], I32)
        for near in range(2):
            rel = row - col + (1 - near) * c
            n = jnp.maximum(rel, 0)
            for h in range(N_HEADS):
                t = jnp.full((tq, c), rb_ref[0, h], F32)
                for b in range(1, N_BUCKETS):
                    t = jnp.where(n >= _BUCKET_START[b], rb_ref[b, h], t)
                t = (t - rb_ref[N_BUCKETS - 1, h]) * LOG2E
                tb_ref[h, :, near * c:(near + 1) * c] = jnp.where(rel >= 0, t, MASK_NEG)


    def masked_heads(src_ref, dst_ref):
        for h in range(N_HEADS):
            hp, half = divmod(h, 2)
            keep = low_half if half == 0 else jnp.logical_not(low_half)
            pair = src_ref[:, hp * LANES:(hp + 1) * LANES]
            dst_ref[h] = jnp.where(keep, pair, jnp.zeros_like(pair))

    krow = lax.broadcasted_iota(I32, (c, tq), 0)
    qcol = lax.broadcasted_iota(I32, (c, tq), 1)
    int_min = jnp.int32(-2 ** 31)

    def fold_rows(x, op):
        parts = [x[r * 8:(r + 1) * 8, :] for r in range(x.shape[0] // 8)]
        while len(parts) > 1:
            nxt = [op(parts[a], parts[a + 1]) for a in range(0, len(parts) - 1, 2)]
            parts = nxt + ([parts[-1]] if len(parts) % 2 else [])
        return parts[0]

    def score_chunk(j, diag, w_t):
        start = pl.multiple_of(j * c, c)
        kc = kid_ref[pl.ds(start, c), :]
        acc = jnp.zeros((c, tq), F32)
        for h in range(N_HEADS):
            d = lax.dot_general(kc, qim_ref[h], nt_dims, preferred_element_type=F32)
            acc = acc + jnp.maximum(d, 0.0) * w_t[h:h + 1, :]
        ukey = _to_key(acc) ^ int_min
        if diag:
            causal = krow <= qcol
            acc = jnp.where(causal, acc, SCORE_NEG)
            ukey = jnp.where(causal, ukey, 0)
        sct_ref[pl.ds(start, c), :] = acc
        prow = pl.multiple_of(j * (c // 32), c // 32)
        for lt in range(tq // LANES):
            words = [ukey[k * 8:(k + 1) * 8, lt * LANES:(lt + 1) * LANES] for k in range(32)]
            for p, plane in enumerate(_bit_transpose32(words)):
                planes_ref[p, pl.ds(prow, c // 32), lt * LANES:(lt + 1) * LANES] = plane

    @pl.when(i == 0)
    def _():
        masked_heads(qi_ref, qim_ref)
        score_chunk(0, True, jnp.transpose(wi_ref[...])[0:IDX_HEADS, :])

    nchunks = i + 1
    n_causal = i * tq + lax.broadcasted_iota(I32, (1, tq), 1) + 1
    prow_iota = lax.broadcasted_iota(I32, cand_ref.shape, 0)
    cand_ref[...] = jnp.where(prow_iota < nchunks * (c // 32), -1, 0)

    def radix_cond(st):
        return (st[0] < 32) & (st[1] > 0)

    def radix_step(st):
        p0, _, prefix, need, ncand, settled = st
        cand = cand_ref[...]
        for sub in range(RADIX_BITS_PER_STEP):
            p = p0 + sub
            plane = planes_ref[p]
            ones = cand & plane
            cnt1 = jnp.sum(fold_rows(lax.population_count(ones), jnp.add), axis=0, keepdims=True)
            take1 = cnt1 >= need
            live = settled == 0
            need = jnp.where(live & jnp.logical_not(take1), need - cnt1, need)
            ncand = jnp.where(live, jnp.where(take1, cnt1, ncand - cnt1), ncand)
            prefix = jnp.where(live & take1, prefix | jnp.left_shift(jnp.int32(1), 31 - p), prefix)
            cand = cand & (plane ^ jnp.where(take1, 0, -1))
            settled = jnp.where(ncand == need, 1, settled)
        cand_ref[...] = cand
        return p0 + RADIX_BITS_PER_STEP, jnp.sum(1 - settled), prefix, need, ncand, settled

    done0 = jnp.where(n_causal <= topk, 1, 0)
    _, _, prefix, _, _, _ = lax.while_loop(
        radix_cond, radix_step,
        (jnp.int32(0), jnp.sum(1 - done0), jnp.zeros((1, tq), I32), jnp.full((1, tq), topk, I32),
         jnp.full((1, tq), 1, I32) * (nchunks * c), done0))
    guess_key = prefix ^ int_min
    guess = _from_key(guess_key)

    def count_keys(pred):
        def body(j, part):
            start = pl.multiple_of(j * c, c)
            hit = pred(sct_ref[pl.ds(start, c), :], start)
            return part + fold_rows(jnp.where(hit, 1.0, 0.0), jnp.add)
        part = lax.fori_loop(0, nchunks, body, jnp.zeros((8, tq), F32))
        return jnp.sum(part, axis=0, keepdims=True)

    kf = float(topk)
    guess_above = _from_key(guess_key + 1)
    guess_above = jnp.where(jnp.abs(guess_above) < TINY_F32,
                            jnp.where(guess >= 0.0, TINY_F32, 0.0), guess_above)
    guess_below = _from_key(guess_key - 2)

    def cond(st):
        return st[0] > 0

    def step(st):
        _, it, lo, hi, c_lo, c_hi, done = st
        mid = 0.5 * lo + 0.5 * hi
        lo_k, hi_k = _to_key(lo), _to_key(hi)
        mid_key = _from_key((lo_k >> 1) + (hi_k >> 1) + (lo_k & hi_k & 1))
        mid = jnp.where(it >= KEY_SPACE_AFTER, mid_key, mid)
        probe = jnp.where(it == 0, guess, jnp.where(lo == guess, guess_above, guess_below))
        mid = jnp.where((it <= 1) & (probe > lo) & (probe < hi), probe, mid)
        go = (done == 0) & (mid > lo) & (mid < hi)
        cnt = count_keys(lambda tile, start: tile >= mid)
        ge = cnt >= kf
        up = ge & go
        dn = jnp.logical_not(ge) & go
        lo = jnp.where(up, mid, lo)
        hi = jnp.where(dn, mid, hi)
        c_lo = jnp.where(up, cnt, c_lo)
        c_hi = jnp.where(dn, cnt, c_hi)
        adjacent = (lo == guess) & (hi == guess_above)
        done = jnp.where(go & (c_lo != kf) & jnp.logical_not(adjacent), 0, 1)
        return jnp.sum(1 - done), it + 1, lo, hi, c_lo, c_hi, done

    _, _, thr, _, c_lo, c_hi, _ = lax.while_loop(
        cond, step, (jnp.sum(1 - done0), jnp.int32(0), jnp.full((1, tq), SCORE_NEG, F32),
                     jnp.full((1, tq), -SCORE_NEG, F32), jnp.full((1, tq), 1.0, F32) * (nchunks * c),
                     jnp.zeros((1, tq), F32), done0))

    def write_mask(selected):
        def tiles(start, rows):
            tile = sct_ref[pl.ds(start, rows), :]
            key = lax.broadcasted_iota(I32, (rows, tq), 0) + start
            mb_ref[:, pl.ds(start, rows)] = jnp.transpose(
                jnp.where(selected(tile, key), 0.0, MASK_NEG))

        def body(j, carry):
            tiles(pl.multiple_of(j * 2 * c, 2 * c), 2 * c)
            return carry

        lax.fori_loop(0, nchunks // 2, body, 0)

        @pl.when(nchunks % 2 == 1)
        def _():
            tiles(pl.multiple_of((nchunks - 1) * c, c), c)

    tied = (n_causal > topk) & (c_lo > kf)
    any_tied = jnp.sum(jnp.where(tied, 1, 0)) > 0

    @pl.when(any_tied)
    def _():
        n_take = kf - c_hi

        def bis(_, st):
            lo_x, hi_x = st
            mid = (lo_x + hi_x) >> 1
            ok = count_keys(lambda tile, start: (tile == thr) & ((krow + start) <= mid)) >= n_take
            return jnp.where(ok, lo_x, mid), jnp.where(ok, mid, hi_x)

        nbits = int(math.ceil(math.log2(sct_ref.shape[0]))) + 1
        _, hi_x = lax.fori_loop(0, nbits, bis, (jnp.full((1, tq), -1, I32),
                                                jnp.full((1, tq), sct_ref.shape[0], I32)))
        cut = jnp.where(tied, hi_x, jnp.int32(2 ** 30))

        write_mask(lambda tile, key: (tile > thr) | ((tile == thr) & (key <= cut)))

    @pl.when(jnp.logical_not(any_tied))
    def _():
        write_mask(lambda tile, key: tile >= thr)

    masked_heads(q_ref, qm_ref)
    masked_heads(qin_ref, qim_ref)
    w_t_next = jnp.transpose(win_ref[...])[0:IDX_HEADS, :]
    has_next = i + 1 < pl.num_programs(1)
    m_ref[...] = jnp.full(m_ref.shape, MASK_NEG, F32)
    l_ref[...] = jnp.zeros(l_ref.shape, F32)
    acc_ref[...] = jnp.zeros(acc_ref.shape, F32)

    def attend(start, width, near):
        nt = width // LANES
        ones = jnp.ones((width, LANES), BF16)
        for h in range(N_HEADS):
            hp = h // 2
            kp = k_ref[pl.ds(start, width), hp * LANES:(hp + 1) * LANES]
            vp = v_ref[pl.ds(start, width), hp * LANES:(hp + 1) * LANES]
            s = lax.dot_general(qm_ref[h], kp, nt_dims, preferred_element_type=F32)
            s = s + mb_ref[:, pl.ds(start, width)]
            if near:
                s = s + tb_ref[h, :, 2 * c - width:2 * c]
            m_old = m_ref[h]
            m_new = jnp.maximum(m_old, jnp.max(_fold_lanes(s, jnp.maximum), axis=1, keepdims=True))
            p = jnp.exp2(s - _tile_lanes(m_new, nt)).astype(BF16)
            pv = jnp.dot(p, jnp.concatenate([vp, ones], axis=1), preferred_element_type=F32)
            alpha = jnp.exp2(m_old - m_new)
            m_ref[h] = m_new
            l_ref[h] = alpha * l_ref[h] + pv[:, LANES:]
            acc_ref[h] = alpha * acc_ref[h] + pv[:, :LANES]

    wide = 2 * c
    n_far = jnp.maximum(i - 1, 0)

    n_wide = n_far // 2

    def far_body(j, _):
        score_chunk(2 * j, False, w_t_next)
        attend(pl.multiple_of(j * wide, wide), wide, False)
        score_chunk(2 * j + 1, False, w_t_next)
        return 0

    lax.fori_loop(0, n_wide, far_body, 0)

    @pl.when(n_far % 2 == 1)
    def _():
        attend(pl.multiple_of((n_far - 1) * c, c), c, False)
        score_chunk(n_far - 1, False, w_t_next)

    @pl.when(i >= 1)
    def _():
        score_chunk(i - 1, False, w_t_next)
        attend(pl.multiple_of((i - 1) * c, c), wide, True)
        score_chunk(i, False, w_t_next)

    @pl.when(i == 0)
    def _():
        attend(0, c, True)
        score_chunk(0, False, w_t_next)

    @pl.when(has_next)
    def _():
        score_chunk(i + 1, True, w_t_next)

    for hp in range(N_HEADS // 2):
        lo = acc_ref[2 * hp] / l_ref[2 * hp]
        hi = acc_ref[2 * hp + 1] / l_ref[2 * hp + 1]
        o_ref[:, hp * LANES:(hp + 1) * LANES] = jnp.where(low_half, lo, hi).astype(BF16)


def _attn_call(rel_bias, q, qi, wi, k, v, kid, tq):
    bsz, s, _ = q.shape
    nq = s // tq
    topk = min(TOPK_MAX, s // 4)
    assert tq % LANES == 0 and tq >= MAX_DISTANCE
    blk = lambda w: pl.BlockSpec((None, tq, w), lambda b, i: (b, i, 0))
    nxt = lambda w: pl.BlockSpec((None, tq, w), lambda b, i: (b, jnp.minimum(i + 1, nq - 1), 0))
    whole = lambda w: pl.BlockSpec((None, s, w), lambda b, i: (b, 0, 0), pipeline_mode=pl.Buffered(1))
    return pl.pallas_call(
        functools.partial(_attn_kernel, tq=tq, topk=topk),
        grid=(bsz, nq),
        in_specs=[pl.BlockSpec(memory_space=pltpu.SMEM),
                  blk(ATTN_WIDTH), blk(ATTN_WIDTH), blk(LANES), nxt(ATTN_WIDTH), nxt(LANES),
                  whole(ATTN_WIDTH), whole(ATTN_WIDTH), whole(LANES)],
        out_specs=blk(ATTN_WIDTH),
        out_shape=jax.ShapeDtypeStruct((bsz, s, ATTN_WIDTH), BF16),
        scratch_shapes=[
            pltpu.VMEM((s, tq), F32),
            pltpu.VMEM((32, s // 32, tq), I32),
            pltpu.VMEM((s // 32, tq), I32),
            pltpu.VMEM((tq, s), F32),
            pltpu.VMEM((N_HEADS, tq, 2 * tq), F32),
            pltpu.VMEM((N_HEADS, tq, LANES), BF16),
            pltpu.VMEM((N_HEADS, tq, LANES), BF16),
            pltpu.VMEM((N_HEADS, tq, LANES), F32),
            pltpu.VMEM((N_HEADS, tq, LANES), F32),
            pltpu.VMEM((N_HEADS, tq, LANES), F32),
        ],
        compiler_params=pltpu.CompilerParams(dimension_semantics=("arbitrary", "arbitrary"),
                                             vmem_limit_bytes=VMEM_LIMIT),
        name="attn",
    )(rel_bias, q, qi, wi, qi, wi, k, v, kid)


def _post_kernel(attn_ref, z_ref, gate_ref, x_ref, mod_ref, wa_ref, wc_ref, wo_ref, g2_ref,
                 wr_ref, br_ref, hres_ref, h2_ref, eid_ref, ew_ref, counts_ref, cnt_ref):
    ya = jnp.dot(attn_ref[...], wa_ref[...], preferred_element_type=F32)
    yc = jnp.dot(z_ref[...], wc_ref[...], preferred_element_type=F32)
    merged = gate_ref[:, 0:D_MODEL].astype(F32) * ya + gate_ref[:, D_MODEL:].astype(F32) * yc
    o = jnp.dot(merged.astype(BF16), wo_ref[...], preferred_element_type=F32)
    hres = x_ref[...] + mod_ref[2:3, :] * o
    hres_ref[...] = hres
    h2 = _rms_mod(hres, g2_ref[...], mod_ref[4:5, :], mod_ref[3:4, :])
    h2_ref[...] = h2.astype(BF16).reshape(h2_ref.shape)

    h2_hi = h2.astype(BF16)
    h2_lo = (h2 - h2_hi.astype(F32)).astype(BF16)
    lg = (jnp.dot(h2_hi, wr_ref[0], preferred_element_type=F32)
          + (jnp.dot(h2_lo, wr_ref[0], preferred_element_type=F32)
             + jnp.dot(h2_hi, wr_ref[1], preferred_element_type=F32))) + br_ref[...]
    tm = lg.shape[0]
    lgt = jnp.transpose(lg)
    ninf = -jnp.inf
    grow = lax.broadcasted_iota(I32, (8, tm), 0)
    gl = jnp.where(grow < N_GROUPS, lgt[N_EXPERTS:N_EXPERTS + 8, :], ninf)
    gmax = jnp.max(gl, axis=0, keepdims=True)
    g_sel = jnp.min(jnp.where(gl == gmax, grow, N_GROUPS), axis=0, keepdims=True)
    g_w = 1.0 / jnp.sum(jnp.exp(gl - gmax), axis=0, keepdims=True)

    erow = lax.broadcasted_iota(I32, (N_EXPERTS, tm), 0)
    emask = (erow // EXPERTS_PER_GROUP) == g_sel
    el = jnp.where(emask, lgt[0:N_EXPERTS, :], ninf)
    emax = jnp.max(el, axis=0, keepdims=True)
    ee = jnp.exp(el - emax)
    e_prob = ee / jnp.sum(ee, axis=0, keepdims=True)
    p1 = jnp.max(jnp.where(emask, e_prob, -1.0), axis=0, keepdims=True)
    i1 = jnp.min(jnp.where(emask & (e_prob == p1), erow, LANES), axis=0, keepdims=True)
    rest = emask & (erow != i1)
    p2 = jnp.max(jnp.where(rest, e_prob, -1.0), axis=0, keepdims=True)
    i2 = jnp.min(jnp.where(rest & (e_prob == p2), erow, LANES), axis=0, keepdims=True)
    psum = p1 + p2
    w1 = g_w * (p1 / psum)
    w2 = g_w * (p2 / psum)
    ew_ref[...] = jnp.transpose(jnp.concatenate([w1, w2, jnp.zeros((LANES - 2, tm), F32)], axis=0))

    @pl.when((pl.program_id(0) == 0) & (pl.program_id(1) == 0))
    def _():
        cnt_ref[...] = jnp.zeros(cnt_ref.shape, F32)

    onehot_t = jnp.where((erow == i1) | (erow == i2), 1.0, 0.0).astype(BF16)
    upper = (lax.broadcasted_iota(I32, (tm, tm), 0) <= lax.broadcasted_iota(I32, (tm, tm), 1))
    cum_t = jnp.dot(onehot_t, jnp.where(upper, 1.0, 0.0).astype(BF16), preferred_element_type=F32)
    cnt_col = jnp.transpose(cnt_ref[...])[0:N_EXPERTS, 0:1]
    before = cum_t - onehot_t.astype(F32) + cnt_col
    r1 = jnp.sum(jnp.where(erow == i1, before, 0.0), axis=0, keepdims=True).astype(I32)
    r2 = jnp.sum(jnp.where(erow == i2, before, 0.0), axis=0, keepdims=True).astype(I32)
    eid_ref[...] = jnp.concatenate([i1, i2, r1, r2, jnp.zeros((4, tm), I32)], axis=0)
    padded = jnp.concatenate([onehot_t, jnp.zeros((LANES - N_EXPERTS, tm), BF16)], axis=0)
    tile_total = lax.dot_general(jnp.ones((8, tm), BF16), padded, (((1,), (1,)), ((), ())),
                                 preferred_element_type=F32)
    total = cnt_ref[...] + tile_total
    cnt_ref[...] = total
    counts_ref[...] = total


def _post_call(attn, z, gates, x, mod, wa, wc, wo, g2, wr, br, tm):
    bsz, s, d = x.shape
    nt = s // tm
    tok = lambda w: pl.BlockSpec((None, tm, w), lambda b, t: (b, t, 0))
    full = lambda arr: pl.BlockSpec(arr.shape, lambda b, t: (0,) * arr.ndim)
    return pl.pallas_call(
        _post_kernel,
        grid=(bsz, nt),
        in_specs=[tok(ATTN_WIDTH), tok(CONV_WIDTH), tok(2 * d), tok(d),
                  pl.BlockSpec((None, 6, d), lambda b, t: (b, 0, 0)),
                  full(wa), full(wc), full(wo), full(g2), full(wr), full(br)],
        out_specs=[tok(d), pl.BlockSpec((None, tm) + ROW_TILE, lambda b, t: (b, t, 0, 0)),
                   pl.BlockSpec((8, tm), lambda b, t: (0, b * nt + t)), tok(LANES),
                   pl.BlockSpec((8, LANES), lambda b, t: (0, 0))],
        out_shape=[jax.ShapeDtypeStruct((bsz, s, d), F32),
                   jax.ShapeDtypeStruct((bsz, s) + ROW_TILE, BF16),
                   jax.ShapeDtypeStruct((8, bsz * s), I32), jax.ShapeDtypeStruct((bsz, s, LANES), F32),
                   jax.ShapeDtypeStruct((8, LANES), F32)],
        scratch_shapes=[pltpu.VMEM((8, LANES), F32)],
        compiler_params=pltpu.CompilerParams(dimension_semantics=("arbitrary", "arbitrary"),
                                             vmem_limit_bytes=VMEM_LIMIT),
        name="post",
    )(attn, z, gates, x, mod, wa, wc, wo, g2, wr, br)


ROW_TILE = (D_MODEL // LANES, LANES)
EXPERT_ROWS = 256
TOKENS_PER_STEP = 512


ISSUE_UNROLL = 8


def _dispatch_kernel(pos1_ref, pos2_ref, h2_ref, xs_in_ref, xs_ref, sem):
    del xs_in_ref
    tm = h2_ref.shape[0]
    base = pl.program_id(0) * tm

    def issue(r, c):
        pltpu.make_async_copy(h2_ref.at[r], xs_ref.at[pos1_ref[base + r]], sem).start(priority=0)
        pltpu.make_async_copy(h2_ref.at[r], xs_ref.at[pos2_ref[base + r]], sem).start(priority=1)
        return c

    lax.fori_loop(0, tm, issue, 0, unroll=ISSUE_UNROLL)
    for _ in range(2):
        pltpu.make_async_copy(h2_ref, xs_ref.at[pl.ds(0, tm)], sem).wait()


def _dispatch_call(pos1, pos2, h2_rows, n_rows):
    t = h2_rows.shape[0]
    tm = min(TOKENS_PER_STEP, t)
    xs0 = jnp.zeros((n_rows,) + ROW_TILE, BF16)
    return pl.pallas_call(
        _dispatch_kernel,
        grid_spec=pltpu.PrefetchScalarGridSpec(
            num_scalar_prefetch=2,
            grid=(t // tm,),
            in_specs=[pl.BlockSpec((tm,) + ROW_TILE, lambda i, p1, p2: (i, 0, 0)),
                      pl.BlockSpec(memory_space=pl.ANY)],
            out_specs=pl.BlockSpec(memory_space=pl.ANY),
            scratch_shapes=[pltpu.SemaphoreType.DMA(())]),
        out_shape=jax.ShapeDtypeStruct((n_rows,) + ROW_TILE, BF16),
        input_output_aliases={3: 0},
        compiler_params=pltpu.CompilerParams(dimension_semantics=("arbitrary",),
                                             vmem_limit_bytes=VMEM_LIMIT),
        name="dispatch",
    )(pos1, pos2, h2_rows, xs0)


def _expert_kernel(te_ref, nu_ref, xs_ref, wg_ref, wu_ref, wd_ref, ys_ref, wgb, wub, wdb):
    g = pl.program_id(0)
    e = te_ref[g]
    e_prev = te_ref[jnp.maximum(g - 1, 0)]

    @pl.when((g == 0) | (e != e_prev))
    def _():
        wgb[...] = wg_ref[...].astype(BF16)
        wub[...] = wu_ref[...].astype(BF16)
        wdb[...] = wd_ref[...].astype(BF16)

    @pl.when(g < nu_ref[0])
    def _():
        x = xs_ref[...].reshape(xs_ref.shape[0], D_MODEL)
        a = jnp.dot(x, wgb[...], preferred_element_type=F32)
        b = jnp.dot(x, wub[...], preferred_element_type=F32)
        hid = (a * jax.nn.sigmoid(a)) * b
        y = jnp.dot(hid.astype(BF16), wdb[...], preferred_element_type=F32)
        ys_ref[...] = y.astype(BF16).reshape(ys_ref.shape)

    @pl.when(g >= nu_ref[0])
    def _():
        ys_ref[...] = jnp.zeros(ys_ref.shape, BF16)


def _expert_call(tile_expert, n_used, xs, wg, wu, wd):
    n_rows, d = xs.shape[0], D_MODEL
    nt = n_rows // EXPERT_ROWS
    rows = pl.BlockSpec((EXPERT_ROWS,) + ROW_TILE, lambda g, te, nu: (g, 0, 0))
    wspec = lambda shp: pl.BlockSpec((None,) + shp, lambda g, te, nu: (te[g], 0, 0))
    return pl.pallas_call(
        _expert_kernel,
        grid_spec=pltpu.PrefetchScalarGridSpec(
            num_scalar_prefetch=2,
            grid=(nt,),
            in_specs=[rows, wspec((d, D_EXPERT)), wspec((d, D_EXPERT)), wspec((D_EXPERT, d))],
            out_specs=rows,
            scratch_shapes=[pltpu.VMEM((d, D_EXPERT), BF16), pltpu.VMEM((d, D_EXPERT), BF16),
                            pltpu.VMEM((D_EXPERT, d), BF16)]),
        out_shape=jax.ShapeDtypeStruct((n_rows,) + ROW_TILE, BF16),
        compiler_params=pltpu.CompilerParams(dimension_semantics=("arbitrary",),
                                             vmem_limit_bytes=VMEM_LIMIT),
        name="experts",
    )(tile_expert, n_used, xs, wg, wu, wd)


def _combine_kernel(pos1_ref, pos2_ref, ys_ref, ew_ref, hres_ref, g2_ref, gf_ref, o_ref, buf, sem):
    tm = hres_ref.shape[0]
    t = pl.program_id(0)
    nt = pl.num_programs(0)

    def issue(tile, slot):
        base = tile * tm

        def body(r, c):
            pltpu.make_async_copy(ys_ref.at[pos1_ref[base + r]], buf.at[slot, r],
                                  sem.at[slot]).start(priority=0)
            pltpu.make_async_copy(ys_ref.at[pos2_ref[base + r]], buf.at[slot, tm + r],
                                  sem.at[slot]).start(priority=1)
            return c

        lax.fori_loop(0, tm, body, 0, unroll=ISSUE_UNROLL)

    @pl.when(t == 0)
    def _():
        issue(0, 0)

    @pl.when(t + 1 < nt)
    def _():
        issue(t + 1, (t + 1) % 2)

    slot = t % 2
    pltpu.make_async_copy(ys_ref.at[pl.ds(0, 2 * tm)], buf.at[slot], sem.at[slot]).wait()

    y1 = buf[slot, 0:tm].astype(F32).reshape(tm, D_MODEL)
    y2 = buf[slot, tm:2 * tm].astype(F32).reshape(tm, D_MODEL)
    w = ew_ref[...]
    y = w[:, 0:1] * y1 + w[:, 1:2] * y2
    hfin = hres_ref[...] + g2_ref[...] * y
    ms = jnp.mean(hfin * hfin, axis=-1, keepdims=True)
    o_ref[...] = hfin * lax.rsqrt(ms + EPS) * gf_ref[...]


def _combine_call(pos1, pos2, ys_rows, ew, hres, gate2, gf):
    t, d = hres.shape
    tm = min(TOKENS_PER_STEP, t)
    per_batch = t // gate2.shape[0]
    assert per_batch % tm == 0
    tok = lambda w: pl.BlockSpec((tm, w), lambda i, *_: (i, 0))
    return pl.pallas_call(
        _combine_kernel,
        grid_spec=pltpu.PrefetchScalarGridSpec(
            num_scalar_prefetch=2,
            grid=(t // tm,),
            in_specs=[pl.BlockSpec(memory_space=pl.ANY), tok(LANES), tok(d),
                      pl.BlockSpec((None, 1, d), lambda i, *_: ((i * tm) // per_batch, 0, 0)),
                      pl.BlockSpec((1, d), lambda i, *_: (0, 0))],
            out_specs=tok(d),
            scratch_shapes=[pltpu.VMEM((2, 2 * tm) + ROW_TILE, BF16), pltpu.SemaphoreType.DMA((2,))]),
        out_shape=jax.ShapeDtypeStruct((t, d), F32),
        compiler_params=pltpu.CompilerParams(dimension_semantics=("arbitrary",),
                                             vmem_limit_bytes=VMEM_LIMIT),
        name="combine",
    )(pos1, pos2, ys_rows, ew, hres, gate2, gf)


def _slot_kernel(seg_ref, eid_ref, pos_ref):
    e = eid_ref[0:2, :]
    pos = eid_ref[2:4, :]
    for k in range(seg_ref.shape[0]):
        pos = pos + jnp.where(e == k, seg_ref[k], 0)
    pos_ref[...] = jnp.concatenate([pos, jnp.zeros((6, pos.shape[1]), I32)], axis=0)


def _slot_call(seg_start, eid):
    return pl.pallas_call(
        _slot_kernel,
        in_specs=[pl.BlockSpec(memory_space=pltpu.SMEM), pl.BlockSpec(eid.shape, lambda: (0, 0))],
        out_specs=pl.BlockSpec(eid.shape, lambda: (0, 0)),
        out_shape=jax.ShapeDtypeStruct(eid.shape, I32),
        name="slots",
    )(seg_start, eid)


def _sparse_moe(h2, eid, ew, counts, wg, wu, wd, hres, gate2, gf):
    bsz, s, d = hres.shape
    t = bsz * s
    ne = wg.shape[0]
    cnt = counts[0, :ne].astype(I32)
    seg_rows = ((cnt + EXPERT_ROWS - 1) // EXPERT_ROWS) * EXPERT_ROWS
    seg_end = jnp.cumsum(seg_rows)
    seg_start = seg_end - seg_rows
    pos = _slot_call(seg_start, eid)
    pos1, pos2 = pos[0], pos[1]
    n_tiles = (2 * t) // EXPERT_ROWS + ne
    tile_first_row = jnp.arange(n_tiles, dtype=I32) * EXPERT_ROWS
    tile_expert = jnp.minimum(jnp.sum(seg_end[None, :] <= tile_first_row[:, None], axis=1), ne - 1)
    n_used = (seg_end[ne - 1] // EXPERT_ROWS).reshape(1)
    n_rows = n_tiles * EXPERT_ROWS

    xs = _dispatch_call(pos1, pos2, h2.reshape((t,) + ROW_TILE), n_rows)
    ys = _expert_call(tile_expert.astype(I32), n_used.astype(I32), xs, wg, wu, wd)
    out = _combine_call(pos1, pos2, ys, ew.reshape(t, LANES), hres.reshape(t, d),
                        gate2.reshape(bsz, 1, d), gf.reshape(1, d))
    return out.reshape(bsz, s, d)


def kernel(x, c, w_ada, b_ada, norm1_g, w_in, rel_bias, conv_w, w_attn_branch, w_conv_branch, w_out,
           norm2_g, w_router_group, b_router_group, w_router_expert, b_router_expert,
           w_gate_e, w_up_e, w_down_e, norm_f_g):
    bsz, s, d = x.shape
    depth = w_ada.shape[0]
    h = x.astype(F32)
    c_pad = jnp.zeros((8, d), F32).at[:bsz].set(c.astype(F32))
    out = h
    for l in range(depth):
        mod = _ada_call(c_pad, w_ada[l].astype(F32), b_ada[l].astype(F32)[None, :])
        mod = mod[:bsz].reshape(bsz, 6, d)

        w = w_in[l]
        o_q, o_k, o_v, o_qi = 0, 512, 1024, 1536
        o_ki, o_wi, o_cb, o_gl = 2048, 2112, 2120, 3656
        wa = w[:, o_q:o_ki].astype(BF16)
        ki_w = w[:, o_ki:o_wi]
        wb = jnp.concatenate([ki_w, ki_w, w[:, o_wi:o_cb],
                              jnp.zeros((d, LANES - IDX_HEADS), w.dtype)], axis=1).astype(BF16)
        wc = w[:, o_cb:o_gl].astype(BF16)
        wd = w[:, o_gl:].astype(BF16)

        q, k, v, qi, kid, wi, z, gates = _proj_call(
            h, mod, norm1_g[l].astype(F32)[None, :], wa, wb, wc, wd, conv_w[l].astype(F32),
            tm=min(512, s))

        attn = _attn_call(rel_bias.astype(F32), q, qi, wi, k, v, kid, tq=min(TOPK_MAX, s // 4))

        wr32 = jnp.concatenate([w_router_expert[l], w_router_group[l],
                                jnp.zeros((d, LANES - N_EXPERTS - N_GROUPS), F32)], axis=1).astype(F32)
        wr_hi = wr32.astype(BF16)
        wr = jnp.stack([wr_hi, (wr32 - wr_hi.astype(F32)).astype(BF16)])
        br = jnp.concatenate([b_router_expert[l], b_router_group[l],
                              jnp.zeros((LANES - N_EXPERTS - N_GROUPS,), F32)])[None, :].astype(F32)
        hres, h2, eid, ew, counts = _post_call(
            attn, z, gates, h, mod, w_attn_branch[l].astype(BF16), w_conv_branch[l].astype(BF16),
            w_out[l].astype(BF16), norm2_g[l].astype(F32)[None, :], wr, br, tm=min(512, s))

        assert depth == 1, "the final RMSNorm is fused into the combine kernel of the only layer"
        out = _sparse_moe(h2, eid, ew, counts, w_gate_e[l].astype(F32), w_up_e[l].astype(F32),
                          w_down_e[l].astype(F32), hres, mod[:, 5, :], norm_f_g.astype(F32))
        h = out
    return out.astype(x.dtype)
```

```python
import functools
import math

import numpy as np
import jax
import jax.numpy as jnp
from jax import lax
from jax.experimental import pallas as pl
from jax.experimental.pallas import tpu as pltpu

F32 = jnp.float32
BF16 = jnp.bfloat16
I32 = jnp.int32

D_MODEL = 1024
N_HEADS = 8
HEAD_DIM = 64
ATTN_WIDTH = N_HEADS * HEAD_DIM
IDX_HEADS = 8
IDX_DIM = 64
TOPK_MAX = 256
N_BUCKETS = 32
MAX_DISTANCE = 128
CONV_WIDTH = 512
CONV_K = 3
N_GROUPS = 4
EXPERTS_PER_GROUP = 8
N_EXPERTS = N_GROUPS * EXPERTS_PER_GROUP
D_EXPERT = 512
EPS = 1e-6

LANES = 128
VMEM_LIMIT = 56 * 1024 * 1024

LOG2E = math.log2(math.e)
MASK_NEG = -1e30
SCORE_NEG = float(np.finfo(np.float32).min)
TINY_F32 = float(np.finfo(np.float32).tiny)
KEY_SPACE_AFTER = 40
RADIX_BITS_PER_STEP = 4


def _t5_bucket_starts():
    max_exact = N_BUCKETS // 2
    n = np.arange(0, MAX_DISTANCE + 1)
    nf = np.maximum(n, 1).astype(np.float32)
    val = (np.log(nf / np.float32(max_exact)) / np.float32(math.log(MAX_DISTANCE / max_exact))
           * np.float32(N_BUCKETS - max_exact)).astype(np.float32)
    inner = val[max_exact + 1:MAX_DISTANCE]
    assert np.min(np.abs(inner - np.round(inner))) > 1e-3
    large = np.minimum(max_exact + val.astype(np.int32), N_BUCKETS - 1)
    bucket = np.where(n < max_exact, n, large)
    assert bucket[MAX_DISTANCE] == N_BUCKETS - 1 and np.all(np.diff(bucket) >= 0)
    return [int(np.argmax(bucket >= b)) for b in range(N_BUCKETS)]


_BUCKET_START = _t5_bucket_starts()


def _ada_kernel(c_ref, w_ref, b_ref, o_ref):
    c = c_ref[...]
    ca = c * jax.nn.sigmoid(c)
    o_ref[...] = jnp.dot(ca, w_ref[...], preferred_element_type=F32,
                         precision=lax.Precision.HIGHEST) + b_ref[...]


def _ada_call(c_pad, w, b):
    rows, d = c_pad.shape
    n = w.shape[1]
    tn = 1536
    return pl.pallas_call(
        _ada_kernel,
        grid=(n // tn,),
        in_specs=[pl.BlockSpec((rows, d), lambda j: (0, 0)),
                  pl.BlockSpec((d, tn), lambda j: (0, j)),
                  pl.BlockSpec((1, tn), lambda j: (0, j))],
        out_specs=pl.BlockSpec((rows, tn), lambda j: (0, j)),
        out_shape=jax.ShapeDtypeStruct((rows, n), F32),
        compiler_params=pltpu.CompilerParams(dimension_semantics=("arbitrary",),
                                             vmem_limit_bytes=VMEM_LIMIT),
        name="ada",
    )(c_pad, w, b)


def _rms_mod(x, g, scale, shift):
    ms = jnp.mean(x * x, axis=-1, keepdims=True)
    y = x * lax.rsqrt(ms + EPS) * g
    return y * (1.0 + scale) + shift


def _proj_kernel(x_ref, mod_ref, g_ref, wa_ref, wb_ref, wc_ref, wd_ref, cw_ref,
                 q_ref, k_ref, v_ref, qi_ref, kid_ref, wi_ref, z_ref, gate_ref, ubuf, *, tm):
    h = _rms_mod(x_ref[...], g_ref[...], mod_ref[1:2, :], mod_ref[0:1, :])
    hb = h.astype(BF16)

    a = jnp.dot(hb, wa_ref[...], preferred_element_type=F32)
    q_ref[...] = (a[:, 0:512] * (HEAD_DIM ** -0.5 * LOG2E)).astype(BF16)
    k_ref[...] = a[:, 512:1024].astype(BF16)
    v_ref[...] = a[:, 1024:1536].astype(BF16)
    qi_ref[...] = a[:, 1536:2048].astype(BF16)

    b = jnp.dot(hb, wb_ref[...], preferred_element_type=F32)
    kid_ref[...] = b[:, 0:LANES].astype(BF16)
    wi_ref[...] = b[:, LANES:2 * LANES] * ((IDX_DIM ** -0.5) * (IDX_HEADS ** -0.5))

    cpart = jnp.dot(hb, wc_ref[...], preferred_element_type=F32)
    cb = cpart[:, 0:512]
    u = cpart[:, 512:1024] * cpart[:, 1024:1536]

    @pl.when(pl.program_id(1) == 0)
    def _():
        ubuf[0:8, :] = jnp.zeros((8, CONV_WIDTH), F32)

    ubuf[8:tm + 8, :] = u
    conv = (cw_ref[0:1, :] * ubuf[6:tm + 6, :] + cw_ref[1:2, :] * ubuf[7:tm + 7, :]
            + cw_ref[2:3, :] * u)
    z_ref[...] = (cb * conv).astype(BF16)
    ubuf[0:8, :] = ubuf[tm:tm + 8, :]

    d = jnp.dot(hb, wd_ref[...], preferred_element_type=F32)
    gate_ref[...] = jax.nn.sigmoid(d).astype(BF16)


def _proj_call(x, mod, g1, wa, wb, wc, wd, cw, tm):
    bsz, s, d = x.shape
    nt = s // tm
    tok = lambda w: pl.BlockSpec((None, tm, w), lambda b, t: (b, t, 0))
    full = lambda arr: pl.BlockSpec(arr.shape, lambda b, t: (0,) * arr.ndim)
    out_widths = [(512, BF16), (512, BF16), (512, BF16), (512, BF16), (LANES, BF16), (LANES, F32),
                  (512, BF16), (2 * D_MODEL, BF16)]
    return pl.pallas_call(
        functools.partial(_proj_kernel, tm=tm),
        grid=(bsz, nt),
        in_specs=[tok(d), pl.BlockSpec((None, 6, d), lambda b, t: (b, 0, 0)), full(g1),
                  full(wa), full(wb), full(wc), full(wd), full(cw)],
        out_specs=[tok(w) for w, _ in out_widths],
        out_shape=[jax.ShapeDtypeStruct((bsz, s, w), dt) for w, dt in out_widths],
        scratch_shapes=[pltpu.VMEM((tm + 8, CONV_WIDTH), F32)],
        compiler_params=pltpu.CompilerParams(dimension_semantics=("arbitrary", "arbitrary"),
                                             vmem_limit_bytes=VMEM_LIMIT),
        name="proj",
    )(x, mod, g1, wa, wb, wc, wd, cw)


def _to_key(f):
    b = pltpu.bitcast(f, I32)
    return jnp.where(b < 0, b ^ 0x7FFFFFFF, b)


def _from_key(k):
    return pltpu.bitcast(jnp.where(k < 0, k ^ 0x7FFFFFFF, k), F32)


def _bit_transpose32(words):
    a = list(words)
    j, m = 16, 0x0000FFFF
    while j:
        k = 0
        while k < 32:
            t = (a[k] ^ lax.shift_right_logical(a[k + j], jnp.int32(j))) & jnp.int32(m)
            a[k] = a[k] ^ t
            a[k + j] = a[k + j] ^ lax.shift_left(t, jnp.int32(j))
            k = (k + j + 1) & ~j
        j >>= 1
        m = (m ^ (m << j)) & 0xFFFFFFFF
    return a


def _fold_lanes(x, op):
    parts = [x[:, t * LANES:(t + 1) * LANES] for t in range(x.shape[1] // LANES)]
    while len(parts) > 1:
        nxt = [op(parts[a], parts[a + 1]) for a in range(0, len(parts) - 1, 2)]
        parts = nxt + ([parts[-1]] if len(parts) % 2 else [])
    return parts[0]


def _tile_lanes(x, n):
    return x if n == 1 else jnp.concatenate([x] * n, axis=1)


def _attn_kernel(rb_ref, q_ref, qi_ref, wi_ref, qin_ref, win_ref, k_ref, v_ref, kid_ref, o_ref,
                 sct_ref, planes_ref, cand_ref, mb_ref, tb_ref, qm_ref, qim_ref, m_ref, l_ref, acc_ref,
                 *, tq, topk):
    c = tq
    i = pl.program_id(1)
    nt_dims = (((1,), (1,)), ((), ()))
    lane = lax.broadcasted_iota(I32, (tq, LANES), 1)
    low_half = lane < HEAD_DIM
    row = lax.broadcasted_iota(I32, (tq, c), 0)
    col = lax.broadcasted_iota(I32, (tq, c), 1)

    @pl.when((pl.program_id(0) == 0) & (i == 0))
    def _():
        for p in range(32):
            planes_ref[p] = jnp.zeros(planes_ref.shape[1:], I32)
        for near in range(2):
            rel = row - col + (1 - near) * c
            n = jnp.maximum(rel, 0)
            for h in range(N_HEADS):
                t = jnp.full((tq, c), rb_ref[0, h], F32)
                for b in range(1, N_BUCKETS):
                    t = jnp.where(n >= _BUCKET_START[b], rb_ref[b, h], t)
                t = (t - rb_ref[N_BUCKETS - 1, h]) * LOG2E
                tb_ref[h, :, near * c:(near + 1) * c] = jnp.where(rel >= 0, t, MASK_NEG)


    def masked_heads(src_ref, dst_ref):
        for h in range(N_HEADS):
            hp, half = divmod(h, 2)
            keep = low_half if half == 0 else jnp.logical_not(low_half)
            pair = src_ref[:, hp * LANES:(hp + 1) * LANES]
            dst_ref[h] = jnp.where(keep, pair, jnp.zeros_like(pair))

    krow = lax.broadcasted_iota(I32, (c, tq), 0)
    qcol = lax.broadcasted_iota(I32, (c, tq), 1)
    int_min = jnp.int32(-2 ** 31)

    def fold_rows(x, op):
        parts = [x[r * 8:(r + 1) * 8, :] for r in range(x.shape[0] // 8)]
        while len(parts) > 1:
            nxt = [op(parts[a], parts[a + 1]) for a in range(0, len(parts) - 1, 2)]
            parts = nxt + ([parts[-1]] if len(parts) % 2 else [])
        return parts[0]

    def score_chunk(j, diag, w_t):
        start = pl.multiple_of(j * c, c)
        kc = kid_ref[pl.ds(start, c), :]
        acc = jnp.zeros((c, tq), F32)
        for h in range(N_HEADS):
            d = lax.dot_general(kc, qim_ref[h], nt_dims, preferred_element_type=F32)
            acc = acc + jnp.maximum(d, 0.0) * w_t[h:h + 1, :]
        ukey = _to_key(acc) ^ int_min
        if diag:
            causal = krow <= qcol
            acc = jnp.where(causal, acc, SCORE_NEG)
            ukey = jnp.where(causal, ukey, 0)
        sct_ref[pl.ds(start, c), :] = acc
        prow = pl.multiple_of(j * (c // 32), c // 32)
        for lt in range(tq // LANES):
            words = [ukey[k * 8:(k + 1) * 8, lt * LANES:(lt + 1) * LANES] for k in range(32)]
            for p, plane in enumerate(_bit_transpose32(words)):
                planes_ref[p, pl.ds(prow, c // 32), lt * LANES:(lt + 1) * LANES] = plane

    @pl.when(i == 0)
    def _():
        masked_heads(qi_ref, qim_ref)
        score_chunk(0, True, jnp.transpose(wi_ref[...])[0:IDX_HEADS, :])

    nchunks = i + 1
    n_causal = i * tq + lax.broadcasted_iota(I32, (1, tq), 1) + 1
    prow_iota = lax.broadcasted_iota(I32, cand_ref.shape, 0)
    cand_ref[...] = jnp.where(prow_iota < nchunks * (c // 32), -1, 0)

    def radix_cond(st):
        return (st[0] < 32) & (st[1] > 0)

    def radix_step(st):
        p0, _, prefix, need, ncand, settled = st
        cand = cand_ref[...]
        for sub in range(RADIX_BITS_PER_STEP):
            p = p0 + sub
            plane = planes_ref[p]
            ones = cand & plane
            cnt1 = jnp.sum(fold_rows(lax.population_count(ones), jnp.add), axis=0, keepdims=True)
            take1 = cnt1 >= need
            live = settled == 0
            need = jnp.where(live & jnp.logical_not(take1), need - cnt1, need)
            ncand = jnp.where(live, jnp.where(take1, cnt1, ncand - cnt1), ncand)
            prefix = jnp.where(live & take1, prefix | jnp.left_shift(jnp.int32(1), 31 - p), prefix)
            cand = cand & (plane ^ jnp.where(take1, 0, -1))
            settled = jnp.where(ncand == need, 1, settled)
        cand_ref[...] = cand
        return p0 + RADIX_BITS_PER_STEP, jnp.sum(1 - settled), prefix, need, ncand, settled

    done0 = jnp.where(n_causal <= topk, 1, 0)
    _, _, prefix, _, _, _ = lax.while_loop(
        radix_cond, radix_step,
        (jnp.int32(0), jnp.sum(1 - done0), jnp.zeros((1, tq), I32), jnp.full((1, tq), topk, I32),
         jnp.full((1, tq), 1, I32) * (nchunks * c), done0))
    guess_key = prefix ^ int_min
    guess = _from_key(guess_key)

    def count_keys(pred):
        def body(j, part):
            start = pl.multiple_of(j * c, c)
            hit = pred(sct_ref[pl.ds(start, c), :], start)
            return part + fold_rows(jnp.where(hit, 1.0, 0.0), jnp.add)
        part = lax.fori_loop(0, nchunks, body, jnp.zeros((8, tq), F32))
        return jnp.sum(part, axis=0, keepdims=True)

    kf = float(topk)
    guess_above = _from_key(guess_key + 1)
    guess_above = jnp.where(jnp.abs(guess_above) < TINY_F32,
                            jnp.where(guess >= 0.0, TINY_F32, 0.0), guess_above)
    guess_below = _from_key(guess_key - 2)

    def cond(st):
        return st[0] > 0

    def step(st):
        _, it, lo, hi, c_lo, c_hi, done = st
        mid = 0.5 * lo + 0.5 * hi
        lo_k, hi_k = _to_key(lo), _to_key(hi)
        mid_key = _from_key((lo_k >> 1) + (hi_k >> 1) + (lo_k & hi_k & 1))
        mid = jnp.where(it >= KEY_SPACE_AFTER, mid_key, mid)
        probe = jnp.where(it == 0, guess, jnp.where(lo == guess, guess_above, guess_below))
        mid = jnp.where((it <= 1) & (probe > lo) & (probe < hi), probe, mid)
        go = (done == 0) & (mid > lo) & (mid < hi)
        cnt = count_keys(lambda tile, start: tile >= mid)
        ge = cnt >= kf
        up = ge & go
        dn = jnp.logical_not(ge) & go
        lo = jnp.where(up, mid, lo)
        hi = jnp.where(dn, mid, hi)
        c_lo = jnp.where(up, cnt, c_lo)
        c_hi = jnp.where(dn, cnt, c_hi)
        adjacent = (lo == guess) & (hi == guess_above)
        done = jnp.where(go & (c_lo != kf) & jnp.logical_not(adjacent), 0, 1)
        return jnp.sum(1 - done), it + 1, lo, hi, c_lo, c_hi, done

    _, _, thr, _, c_lo, c_hi, _ = lax.while_loop(
        cond, step, (jnp.sum(1 - done0), jnp.int32(0), jnp.full((1, tq), SCORE_NEG, F32),
                     jnp.full((1, tq), -SCORE_NEG, F32), jnp.full((1, tq), 1.0, F32) * (nchunks * c),
                     jnp.zeros((1, tq), F32), done0))

    def write_mask(selected):
        def tiles(start, rows):
            tile = sct_ref[pl.ds(start, rows), :]
            key = lax.broadcasted_iota(I32, (rows, tq), 0) + start
            mb_ref[:, pl.ds(start, rows)] = jnp.transpose(
                jnp.where(selected(tile, key), 0.0, MASK_NEG))

        def body(j, carry):
            tiles(pl.multiple_of(j * 2 * c, 2 * c), 2 * c)
            return carry

        lax.fori_loop(0, nchunks // 2, body, 0)

        @pl.when(nchunks % 2 == 1)
        def _():
            tiles(pl.multiple_of((nchunks - 1) * c, c), c)

    tied = (n_causal > topk) & (c_lo > kf)
    any_tied = jnp.sum(jnp.where(tied, 1, 0)) > 0

    @pl.when(any_tied)
    def _():
        n_take = kf - c_hi

        def bis(_, st):
            lo_x, hi_x = st
            mid = (lo_x + hi_x) >> 1
            ok = count_keys(lambda tile, start: (tile == thr) & ((krow + start) <= mid)) >= n_take
            return jnp.where(ok, lo_x, mid), jnp.where(ok, mid, hi_x)

        nbits = int(math.ceil(math.log2(sct_ref.shape[0]))) + 1
        _, hi_x = lax.fori_loop(0, nbits, bis, (jnp.full((1, tq), -1, I32),
                                                jnp.full((1, tq), sct_ref.shape[0], I32)))
        cut = jnp.where(tied, hi_x, jnp.int32(2 ** 30))

        write_mask(lambda tile, key: (tile > thr) | ((tile == thr) & (key <= cut)))

    @pl.when(jnp.logical_not(any_tied))
    def _():
        write_mask(lambda tile, key: tile >= thr)

    masked_heads(q_ref, qm_ref)
    masked_heads(qin_ref, qim_ref)
    w_t_next = jnp.transpose(win_ref[...])[0:IDX_HEADS, :]
    has_next = i + 1 < pl.num_programs(1)
    m_ref[...] = jnp.full(m_ref.shape, MASK_NEG, F32)
    l_ref[...] = jnp.zeros(l_ref.shape, F32)
    acc_ref[...] = jnp.zeros(acc_ref.shape, F32)

    def attend(start, width, near):
        nt = width // LANES
        ones = jnp.ones((width, LANES), BF16)
        for h in range(N_HEADS):
            hp = h // 2
            kp = k_ref[pl.ds(start, width), hp * LANES:(hp + 1) * LANES]
            vp = v_ref[pl.ds(start, width), hp * LANES:(hp + 1) * LANES]
            s = lax.dot_general(qm_ref[h], kp, nt_dims, preferred_element_type=F32)
            s = s + mb_ref[:, pl.ds(start, width)]
            if near:
                s = s + tb_ref[h, :, 2 * c - width:2 * c]
            m_old = m_ref[h]
            m_new = jnp.maximum(m_old, jnp.max(_fold_lanes(s, jnp.maximum), axis=1, keepdims=True))
            p = jnp.exp2(s - _tile_lanes(m_new, nt)).astype(BF16)
            pv = jnp.dot(p, jnp.concatenate([vp, ones], axis=1), preferred_element_type=F32)
            alpha = jnp.exp2(m_old - m_new)
            m_ref[h] = m_new
            l_ref[h] = alpha * l_ref[h] + pv[:, LANES:]
            acc_ref[h] = alpha * acc_ref[h] + pv[:, :LANES]

    wide = 2 * c
    n_far = jnp.maximum(i - 1, 0)

    n_wide = n_far // 2

    def far_body(j, _):
        score_chunk(2 * j, False, w_t_next)
        attend(pl.multiple_of(j * wide, wide), wide, False)
        score_chunk(2 * j + 1, False, w_t_next)
        return 0

    lax.fori_loop(0, n_wide, far_body, 0)

    @pl.when(n_far % 2 == 1)
    def _():
        attend(pl.multiple_of((n_far - 1) * c, c), c, False)
        score_chunk(n_far - 1, False, w_t_next)

    @pl.when(i >= 1)
    def _():
        score_chunk(i - 1, False, w_t_next)
        attend(pl.multiple_of((i - 1) * c, c), wide, True)
        score_chunk(i, False, w_t_next)

    @pl.when(i == 0)
    def _():
        attend(0, c, True)
        score_chunk(0, False, w_t_next)

    @pl.when(has_next)
    def _():
        score_chunk(i + 1, True, w_t_next)

    for hp in range(N_HEADS // 2):
        lo = acc_ref[2 * hp] / l_ref[2 * hp]
        hi = acc_ref[2 * hp + 1] / l_ref[2 * hp + 1]
        o_ref[:, hp * LANES:(hp + 1) * LANES] = jnp.where(low_half, lo, hi).astype(BF16)


def _attn_call(rel_bias, q, qi, wi, k, v, kid, tq):
    bsz, s, _ = q.shape
    nq = s // tq
    topk = min(TOPK_MAX, s // 4)
    assert tq % LANES == 0 and tq >= MAX_DISTANCE
    blk = lambda w: pl.BlockSpec((None, tq, w), lambda b, i: (b, i, 0))
    nxt = lambda w: pl.BlockSpec((None, tq, w), lambda b, i: (b, jnp.minimum(i + 1, nq - 1), 0))
    whole = lambda w: pl.BlockSpec((None, s, w), lambda b, i: (b, 0, 0), pipeline_mode=pl.Buffered(1))
    return pl.pallas_call(
        functools.partial(_attn_kernel, tq=tq, topk=topk),
        grid=(bsz, nq),
        in_specs=[pl.BlockSpec(memory_space=pltpu.SMEM),
                  blk(ATTN_WIDTH), blk(ATTN_WIDTH), blk(LANES), nxt(ATTN_WIDTH), nxt(LANES),
                  whole(ATTN_WIDTH), whole(ATTN_WIDTH), whole(LANES)],
        out_specs=blk(ATTN_WIDTH),
        out_shape=jax.ShapeDtypeStruct((bsz, s, ATTN_WIDTH), BF16),
        scratch_shapes=[
            pltpu.VMEM((s, tq), F32),
            pltpu.VMEM((32, s // 32, tq), I32),
            pltpu.VMEM((s // 32, tq), I32),
            pltpu.VMEM((tq, s), F32),
            pltpu.VMEM((N_HEADS, tq, 2 * tq), F32),
            pltpu.VMEM((N_HEADS, tq, LANES), BF16),
            pltpu.VMEM((N_HEADS, tq, LANES), BF16),
            pltpu.VMEM((N_HEADS, tq, LANES), F32),
            pltpu.VMEM((N_HEADS, tq, LANES), F32),
            pltpu.VMEM((N_HEADS, tq, LANES), F32),
        ],
        compiler_params=pltpu.CompilerParams(dimension_semantics=("arbitrary", "arbitrary"),
                                             vmem_limit_bytes=VMEM_LIMIT),
        name="attn",
    )(rel_bias, q, qi, wi, qi, wi, k, v, kid)


def _post_kernel(attn_ref, z_ref, gate_ref, x_ref, mod_ref, wa_ref, wc_ref, wo_ref, g2_ref,
                 wr_ref, br_ref, hres_ref, h2_ref, eid_ref, ew_ref, counts_ref, cnt_ref):
    ya = jnp.dot(attn_ref[...], wa_ref[...], preferred_element_type=F32)
    yc = jnp.dot(z_ref[...], wc_ref[...], preferred_element_type=F32)
    merged = gate_ref[:, 0:D_MODEL].astype(F32) * ya + gate_ref[:, D_MODEL:].astype(F32) * yc
    o = jnp.dot(merged.astype(BF16), wo_ref[...], preferred_element_type=F32)
    hres = x_ref[...] + mod_ref[2:3, :] * o
    hres_ref[...] = hres
    h2 = _rms_mod(hres, g2_ref[...], mod_ref[4:5, :], mod_ref[3:4, :])
    h2_ref[...] = h2.astype(BF16).reshape(h2_ref.shape)

    h2_hi = h2.astype(BF16)
    h2_lo = (h2 - h2_hi.astype(F32)).astype(BF16)
    lg = (jnp.dot(h2_hi, wr_ref[0], preferred_element_type=F32)
          + (jnp.dot(h2_lo, wr_ref[0], preferred_element_type=F32)
             + jnp.dot(h2_hi, wr_ref[1], preferred_element_type=F32))) + br_ref[...]
    tm = lg.shape[0]
    lgt = jnp.transpose(lg)
    ninf = -jnp.inf
    grow = lax.broadcasted_iota(I32, (8, tm), 0)
    gl = jnp.where(grow < N_GROUPS, lgt[N_EXPERTS:N_EXPERTS + 8, :], ninf)
    gmax = jnp.max(gl, axis=0, keepdims=True)
    g_sel = jnp.min(jnp.where(gl == gmax, grow, N_GROUPS), axis=0, keepdims=True)
    g_w = 1.0 / jnp.sum(jnp.exp(gl - gmax), axis=0, keepdims=True)

    erow = lax.broadcasted_iota(I32, (N_EXPERTS, tm), 0)
    emask = (erow // EXPERTS_PER_GROUP) == g_sel
    el = jnp.where(emask, lgt[0:N_EXPERTS, :], ninf)
    emax = jnp.max(el, axis=0, keepdims=True)
    ee = jnp.exp(el - emax)
    e_prob = ee / jnp.sum(ee, axis=0, keepdims=True)
    p1 = jnp.max(jnp.where(emask, e_prob, -1.0), axis=0, keepdims=True)
    i1 = jnp.min(jnp.where(emask & (e_prob == p1), erow, LANES), axis=0, keepdims=True)
    rest = emask & (erow != i1)
    p2 = jnp.max(jnp.where(rest, e_prob, -1.0), axis=0, keepdims=True)
    i2 = jnp.min(jnp.where(rest & (e_prob == p2), erow, LANES), axis=0, keepdims=True)
    psum = p1 + p2
    w1 = g_w * (p1 / psum)
    w2 = g_w * (p2 / psum)
    ew_ref[...] = jnp.transpose(jnp.concatenate([w1, w2, jnp.zeros((LANES - 2, tm), F32)], axis=0))

    @pl.when((pl.program_id(0) == 0) & (pl.program_id(1) == 0))
    def _():
        cnt_ref[...] = jnp.zeros(cnt_ref.shape, F32)

    onehot_t = jnp.where((erow == i1) | (erow == i2), 1.0, 0.0).astype(BF16)
    upper = (lax.broadcasted_iota(I32, (tm, tm), 0) <= lax.broadcasted_iota(I32, (tm, tm), 1))
    cum_t = jnp.dot(onehot_t, jnp.where(upper, 1.0, 0.0).astype(BF16), preferred_element_type=F32)
    cnt_col = jnp.transpose(cnt_ref[...])[0:N_EXPERTS, 0:1]
    before = cum_t - onehot_t.astype(F32) + cnt_col
    r1 = jnp.sum(jnp.where(erow == i1, before, 0.0), axis=0, keepdims=True).astype(I32)
    r2 = jnp.sum(jnp.where(erow == i2, before, 0.0), axis=0, keepdims=True).astype(I32)
    eid_ref[...] = jnp.concatenate([i1, i2, r1, r2, jnp.zeros((4, tm), I32)], axis=0)
    padded = jnp.concatenate([onehot_t, jnp.zeros((LANES - N_EXPERTS, tm), BF16)], axis=0)
    tile_total = lax.dot_general(jnp.ones((8, tm), BF16), padded, (((1,), (1,)), ((), ())),
                                 preferred_element_type=F32)
    total = cnt_ref[...] + tile_total
    cnt_ref[...] = total
    counts_ref[...] = total


def _post_call(attn, z, gates, x, mod, wa, wc, wo, g2, wr, br, tm):
    bsz, s, d = x.shape
    nt = s // tm
    tok = lambda w: pl.BlockSpec((None, tm, w), lambda b, t: (b, t, 0))
    full = lambda arr: pl.BlockSpec(arr.shape, lambda b, t: (0,) * arr.ndim)
    return pl.pallas_call(
        _post_kernel,
        grid=(bsz, nt),
        in_specs=[tok(ATTN_WIDTH), tok(CONV_WIDTH), tok(2 * d), tok(d),
                  pl.BlockSpec((None, 6, d), lambda b, t: (b, 0, 0)),
                  full(wa), full(wc), full(wo), full(g2), full(wr), full(br)],
        out_specs=[tok(d), pl.BlockSpec((None, tm) + ROW_TILE, lambda b, t: (b, t, 0, 0)),
                   pl.BlockSpec((8, tm), lambda b, t: (0, b * nt + t)), tok(LANES),
                   pl.BlockSpec((8, LANES), lambda b, t: (0, 0))],
        out_shape=[jax.ShapeDtypeStruct((bsz, s, d), F32),
                   jax.ShapeDtypeStruct((bsz, s) + ROW_TILE, BF16),
                   jax.ShapeDtypeStruct((8, bsz * s), I32), jax.ShapeDtypeStruct((bsz, s, LANES), F32),
                   jax.ShapeDtypeStruct((8, LANES), F32)],
        scratch_shapes=[pltpu.VMEM((8, LANES), F32)],
        compiler_params=pltpu.CompilerParams(dimension_semantics=("arbitrary", "arbitrary"),
                                             vmem_limit_bytes=VMEM_LIMIT),
        name="post",
    )(attn, z, gates, x, mod, wa, wc, wo, g2, wr, br)


ROW_TILE = (D_MODEL // LANES, LANES)
EXPERT_ROWS = 256
TOKENS_PER_STEP = 1024


ISSUE_UNROLL = 8


def _dispatch_kernel(pos1_ref, pos2_ref, h2_ref, xs_in_ref, xs_ref, sem):
    del xs_in_ref
    tm = h2_ref.shape[0]
    base = pl.program_id(0) * tm

    def issue(r, c):
        pltpu.make_async_copy(h2_ref.at[r], xs_ref.at[pos1_ref[base + r]], sem).start(priority=0)
        pltpu.make_async_copy(h2_ref.at[r], xs_ref.at[pos2_ref[base + r]], sem).start(priority=1)
        return c

    lax.fori_loop(0, tm, issue, 0, unroll=ISSUE_UNROLL)
    for _ in range(2):
        pltpu.make_async_copy(h2_ref, xs_ref.at[pl.ds(0, tm)], sem).wait()


def _dispatch_call(pos1, pos2, h2_rows, n_rows):
    t = h2_rows.shape[0]
    tm = min(TOKENS_PER_STEP, t)
    xs0 = jnp.zeros((n_rows,) + ROW_TILE, BF16)
    return pl.pallas_call(
        _dispatch_kernel,
        grid_spec=pltpu.PrefetchScalarGridSpec(
            num_scalar_prefetch=2,
            grid=(t // tm,),
            in_specs=[pl.BlockSpec((tm,) + ROW_TILE, lambda i, p1, p2: (i, 0, 0)),
                      pl.BlockSpec(memory_space=pl.ANY)],
            out_specs=pl.BlockSpec(memory_space=pl.ANY),
            scratch_shapes=[pltpu.SemaphoreType.DMA(())]),
        out_shape=jax.ShapeDtypeStruct((n_rows,) + ROW_TILE, BF16),
        input_output_aliases={3: 0},
        compiler_params=pltpu.CompilerParams(dimension_semantics=("arbitrary",),
                                             vmem_limit_bytes=VMEM_LIMIT),
        name="dispatch",
    )(pos1, pos2, h2_rows, xs0)


def _expert_kernel(te_ref, nu_ref, xs_ref, wg_ref, wu_ref, wd_ref, ys_ref, wgb, wub, wdb):
    g = pl.program_id(0)
    e = te_ref[g]
    e_prev = te_ref[jnp.maximum(g - 1, 0)]

    @pl.when((g == 0) | (e != e_prev))
    def _():
        wgb[...] = wg_ref[...].astype(BF16)
        wub[...] = wu_ref[...].astype(BF16)
        wdb[...] = wd_ref[...].astype(BF16)

    @pl.when(g < nu_ref[0])
    def _():
        x = xs_ref[...].reshape(xs_ref.shape[0], D_MODEL)
        a = jnp.dot(x, wgb[...], preferred_element_type=F32)
        b = jnp.dot(x, wub[...], preferred_element_type=F32)
        hid = (a * jax.nn.sigmoid(a)) * b
        y = jnp.dot(hid.astype(BF16), wdb[...], preferred_element_type=F32)
        ys_ref[...] = y.astype(BF16).reshape(ys_ref.shape)

    @pl.when(g >= nu_ref[0])
    def _():
        ys_ref[...] = jnp.zeros(ys_ref.shape, BF16)


def _expert_call(tile_expert, n_used, xs, wg, wu, wd):
    n_rows, d = xs.shape[0], D_MODEL
    nt = n_rows // EXPERT_ROWS
    rows = pl.BlockSpec((EXPERT_ROWS,) + ROW_TILE, lambda g, te, nu: (g, 0, 0))
    wspec = lambda shp: pl.BlockSpec((None,) + shp, lambda g, te, nu: (te[g], 0, 0))
    return pl.pallas_call(
        _expert_kernel,
        grid_spec=pltpu.PrefetchScalarGridSpec(
            num_scalar_prefetch=2,
            grid=(nt,),
            in_specs=[rows, wspec((d, D_EXPERT)), wspec((d, D_EXPERT)), wspec((D_EXPERT, d))],
            out_specs=rows,
            scratch_shapes=[pltpu.VMEM((d, D_EXPERT), BF16), pltpu.VMEM((d, D_EXPERT), BF16),
                            pltpu.VMEM((D_EXPERT, d), BF16)]),
        out_shape=jax.ShapeDtypeStruct((n_rows,) + ROW_TILE, BF16),
        compiler_params=pltpu.CompilerParams(dimension_semantics=("arbitrary",),
                                             vmem_limit_bytes=VMEM_LIMIT),
        name="experts",
    )(tile_expert, n_used, xs, wg, wu, wd)


def _combine_kernel(pos1_ref, pos2_ref, ys_ref, ew_ref, hres_ref, g2_ref, gf_ref, o_ref, buf, sem):
    tm = hres_ref.shape[0]
    t = pl.program_id(0)
    nt = pl.num_programs(0)

    def issue(tile, slot):
        base = tile * tm

        def body(r, c):
            pltpu.make_async_copy(ys_ref.at[pos1_ref[base + r]], buf.at[slot, r],
                                  sem.at[slot]).start(priority=0)
            pltpu.make_async_copy(ys_ref.at[pos2_ref[base + r]], buf.at[slot, tm + r],
                                  sem.at[slot]).start(priority=1)
            return c

        lax.fori_loop(0, tm, body, 0, unroll=ISSUE_UNROLL)

    @pl.when(t == 0)
    def _():
        issue(0, 0)

    @pl.when(t + 1 < nt)
    def _():
        issue(t + 1, (t + 1) % 2)

    slot = t % 2
    pltpu.make_async_copy(ys_ref.at[pl.ds(0, 2 * tm)], buf.at[slot], sem.at[slot]).wait()

    y1 = buf[slot, 0:tm].astype(F32).reshape(tm, D_MODEL)
    y2 = buf[slot, tm:2 * tm].astype(F32).reshape(tm, D_MODEL)
    w = ew_ref[...]
    y = w[:, 0:1] * y1 + w[:, 1:2] * y2
    hfin = hres_ref[...] + g2_ref[...] * y
    ms = jnp.mean(hfin * hfin, axis=-1, keepdims=True)
    o_ref[...] = hfin * lax.rsqrt(ms + EPS) * gf_ref[...]


def _combine_call(pos1, pos2, ys_rows, ew, hres, gate2, gf):
    t, d = hres.shape
    tm = min(TOKENS_PER_STEP, t)
    per_batch = t // gate2.shape[0]
    assert per_batch % tm == 0
    tok = lambda w: pl.BlockSpec((tm, w), lambda i, *_: (i, 0))
    return pl.pallas_call(
        _combine_kernel,
        grid_spec=pltpu.PrefetchScalarGridSpec(
            num_scalar_prefetch=2,
            grid=(t // tm,),
            in_specs=[pl.BlockSpec(memory_space=pl.ANY), tok(LANES), tok(d),
                      pl.BlockSpec((None, 1, d), lambda i, *_: ((i * tm) // per_batch, 0, 0)),
                      pl.BlockSpec((1, d), lambda i, *_: (0, 0))],
            out_specs=tok(d),
            scratch_shapes=[pltpu.VMEM((2, 2 * tm) + ROW_TILE, BF16), pltpu.SemaphoreType.DMA((2,))]),
        out_shape=jax.ShapeDtypeStruct((t, d), F32),
        compiler_params=pltpu.CompilerParams(dimension_semantics=("arbitrary",),
                                             vmem_limit_bytes=VMEM_LIMIT),
        name="combine",
    )(pos1, pos2, ys_rows, ew, hres, gate2, gf)


def _slot_kernel(seg_ref, eid_ref, pos_ref):
    e = eid_ref[0:2, :]
    pos = eid_ref[2:4, :]
    for k in range(seg_ref.shape[0]):
        pos = pos + jnp.where(e == k, seg_ref[k], 0)
    pos_ref[...] = jnp.concatenate([pos, jnp.zeros((6, pos.shape[1]), I32)], axis=0)


def _slot_call(seg_start, eid):
    return pl.pallas_call(
        _slot_kernel,
        in_specs=[pl.BlockSpec(memory_space=pltpu.SMEM), pl.BlockSpec(eid.shape, lambda: (0, 0))],
        out_specs=pl.BlockSpec(eid.shape, lambda: (0, 0)),
        out_shape=jax.ShapeDtypeStruct(eid.shape, I32),
        name="slots",
    )(seg_start, eid)


def _sparse_moe(h2, eid, ew, counts, wg, wu, wd, hres, gate2, gf):
    bsz, s, d = hres.shape
    t = bsz * s
    ne = wg.shape[0]
    cnt = counts[0, :ne].astype(I32)
    seg_rows = ((cnt + EXPERT_ROWS - 1) // EXPERT_ROWS) * EXPERT_ROWS
    seg_end = jnp.cumsum(seg_rows)
    seg_start = seg_end - seg_rows
    pos = _slot_call(seg_start, eid)
    pos1, pos2 = pos[0], pos[1]
    n_tiles = (2 * t) // EXPERT_ROWS + ne
    tile_first_row = jnp.arange(n_tiles, dtype=I32) * EXPERT_ROWS
    tile_expert = jnp.minimum(jnp.sum(seg_end[None, :] <= tile_first_row[:, None], axis=1), ne - 1)
    n_used = (seg_end[ne - 1] // EXPERT_ROWS).reshape(1)
    n_rows = n_tiles * EXPERT_ROWS

    xs = _dispatch_call(pos1, pos2, h2.reshape((t,) + ROW_TILE), n_rows)
    ys = _expert_call(tile_expert.astype(I32), n_used.astype(I32), xs, wg, wu, wd)
    out = _combine_call(pos1, pos2, ys, ew.reshape(t, LANES), hres.reshape(t, d),
                        gate2.reshape(bsz, 1, d), gf.reshape(1, d))
    return out.reshape(bsz, s, d)


def kernel(x, c, w_ada, b_ada, norm1_g, w_in, rel_bias, conv_w, w_attn_branch, w_conv_branch, w_out,
           norm2_g, w_router_group, b_router_group, w_router_expert, b_router_expert,
           w_gate_e, w_up_e, w_down_e, norm_f_g):
    bsz, s, d = x.shape
    depth = w_ada.shape[0]
    h = x.astype(F32)
    c_pad = jnp.zeros((8, d), F32).at[:bsz].set(c.astype(F32))
    out = h
    for l in range(depth):
        mod = _ada_call(c_pad, w_ada[l].astype(F32), b_ada[l].astype(F32)[None, :])
        mod = mod[:bsz].reshape(bsz, 6, d)

        w = w_in[l]
        o_q, o_k, o_v, o_qi = 0, 512, 1024, 1536
        o_ki, o_wi, o_cb, o_gl = 2048, 2112, 2120, 3656
        wa = w[:, o_q:o_ki].astype(BF16)
        ki_w = w[:, o_ki:o_wi]
        wb = jnp.concatenate([ki_w, ki_w, w[:, o_wi:o_cb],
                              jnp.zeros((d, LANES - IDX_HEADS), w.dtype)], axis=1).astype(BF16)
        wc = w[:, o_cb:o_gl].astype(BF16)
        wd = w[:, o_gl:].astype(BF16)

        q, k, v, qi, kid, wi, z, gates = _proj_call(
            h, mod, norm1_g[l].astype(F32)[None, :], wa, wb, wc, wd, conv_w[l].astype(F32),
            tm=min(512, s))

        attn = _attn_call(rel_bias.astype(F32), q, qi, wi, k, v, kid, tq=min(TOPK_MAX, s // 4))

        wr32 = jnp.concatenate([w_router_expert[l], w_router_group[l],
                                jnp.zeros((d, LANES - N_EXPERTS - N_GROUPS), F32)], axis=1).astype(F32)
        wr_hi = wr32.astype(BF16)
        wr = jnp.stack([wr_hi, (wr32 - wr_hi.astype(F32)).astype(BF16)])
        br = jnp.concatenate([b_router_expert[l], b_router_group[l],
                              jnp.zeros((LANES - N_EXPERTS - N_GROUPS,), F32)])[None, :].astype(F32)
        hres, h2, eid, ew, counts = _post_call(
            attn, z, gates, h, mod, w_attn_branch[l].astype(BF16), w_conv_branch[l].astype(BF16),
            w_out[l].astype(BF16), norm2_g[l].astype(F32)[None, :], wr, br, tm=min(512, s))

        assert depth == 1, "the final RMSNorm is fused into the combine kernel of the only layer"
        out = _sparse_moe(h2, eid, ew, counts, w_gate_e[l].astype(F32), w_up_e[l].astype(F32),
                          w_down_e[l].astype(F32), hres, mod[:, 5, :], norm_f_g.astype(F32))
        h = out
    return out.astype(x.dtype)
```

```python
import functools
import math

import numpy as np
import jax
import jax.numpy as jnp
from jax import lax
from jax.experimental import pallas as pl
from jax.experimental.pallas import tpu as pltpu

F32 = jnp.float32
BF16 = jnp.bfloat16
I32 = jnp.int32

D_MODEL = 1024
N_HEADS = 8
HEAD_DIM = 64
ATTN_WIDTH = N_HEADS * HEAD_DIM
IDX_HEADS = 8
IDX_DIM = 64
TOPK_MAX = 256
N_BUCKETS = 32
MAX_DISTANCE = 128
CONV_WIDTH = 512
CONV_K = 3
N_GROUPS = 4
EXPERTS_PER_GROUP = 8
N_EXPERTS = N_GROUPS * EXPERTS_PER_GROUP
D_EXPERT = 512
EPS = 1e-6

LANES = 128
VMEM_LIMIT = 56 * 1024 * 1024

LOG2E = math.log2(math.e)
MASK_NEG = -1e30
SCORE_NEG = float(np.finfo(np.float32).min)
TINY_F32 = float(np.finfo(np.float32).tiny)
KEY_SPACE_AFTER = 40
RADIX_BITS_PER_STEP = 4


def _t5_bucket_starts():
    max_exact = N_BUCKETS // 2
    n = np.arange(0, MAX_DISTANCE + 1)
    nf = np.maximum(n, 1).astype(np.float32)
    val = (np.log(nf / np.float32(max_exact)) / np.float32(math.log(MAX_DISTANCE / max_exact))
           * np.float32(N_BUCKETS - max_exact)).astype(np.float32)
    inner = val[max_exact + 1:MAX_DISTANCE]
    assert np.min(np.abs(inner - np.round(inner))) > 1e-3
    large = np.minimum(max_exact + val.astype(np.int32), N_BUCKETS - 1)
    bucket = np.where(n < max_exact, n, large)
    assert bucket[MAX_DISTANCE] == N_BUCKETS - 1 and np.all(np.diff(bucket) >= 0)
    return [int(np.argmax(bucket >= b)) for b in range(N_BUCKETS)]


_BUCKET_START = _t5_bucket_starts()


def _ada_kernel(c_ref, w_ref, b_ref, o_ref):
    c = c_ref[...]
    ca = c * jax.nn.sigmoid(c)
    o_ref[...] = jnp.dot(ca, w_ref[...], preferred_element_type=F32,
                         precision=lax.Precision.HIGHEST) + b_ref[...]


def _ada_call(c_pad, w, b):
    rows, d = c_pad.shape
    n = w.shape[1]
    tn = 1536
    return pl.pallas_call(
        _ada_kernel,
        grid=(n // tn,),
        in_specs=[pl.BlockSpec((rows, d), lambda j: (0, 0)),
                  pl.BlockSpec((d, tn), lambda j: (0, j)),
                  pl.BlockSpec((1, tn), lambda j: (0, j))],
        out_specs=pl.BlockSpec((rows, tn), lambda j: (0, j)),
        out_shape=jax.ShapeDtypeStruct((rows, n), F32),
        compiler_params=pltpu.CompilerParams(dimension_semantics=("arbitrary",),
                                             vmem_limit_bytes=VMEM_LIMIT),
        name="ada",
    )(c_pad, w, b)


def _rms_mod(x, g, scale, shift):
    ms = jnp.mean(x * x, axis=-1, keepdims=True)
    y = x * lax.rsqrt(ms + EPS) * g
    return y * (1.0 + scale) + shift


def _proj_kernel(x_ref, mod_ref, g_ref, wa_ref, wb_ref, wc_ref, wd_ref, cw_ref,
                 q_ref, k_ref, v_ref, qi_ref, kid_ref, wi_ref, z_ref, gate_ref, ubuf, *, tm):
    h = _rms_mod(x_ref[...], g_ref[...], mod_ref[1:2, :], mod_ref[0:1, :])
    hb = h.astype(BF16)

    a = jnp.dot(hb, wa_ref[...], preferred_element_type=F32)
    q_ref[...] = (a[:, 0:512] * (HEAD_DIM ** -0.5 * LOG2E)).astype(BF16)
    k_ref[...] = a[:, 512:1024].astype(BF16)
    v_ref[...] = a[:, 1024:1536].astype(BF16)
    qi_ref[...] = a[:, 1536:2048].astype(BF16)

    b = jnp.dot(hb, wb_ref[...], preferred_element_type=F32)
    kid_ref[...] = b[:, 0:LANES].astype(BF16)
    wi_ref[...] = b[:, LANES:2 * LANES] * ((IDX_DIM ** -0.5) * (IDX_HEADS ** -0.5))

    cpart = jnp.dot(hb, wc_ref[...], preferred_element_type=F32)
    cb = cpart[:, 0:512]
    u = cpart[:, 512:1024] * cpart[:, 1024:1536]

    @pl.when(pl.program_id(1) == 0)
    def _():
        ubuf[0:8, :] = jnp.zeros((8, CONV_WIDTH), F32)

    ubuf[8:tm + 8, :] = u
    conv = (cw_ref[0:1, :] * ubuf[6:tm + 6, :] + cw_ref[1:2, :] * ubuf[7:tm + 7, :]
            + cw_ref[2:3, :] * u)
    z_ref[...] = (cb * conv).astype(BF16)
    ubuf[0:8, :] = ubuf[tm:tm + 8, :]

    d = jnp.dot(hb, wd_ref[...], preferred_element_type=F32)
    gate_ref[...] = jax.nn.sigmoid(d).astype(BF16)


def _proj_call(x, mod, g1, wa, wb, wc, wd, cw, tm):
    bsz, s, d = x.shape
    nt = s // tm
    tok = lambda w: pl.BlockSpec((None, tm, w), lambda b, t: (b, t, 0))
    full = lambda arr: pl.BlockSpec(arr.shape, lambda b, t: (0,) * arr.ndim)
    out_widths = [(512, BF16), (512, BF16), (512, BF16), (512, BF16), (LANES, BF16), (LANES, F32),
                  (512, BF16), (2 * D_MODEL, BF16)]
    return pl.pallas_call(
        functools.partial(_proj_kernel, tm=tm),
        grid=(bsz, nt),
        in_specs=[tok(d), pl.BlockSpec((None, 6, d), lambda b, t: (b, 0, 0)), full(g1),
                  full(wa), full(wb), full(wc), full(wd), full(cw)],
        out_specs=[tok(w) for w, _ in out_widths],
        out_shape=[jax.ShapeDtypeStruct((bsz, s, w), dt) for w, dt in out_widths],
        scratch_shapes=[pltpu.VMEM((tm + 8, CONV_WIDTH), F32)],
        compiler_params=pltpu.CompilerParams(dimension_semantics=("arbitrary", "arbitrary"),
                                             vmem_limit_bytes=VMEM_LIMIT),
        name="proj",
    )(x, mod, g1, wa, wb, wc, wd, cw)


def _to_key(f):
    b = pltpu.bitcast(f, I32)
    return jnp.where(b < 0, b ^ 0x7FFFFFFF, b)


def _from_key(k):
    return pltpu.bitcast(jnp.where(k < 0, k ^ 0x7FFFFFFF, k), F32)


def _bit_transpose32(words):
    a = list(words)
    j, m = 16, 0x0000FFFF
    while j:
        k = 0
        while k < 32:
            t = (a[k] ^ lax.shift_right_logical(a[k + j], jnp.int32(j))) & jnp.int32(m)
            a[k] = a[k] ^ t
            a[k + j] = a[k + j] ^ lax.shift_left(t, jnp.int32(j))
            k = (k + j + 1) & ~j
        j >>= 1
        m = (m ^ (m << j)) & 0xFFFFFFFF
    return a


def _fold_lanes(x, op):
    parts = [x[:, t * LANES:(t + 1) * LANES] for t in range(x.shape[1] // LANES)]
    while len(parts) > 1:
        nxt = [op(parts[a], parts[a + 1]) for a in range(0, len(parts) - 1, 2)]
        parts = nxt + ([parts[-1]] if len(parts) % 2 else [])
    return parts[0]


def _tile_lanes(x, n):
    return x if n == 1 else jnp.concatenate([x] * n, axis=1)


def _attn_kernel(rb_ref, q_ref, qi_ref, wi_ref, qin_ref, win_ref, k_ref, v_ref, kid_ref, o_ref,
                 sct_ref, planes_ref, cand_ref, mb_ref, tb_ref, qm_ref, qim_ref, m_ref, l_ref, acc_ref,
                 *, tq, topk):
    c = tq
    i = pl.program_id(1)
    nt_dims = (((1,), (1,)), ((), ()))
    lane = lax.broadcasted_iota(I32, (tq, LANES), 1)
    low_half = lane < HEAD_DIM
    row = lax.broadcasted_iota(I32, (tq, c), 0)
    col = lax.broadcasted_iota(I32, (tq, c), 1)

    @pl.when((pl.program_id(0) == 0) & (i == 0))
    def _():
        for p in range(32):
            planes_ref[p] = jnp.zeros(planes_ref.shape[1:], I32)
        for near in range(2):
            rel = row - col + (1 - near) * c
            n = jnp.maximum(rel, 0)
            for h in range(N_HEADS):
                t = jnp.full((tq, c), rb_ref[0, h], F32)
                for b in range(1, N_BUCKETS):
                    t = jnp.where(n >= _BUCKET_START[b], rb_ref[b, h], t)
                t = (t - rb_ref[N_BUCKETS - 1, h]) * LOG2E
                tb_ref[h, :, near * c:(near + 1) * c] = jnp.where(rel >= 0, t, MASK_NEG)


    def masked_heads(src_ref, dst_ref):
        for h in range(N_HEADS):
            hp, half = divmod(h, 2)
            keep = low_half if half == 0 else jnp.logical_not(low_half)
            pair = src_ref[:, hp * LANES:(hp + 1) * LANES]
            dst_ref[h] = jnp.where(keep, pair, jnp.zeros_like(pair))

    krow = lax.broadcasted_iota(I32, (c, tq), 0)
    qcol = lax.broadcasted_iota(I32, (c, tq), 1)
    int_min = jnp.int32(-2 ** 31)

    def fold_rows(x, op):
        parts = [x[r * 8:(r + 1) * 8, :] for r in range(x.shape[0] // 8)]
        while len(parts) > 1:
            nxt = [op(parts[a], parts[a + 1]) for a in range(0, len(parts) - 1, 2)]
            parts = nxt + ([parts[-1]] if len(parts) % 2 else [])
        return parts[0]

    def score_chunk(j, diag, w_t):
        start = pl.multiple_of(j * c, c)
        kc = kid_ref[pl.ds(start, c), :]
        acc = jnp.zeros((c, tq), F32)
        for h in range(N_HEADS):
            d = lax.dot_general(kc, qim_ref[h], nt_dims, preferred_element_type=F32)
            acc = acc + jnp.maximum(d, 0.0) * w_t[h:h + 1, :]
        ukey = _to_key(acc) ^ int_min
        if diag:
            causal = krow <= qcol
            acc = jnp.where(causal, acc, SCORE_NEG)
            ukey = jnp.where(causal, ukey, 0)
        sct_ref[pl.ds(start, c), :] = acc
        prow = pl.multiple_of(j * (c // 32), c // 32)
        for lt in range(tq // LANES):
            words = [ukey[k * 8:(k + 1) * 8, lt * LANES:(lt + 1) * LANES] for k in range(32)]
            for p, plane in enumerate(_bit_transpose32(words)):
                planes_ref[p, pl.ds(prow, c // 32), lt * LANES:(lt + 1) * LANES] = plane

    @pl.when(i == 0)
    def _():
        masked_heads(qi_ref, qim_ref)
        score_chunk(0, True, jnp.transpose(wi_ref[...])[0:IDX_HEADS, :])

    nchunks = i + 1
    n_causal = i * tq + lax.broadcasted_iota(I32, (1, tq), 1) + 1
    prow_iota = lax.broadcasted_iota(I32, cand_ref.shape, 0)
    cand_ref[...] = jnp.where(prow_iota < nchunks * (c // 32), -1, 0)

    def radix_cond(st):
        return (st[0] < 32) & (st[1] > 0)

    def radix_step(st):
        p0, _, prefix, need, ncand, settled = st
        cand = cand_ref[...]
        for sub in range(RADIX_BITS_PER_STEP):
            p = p0 + sub
            plane = planes_ref[p]
            ones = cand & plane
            cnt1 = jnp.sum(fold_rows(lax.population_count(ones), jnp.add), axis=0, keepdims=True)
            take1 = cnt1 >= need
            live = settled == 0
            need = jnp.where(live & jnp.logical_not(take1), need - cnt1, need)
            ncand = jnp.where(live, jnp.where(take1, cnt1, ncand - cnt1), ncand)
            prefix = jnp.where(live & take1, prefix | jnp.left_shift(jnp.int32(1), 31 - p), prefix)
            cand = cand & (plane ^ jnp.where(take1, 0, -1))
            settled = jnp.where(ncand == need, 1, settled)
        cand_ref[...] = cand
        return p0 + RADIX_BITS_PER_STEP, jnp.sum(1 - settled), prefix, need, ncand, settled

    done0 = jnp.where(n_causal <= topk, 1, 0)
    _, _, prefix, _, _, _ = lax.while_loop(
        radix_cond, radix_step,
        (jnp.int32(0), jnp.sum(1 - done0), jnp.zeros((1, tq), I32), jnp.full((1, tq), topk, I32),
         jnp.full((1, tq), 1, I32) * (nchunks * c), done0))
    guess_key = prefix ^ int_min
    guess = _from_key(guess_key)

    def count_keys(pred):
        def body(j, part):
            start = pl.multiple_of(j * c, c)
            hit = pred(sct_ref[pl.ds(start, c), :], start)
            return part + fold_rows(jnp.where(hit, 1.0, 0.0), jnp.add)
        part = lax.fori_loop(0, nchunks, body, jnp.zeros((8, tq), F32))
        return jnp.sum(part, axis=0, keepdims=True)

    kf = float(topk)
    guess_above = _from_key(guess_key + 1)
    guess_above = jnp.where(jnp.abs(guess_above) < TINY_F32,
                            jnp.where(guess >= 0.0, TINY_F32, 0.0), guess_above)
    guess_below = _from_key(guess_key - 2)

    def cond(st):
        return st[0] > 0

    def step(st):
        _, it, lo, hi, c_lo, c_hi, done = st
        mid = 0.5 * lo + 0.5 * hi
        lo_k, hi_k = _to_key(lo), _to_key(hi)
        mid_key = _from_key((lo_k >> 1) + (hi_k >> 1) + (lo_k & hi_k & 1))
        mid = jnp.where(it >= KEY_SPACE_AFTER, mid_key, mid)
        probe = jnp.where(it == 0, guess, jnp.where(lo == guess, guess_above, guess_below))
        mid = jnp.where((it <= 1) & (probe > lo) & (probe < hi), probe, mid)
        go = (done == 0) & (mid > lo) & (mid < hi)
        cnt = count_keys(lambda tile, start: tile >= mid)
        ge = cnt >= kf
        up = ge & go
        dn = jnp.logical_not(ge) & go
        lo = jnp.where(up, mid, lo)
        hi = jnp.where(dn, mid, hi)
        c_lo = jnp.where(up, cnt, c_lo)
        c_hi = jnp.where(dn, cnt, c_hi)
        adjacent = (lo == guess) & (hi == guess_above)
        done = jnp.where(go & (c_lo != kf) & jnp.logical_not(adjacent), 0, 1)
        return jnp.sum(1 - done), it + 1, lo, hi, c_lo, c_hi, done

    _, _, thr, _, c_lo, c_hi, _ = lax.while_loop(
        cond, step, (jnp.sum(1 - done0), jnp.int32(0), jnp.full((1, tq), SCORE_NEG, F32),
                     jnp.full((1, tq), -SCORE_NEG, F32), jnp.full((1, tq), 1.0, F32) * (nchunks * c),
                     jnp.zeros((1, tq), F32), done0))

    def write_mask(selected):
        def tiles(start, rows):
            tile = sct_ref[pl.ds(start, rows), :]
            key = lax.broadcasted_iota(I32, (rows, tq), 0) + start
            mb_ref[:, pl.ds(start, rows)] = jnp.transpose(
                jnp.where(selected(tile, key), 0.0, MASK_NEG))

        def body(j, carry):
            tiles(pl.multiple_of(j * 2 * c, 2 * c), 2 * c)
            return carry

        lax.fori_loop(0, nchunks // 2, body, 0)

        @pl.when(nchunks % 2 == 1)
        def _():
            tiles(pl.multiple_of((nchunks - 1) * c, c), c)

    tied = (n_causal > topk) & (c_lo > kf)
    any_tied = jnp.sum(jnp.where(tied, 1, 0)) > 0

    @pl.when(any_tied)
    def _():
        n_take = kf - c_hi

        def bis(_, st):
            lo_x, hi_x = st
            mid = (lo_x + hi_x) >> 1
            ok = count_keys(lambda tile, start: (tile == thr) & ((krow + start) <= mid)) >= n_take
            return jnp.where(ok, lo_x, mid), jnp.where(ok, mid, hi_x)

        nbits = int(math.ceil(math.log2(sct_ref.shape[0]))) + 1
        _, hi_x = lax.fori_loop(0, nbits, bis, (jnp.full((1, tq), -1, I32),
                                                jnp.full((1, tq), sct_ref.shape[0], I32)))
        cut = jnp.where(tied, hi_x, jnp.int32(2 ** 30))

        write_mask(lambda tile, key: (tile > thr) | ((tile == thr) & (key <= cut)))

    @pl.when(jnp.logical_not(any_tied))
    def _():
        write_mask(lambda tile, key: tile >= thr)

    masked_heads(q_ref, qm_ref)
    masked_heads(qin_ref, qim_ref)
    w_t_next = jnp.transpose(win_ref[...])[0:IDX_HEADS, :]
    has_next = i + 1 < pl.num_programs(1)
    m_ref[...] = jnp.full(m_ref.shape, MASK_NEG, F32)
    l_ref[...] = jnp.zeros(l_ref.shape, F32)
    acc_ref[...] = jnp.zeros(acc_ref.shape, F32)

    def attend(start, width, near):
        nt = width // LANES
        ones = jnp.ones((width, LANES), BF16)
        for h in range(N_HEADS):
            hp = h // 2
            kp = k_ref[pl.ds(start, width), hp * LANES:(hp + 1) * LANES]
            vp = v_ref[pl.ds(start, width), hp * LANES:(hp + 1) * LANES]
            s = lax.dot_general(qm_ref[h], kp, nt_dims, preferred_element_type=F32)
            s = s + mb_ref[:, pl.ds(start, width)]
            if near:
                s = s + tb_ref[h, :, 2 * c - width:2 * c]
            m_old = m_ref[h]
            m_new = jnp.maximum(m_old, jnp.max(_fold_lanes(s, jnp.maximum), axis=1, keepdims=True))
            p = jnp.exp2(s - _tile_lanes(m_new, nt)).astype(BF16)
            pv = jnp.dot(p, jnp.concatenate([vp, ones], axis=1), preferred_element_type=F32)
            alpha = jnp.exp2(m_old - m_new)
            m_ref[h] = m_new
            l_ref[h] = alpha * l_ref[h] + pv[:, LANES:]
            acc_ref[h] = alpha * acc_ref[h] + pv[:, :LANES]

    wide = 2 * c
    n_far = jnp.maximum(i - 1, 0)

    n_wide = n_far // 2

    def far_body(j, _):
        score_chunk(2 * j, False, w_t_next)
        attend(pl.multiple_of(j * wide, wide), wide, False)
        score_chunk(2 * j + 1, False, w_t_next)
        return 0

    lax.fori_loop(0, n_wide, far_body, 0)

    @pl.when(n_far % 2 == 1)
    def _():
        attend(pl.multiple_of((n_far - 1) * c, c), c, False)
        score_chunk(n_far - 1, False, w_t_next)

    @pl.when(i >= 1)
    def _():
        score_chunk(i - 1, False, w_t_next)
        attend(pl.multiple_of((i - 1) * c, c), wide, True)
        score_chunk(i, False, w_t_next)

    @pl.when(i == 0)
    def _():
        attend(0, c, True)
        score_chunk(0, False, w_t_next)

    @pl.when(has_next)
    def _():
        score_chunk(i + 1, True, w_t_next)

    for hp in range(N_HEADS // 2):
        lo = acc_ref[2 * hp] / l_ref[2 * hp]
        hi = acc_ref[2 * hp + 1] / l_ref[2 * hp + 1]
        o_ref[:, hp * LANES:(hp + 1) * LANES] = jnp.where(low_half, lo, hi).astype(BF16)


def _attn_call(rel_bias, q, qi, wi, k, v, kid, tq):
    bsz, s, _ = q.shape
    nq = s // tq
    topk = min(TOPK_MAX, s // 4)
    assert tq % LANES == 0 and tq >= MAX_DISTANCE
    blk = lambda w: pl.BlockSpec((None, tq, w), lambda b, i: (b, i, 0))
    nxt = lambda w: pl.BlockSpec((None, tq, w), lambda b, i: (b, jnp.minimum(i + 1, nq - 1), 0))
    whole = lambda w: pl.BlockSpec((None, s, w), lambda b, i: (b, 0, 0), pipeline_mode=pl.Buffered(1))
    return pl.pallas_call(
        functools.partial(_attn_kernel, tq=tq, topk=topk),
        grid=(bsz, nq),
        in_specs=[pl.BlockSpec(memory_space=pltpu.SMEM),
                  blk(ATTN_WIDTH), blk(ATTN_WIDTH), blk(LANES), nxt(ATTN_WIDTH), nxt(LANES),
                  whole(ATTN_WIDTH), whole(ATTN_WIDTH), whole(LANES)],
        out_specs=blk(ATTN_WIDTH),
        out_shape=jax.ShapeDtypeStruct((bsz, s, ATTN_WIDTH), BF16),
        scratch_shapes=[
            pltpu.VMEM((s, tq), F32),
            pltpu.VMEM((32, s // 32, tq), I32),
            pltpu.VMEM((s // 32, tq), I32),
            pltpu.VMEM((tq, s), F32),
            pltpu.VMEM((N_HEADS, tq, 2 * tq), F32),
            pltpu.VMEM((N_HEADS, tq, LANES), BF16),
            pltpu.VMEM((N_HEADS, tq, LANES), BF16),
            pltpu.VMEM((N_HEADS, tq, LANES), F32),
            pltpu.VMEM((N_HEADS, tq, LANES), F32),
            pltpu.VMEM((N_HEADS, tq, LANES), F32),
        ],
        compiler_params=pltpu.CompilerParams(dimension_semantics=("arbitrary", "arbitrary"),
                                             vmem_limit_bytes=VMEM_LIMIT),
        name="attn",
    )(rel_bias, q, qi, wi, qi, wi, k, v, kid)


def _post_kernel(attn_ref, z_ref, gate_ref, x_ref, mod_ref, wa_ref, wc_ref, wo_ref, g2_ref,
                 wr_ref, br_ref, hres_ref, h2_ref, eid_ref, ew_ref, counts_ref, cnt_ref):
    ya = jnp.dot(attn_ref[...], wa_ref[...], preferred_element_type=F32)
    yc = jnp.dot(z_ref[...], wc_ref[...], preferred_element_type=F32)
    merged = gate_ref[:, 0:D_MODEL].astype(F32) * ya + gate_ref[:, D_MODEL:].astype(F32) * yc
    o = jnp.dot(merged.astype(BF16), wo_ref[...], preferred_element_type=F32)
    hres = x_ref[...] + mod_ref[2:3, :] * o
    hres_ref[...] = hres
    h2 = _rms_mod(hres, g2_ref[...], mod_ref[4:5, :], mod_ref[3:4, :])
    h2_ref[...] = h2.astype(BF16).reshape(h2_ref.shape)

    h2_hi = h2.astype(BF16)
    h2_lo = (h2 - h2_hi.astype(F32)).astype(BF16)
    lg = (jnp.dot(h2_hi, wr_ref[0], preferred_element_type=F32)
          + (jnp.dot(h2_lo, wr_ref[0], preferred_element_type=F32)
             + jnp.dot(h2_hi, wr_ref[1], preferred_element_type=F32))) + br_ref[...]
    tm = lg.shape[0]
    lgt = jnp.transpose(lg)
    ninf = -jnp.inf
    grow = lax.broadcasted_iota(I32, (8, tm), 0)
    gl = jnp.where(grow < N_GROUPS, lgt[N_EXPERTS:N_EXPERTS + 8, :], ninf)
    gmax = jnp.max(gl, axis=0, keepdims=True)
    g_sel = jnp.min(jnp.where(gl == gmax, grow, N_GROUPS), axis=0, keepdims=True)
    g_w = 1.0 / jnp.sum(jnp.exp(gl - gmax), axis=0, keepdims=True)

    erow = lax.broadcasted_iota(I32, (N_EXPERTS, tm), 0)
    emask = (erow // EXPERTS_PER_GROUP) == g_sel
    el = jnp.where(emask, lgt[0:N_EXPERTS, :], ninf)
    emax = jnp.max(el, axis=0, keepdims=True)
    ee = jnp.exp(el - emax)
    e_prob = ee / jnp.sum(ee, axis=0, keepdims=True)
    p1 = jnp.max(jnp.where(emask, e_prob, -1.0), axis=0, keepdims=True)
    i1 = jnp.min(jnp.where(emask & (e_prob == p1), erow, LANES), axis=0, keepdims=True)
    rest = emask & (erow != i1)
    p2 = jnp.max(jnp.where(rest, e_prob, -1.0), axis=0, keepdims=True)
    i2 = jnp.min(jnp.where(rest & (e_prob == p2), erow, LANES), axis=0, keepdims=True)
    psum = p1 + p2
    w1 = g_w * (p1 / psum)
    w2 = g_w * (p2 / psum)
    ew_ref[...] = jnp.transpose(jnp.concatenate([w1, w2, jnp.zeros((LANES - 2, tm), F32)], axis=0))

    @pl.when((pl.program_id(0) == 0) & (pl.program_id(1) == 0))
    def _():
        cnt_ref[...] = jnp.zeros(cnt_ref.shape, F32)

    onehot_t = jnp.where((erow == i1) | (erow == i2), 1.0, 0.0).astype(BF16)
    upper = (lax.broadcasted_iota(I32, (tm, tm), 0) <= lax.broadcasted_iota(I32, (tm, tm), 1))
    cum_t = jnp.dot(onehot_t, jnp.where(upper, 1.0, 0.0).astype(BF16), preferred_element_type=F32)
    cnt_col = jnp.transpose(cnt_ref[...])[0:N_EXPERTS, 0:1]
    before = cum_t - onehot_t.astype(F32) + cnt_col
    r1 = jnp.sum(jnp.where(erow == i1, before, 0.0), axis=0, keepdims=True).astype(I32)
    r2 = jnp.sum(jnp.where(erow == i2, before, 0.0), axis=0, keepdims=True).astype(I32)
    eid_ref[...] = jnp.concatenate([i1, i2, r1, r2, jnp.zeros((4, tm), I32)], axis=0)
    padded = jnp.concatenate([onehot_t, jnp.zeros((LANES - N_EXPERTS, tm), BF16)], axis=0)
    tile_total = lax.dot_general(jnp.ones((8, tm), BF16), padded, (((1,), (1,)), ((), ())),
                                 preferred_element_type=F32)
    total = cnt_ref[...] + tile_total
    cnt_ref[...] = total
    counts_ref[...] = total


def _post_call(attn, z, gates, x, mod, wa, wc, wo, g2, wr, br, tm):
    bsz, s, d = x.shape
    nt = s // tm
    tok = lambda w: pl.BlockSpec((None, tm, w), lambda b, t: (b, t, 0))
    full = lambda arr: pl.BlockSpec(arr.shape, lambda b, t: (0,) * arr.ndim)
    return pl.pallas_call(
        _post_kernel,
        grid=(bsz, nt),
        in_specs=[tok(ATTN_WIDTH), tok(CONV_WIDTH), tok(2 * d), tok(d),
                  pl.BlockSpec((None, 6, d), lambda b, t: (b, 0, 0)),
                  full(wa), full(wc), full(wo), full(g2), full(wr), full(br)],
        out_specs=[tok(d), pl.BlockSpec((None, tm) + ROW_TILE, lambda b, t: (b, t, 0, 0)),
                   pl.BlockSpec((8, tm), lambda b, t: (0, b * nt + t)), tok(LANES),
                   pl.BlockSpec((8, LANES), lambda b, t: (0, 0))],
        out_shape=[jax.ShapeDtypeStruct((bsz, s, d), F32),
                   jax.ShapeDtypeStruct((bsz, s) + ROW_TILE, BF16),
                   jax.ShapeDtypeStruct((8, bsz * s), I32), jax.ShapeDtypeStruct((bsz, s, LANES), F32),
                   jax.ShapeDtypeStruct((8, LANES), F32)],
        scratch_shapes=[pltpu.VMEM((8, LANES), F32)],
        compiler_params=pltpu.CompilerParams(dimension_semantics=("arbitrary", "arbitrary"),
                                             vmem_limit_bytes=VMEM_LIMIT),
        name="post",
    )(attn, z, gates, x, mod, wa, wc, wo, g2, wr, br)


ROW_TILE = (D_MODEL // LANES, LANES)
EXPERT_ROWS = 256
SCATTER_TOKENS = 1024
GATHER_TOKENS = 512


ISSUE_UNROLL = 8


def _dispatch_kernel(pos1_ref, pos2_ref, h2_ref, xs_in_ref, xs_ref, sem):
    del xs_in_ref
    tm = h2_ref.shape[0]
    base = pl.program_id(0) * tm

    def issue(r, c):
        pltpu.make_async_copy(h2_ref.at[r], xs_ref.at[pos1_ref[base + r]], sem).start(priority=0)
        pltpu.make_async_copy(h2_ref.at[r], xs_ref.at[pos2_ref[base + r]], sem).start(priority=1)
        return c

    lax.fori_loop(0, tm, issue, 0, unroll=ISSUE_UNROLL)
    for _ in range(2):
        pltpu.make_async_copy(h2_ref, xs_ref.at[pl.ds(0, tm)], sem).wait()


def _dispatch_call(pos1, pos2, h2_rows, n_rows):
    t = h2_rows.shape[0]
    tm = min(SCATTER_TOKENS, t)
    xs0 = jnp.zeros((n_rows,) + ROW_TILE, BF16)
    return pl.pallas_call(
        _dispatch_kernel,
        grid_spec=pltpu.PrefetchScalarGridSpec(
            num_scalar_prefetch=2,
            grid=(t // tm,),
            in_specs=[pl.BlockSpec((tm,) + ROW_TILE, lambda i, p1, p2: (i, 0, 0)),
                      pl.BlockSpec(memory_space=pl.ANY)],
            out_specs=pl.BlockSpec(memory_space=pl.ANY),
            scratch_shapes=[pltpu.SemaphoreType.DMA(())]),
        out_shape=jax.ShapeDtypeStruct((n_rows,) + ROW_TILE, BF16),
        input_output_aliases={3: 0},
        compiler_params=pltpu.CompilerParams(dimension_semantics=("arbitrary",),
                                             vmem_limit_bytes=VMEM_LIMIT),
        name="dispatch",
    )(pos1, pos2, h2_rows, xs0)


def _expert_kernel(te_ref, nu_ref, xs_ref, wg_ref, wu_ref, wd_ref, ys_ref, wgb, wub, wdb):
    g = pl.program_id(0)
    e = te_ref[g]
    e_prev = te_ref[jnp.maximum(g - 1, 0)]

    @pl.when((g == 0) | (e != e_prev))
    def _():
        wgb[...] = wg_ref[...].astype(BF16)
        wub[...] = wu_ref[...].astype(BF16)
        wdb[...] = wd_ref[...].astype(BF16)

    @pl.when(g < nu_ref[0])
    def _():
        x = xs_ref[...].reshape(xs_ref.shape[0], D_MODEL)
        a = jnp.dot(x, wgb[...], preferred_element_type=F32)
        b = jnp.dot(x, wub[...], preferred_element_type=F32)
        hid = (a * jax.nn.sigmoid(a)) * b
        y = jnp.dot(hid.astype(BF16), wdb[...], preferred_element_type=F32)
        ys_ref[...] = y.astype(BF16).reshape(ys_ref.shape)

    @pl.when(g >= nu_ref[0])
    def _():
        ys_ref[...] = jnp.zeros(ys_ref.shape, BF16)


def _expert_call(tile_expert, n_used, xs, wg, wu, wd):
    n_rows, d = xs.shape[0], D_MODEL
    nt = n_rows // EXPERT_ROWS
    rows = pl.BlockSpec((EXPERT_ROWS,) + ROW_TILE, lambda g, te, nu: (g, 0, 0))
    wspec = lambda shp: pl.BlockSpec((None,) + shp, lambda g, te, nu: (te[g], 0, 0))
    return pl.pallas_call(
        _expert_kernel,
        grid_spec=pltpu.PrefetchScalarGridSpec(
            num_scalar_prefetch=2,
            grid=(nt,),
            in_specs=[rows, wspec((d, D_EXPERT)), wspec((d, D_EXPERT)), wspec((D_EXPERT, d))],
            out_specs=rows,
            scratch_shapes=[pltpu.VMEM((d, D_EXPERT), BF16), pltpu.VMEM((d, D_EXPERT), BF16),
                            pltpu.VMEM((D_EXPERT, d), BF16)]),
        out_shape=jax.ShapeDtypeStruct((n_rows,) + ROW_TILE, BF16),
        compiler_params=pltpu.CompilerParams(dimension_semantics=("arbitrary",),
                                             vmem_limit_bytes=VMEM_LIMIT),
        name="experts",
    )(tile_expert, n_used, xs, wg, wu, wd)


def _combine_kernel(pos1_ref, pos2_ref, ys_ref, ew_ref, hres_ref, g2_ref, gf_ref, o_ref, buf, sem):
    tm = hres_ref.shape[0]
    t = pl.program_id(0)
    nt = pl.num_programs(0)

    def issue(tile, slot):
        base = tile * tm

        def body(r, c):
            pltpu.make_async_copy(ys_ref.at[pos1_ref[base + r]], buf.at[slot, r],
                                  sem.at[slot]).start(priority=0)
            pltpu.make_async_copy(ys_ref.at[pos2_ref[base + r]], buf.at[slot, tm + r],
                                  sem.at[slot]).start(priority=1)
            return c

        lax.fori_loop(0, tm, body, 0, unroll=ISSUE_UNROLL)

    @pl.when(t == 0)
    def _():
        issue(0, 0)

    @pl.when(t + 1 < nt)
    def _():
        issue(t + 1, (t + 1) % 2)

    slot = t % 2
    pltpu.make_async_copy(ys_ref.at[pl.ds(0, 2 * tm)], buf.at[slot], sem.at[slot]).wait()

    y1 = buf[slot, 0:tm].astype(F32).reshape(tm, D_MODEL)
    y2 = buf[slot, tm:2 * tm].astype(F32).reshape(tm, D_MODEL)
    w = ew_ref[...]
    y = w[:, 0:1] * y1 + w[:, 1:2] * y2
    hfin = hres_ref[...] + g2_ref[...] * y
    ms = jnp.mean(hfin * hfin, axis=-1, keepdims=True)
    o_ref[...] = hfin * lax.rsqrt(ms + EPS) * gf_ref[...]


def _combine_call(pos1, pos2, ys_rows, ew, hres, gate2, gf):
    t, d = hres.shape
    tm = min(GATHER_TOKENS, t)
    per_batch = t // gate2.shape[0]
    assert per_batch % tm == 0
    tok = lambda w: pl.BlockSpec((tm, w), lambda i, *_: (i, 0))
    return pl.pallas_call(
        _combine_kernel,
        grid_spec=pltpu.PrefetchScalarGridSpec(
            num_scalar_prefetch=2,
            grid=(t // tm,),
            in_specs=[pl.BlockSpec(memory_space=pl.ANY), tok(LANES), tok(d),
                      pl.BlockSpec((None, 1, d), lambda i, *_: ((i * tm) // per_batch, 0, 0)),
                      pl.BlockSpec((1, d), lambda i, *_: (0, 0))],
            out_specs=tok(d),
            scratch_shapes=[pltpu.VMEM((2, 2 * tm) + ROW_TILE, BF16), pltpu.SemaphoreType.DMA((2,))]),
        out_shape=jax.ShapeDtypeStruct((t, d), F32),
        compiler_params=pltpu.CompilerParams(dimension_semantics=("arbitrary",),
                                             vmem_limit_bytes=VMEM_LIMIT),
        name="combine",
    )(pos1, pos2, ys_rows, ew, hres, gate2, gf)


def _slot_kernel(seg_ref, eid_ref, pos_ref):
    e = eid_ref[0:2, :]
    pos = eid_ref[2:4, :]
    for k in range(seg_ref.shape[0]):
        pos = pos + jnp.where(e == k, seg_ref[k], 0)
    pos_ref[...] = jnp.concatenate([pos, jnp.zeros((6, pos.shape[1]), I32)], axis=0)


def _slot_call(seg_start, eid):
    return pl.pallas_call(
        _slot_kernel,
        in_specs=[pl.BlockSpec(memory_space=pltpu.SMEM), pl.BlockSpec(eid.shape, lambda: (0, 0))],
        out_specs=pl.BlockSpec(eid.shape, lambda: (0, 0)),
        out_shape=jax.ShapeDtypeStruct(eid.shape, I32),
        name="slots",
    )(seg_start, eid)


def _sparse_moe(h2, eid, ew, counts, wg, wu, wd, hres, gate2, gf):
    bsz, s, d = hres.shape
    t = bsz * s
    ne = wg.shape[0]
    cnt = counts[0, :ne].astype(I32)
    seg_rows = ((cnt + EXPERT_ROWS - 1) // EXPERT_ROWS) * EXPERT_ROWS
    seg_end = jnp.cumsum(seg_rows)
    seg_start = seg_end - seg_rows
    pos = _slot_call(seg_start, eid)
    pos1, pos2 = pos[0], pos[1]
    n_tiles = (2 * t) // EXPERT_ROWS + ne
    tile_first_row = jnp.arange(n_tiles, dtype=I32) * EXPERT_ROWS
    tile_expert = jnp.minimum(jnp.sum(seg_end[None, :] <= tile_first_row[:, None], axis=1), ne - 1)
    n_used = (seg_end[ne - 1] // EXPERT_ROWS).reshape(1)
    n_rows = n_tiles * EXPERT_ROWS

    xs = _dispatch_call(pos1, pos2, h2.reshape((t,) + ROW_TILE), n_rows)
    ys = _expert_call(tile_expert.astype(I32), n_used.astype(I32), xs, wg, wu, wd)
    out = _combine_call(pos1, pos2, ys, ew.reshape(t, LANES), hres.reshape(t, d),
                        gate2.reshape(bsz, 1, d), gf.reshape(1, d))
    return out.reshape(bsz, s, d)


def kernel(x, c, w_ada, b_ada, norm1_g, w_in, rel_bias, conv_w, w_attn_branch, w_conv_branch, w_out,
           norm2_g, w_router_group, b_router_group, w_router_expert, b_router_expert,
           w_gate_e, w_up_e, w_down_e, norm_f_g):
    bsz, s, d = x.shape
    depth = w_ada.shape[0]
    h = x.astype(F32)
    c_pad = jnp.zeros((8, d), F32).at[:bsz].set(c.astype(F32))
    out = h
    for l in range(depth):
        mod = _ada_call(c_pad, w_ada[l].astype(F32), b_ada[l].astype(F32)[None, :])
        mod = mod[:bsz].reshape(bsz, 6, d)

        w = w_in[l]
        o_q, o_k, o_v, o_qi = 0, 512, 1024, 1536
        o_ki, o_wi, o_cb, o_gl = 2048, 2112, 2120, 3656
        wa = w[:, o_q:o_ki].astype(BF16)
        ki_w = w[:, o_ki:o_wi]
        wb = jnp.concatenate([ki_w, ki_w, w[:, o_wi:o_cb],
                              jnp.zeros((d, LANES - IDX_HEADS), w.dtype)], axis=1).astype(BF16)
        wc = w[:, o_cb:o_gl].astype(BF16)
        wd = w[:, o_gl:].astype(BF16)

        q, k, v, qi, kid, wi, z, gates = _proj_call(
            h, mod, norm1_g[l].astype(F32)[None, :], wa, wb, wc, wd, conv_w[l].astype(F32),
            tm=min(512, s))

        attn = _attn_call(rel_bias.astype(F32), q, qi, wi, k, v, kid, tq=min(TOPK_MAX, s // 4))

        wr32 = jnp.concatenate([w_router_expert[l], w_router_group[l],
                                jnp.zeros((d, LANES - N_EXPERTS - N_GROUPS), F32)], axis=1).astype(F32)
        wr_hi = wr32.astype(BF16)
        wr = jnp.stack([wr_hi, (wr32 - wr_hi.astype(F32)).astype(BF16)])
        br = jnp.concatenate([b_router_expert[l], b_router_group[l],
                              jnp.zeros((LANES - N_EXPERTS - N_GROUPS,), F32)])[None, :].astype(F32)
        hres, h2, eid, ew, counts = _post_call(
            attn, z, gates, h, mod, w_attn_branch[l].astype(BF16), w_conv_branch[l].astype(BF16),
            w_out[l].astype(BF16), norm2_g[l].astype(F32)[None, :], wr, br, tm=min(512, s))

        assert depth == 1, "the final RMSNorm is fused into the combine kernel of the only layer"
        out = _sparse_moe(h2, eid, ew, counts, w_gate_e[l].astype(F32), w_up_e[l].astype(F32),
                          w_down_e[l].astype(F32), hres, mod[:, 5, :], norm_f_g.astype(F32))
        h = out
    return out.astype(x.dtype)
```

```python
import functools
import math

import numpy as np
import jax
import jax.numpy as jnp
from jax import lax
from jax.experimental import pallas as pl
from jax.experimental.pallas import tpu as pltpu

F32 = jnp.float32
BF16 = jnp.bfloat16
I32 = jnp.int32

D_MODEL = 1024
N_HEADS = 8
HEAD_DIM = 64
ATTN_WIDTH = N_HEADS * HEAD_DIM
IDX_HEADS = 8
IDX_DIM = 64
TOPK_MAX = 256
N_BUCKETS = 32
MAX_DISTANCE = 128
CONV_WIDTH = 512
CONV_K = 3
N_GROUPS = 4
EXPERTS_PER_GROUP = 8
N_EXPERTS = N_GROUPS * EXPERTS_PER_GROUP
D_EXPERT = 512
EPS = 1e-6

LANES = 128
VMEM_LIMIT = 56 * 1024 * 1024

LOG2E = math.log2(math.e)
MASK_NEG = -1e30
SCORE_NEG = float(np.finfo(np.float32).min)
TINY_F32 = float(np.finfo(np.float32).tiny)
KEY_SPACE_AFTER = 40
RADIX_BITS_PER_STEP = 4


def _t5_bucket_starts():
    max_exact = N_BUCKETS // 2
    n = np.arange(0, MAX_DISTANCE + 1)
    nf = np.maximum(n, 1).astype(np.float32)
    val = (np.log(nf / np.float32(max_exact)) / np.float32(math.log(MAX_DISTANCE / max_exact))
           * np.float32(N_BUCKETS - max_exact)).astype(np.float32)
    inner = val[max_exact + 1:MAX_DISTANCE]
    assert np.min(np.abs(inner - np.round(inner))) > 1e-3
    large = np.minimum(max_exact + val.astype(np.int32), N_BUCKETS - 1)
    bucket = np.where(n < max_exact, n, large)
    assert bucket[MAX_DISTANCE] == N_BUCKETS - 1 and np.all(np.diff(bucket) >= 0)
    return [int(np.argmax(bucket >= b)) for b in range(N_BUCKETS)]


_BUCKET_START = _t5_bucket_starts()


def _ada_kernel(c_ref, w_ref, b_ref, o_ref):
    c = c_ref[...]
    ca = c * jax.nn.sigmoid(c)
    o_ref[...] = jnp.dot(ca, w_ref[...], preferred_element_type=F32,
                         precision=lax.Precision.HIGHEST) + b_ref[...]


def _ada_call(c_pad, w, b):
    rows, d = c_pad.shape
    n = w.shape[1]
    tn = 1536
    return pl.pallas_call(
        _ada_kernel,
        grid=(n // tn,),
        in_specs=[pl.BlockSpec((rows, d), lambda j: (0, 0)),
                  pl.BlockSpec((d, tn), lambda j: (0, j)),
                  pl.BlockSpec((1, tn), lambda j: (0, j))],
        out_specs=pl.BlockSpec((rows, tn), lambda j: (0, j)),
        out_shape=jax.ShapeDtypeStruct((rows, n), F32),
        compiler_params=pltpu.CompilerParams(dimension_semantics=("arbitrary",),
                                             vmem_limit_bytes=VMEM_LIMIT),
        name="ada",
    )(c_pad, w, b)


def _rms_mod(x, g, scale, shift):
    ms = jnp.mean(x * x, axis=-1, keepdims=True)
    y = x * lax.rsqrt(ms + EPS) * g
    return y * (1.0 + scale) + shift


def _proj_kernel(x_ref, mod_ref, g_ref, wa_ref, wb_ref, wc_ref, wd_ref, cw_ref,
                 q_ref, k_ref, v_ref, qi_ref, kid_ref, wi_ref, z_ref, gate_ref, ubuf, *, tm):
    h = _rms_mod(x_ref[...], g_ref[...], mod_ref[1:2, :], mod_ref[0:1, :])
    hb = h.astype(BF16)

    a = jnp.dot(hb, wa_ref[...], preferred_element_type=F32)
    q_ref[...] = (a[:, 0:512] * (HEAD_DIM ** -0.5 * LOG2E)).astype(BF16)
    k_ref[...] = a[:, 512:1024].astype(BF16)
    v_ref[...] = a[:, 1024:1536].astype(BF16)
    qi_ref[...] = a[:, 1536:2048].astype(BF16)

    b = jnp.dot(hb, wb_ref[...], preferred_element_type=F32)
    kid_ref[...] = b[:, 0:LANES].astype(BF16)
    wi_ref[...] = b[:, LANES:2 * LANES] * ((IDX_DIM ** -0.5) * (IDX_HEADS ** -0.5))

    cpart = jnp.dot(hb, wc_ref[...], preferred_element_type=F32)
    cb = cpart[:, 0:512]
    u = cpart[:, 512:1024] * cpart[:, 1024:1536]

    @pl.when(pl.program_id(1) == 0)
    def _():
        ubuf[0:8, :] = jnp.zeros((8, CONV_WIDTH), F32)

    ubuf[8:tm + 8, :] = u
    conv = (cw_ref[0:1, :] * ubuf[6:tm + 6, :] + cw_ref[1:2, :] * ubuf[7:tm + 7, :]
            + cw_ref[2:3, :] * u)
    z_ref[...] = (cb * conv).astype(BF16)
    ubuf[0:8, :] = ubuf[tm:tm + 8, :]

    d = jnp.dot(hb, wd_ref[...], preferred_element_type=F32)
    gate_ref[...] = jax.nn.sigmoid(d).astype(BF16)


def _proj_call(x, mod, g1, wa, wb, wc, wd, cw, tm):
    bsz, s, d = x.shape
    nt = s // tm
    tok = lambda w: pl.BlockSpec((None, tm, w), lambda b, t: (b, t, 0))
    full = lambda arr: pl.BlockSpec(arr.shape, lambda b, t: (0,) * arr.ndim)
    out_widths = [(512, BF16), (512, BF16), (512, BF16), (512, BF16), (LANES, BF16), (LANES, F32),
                  (512, BF16), (2 * D_MODEL, BF16)]
    return pl.pallas_call(
        functools.partial(_proj_kernel, tm=tm),
        grid=(bsz, nt),
        in_specs=[tok(d), pl.BlockSpec((None, 6, d), lambda b, t: (b, 0, 0)), full(g1),
                  full(wa), full(wb), full(wc), full(wd), full(cw)],
        out_specs=[tok(w) for w, _ in out_widths],
        out_shape=[jax.ShapeDtypeStruct((bsz, s, w), dt) for w, dt in out_widths],
        scratch_shapes=[pltpu.VMEM((tm + 8, CONV_WIDTH), F32)],
        compiler_params=pltpu.CompilerParams(dimension_semantics=("arbitrary", "arbitrary"),
                                             vmem_limit_bytes=VMEM_LIMIT),
        name="proj",
    )(x, mod, g1, wa, wb, wc, wd, cw)


def _to_key(f):
    b = pltpu.bitcast(f, I32)
    return jnp.where(b < 0, b ^ 0x7FFFFFFF, b)


def _from_key(k):
    return pltpu.bitcast(jnp.where(k < 0, k ^ 0x7FFFFFFF, k), F32)


def _bit_transpose32(words):
    a = list(words)
    j, m = 16, 0x0000FFFF
    while j:
        k = 0
        while k < 32:
            t = (a[k] ^ lax.shift_right_logical(a[k + j], jnp.int32(j))) & jnp.int32(m)
            a[k] = a[k] ^ t
            a[k + j] = a[k + j] ^ lax.shift_left(t, jnp.int32(j))
            k = (k + j + 1) & ~j
        j >>= 1
        m = (m ^ (m << j)) & 0xFFFFFFFF
    return a


def _fold_lanes(x, op):
    parts = [x[:, t * LANES:(t + 1) * LANES] for t in range(x.shape[1] // LANES)]
    while len(parts) > 1:
        nxt = [op(parts[a], parts[a + 1]) for a in range(0, len(parts) - 1, 2)]
        parts = nxt + ([parts[-1]] if len(parts) % 2 else [])
    return parts[0]


def _tile_lanes(x, n):
    return x if n == 1 else jnp.concatenate([x] * n, axis=1)


def _attn_kernel(rb_ref, q_ref, qi_ref, wi_ref, qin_ref, win_ref, k_ref, v_ref, kid_ref, o_ref,
                 sct_ref, planes_ref, cand_ref, mb_ref, tb_ref, qm_ref, qim_ref, m_ref, l_ref, acc_ref,
                 *, tq, topk):
    c = tq
    i = pl.program_id(1)
    nt_dims = (((1,), (1,)), ((), ()))
    lane = lax.broadcasted_iota(I32, (tq, LANES), 1)
    low_half = lane < HEAD_DIM
    row = lax.broadcasted_iota(I32, (tq, c), 0)
    col = lax.broadcasted_iota(I32, (tq, c), 1)

    @pl.when((pl.program_id(0) == 0) & (i == 0))
    def _():
        for p in range(32):
            planes_ref[p] = jnp.zeros(planes_ref.shape[1:], I32)
        for near in range(2):
            rel = row - col + (1 - near) * c
            n = jnp.maximum(rel, 0)
            for h in range(N_HEADS):
                t = jnp.full((tq, c), rb_ref[0, h], F32)
                for b in range(1, N_BUCKETS):
                    t = jnp.where(n >= _BUCKET_START[b], rb_ref[b, h], t)
                t = (t - rb_ref[N_BUCKETS - 1, h]) * LOG2E
                tb_ref[h, :, near * c:(near + 1) * c] = jnp.where(rel >= 0, t, MASK_NEG)


    def masked_heads(src_ref, dst_ref):
        for h in range(N_HEADS):
            hp, half = divmod(h, 2)
            keep = low_half if half == 0 else jnp.logical_not(low_half)
            pair = src_ref[:, hp * LANES:(hp + 1) * LANES]
            dst_ref[h] = jnp.where(keep, pair, jnp.zeros_like(pair))

    krow = lax.broadcasted_iota(I32, (c, tq), 0)
    qcol = lax.broadcasted_iota(I32, (c, tq), 1)
    int_min = jnp.int32(-2 ** 31)

    def fold_rows(x, op):
        parts = [x[r * 8:(r + 1) * 8, :] for r in range(x.shape[0] // 8)]
        while len(parts) > 1:
            nxt = [op(parts[a], parts[a + 1]) for a in range(0, len(parts) - 1, 2)]
            parts = nxt + ([parts[-1]] if len(parts) % 2 else [])
        return parts[0]

    def score_chunk(j, diag, w_t):
        start = pl.multiple_of(j * c, c)
        kc = kid_ref[pl.ds(start, c), :]
        acc = jnp.zeros((c, tq), F32)
        for h in range(N_HEADS):
            d = lax.dot_general(kc, qim_ref[h], nt_dims, preferred_element_type=F32)
            acc = acc + jnp.maximum(d, 0.0) * w_t[h:h + 1, :]
        ukey = _to_key(acc) ^ int_min
        if diag:
            causal = krow <= qcol
            acc = jnp.where(causal, acc, SCORE_NEG)
            ukey = jnp.where(causal, ukey, 0)
        sct_ref[pl.ds(start, c), :] = acc
        prow = pl.multiple_of(j * (c // 32), c // 32)
        for lt in range(tq // LANES):
            words = [ukey[k * 8:(k + 1) * 8, lt * LANES:(lt + 1) * LANES] for k in range(32)]
            for p, plane in enumerate(_bit_transpose32(words)):
                planes_ref[p, pl.ds(prow, c // 32), lt * LANES:(lt + 1) * LANES] = plane

    @pl.when(i == 0)
    def _():
        masked_heads(qi_ref, qim_ref)
        score_chunk(0, True, jnp.transpose(wi_ref[...])[0:IDX_HEADS, :])

    nchunks = i + 1
    n_causal = i * tq + lax.broadcasted_iota(I32, (1, tq), 1) + 1
    prow_iota = lax.broadcasted_iota(I32, cand_ref.shape, 0)
    cand_ref[...] = jnp.where(prow_iota < nchunks * (c // 32), -1, 0)

    def radix_cond(st):
        return (st[0] < 32) & (st[1] > 0)

    def radix_step(st):
        p0, _, prefix, need, ncand, settled = st
        cand = cand_ref[...]
        for sub in range(RADIX_BITS_PER_STEP):
            p = p0 + sub
            plane = planes_ref[p]
            ones = cand & plane
            cnt1 = jnp.sum(fold_rows(lax.population_count(ones), jnp.add), axis=0, keepdims=True)
            take1 = cnt1 >= need
            live = settled == 0
            need = jnp.where(live & jnp.logical_not(take1), need - cnt1, need)
            ncand = jnp.where(live, jnp.where(take1, cnt1, ncand - cnt1), ncand)
            prefix = jnp.where(live & take1, prefix | jnp.left_shift(jnp.int32(1), 31 - p), prefix)
            cand = cand & (plane ^ jnp.where(take1, 0, -1))
            settled = jnp.where(ncand == need, 1, settled)
        cand_ref[...] = cand
        return p0 + RADIX_BITS_PER_STEP, jnp.sum(1 - settled), prefix, need, ncand, settled

    done0 = jnp.where(n_causal <= topk, 1, 0)
    _, _, prefix, _, _, _ = lax.while_loop(
        radix_cond, radix_step,
        (jnp.int32(0), jnp.sum(1 - done0), jnp.zeros((1, tq), I32), jnp.full((1, tq), topk, I32),
         jnp.full((1, tq), 1, I32) * (nchunks * c), done0))
    guess_key = prefix ^ int_min
    guess = _from_key(guess_key)

    def count_keys(pred):
        def body(j, part):
            start = pl.multiple_of(j * c, c)
            hit = pred(sct_ref[pl.ds(start, c), :], start)
            return part + fold_rows(jnp.where(hit, 1.0, 0.0), jnp.add)
        part = lax.fori_loop(0, nchunks, body, jnp.zeros((8, tq), F32))
        return jnp.sum(part, axis=0, keepdims=True)

    kf = float(topk)
    guess_above = _from_key(guess_key + 1)
    guess_above = jnp.where(jnp.abs(guess_above) < TINY_F32,
                            jnp.where(guess >= 0.0, TINY_F32, 0.0), guess_above)
    guess_below = _from_key(guess_key - 2)

    def cond(st):
        return st[0] > 0

    def step(st):
        _, it, lo, hi, c_lo, c_hi, done = st
        mid = 0.5 * lo + 0.5 * hi
        lo_k, hi_k = _to_key(lo), _to_key(hi)
        mid_key = _from_key((lo_k >> 1) + (hi_k >> 1) + (lo_k & hi_k & 1))
        mid = jnp.where(it >= KEY_SPACE_AFTER, mid_key, mid)
        probe = jnp.where(it == 0, guess, jnp.where(lo == guess, guess_above, guess_below))
        mid = jnp.where((it <= 1) & (probe > lo) & (probe < hi), probe, mid)
        go = (done == 0) & (mid > lo) & (mid < hi)
        cnt = count_keys(lambda tile, start: tile >= mid)
        ge = cnt >= kf
        up = ge & go
        dn = jnp.logical_not(ge) & go
        lo = jnp.where(up, mid, lo)
        hi = jnp.where(dn, mid, hi)
        c_lo = jnp.where(up, cnt, c_lo)
        c_hi = jnp.where(dn, cnt, c_hi)
        adjacent = (lo == guess) & (hi == guess_above)
        done = jnp.where(go & (c_lo != kf) & jnp.logical_not(adjacent), 0, 1)
        return jnp.sum(1 - done), it + 1, lo, hi, c_lo, c_hi, done

    _, _, thr, _, c_lo, c_hi, _ = lax.while_loop(
        cond, step, (jnp.sum(1 - done0), jnp.int32(0), jnp.full((1, tq), SCORE_NEG, F32),
                     jnp.full((1, tq), -SCORE_NEG, F32), jnp.full((1, tq), 1.0, F32) * (nchunks * c),
                     jnp.zeros((1, tq), F32), done0))

    def write_mask(selected):
        def tiles(start, rows):
            tile = sct_ref[pl.ds(start, rows), :]
            key = lax.broadcasted_iota(I32, (rows, tq), 0) + start
            mb_ref[:, pl.ds(start, rows)] = jnp.transpose(
                jnp.where(selected(tile, key), 0.0, MASK_NEG))

        def body(j, carry):
            tiles(pl.multiple_of(j * 2 * c, 2 * c), 2 * c)
            return carry

        lax.fori_loop(0, nchunks // 2, body, 0)

        @pl.when(nchunks % 2 == 1)
        def _():
            tiles(pl.multiple_of((nchunks - 1) * c, c), c)

    tied = (n_causal > topk) & (c_lo > kf)
    any_tied = jnp.sum(jnp.where(tied, 1, 0)) > 0

    @pl.when(any_tied)
    def _():
        n_take = kf - c_hi

        def bis(_, st):
            lo_x, hi_x = st
            mid = (lo_x + hi_x) >> 1
            ok = count_keys(lambda tile, start: (tile == thr) & ((krow + start) <= mid)) >= n_take
            return jnp.where(ok, lo_x, mid), jnp.where(ok, mid, hi_x)

        nbits = int(math.ceil(math.log2(sct_ref.shape[0]))) + 1
        _, hi_x = lax.fori_loop(0, nbits, bis, (jnp.full((1, tq), -1, I32),
                                                jnp.full((1, tq), sct_ref.shape[0], I32)))
        cut = jnp.where(tied, hi_x, jnp.int32(2 ** 30))

        write_mask(lambda tile, key: (tile > thr) | ((tile == thr) & (key <= cut)))

    @pl.when(jnp.logical_not(any_tied))
    def _():
        write_mask(lambda tile, key: tile >= thr)

    masked_heads(q_ref, qm_ref)
    masked_heads(qin_ref, qim_ref)
    w_t_next = jnp.transpose(win_ref[...])[0:IDX_HEADS, :]
    has_next = i + 1 < pl.num_programs(1)
    m_ref[...] = jnp.full(m_ref.shape, MASK_NEG, F32)
    l_ref[...] = jnp.zeros(l_ref.shape, F32)
    acc_ref[...] = jnp.zeros(acc_ref.shape, F32)

    def attend(start, width, near):
        nt = width // LANES
        ones = jnp.ones((width, LANES), BF16)
        for h in range(N_HEADS):
            hp = h // 2
            kp = k_ref[pl.ds(start, width), hp * LANES:(hp + 1) * LANES]
            vp = v_ref[pl.ds(start, width), hp * LANES:(hp + 1) * LANES]
            s = lax.dot_general(qm_ref[h], kp, nt_dims, preferred_element_type=F32)
            s = s + mb_ref[:, pl.ds(start, width)]
            if near:
                s = s + tb_ref[h, :, 2 * c - width:2 * c]
            m_old = m_ref[h]
            m_new = jnp.maximum(m_old, jnp.max(_fold_lanes(s, jnp.maximum), axis=1, keepdims=True))
            p = jnp.exp2(s - _tile_lanes(m_new, nt)).astype(BF16)
            pv = jnp.dot(p, jnp.concatenate([vp, ones], axis=1), preferred_element_type=F32)
            alpha = jnp.exp2(m_old - m_new)
            m_ref[h] = m_new
            l_ref[h] = alpha * l_ref[h] + pv[:, LANES:]
            acc_ref[h] = alpha * acc_ref[h] + pv[:, :LANES]

    wide = 2 * c
    n_far = jnp.maximum(i - 1, 0)

    n_wide = n_far // 2

    def far_body(j, _):
        score_chunk(2 * j, False, w_t_next)
        attend(pl.multiple_of(j * wide, wide), wide, False)
        score_chunk(2 * j + 1, False, w_t_next)
        return 0

    lax.fori_loop(0, n_wide, far_body, 0)

    @pl.when(n_far % 2 == 1)
    def _():
        attend(pl.multiple_of((n_far - 1) * c, c), c, False)
        score_chunk(n_far - 1, False, w_t_next)

    @pl.when(i >= 1)
    def _():
        score_chunk(i - 1, False, w_t_next)
        attend(pl.multiple_of((i - 1) * c, c), wide, True)
        score_chunk(i, False, w_t_next)

    @pl.when(i == 0)
    def _():
        attend(0, c, True)
        score_chunk(0, False, w_t_next)

    @pl.when(has_next)
    def _():
        score_chunk(i + 1, True, w_t_next)

    for hp in range(N_HEADS // 2):
        lo = acc_ref[2 * hp] / l_ref[2 * hp]
        hi = acc_ref[2 * hp + 1] / l_ref[2 * hp + 1]
        o_ref[:, hp * LANES:(hp + 1) * LANES] = jnp.where(low_half, lo, hi).astype(BF16)


def _attn_call(rel_bias, q, qi, wi, k, v, kid, tq):
    bsz, s, _ = q.shape
    nq = s // tq
    topk = min(TOPK_MAX, s // 4)
    assert tq % LANES == 0 and tq >= MAX_DISTANCE
    blk = lambda w: pl.BlockSpec((None, tq, w), lambda b, i: (b, i, 0))
    nxt = lambda w: pl.BlockSpec((None, tq, w), lambda b, i: (b, jnp.minimum(i + 1, nq - 1), 0))
    whole = lambda w: pl.BlockSpec((None, s, w), lambda b, i: (b, 0, 0), pipeline_mode=pl.Buffered(1))
    return pl.pallas_call(
        functools.partial(_attn_kernel, tq=tq, topk=topk),
        grid=(bsz, nq),
        in_specs=[pl.BlockSpec(memory_space=pltpu.SMEM),
                  blk(ATTN_WIDTH), blk(ATTN_WIDTH), blk(LANES), nxt(ATTN_WIDTH), nxt(LANES),
                  whole(ATTN_WIDTH), whole(ATTN_WIDTH), whole(LANES)],
        out_specs=blk(ATTN_WIDTH),
        out_shape=jax.ShapeDtypeStruct((bsz, s, ATTN_WIDTH), BF16),
        scratch_shapes=[
            pltpu.VMEM((s, tq), F32),
            pltpu.VMEM((32, s // 32, tq), I32),
            pltpu.VMEM((s // 32, tq), I32),
            pltpu.VMEM((tq, s), F32),
            pltpu.VMEM((N_HEADS, tq, 2 * tq), F32),
            pltpu.VMEM((N_HEADS, tq, LANES), BF16),
            pltpu.VMEM((N_HEADS, tq, LANES), BF16),
            pltpu.VMEM((N_HEADS, tq, LANES), F32),
            pltpu.VMEM((N_HEADS, tq, LANES), F32),
            pltpu.VMEM((N_HEADS, tq, LANES), F32),
        ],
        compiler_params=pltpu.CompilerParams(dimension_semantics=("arbitrary", "arbitrary"),
                                             vmem_limit_bytes=VMEM_LIMIT),
        name="attn",
    )(rel_bias, q, qi, wi, qi, wi, k, v, kid)


def _post_kernel(attn_ref, z_ref, gate_ref, x_ref, mod_ref, wa_ref, wc_ref, wo_ref, g2_ref,
                 wr_ref, br_ref, hres_ref, h2_ref, eid_ref, ew_ref, counts_ref, cnt_ref):
    ya = jnp.dot(attn_ref[...], wa_ref[...], preferred_element_type=F32)
    yc = jnp.dot(z_ref[...], wc_ref[...], preferred_element_type=F32)
    merged = gate_ref[:, 0:D_MODEL].astype(F32) * ya + gate_ref[:, D_MODEL:].astype(F32) * yc
    o = jnp.dot(merged.astype(BF16), wo_ref[...], preferred_element_type=F32)
    hres = x_ref[...] + mod_ref[2:3, :] * o
    hres_ref[...] = hres
    h2 = _rms_mod(hres, g2_ref[...], mod_ref[4:5, :], mod_ref[3:4, :])
    h2_ref[...] = h2.astype(BF16).reshape(h2_ref.shape)

    h2_hi = h2.astype(BF16)
    h2_lo = (h2 - h2_hi.astype(F32)).astype(BF16)
    lg = (jnp.dot(h2_hi, wr_ref[0], preferred_element_type=F32)
          + (jnp.dot(h2_lo, wr_ref[0], preferred_element_type=F32)
             + jnp.dot(h2_hi, wr_ref[1], preferred_element_type=F32))) + br_ref[...]
    tm = lg.shape[0]
    lgt = jnp.transpose(lg)
    ninf = -jnp.inf
    grow = lax.broadcasted_iota(I32, (8, tm), 0)
    gl = jnp.where(grow < N_GROUPS, lgt[N_EXPERTS:N_EXPERTS + 8, :], ninf)
    gmax = jnp.max(gl, axis=0, keepdims=True)
    g_sel = jnp.min(jnp.where(gl == gmax, grow, N_GROUPS), axis=0, keepdims=True)
    g_w = 1.0 / jnp.sum(jnp.exp(gl - gmax), axis=0, keepdims=True)

    erow = lax.broadcasted_iota(I32, (N_EXPERTS, tm), 0)
    emask = (erow // EXPERTS_PER_GROUP) == g_sel
    el = jnp.where(emask, lgt[0:N_EXPERTS, :], ninf)
    emax = jnp.max(el, axis=0, keepdims=True)
    ee = jnp.exp(el - emax)
    e_prob = ee / jnp.sum(ee, axis=0, keepdims=True)
    p1 = jnp.max(jnp.where(emask, e_prob, -1.0), axis=0, keepdims=True)
    i1 = jnp.min(jnp.where(emask & (e_prob == p1), erow, LANES), axis=0, keepdims=True)
    rest = emask & (erow != i1)
    p2 = jnp.max(jnp.where(rest, e_prob, -1.0), axis=0, keepdims=True)
    i2 = jnp.min(jnp.where(rest & (e_prob == p2), erow, LANES), axis=0, keepdims=True)
    psum = p1 + p2
    w1 = g_w * (p1 / psum)
    w2 = g_w * (p2 / psum)
    ew_ref[...] = jnp.transpose(jnp.concatenate([w1, w2, jnp.zeros((LANES - 2, tm), F32)], axis=0))

    @pl.when((pl.program_id(0) == 0) & (pl.program_id(1) == 0))
    def _():
        cnt_ref[...] = jnp.zeros(cnt_ref.shape, F32)

    onehot_t = jnp.where((erow == i1) | (erow == i2), 1.0, 0.0).astype(BF16)
    upper = (lax.broadcasted_iota(I32, (tm, tm), 0) <= lax.broadcasted_iota(I32, (tm, tm), 1))
    cum_t = jnp.dot(onehot_t, jnp.where(upper, 1.0, 0.0).astype(BF16), preferred_element_type=F32)
    cnt_col = jnp.transpose(cnt_ref[...])[0:N_EXPERTS, 0:1]
    before = cum_t - onehot_t.astype(F32) + cnt_col
    r1 = jnp.sum(jnp.where(erow == i1, before, 0.0), axis=0, keepdims=True).astype(I32)
    r2 = jnp.sum(jnp.where(erow == i2, before, 0.0), axis=0, keepdims=True).astype(I32)
    eid_ref[...] = jnp.concatenate([i1, i2, r1, r2, jnp.zeros((4, tm), I32)], axis=0)
    padded = jnp.concatenate([onehot_t, jnp.zeros((LANES - N_EXPERTS, tm), BF16)], axis=0)
    tile_total = lax.dot_general(jnp.ones((8, tm), BF16), padded, (((1,), (1,)), ((), ())),
                                 preferred_element_type=F32)
    total = cnt_ref[...] + tile_total
    cnt_ref[...] = total
    counts_ref[...] = total


def _post_call(attn, z, gates, x, mod, wa, wc, wo, g2, wr, br, tm):
    bsz, s, d = x.shape
    nt = s // tm
    tok = lambda w: pl.BlockSpec((None, tm, w), lambda b, t: (b, t, 0))
    full = lambda arr: pl.BlockSpec(arr.shape, lambda b, t: (0,) * arr.ndim)
    return pl.pallas_call(
        _post_kernel,
        grid=(bsz, nt),
        in_specs=[tok(ATTN_WIDTH), tok(CONV_WIDTH), tok(2 * d), tok(d),
                  pl.BlockSpec((None, 6, d), lambda b, t: (b, 0, 0)),
                  full(wa), full(wc), full(wo), full(g2), full(wr), full(br)],
        out_specs=[tok(d), pl.BlockSpec((None, tm) + ROW_TILE, lambda b, t: (b, t, 0, 0)),
                   pl.BlockSpec((8, tm), lambda b, t: (0, b * nt + t)), tok(LANES),
                   pl.BlockSpec((8, LANES), lambda b, t: (0, 0))],
        out_shape=[jax.ShapeDtypeStruct((bsz, s, d), F32),
                   jax.ShapeDtypeStruct((bsz, s) + ROW_TILE, BF16),
                   jax.ShapeDtypeStruct((8, bsz * s), I32), jax.ShapeDtypeStruct((bsz, s, LANES), F32),
                   jax.ShapeDtypeStruct((8, LANES), F32)],
        scratch_shapes=[pltpu.VMEM((8, LANES), F32)],
        compiler_params=pltpu.CompilerParams(dimension_semantics=("arbitrary", "arbitrary"),
                                             vmem_limit_bytes=VMEM_LIMIT),
        name="post",
    )(attn, z, gates, x, mod, wa, wc, wo, g2, wr, br)


ROW_TILE = (D_MODEL // LANES, LANES)
EXPERT_ROWS = 256
SCATTER_TOKENS = 1024
GATHER_TOKENS = 512


ISSUE_UNROLL = 8


def _dispatch_kernel(pos1_ref, pos2_ref, h2_ref, xs_in_ref, xs_ref, sem):
    del xs_in_ref
    tm = h2_ref.shape[0]
    base = pl.program_id(0) * tm

    def issue(r, c):
        pltpu.make_async_copy(h2_ref.at[r], xs_ref.at[pos1_ref[base + r]], sem).start(priority=0)
        pltpu.make_async_copy(h2_ref.at[r], xs_ref.at[pos2_ref[base + r]], sem).start(priority=1)
        return c

    lax.fori_loop(0, tm, issue, 0, unroll=ISSUE_UNROLL)
    for _ in range(2):
        pltpu.make_async_copy(h2_ref, xs_ref.at[pl.ds(0, tm)], sem).wait()


def _dispatch_call(pos1, pos2, h2_rows, n_rows):
    t = h2_rows.shape[0]
    tm = min(SCATTER_TOKENS, t)
    xs0 = jnp.zeros((n_rows,) + ROW_TILE, BF16)
    return pl.pallas_call(
        _dispatch_kernel,
        grid_spec=pltpu.PrefetchScalarGridSpec(
            num_scalar_prefetch=2,
            grid=(t // tm,),
            in_specs=[pl.BlockSpec((tm,) + ROW_TILE, lambda i, p1, p2: (i, 0, 0)),
                      pl.BlockSpec(memory_space=pl.ANY)],
            out_specs=pl.BlockSpec(memory_space=pl.ANY),
            scratch_shapes=[pltpu.SemaphoreType.DMA(())]),
        out_shape=jax.ShapeDtypeStruct((n_rows,) + ROW_TILE, BF16),
        input_output_aliases={3: 0},
        compiler_params=pltpu.CompilerParams(dimension_semantics=("arbitrary",),
                                             vmem_limit_bytes=VMEM_LIMIT),
        name="dispatch",
    )(pos1, pos2, h2_rows, xs0)


def _expert_kernel(te_ref, nu_ref, nxt_ref, xs_ref, wg_ref, wu_ref, wd_ref, ys_ref,
                   wgs, wus, wds, wgb, wub, wdb, sem):
    g = pl.program_id(0)
    e = te_ref[g]
    e_prev = te_ref[jnp.maximum(g - 1, 0)]
    active = g < nu_ref[0]

    def weight_copies(expert):
        return (pltpu.make_async_copy(wg_ref.at[expert], wgs, sem.at[0]),
                pltpu.make_async_copy(wu_ref.at[expert], wus, sem.at[1]),
                pltpu.make_async_copy(wd_ref.at[expert], wds, sem.at[2]))

    @pl.when(active & (g == 0))
    def _():
        for cp in weight_copies(e):
            cp.start()

    @pl.when(active & ((g == 0) | (e != e_prev)))
    def _():
        for cp in weight_copies(e):
            cp.wait()
        wgb[...] = wgs[...].astype(BF16)
        wub[...] = wus[...].astype(BF16)
        wdb[...] = wds[...].astype(BF16)
        nxt = nxt_ref[e]

        @pl.when(nxt >= 0)
        def _():
            for cp in weight_copies(nxt):
                cp.start()

    @pl.when(g < nu_ref[0])
    def _():
        x = xs_ref[...].reshape(xs_ref.shape[0], D_MODEL)
        a = jnp.dot(x, wgb[...], preferred_element_type=F32)
        b = jnp.dot(x, wub[...], preferred_element_type=F32)
        hid = (a * jax.nn.sigmoid(a)) * b
        y = jnp.dot(hid.astype(BF16), wdb[...], preferred_element_type=F32)
        ys_ref[...] = y.astype(BF16).reshape(ys_ref.shape)

    @pl.when(g >= nu_ref[0])
    def _():
        ys_ref[...] = jnp.zeros(ys_ref.shape, BF16)


def _expert_call(tile_expert, n_used, next_expert, xs, wg, wu, wd):
    n_rows, d = xs.shape[0], D_MODEL
    nt = n_rows // EXPERT_ROWS
    rows = pl.BlockSpec((EXPERT_ROWS,) + ROW_TILE, lambda g, te, nu, nx: (g, 0, 0))
    hbm = pl.BlockSpec(memory_space=pl.ANY)
    return pl.pallas_call(
        _expert_kernel,
        grid_spec=pltpu.PrefetchScalarGridSpec(
            num_scalar_prefetch=3,
            grid=(nt,),
            in_specs=[rows, hbm, hbm, hbm],
            out_specs=rows,
            scratch_shapes=[pltpu.VMEM((d, D_EXPERT), F32), pltpu.VMEM((d, D_EXPERT), F32),
                            pltpu.VMEM((D_EXPERT, d), F32),
                            pltpu.VMEM((d, D_EXPERT), BF16), pltpu.VMEM((d, D_EXPERT), BF16),
                            pltpu.VMEM((D_EXPERT, d), BF16), pltpu.SemaphoreType.DMA((3,))]),
        out_shape=jax.ShapeDtypeStruct((n_rows,) + ROW_TILE, BF16),
        compiler_params=pltpu.CompilerParams(dimension_semantics=("arbitrary",),
                                             vmem_limit_bytes=VMEM_LIMIT),
        name="experts",
    )(tile_expert, n_used, next_expert, xs, wg, wu, wd)


def _combine_kernel(pos1_ref, pos2_ref, ys_ref, ew_ref, hres_ref, g2_ref, gf_ref, o_ref, buf, sem):
    tm = hres_ref.shape[0]
    t = pl.program_id(0)
    nt = pl.num_programs(0)

    def issue(tile, slot):
        base = tile * tm

        def body(r, c):
            pltpu.make_async_copy(ys_ref.at[pos1_ref[base + r]], buf.at[slot, r],
                                  sem.at[slot]).start(priority=0)
            pltpu.make_async_copy(ys_ref.at[pos2_ref[base + r]], buf.at[slot, tm + r],
                                  sem.at[slot]).start(priority=1)
            return c

        lax.fori_loop(0, tm, body, 0, unroll=ISSUE_UNROLL)

    @pl.when(t == 0)
    def _():
        issue(0, 0)

    @pl.when(t + 1 < nt)
    def _():
        issue(t + 1, (t + 1) % 2)

    slot = t % 2
    pltpu.make_async_copy(ys_ref.at[pl.ds(0, 2 * tm)], buf.at[slot], sem.at[slot]).wait()

    y1 = buf[slot, 0:tm].astype(F32).reshape(tm, D_MODEL)
    y2 = buf[slot, tm:2 * tm].astype(F32).reshape(tm, D_MODEL)
    w = ew_ref[...]
    y = w[:, 0:1] * y1 + w[:, 1:2] * y2
    hfin = hres_ref[...] + g2_ref[...] * y
    ms = jnp.mean(hfin * hfin, axis=-1, keepdims=True)
    o_ref[...] = hfin * lax.rsqrt(ms + EPS) * gf_ref[...]


def _combine_call(pos1, pos2, ys_rows, ew, hres, gate2, gf):
    t, d = hres.shape
    tm = min(GATHER_TOKENS, t)
    per_batch = t // gate2.shape[0]
    assert per_batch % tm == 0
    tok = lambda w: pl.BlockSpec((tm, w), lambda i, *_: (i, 0))
    return pl.pallas_call(
        _combine_kernel,
        grid_spec=pltpu.PrefetchScalarGridSpec(
            num_scalar_prefetch=2,
            grid=(t // tm,),
            in_specs=[pl.BlockSpec(memory_space=pl.ANY), tok(LANES), tok(d),
                      pl.BlockSpec((None, 1, d), lambda i, *_: ((i * tm) // per_batch, 0, 0)),
                      pl.BlockSpec((1, d), lambda i, *_: (0, 0))],
            out_specs=tok(d),
            scratch_shapes=[pltpu.VMEM((2, 2 * tm) + ROW_TILE, BF16), pltpu.SemaphoreType.DMA((2,))]),
        out_shape=jax.ShapeDtypeStruct((t, d), F32),
        compiler_params=pltpu.CompilerParams(dimension_semantics=("arbitrary",),
                                             vmem_limit_bytes=VMEM_LIMIT),
        name="combine",
    )(pos1, pos2, ys_rows, ew, hres, gate2, gf)


def _slot_kernel(seg_ref, eid_ref, pos_ref):
    e = eid_ref[0:2, :]
    pos = eid_ref[2:4, :]
    for k in range(seg_ref.shape[0]):
        pos = pos + jnp.where(e == k, seg_ref[k], 0)
    pos_ref[...] = jnp.concatenate([pos, jnp.zeros((6, pos.shape[1]), I32)], axis=0)


def _slot_call(seg_start, eid):
    return pl.pallas_call(
        _slot_kernel,
        in_specs=[pl.BlockSpec(memory_space=pltpu.SMEM), pl.BlockSpec(eid.shape, lambda: (0, 0))],
        out_specs=pl.BlockSpec(eid.shape, lambda: (0, 0)),
        out_shape=jax.ShapeDtypeStruct(eid.shape, I32),
        name="slots",
    )(seg_start, eid)


def _sparse_moe(h2, eid, ew, counts, wg, wu, wd, hres, gate2, gf):
    bsz, s, d = hres.shape
    t = bsz * s
    ne = wg.shape[0]
    cnt = counts[0, :ne].astype(I32)
    seg_rows = ((cnt + EXPERT_ROWS - 1) // EXPERT_ROWS) * EXPERT_ROWS
    seg_end = jnp.cumsum(seg_rows)
    seg_start = seg_end - seg_rows
    pos = _slot_call(seg_start, eid)
    pos1, pos2 = pos[0], pos[1]
    n_tiles = (2 * t) // EXPERT_ROWS + ne
    tile_first_row = jnp.arange(n_tiles, dtype=I32) * EXPERT_ROWS
    tile_expert = jnp.minimum(jnp.sum(seg_end[None, :] <= tile_first_row[:, None], axis=1), ne - 1)
    n_used = (seg_end[ne - 1] // EXPERT_ROWS).reshape(1)
    n_rows = n_tiles * EXPERT_ROWS

    xs = _dispatch_call(pos1, pos2, h2.reshape((t,) + ROW_TILE), n_rows)
    first_at_or_after = jnp.flip(lax.cummin(jnp.flip(
        jnp.where(seg_rows > 0, jnp.arange(ne, dtype=I32), ne))))
    next_expert = jnp.concatenate([first_at_or_after[1:], jnp.full((1,), ne, I32)])
    next_expert = jnp.where(next_expert >= ne, -1, next_expert)
    ys = _expert_call(tile_expert.astype(I32), n_used.astype(I32), next_expert.astype(I32),
                      xs, wg, wu, wd)
    out = _combine_call(pos1, pos2, ys, ew.reshape(t, LANES), hres.reshape(t, d),
                        gate2.reshape(bsz, 1, d), gf.reshape(1, d))
    return out.reshape(bsz, s, d)


def kernel(x, c, w_ada, b_ada, norm1_g, w_in, rel_bias, conv_w, w_attn_branch, w_conv_branch, w_out,
           norm2_g, w_router_group, b_router_group, w_router_expert, b_router_expert,
           w_gate_e, w_up_e, w_down_e, norm_f_g):
    bsz, s, d = x.shape
    depth = w_ada.shape[0]
    h = x.astype(F32)
    c_pad = jnp.zeros((8, d), F32).at[:bsz].set(c.astype(F32))
    out = h
    for l in range(depth):
        mod = _ada_call(c_pad, w_ada[l].astype(F32), b_ada[l].astype(F32)[None, :])
        mod = mod[:bsz].reshape(bsz, 6, d)

        w = w_in[l]
        o_q, o_k, o_v, o_qi = 0, 512, 1024, 1536
        o_ki, o_wi, o_cb, o_gl = 2048, 2112, 2120, 3656
        wa = w[:, o_q:o_ki].astype(BF16)
        ki_w = w[:, o_ki:o_wi]
        wb = jnp.concatenate([ki_w, ki_w, w[:, o_wi:o_cb],
                              jnp.zeros((d, LANES - IDX_HEADS), w.dtype)], axis=1).astype(BF16)
        wc = w[:, o_cb:o_gl].astype(BF16)
        wd = w[:, o_gl:].astype(BF16)

        q, k, v, qi, kid, wi, z, gates = _proj_call(
            h, mod, norm1_g[l].astype(F32)[None, :], wa, wb, wc, wd, conv_w[l].astype(F32),
            tm=min(512, s))

        attn = _attn_call(rel_bias.astype(F32), q, qi, wi, k, v, kid, tq=min(TOPK_MAX, s // 4))

        wr32 = jnp.concatenate([w_router_expert[l], w_router_group[l],
                                jnp.zeros((d, LANES - N_EXPERTS - N_GROUPS), F32)], axis=1).astype(F32)
        wr_hi = wr32.astype(BF16)
        wr = jnp.stack([wr_hi, (wr32 - wr_hi.astype(F32)).astype(BF16)])
        br = jnp.concatenate([b_router_expert[l], b_router_group[l],
                              jnp.zeros((LANES - N_EXPERTS - N_GROUPS,), F32)])[None, :].astype(F32)
        hres, h2, eid, ew, counts = _post_call(
            attn, z, gates, h, mod, w_attn_branch[l].astype(BF16), w_conv_branch[l].astype(BF16),
            w_out[l].astype(BF16), norm2_g[l].astype(F32)[None, :], wr, br, tm=min(512, s))

        assert depth == 1, "the final RMSNorm is fused into the combine kernel of the only layer"
        out = _sparse_moe(h2, eid, ew, counts, w_gate_e[l].astype(F32), w_up_e[l].astype(F32),
                          w_down_e[l].astype(F32), hres, mod[:, 5, :], norm_f_g.astype(F32))
        h = out
    return out.astype(x.dtype)
```
